```python
import jax
import jax.numpy as jnp
from jax import lax
import numpy as np

D_MODEL = 1024
BATCH = 8
SEQ = 2048
DEPTH = 2

GRID_W = 64
CTX_LEN = 256
EPS = 1e-6
NEG_INF = -1e30
HEAD_DIM = 64

POOL_WIDTH = D_MODEL // 4
POOL_WINDOWS = (2, 4, 8, 16)
POOL_GROUP = POOL_WIDTH // len(POOL_WINDOWS)

GDN_WIDTH = (3 * D_MODEL) // 8
GDN_HEADS = GDN_WIDTH // HEAD_DIM
GDN_CHUNK = 64
CONV_K = 5

ATTN_WIDTH = D_MODEL - POOL_WIDTH - GDN_WIDTH
ATTN_HEADS = ATTN_WIDTH // HEAD_DIM
KV_HEADS = 2
ATTN_GROUP = ATTN_HEADS // KV_HEADS
KV_WIDTH = KV_HEADS * HEAD_DIM
WINDOW = 128
ATTN_BLOCK = 128
ROPE_BASE = 10000.0
ROPE_PAIRS = HEAD_DIM // 4

IN_SPLITS = (POOL_WIDTH, 3 * GDN_WIDTH, GDN_WIDTH, 4 * GDN_HEADS, ATTN_WIDTH, KV_WIDTH, KV_WIDTH)
IN_WIDTH = POOL_WIDTH + 4 * GDN_WIDTH + 4 * GDN_HEADS + ATTN_WIDTH + 2 * KV_WIDTH

PEER_HEADS = 8
N_KEYS = 128
N_EXPERTS = N_KEYS * N_KEYS
PEER_TOPK = 16
PEER_QDIM = 256
PEER_HALF = PEER_QDIM // 2
PEER_BLOCK = 128

kernel_name = "hybrid_pool_gdn_swa_peer_dit"


def rmsnorm(x, gain):
    xf = x.astype(jnp.float32)
    y = xf * lax.rsqrt(jnp.mean(jnp.square(xf), axis=-1, keepdims=True) + EPS)
    return (y * gain.astype(jnp.float32)).astype(x.dtype)


def l2norm(x):
    return x * lax.rsqrt(jnp.sum(jnp.square(x), axis=-1, keepdims=True) + EPS)


def split_cols(p):
    idx, acc = [], 0
    for w in IN_SPLITS[:-1]:
        acc += w
        idx.append(acc)
    return jnp.split(p, idx, axis=-1)


def axial_rope_tables(seq_len):
    rows = seq_len // GRID_W
    r, col = jnp.meshgrid(jnp.arange(rows), jnp.arange(GRID_W), indexing='ij')
    pos = jnp.stack([r.reshape(-1), col.reshape(-1)], axis=-1).astype(jnp.float32)
    inv = jnp.power(ROPE_BASE, -jnp.arange(ROPE_PAIRS, dtype=jnp.float32) / ROPE_PAIRS)
    ang = pos[:, :, None] * inv
    return jnp.cos(ang), jnp.sin(ang)


def apply_rope(x, tables):
    cos, sin = tables
    xf = x.astype(jnp.float32).reshape(x.shape[:-1] + (2, 2, ROPE_PAIRS))
    x1, x2 = xf[..., 0, :], xf[..., 1, :]
    c, s = cos[:, None], sin[:, None]
    out = jnp.stack([x1 * c - x2 * s, x2 * c + x1 * s], axis=-2)
    return out.reshape(x.shape).astype(x.dtype)


def multiscale_pool(a, w_pool, pool_scale):
    L = a.shape[1]
    af = a.astype(jnp.float32)
    cs = jnp.pad(jnp.cumsum(af, axis=1), ((0, 0), (1, 0), (0, 0)))
    t = jnp.arange(L)
    outs = []
    for gi, w in enumerate(POOL_WINDOWS):
        ch = slice(gi * POOL_GROUP, (gi + 1) * POOL_GROUP)
        lo = jnp.clip(t - w // 2, 0, L)
        hi = jnp.clip(t + w - w // 2, 0, L)
        csg = cs[..., ch]
        win_sum = jnp.take(csg, hi, axis=1) - jnp.take(csg, lo, axis=1)
        cnt = (hi - lo).astype(jnp.float32)[None, :, None]
        pooled = win_sum / cnt - af[..., ch]
        outs.append(jnp.einsum('blc,ce->ble', pooled, w_pool[gi].astype(jnp.float32)))
    return (jnp.concatenate(outs, axis=-1) * pool_scale.astype(jnp.float32)).astype(a.dtype)


def short_conv(u, w):
    ch = u.shape[-1]
    y = lax.conv_general_dilated(u, w[:, None, :].astype(u.dtype), window_strides=(1,),
                                 padding=[(CONV_K // 2, CONV_K // 2)],
                                 dimension_numbers=('NWC', 'WIO', 'NWC'), feature_group_count=ch)
    return jax.nn.silu(y)


def gated_delta_chunked(q, k, v, g, beta, state0):
    bn, nh, L, dk = k.shape
    dv = v.shape[-1]
    n = L // GDN_CHUNK
    q = q * (dk ** -0.5)
    rs = lambda t: t.reshape(bn, nh, n, GDN_CHUNK, t.shape[-1])
    qc, kc, vc = rs(q), rs(k), rs(v)
    bc = beta.reshape(bn, nh, n, GDN_CHUNK)
    gc = jnp.cumsum(g.reshape(bn, nh, n, GDN_CHUNK), axis=-1)
    tri = jnp.tril(jnp.ones((GDN_CHUNK, GDN_CHUNK), dtype=bool))
    strict = jnp.tril(jnp.ones((GDN_CHUNK, GDN_CHUNK), dtype=bool), -1)
    diff = gc[..., :, None] - gc[..., None, :]
    decay = jnp.where(tri, jnp.exp(jnp.where(tri, diff, 0.0)), 0.0)
    kb = kc * bc[..., None]
    lower = jnp.where(strict, jnp.einsum('bhnid,bhnjd->bhnij', kb, kc) * decay, 0.0)
    a_mat = lower + jnp.eye(GDN_CHUNK, dtype=jnp.float32)
    rhs = jnp.concatenate([vc * bc[..., None], kb * jnp.exp(gc)[..., None]], axis=-1)
    sol = lax.linalg.triangular_solve(a_mat, rhs, left_side=True, lower=True, unit_diagonal=True)
    u_c, w_c = sol[..., :dv], sol[..., dv:]
    attn_local = jnp.einsum('bhnid,bhnjd->bhnij', qc, kc) * decay
    q_dec = qc * jnp.exp(gc)[..., None]
    k_dec = kc * jnp.exp(gc[..., -1:] - gc)[..., None]
    g_last = jnp.exp(gc[..., -1])

    def step(state, xs):
        u, w, qd, kd, al, gl = xs
        v_new = u - jnp.einsum('bhcd,bhde->bhce', w, state)
        o = jnp.einsum('bhcd,bhde->bhce', qd, state) + jnp.einsum('bhij,bhje->bhie', al, v_new)
        state = state * gl[..., None, None] + jnp.einsum('bhcd,bhce->bhde', kd, v_new)
        return state, o

    xs = tuple(jnp.moveaxis(t, 2, 0) for t in (u_c, w_c, q_dec, k_dec, attn_local, g_last))
    state, o = lax.scan(step, state0, xs)
    o = jnp.moveaxis(o, 0, 2).reshape(bn, nh, L, dv)
    return o, state


def gated_deltanet(qkv_x, z_x, ba_x, qkv_c, z_c, ba_c, conv_w, a_log, dt_bias, out_gain, need_ctx):
    def prep(qkv, ba):
        bn, L, _ = qkv.shape
        y = short_conv(qkv, conv_w).astype(jnp.float32)
        heads = lambda t: t.reshape(bn, L, GDN_HEADS, HEAD_DIM).transpose(0, 2, 1, 3)
        q, k, v = [heads(t) for t in jnp.split(y, 3, axis=-1)]
        ba = ba.astype(jnp.float32).reshape(bn, L, 4, GDN_HEADS).transpose(2, 0, 3, 1)
        beta = jax.nn.sigmoid(ba[:2])
        g = -jnp.exp(a_log.astype(jnp.float32))[:, None, :, None] * jax.nn.softplus(
            ba[2:] + dt_bias.astype(jnp.float32)[:, None, :, None])
        return l2norm(q), l2norm(k), v, beta, g

    def finish(o, z):
        bn, nh, L, d = o.shape
        o = o.transpose(0, 2, 1, 3)
        o = o * lax.rsqrt(jnp.mean(jnp.square(o), axis=-1, keepdims=True) + EPS) * out_gain.astype(jnp.float32)
        return (o.reshape(bn, L, GDN_WIDTH) * jax.nn.silu(z.astype(jnp.float32))).astype(z.dtype)

    qx, kx, vx, beta_x, g_x = prep(qkv_x, ba_x)
    qc, kc, vc, beta_c, g_c = prep(qkv_c, ba_c)
    bn = qx.shape[0]
    outs_x, outs_c = [], []
    for d in range(2):
        fl = (lambda t: jnp.flip(t, axis=2)) if d == 1 else (lambda t: t)
        s0 = jnp.zeros((bn, GDN_HEADS, HEAD_DIM, HEAD_DIM), jnp.float32)
        oc_d, s_ctx = gated_delta_chunked(fl(qc), fl(kc), fl(vc), fl(g_c[d]), fl(beta_c[d]), s0)
        ox_d, _ = gated_delta_chunked(fl(qx), fl(kx), fl(vx), fl(g_x[d]), fl(beta_x[d]), s_ctx)
        outs_x.append(fl(ox_d))
        outs_c.append(fl(oc_d))
    y_x = finish(outs_x[0] + outs_x[1], z_x)
    y_c = finish(outs_c[0] + outs_c[1], z_c) if need_ctx else None
    return y_x, y_c


def window_attention(q_x, k_x, v_x, q_c, k_c, v_c, sink, rope, need_ctx):
    bn, S, _ = q_x.shape
    lc = q_c.shape[1]
    nb = S // ATTN_BLOCK
    nk = 3 * ATTN_BLOCK
    scale = HEAD_DIM ** -0.5
    qx = apply_rope(q_x.reshape(bn, S, ATTN_HEADS, HEAD_DIM), rope)
    kx = apply_rope(k_x.reshape(bn, S, KV_HEADS, HEAD_DIM), rope)
    vx = v_x.reshape(bn, S, KV_HEADS, HEAD_DIM)
    kc = k_c.reshape(bn, lc, KV_HEADS, HEAD_DIM)
    vc = v_c.reshape(bn, lc, KV_HEADS, HEAD_DIM)
    qb = qx.reshape(bn, nb, ATTN_BLOCK, KV_HEADS, ATTN_GROUP, HEAD_DIM)

    def band(t):
        tp = jnp.pad(t, ((0, 0), (ATTN_BLOCK, ATTN_BLOCK), (0, 0), (0, 0)))
        tp = tp.reshape(bn, nb + 2, ATTN_BLOCK, KV_HEADS, HEAD_DIM)
        return jnp.concatenate([tp[:, :-2], tp[:, 1:-1], tp[:, 2:]], axis=2)

    kb, vb = band(kx), band(vx)
    qo = jnp.arange(ATTN_BLOCK)[:, None]
    ko = jnp.arange(nk)[None, :]
    kpos = jnp.arange(nb)[:, None, None] * ATTN_BLOCK - ATTN_BLOCK + ko[None]
    mask = (jnp.abs(ko - ATTN_BLOCK - qo) <= WINDOW)[None] & (kpos >= 0) & (kpos < S)
    s_loc = jnp.einsum('bnqhgd,bnkhd->bnhgqk', qb, kb).astype(jnp.float32) * scale
    s_loc = jnp.where(mask[None, :, None, None], s_loc, NEG_INF)
    s_ctx = jnp.einsum('bnqhgd,bkhd->bnhgqk', qb, kc).astype(jnp.float32) * scale
    sink_h = sink.astype(jnp.float32).reshape(KV_HEADS, ATTN_GROUP, 1, 1)
    sink_l = jnp.broadcast_to(sink_h, s_loc.shape[:-1] + (1,))
    p = jax.nn.softmax(jnp.concatenate([s_loc, s_ctx, sink_l], axis=-1), axis=-1).astype(vx.dtype)
    o = (jnp.einsum('bnhgqk,bnkhd->bnqhgd', p[..., :nk], vb)
         + jnp.einsum('bnhgqk,bkhd->bnqhgd', p[..., nk:nk + lc], vc))
    y_x = o.reshape(bn, S, ATTN_WIDTH)
    if need_ctx:
        qcc = q_c.reshape(bn, lc, KV_HEADS, ATTN_GROUP, HEAD_DIM)
        s = jnp.einsum('bqhgd,bkhd->bhgqk', qcc, kc).astype(jnp.float32) * scale
        sink_c = jnp.broadcast_to(sink_h, s.shape[:-1] + (1,))
        pc = jax.nn.softmax(jnp.concatenate([s, sink_c], axis=-1), axis=-1).astype(vc.dtype)
        y_c = jnp.einsum('bhgqk,bkhd->bqhgd', pc[..., :lc], vc).reshape(bn, lc, ATTN_WIDTH)
    else:
        y_c = None
    return y_x, y_c


def token_mixing(hx, hc, w_in, conv_w, a_log, dt_bias, gdn_norm, w_pool, pool_scale, sink, w_out, rope, need_ctx):
    ax, qkvx, zx, bax, qx, kx, vx = split_cols(hx @ w_in)
    ac, qkvc, zc, bac, qc, kc, vc = split_cols(hc @ w_in)
    ya_x = multiscale_pool(ax, w_pool, pool_scale)
    yb_x, yb_c = gated_deltanet(qkvx, zx, bax, qkvc, zc, bac, conv_w, a_log, dt_bias, gdn_norm, need_ctx)
    yc_x, yc_c = window_attention(qx, kx, vx, qc, kc, vc, sink, rope, need_ctx)
    out_x = jnp.concatenate([ya_x, yb_x, yc_x], axis=-1) @ w_out
    if need_ctx:
        ya_c = multiscale_pool(ac, w_pool, pool_scale)
        out_c = jnp.concatenate([ya_c, yb_c, yc_c], axis=-1) @ w_out
    else:
        out_c = None
    return out_x, out_c


def peer_ffn(h, w_query, sub_keys, u_tab, v_tab):
    bn, L, D = h.shape
    h_blocks = h.reshape(bn * L // PEER_BLOCK, PEER_BLOCK, D)

    def one_block(hb):
        q = (hb @ w_query).reshape(PEER_BLOCK, PEER_HEADS, 2, PEER_HALF)
        s = jnp.einsum('thpd,hpkd->thpk', q, sub_keys).astype(jnp.float32)
        sv, si = lax.top_k(s, PEER_TOPK)
        cand = (sv[:, :, 0, :, None] + sv[:, :, 1, None, :]).reshape(PEER_BLOCK, PEER_HEADS, PEER_TOPK * PEER_TOPK)
        cv, ci = lax.top_k(cand, PEER_TOPK)
        i1 = jnp.take_along_axis(si[:, :, 0], ci // PEER_TOPK, axis=-1)
        i2 = jnp.take_along_axis(si[:, :, 1], ci % PEER_TOPK, axis=-1)
        expert = i1 * N_KEYS + i2
        gate = jax.nn.softmax(cv, axis=-1)
        u = jnp.take(u_tab, expert, axis=0)
        v = jnp.take(v_tab, expert, axis=0)
        act = jax.nn.gelu(jnp.einsum('td,thkd->thk', hb, u).astype(jnp.float32), approximate=False)
        return jnp.einsum('thk,thkd->td', (gate * act).astype(v.dtype), v).astype(h.dtype)

    return lax.map(one_block, h_blocks).reshape(bn, L, D)


def setup_inputs(seed: int = 0) -> dict:
    key = jax.random.key(seed)
    ks = jax.random.split(key, 24)
    f32 = jnp.float32
    nrm = lambda k, shape, s: jax.random.normal(k, shape, f32) * s
    D = D_MODEL
    dt = jnp.exp(jax.random.uniform(ks[11], (DEPTH, 2, GDN_HEADS), f32, np.log(1e-3), np.log(1e-1)))
    return {
        "x": nrm(ks[0], (BATCH, SEQ, D), 1.0),
        "c": nrm(ks[1], (BATCH, D), 1.0),
        "ctx": nrm(ks[2], (BATCH, CTX_LEN, D), 1.0),
        "c_ctx": nrm(ks[3], (D,), 1.0),
        "w_ada": nrm(ks[4], (DEPTH, D, 6 * D), 0.5 * D ** -0.5),
        "b_ada": nrm(ks[5], (DEPTH, 6 * D), 0.02),
        "norm1": 1.0 + nrm(ks[6], (DEPTH, D), 0.02),
        "norm2": 1.0 + nrm(ks[7], (DEPTH, D), 0.02),
        "w_in": nrm(ks[8], (DEPTH, D, IN_WIDTH), D ** -0.5),
        "conv_w": nrm(ks[9], (DEPTH, CONV_K, 3 * GDN_WIDTH), CONV_K ** -0.5),
        "a_log": jnp.log(jax.random.uniform(ks[10], (DEPTH, 2, GDN_HEADS), f32, 1.0, 16.0)),
        "dt_bias": jnp.log(jnp.expm1(dt)),
        "gdn_norm": 1.0 + nrm(ks[12], (DEPTH, HEAD_DIM), 0.02),
        "w_pool": nrm(ks[13], (DEPTH, len(POOL_WINDOWS), POOL_GROUP, POOL_GROUP), POOL_GROUP ** -0.5),
        "pool_scale": 1.0 + nrm(ks[14], (DEPTH, POOL_WIDTH), 0.1),
        "sink": nrm(ks[15], (DEPTH, ATTN_HEADS), 1.0),
        "w_out": nrm(ks[16], (DEPTH, D, D), D ** -0.5),
        "peer_wq": nrm(ks[17], (DEPTH, D, PEER_HEADS * PEER_QDIM), D ** -0.5),
        "peer_keys": nrm(ks[18], (DEPTH, PEER_HEADS, 2, N_KEYS, PEER_HALF), PEER_HALF ** -0.5),
        "peer_u": nrm(ks[19], (DEPTH, N_EXPERTS, D), D ** -0.5),
        "peer_v": nrm(ks[20], (DEPTH, N_EXPERTS, D), 1.0),
        "norm_f": 1.0 + nrm(ks[21], (D,), 0.02),
    }


def reference(x, c, ctx, c_ctx, w_ada, b_ada, norm1, norm2, w_in, conv_w, a_log, dt_bias, gdn_norm,
              w_pool, pool_scale, sink, w_out, peer_wq, peer_keys, peer_u, peer_v, norm_f):
    rope = axial_rope_tables(x.shape[1])
    silu_c = jax.nn.silu(c)
    silu_cc = jax.nn.silu(c_ctx)
    h_ctx = ctx
    for l in range(DEPTH):
        need_ctx = l < DEPTH - 1
        mod_x = (silu_c @ w_ada[l] + b_ada[l])[:, None, :]
        mod_c = silu_cc @ w_ada[l] + b_ada[l]
        sh1, sc1, g1, sh2, sc2, g2 = jnp.split(mod_x, 6, axis=-1)
        csh1, csc1, cg1, csh2, csc2, cg2 = jnp.split(mod_c, 6, axis=-1)
        hx = rmsnorm(x, norm1[l]) * (1.0 + sc1) + sh1
        hc = rmsnorm(h_ctx, norm1[l]) * (1.0 + csc1) + csh1
        out_x, out_c = token_mixing(hx, hc, w_in[l], conv_w[l], a_log[l], dt_bias[l], gdn_norm[l],
                                    w_pool[l], pool_scale[l], sink[l], w_out[l], rope, need_ctx)
        x = x + g1 * out_x
        hx = rmsnorm(x, norm2[l]) * (1.0 + sc2) + sh2
        x = x + g2 * peer_ffn(hx, peer_wq[l], peer_keys[l], peer_u[l], peer_v[l])
        if need_ctx:
            h_ctx = h_ctx + cg1 * out_c
            hc = rmsnorm(h_ctx, norm2[l]) * (1.0 + csc2) + csh2
            h_ctx = h_ctx + cg2 * peer_ffn(hc, peer_wq[l], peer_keys[l], peer_u[l], peer_v[l])
    return rmsnorm(x, norm_f)
```

```python
import functools

import jax
import jax.numpy as jnp
from jax import lax
from jax.experimental import pallas as pl
from jax.experimental.pallas import tpu as pltpu

F32, BF16 = jnp.float32, jnp.bfloat16
HIGHEST = lax.Precision.HIGHEST

EPS = 1e-6
NEG_INF = -1e30
HEAD_DIM = 64
GRID_W = 64
POOL_WINDOWS = (2, 4, 8, 16)
GDN_CHUNK = 64
CONV_K = 5
KV_HEADS = 2
WINDOW = 128
ATTN_BLOCK = 128
ROPE_BASE = 10000.0
ROPE_PAIRS = HEAD_DIM // 4
PEER_HEADS = 8
N_KEYS = 128
PEER_TOPK = 16

LANE = 128
SUBLANE = 8
TOK_TILE = 256
ROW_CHUNK = 256
PEER_ROUTE_TILE = 256
PEER_TOK_TILE = 512
PEER_EXP_TILE = 512
VMEM_LIMIT = 48 * 1024 * 1024

_CAND_ROWS = tuple(PEER_TOPK // (a + 1) for a in range(PEER_TOPK))
_N_CAND = sum(_CAND_ROWS)
_CAND_PAD = -(-_N_CAND // SUBLANE) * SUBLANE


def _cparams(*sem):
    return pltpu.CompilerParams(dimension_semantics=sem, vmem_limit_bytes=VMEM_LIMIT)


def _dot(a, b):
    return jnp.dot(a.astype(BF16), b.astype(BF16), preferred_element_type=F32)


def _dot_nt(a, b):
    return lax.dot_general(a.astype(BF16), b.astype(BF16), (((1,), (1,)), ((), ())),
                           preferred_element_type=F32)


def _dot_tn(a, b):
    return lax.dot_general(a.astype(BF16), b.astype(BF16), (((0,), (0,)), ((), ())),
                           preferred_element_type=F32)


def _dot_hi(a, b):
    return jnp.dot(a, b, precision=HIGHEST, preferred_element_type=F32)


def _silu(x):
    return x * jax.nn.sigmoid(x)


def _softplus(x):
    return jnp.maximum(x, 0.0) + jnp.log1p(jnp.exp(-jnp.abs(x)))


def _modnorm(x, gain, scale, shift):
    y = x * lax.rsqrt(jnp.mean(x * x, axis=-1, keepdims=True) + EPS) * gain
    return y * (1.0 + scale) + shift


def _ada_kernel(c_ref, w_ref, b_ref, o_ref):
    o_ref[0] = _dot_hi(_silu(c_ref[...]), w_ref[0]) + b_ref[0]


def _ada_mod(c_all, w_ada, b_ada):
    depth, d, n = w_ada.shape
    rows = c_all.shape[0]
    tn = n // 4
    return pl.pallas_call(
        _ada_kernel,
        grid=(depth, n // tn),
        in_specs=[pl.BlockSpec((rows, d), lambda l, j: (0, 0)),
                  pl.BlockSpec((1, d, tn), lambda l, j: (l, 0, j)),
                  pl.BlockSpec((1, 1, tn), lambda l, j: (l, 0, j))],
        out_specs=pl.BlockSpec((1, rows, tn), lambda l, j: (l, 0, j)),
        out_shape=jax.ShapeDtypeStruct((depth, rows, n), F32),
        compiler_params=_cparams("parallel", "parallel"),
        name="ada_mod",
    )(c_all, w_ada, b_ada.reshape(depth, 1, n))


def _inproj_kernel(*refs, has_ffn, col_splits):
    if has_ffn:
        h_ref, f_ref, pmod_ref, mod_ref, gain_ref, w_ref, res_ref, *outs = refs
        x = h_ref[0] + pmod_ref[0, 0, 5:6, :] * f_ref[0]
        res_ref[0] = x
    else:
        h_ref, mod_ref, gain_ref, w_ref, *outs = refs
        x = h_ref[0]
    hb = _modnorm(x, gain_ref[...], mod_ref[0, 0, 1:2, :], mod_ref[0, 0, 0:1, :]).astype(BF16)
    off = 0
    for o_ref, width in zip(outs, col_splits):
        o_ref[0] = jnp.dot(hb, w_ref[:, off:off + width], preferred_element_type=F32)
        off += width


def _in_proj(h, ffn, pmod, mod, gain, w, col_splits, n_ctx):
    b, l, d = h.shape
    nt = l // TOK_TILE
    nct = n_ctx // TOK_TILE
    tok = pl.BlockSpec((1, TOK_TILE, d), lambda i, t: (i, t, 0))
    modspec = pl.BlockSpec((1, 1, 6, d), lambda i, t: (i, (t >= nct).astype(jnp.int32), 0, 0))
    has_ffn = ffn is not None
    in_specs = [tok] + ([tok, modspec] if has_ffn else []) + [
        modspec, pl.BlockSpec((1, d), lambda i, t: (0, 0)), pl.BlockSpec(w.shape, lambda i, t: (0, 0))]
    out_shape = [jax.ShapeDtypeStruct((b, l, wd), F32) for wd in col_splits]
    out_specs = [pl.BlockSpec((1, TOK_TILE, wd), lambda i, t: (i, t, 0)) for wd in col_splits]
    if has_ffn:
        out_shape = [jax.ShapeDtypeStruct((b, l, d), F32)] + out_shape
        out_specs = [tok] + out_specs
    args = (h, ffn, pmod, mod, gain, w) if has_ffn else (h, mod, gain, w)
    return pl.pallas_call(
        functools.partial(_inproj_kernel, has_ffn=has_ffn, col_splits=col_splits),
        grid=(b, nt), in_specs=in_specs, out_specs=out_specs, out_shape=out_shape,
        compiler_params=_cparams("parallel", "parallel"), name="in_proj",
    )(*args)


def _pool_kernel(a_ref, wbd_ref, scale_ref, o_ref, pad_ref, *, segments):
    c = a_ref.shape[-1]
    grp = lax.broadcasted_iota(jnp.int32, (1, c), 1) // (c // len(POOL_WINDOWS))
    zeros = jnp.zeros((SUBLANE, c), F32)
    for off, n in segments:
        pad_ref[0:SUBLANE, :] = zeros
        pad_ref[SUBLANE:SUBLANE + n, :] = a_ref[0, off:off + n, :]
        pad_ref[SUBLANE + n:2 * SUBLANE + n, :] = zeros

        def chunk(ci, carry, off=off, n=n):
            r0 = pl.multiple_of(ci * ROW_CHUNK, ROW_CHUNK)
            win = pad_ref[pl.ds(r0, ROW_CHUNK + 2 * SUBLANE), :]
            sh = lambda s: win[SUBLANE + s:SUBLANE + s + ROW_CHUNK, :]
            a0 = sh(0)
            s2 = sh(-1) + a0
            s4 = s2 + sh(-2) + sh(1)
            s8 = s4 + sh(-4) + sh(-3) + sh(2) + sh(3)
            s16 = s8 + (sh(-8) + sh(-7) + sh(-6) + sh(-5)) + (sh(4) + sh(5) + sh(6) + sh(7))
            t = r0 + lax.broadcasted_iota(jnp.int32, (ROW_CHUNK, 1), 0)

            def mean(s, w):
                cnt = jnp.minimum(t + (w - w // 2), n) - jnp.maximum(t - w // 2, 0)
                return s / cnt.astype(F32)

            pooled = jnp.where(grp == 0, mean(s2, 2), jnp.where(grp == 1, mean(s4, 4),
                               jnp.where(grp == 2, mean(s8, 8), mean(s16, 16)))) - a0
            o_ref[0, pl.ds(off + r0, ROW_CHUNK), :] = _dot_hi(pooled, wbd_ref[...]) * scale_ref[...]
            return carry

        lax.fori_loop(0, n // ROW_CHUNK, chunk, 0)


def _pool(a, wbd, scale, n_ctx):
    b, l, c = a.shape
    segments = ((0, n_ctx), (n_ctx, l - n_ctx))
    return pl.pallas_call(
        functools.partial(_pool_kernel, segments=segments),
        grid=(b,),
        in_specs=[pl.BlockSpec((1, l, c), lambda i: (i, 0, 0)),
                  pl.BlockSpec((c, c), lambda i: (0, 0)),
                  pl.BlockSpec((1, c), lambda i: (0, 0))],
        out_specs=pl.BlockSpec((1, l, c), lambda i: (i, 0, 0)),
        out_shape=jax.ShapeDtypeStruct((b, l, c), F32),
        scratch_shapes=[pltpu.VMEM((l - n_ctx + 2 * SUBLANE, c), F32)],
        compiler_params=_cparams("parallel"), name="pool",
    )(a, wbd, scale)


def _gdn_kernel(q_ref, k_ref, v_ref, z_ref, ba_ref, cwq_ref, cwk_ref, cwv_ref, gp_ref, gain_ref, o_ref,
                pad_ref, yq_ref, yk_ref, yv_ref, bg_ref, of_ref, ob_ref, s_ref, *, n_ctx):
    l = q_ref.shape[1]
    n_x = l - n_ctx
    hd = HEAD_DIM
    lane = lax.broadcasted_iota(jnp.int32, (1, LANE), 1)
    lo = lane < hd
    zeros = jnp.zeros((SUBLANE, LANE), F32)
    x_off = 2 * SUBLANE + n_ctx

    def head_scale(ss):
        s_lo = jnp.sum(jnp.where(lo, ss, 0.0), axis=-1, keepdims=True)
        s_hi = jnp.sum(jnp.where(lo, 0.0, ss), axis=-1, keepdims=True)
        return jnp.where(lo, s_lo, s_hi)

    def conv(u_ref, cw_ref, y_ref, post):
        pad_ref[0:SUBLANE, :] = zeros
        pad_ref[SUBLANE:SUBLANE + n_ctx, :] = u_ref[0, 0:n_ctx, :]
        pad_ref[SUBLANE + n_ctx:x_off, :] = zeros
        pad_ref[x_off:x_off + n_x, :] = u_ref[0, n_ctx:l, :]
        pad_ref[x_off + n_x:x_off + n_x + SUBLANE, :] = zeros
        cw = cw_ref[...]
        for poff, yoff, n in ((SUBLANE, 0, n_ctx), (x_off, n_ctx, n_x)):
            def chunk(ci, carry, poff=poff, yoff=yoff):
                r0 = pl.multiple_of(ci * ROW_CHUNK, ROW_CHUNK)
                win = pad_ref[pl.ds(poff - SUBLANE + r0, ROW_CHUNK + 2 * SUBLANE), :]
                acc = win[SUBLANE - 2:SUBLANE - 2 + ROW_CHUNK, :] * cw[0:1, :]
                for j in range(1, CONV_K):
                    acc = acc + win[SUBLANE - 2 + j:SUBLANE - 2 + j + ROW_CHUNK, :] * cw[j:j + 1, :]
                y_ref[pl.ds(yoff + r0, ROW_CHUNK), :] = post(_silu(acc))
                return carry
            lax.fori_loop(0, n // ROW_CHUNK, chunk, 0)

    l2 = lambda y: y * lax.rsqrt(head_scale(y * y) + EPS)
    conv(q_ref, cwq_ref, yq_ref, lambda y: l2(y) * (hd ** -0.5))
    conv(k_ref, cwk_ref, yk_ref, l2)
    conv(v_ref, cwv_ref, yv_ref, lambda y: y)

    ba = ba_ref[0]
    g = -jnp.exp(gp_ref[0, 0:1, :]) * _softplus(ba + gp_ref[0, 1:2, :])
    bg_ref[...] = jnp.where(lane < 4, jax.nn.sigmoid(ba), g)

    ck = GDN_CHUNK
    rr = lax.broadcasted_iota(jnp.int32, (ck, ck), 0)
    cc = lax.broadcasted_iota(jnp.int32, (ck, ck), 1)
    eye = (rr == cc).astype(F32)
    n_chunks = l // ck
    nc_ctx = n_ctx // ck
    s_ref[...] = jnp.zeros(s_ref.shape, F32)

    def body(i, carry):
        cb = jnp.where(i < nc_ctx, nc_ctx - 1 - i, n_chunks - 1 - (i - nc_ctx))
        for d, cidx in ((0, i), (1, cb)):
            mask = (rr >= cc) if d == 0 else (rr <= cc)
            smask = (rr > cc) if d == 0 else (rr < cc)
            r0 = pl.multiple_of(cidx * ck, ck)
            qc = yq_ref[pl.ds(r0, ck), :]
            kc = yk_ref[pl.ds(r0, ck), :]
            vc = yv_ref[pl.ds(r0, ck), :]
            bgc = bg_ref[pl.ds(r0, ck), :]
            gcum = _dot_hi(mask.astype(F32), bgc)
            gcum_t = gcum.T
            last = ck - 1 if d == 0 else 0
            outs = []
            for j in range(2):
                lg, lb = 4 + 2 * d + j, 2 * d + j
                gcol = gcum[:, lg:lg + 1]
                grow = gcum_t[lg:lg + 1, :]
                bcol = bgc[:, lb:lb + 1]
                qh = qc[:, j * hd:(j + 1) * hd]
                kh = kc[:, j * hd:(j + 1) * hd]
                vh = vc[:, j * hd:(j + 1) * hd]
                decay = jnp.where(mask, jnp.exp(jnp.where(mask, gcol - grow, 0.0)), 0.0)
                kk = _dot_nt(kh, kh)
                qk = _dot_nt(qh, kh)
                xp = jnp.where(smask, -(bcol * kk * decay), 0.0)
                inv = eye + xp
                for _ in range(5):
                    xp = _dot(xp, xp)
                    inv = inv + _dot(inv, xp)
                eg = jnp.exp(gcol)
                kb = kh * bcol
                sol = _dot(inv, jnp.concatenate([vh * bcol, kb * eg], axis=1))
                u, w = sol[:, :hd], sol[:, hd:]
                glast = gcol[last:last + 1, :]
                state = s_ref[2 * d + j]
                vnew = u - _dot(w, state)
                outs.append(_dot(qh * eg, state) + _dot(qk * decay, vnew))
                s_ref[2 * d + j] = state * jnp.exp(glast) + _dot_tn(kh * jnp.exp(glast - gcol), vnew)
            o_dst = of_ref if d == 0 else ob_ref
            o_dst[pl.ds(r0, ck), :] = jnp.concatenate(outs, axis=1)
        return carry

    lax.fori_loop(0, n_chunks, body, 0)

    def finish(ci, carry):
        r0 = pl.multiple_of(ci * ROW_CHUNK, ROW_CHUNK)
        o = of_ref[pl.ds(r0, ROW_CHUNK), :] + ob_ref[pl.ds(r0, ROW_CHUNK), :]
        o = o * lax.rsqrt(head_scale(o * o) * (1.0 / hd) + EPS) * gain_ref[...]
        o_ref[0, pl.ds(r0, ROW_CHUNK), :] = o * _silu(z_ref[0, pl.ds(r0, ROW_CHUNK), :])
        return carry

    lax.fori_loop(0, l // ROW_CHUNK, finish, 0)


def _gdn(qkv, z, ba, conv_w, gparams, gain2, n_ctx):
    b, l, w3 = qkv.shape
    npair = w3 // (3 * LANE)
    seq = lambda off: pl.BlockSpec((1, l, LANE), lambda i, p: (i, 0, p + off))
    cw = lambda off: pl.BlockSpec((CONV_K, LANE), lambda i, p: (0, p + off))
    return pl.pallas_call(
        functools.partial(_gdn_kernel, n_ctx=n_ctx),
        grid=(b, npair),
        in_specs=[seq(0), seq(npair), seq(2 * npair), seq(0), seq(0),
                  cw(0), cw(npair), cw(2 * npair),
                  pl.BlockSpec((1, SUBLANE, LANE), lambda i, p: (p, 0, 0)),
                  pl.BlockSpec((1, LANE), lambda i, p: (0, 0))],
        out_specs=seq(0),
        out_shape=jax.ShapeDtypeStruct((b, l, npair * LANE), F32),
        scratch_shapes=[pltpu.VMEM((l + 3 * SUBLANE, LANE), F32)] + [pltpu.VMEM((l, LANE), F32)] * 6
        + [pltpu.VMEM((4, HEAD_DIM, HEAD_DIM), F32)],
        compiler_params=_cparams("parallel", "parallel"), name="gdn",
    )(qkv, qkv, qkv, z, ba, conv_w, conv_w, conv_w, gparams, gain2)


def _rope(x, cos, sin):
    w = x.shape[-1]
    reps = w // cos.shape[-1]
    if reps > 1:
        cos = jnp.concatenate([cos] * reps, axis=1)
        sin = jnp.concatenate([sin] * reps, axis=1)
    first = (lax.broadcasted_iota(jnp.int32, (1, w), 1) % (2 * ROPE_PAIRS)) < ROPE_PAIRS
    partner = jnp.where(first, pltpu.roll(x, w - ROPE_PAIRS, 1), pltpu.roll(x, ROPE_PAIRS, 1))
    return x * cos + partner * sin


def _attend(q, k, v, valid, sink_col):
    s = _dot_nt(q, k)
    if valid is not None:
        s = jnp.where(valid, s, NEG_INF)
    m = jnp.maximum(jnp.max(s, axis=-1, keepdims=True), sink_col)
    p = jnp.exp(s - m)
    den = jnp.sum(p, axis=-1, keepdims=True) + jnp.exp(sink_col - m)
    return _dot(p, v) / den


def _gqa(q, k_all, v_all, valid, sink_ref, o_ref):
    nq = q.shape[0]
    hd = HEAD_DIM
    group = q.shape[1] // hd // KV_HEADS
    outs = []
    for kvh in range(KV_HEADS):
        heads = [kvh * group + g for g in range(group)]
        qg = jnp.concatenate([q[:, h * hd:(h + 1) * hd] for h in heads], axis=0)
        sink_col = jnp.concatenate(
            [jnp.broadcast_to(sink_ref[0:1, h:h + 1], (nq, 1)) for h in heads], axis=0)
        o = _attend(qg, k_all[:, kvh * hd:(kvh + 1) * hd], v_all[:, kvh * hd:(kvh + 1) * hd],
                    valid, sink_col)
        outs += [o[g * nq:(g + 1) * nq, :] for g in range(group)]
    o_ref[0] = jnp.concatenate(outs, axis=1)


def _attn_kernel(q_ref, kp_ref, ko_ref, kn_ref, vp_ref, vo_ref, vn_ref, kc_ref, vc_ref,
                 cp_ref, sp_ref, co_ref, so_ref, cn_ref, sn_ref, sink_ref, o_ref, *, seq):
    i = pl.program_id(1)
    blk = ATTN_BLOCK
    n_ctx = kc_ref.shape[1]
    q = _rope(q_ref[0], co_ref[...], so_ref[...]) * (HEAD_DIM ** -0.5)
    k_all = jnp.concatenate([_rope(kp_ref[0], cp_ref[...], sp_ref[...]),
                             _rope(ko_ref[0], co_ref[...], so_ref[...]),
                             _rope(kn_ref[0], cn_ref[...], sn_ref[...]), kc_ref[0]], axis=0)
    v_all = jnp.concatenate([vp_ref[0], vo_ref[0], vn_ref[0], vc_ref[0]], axis=0)
    group = q.shape[1] // HEAD_DIM // KV_HEADS
    nk = 3 * blk + n_ctx
    qo = lax.broadcasted_iota(jnp.int32, (group * blk, 1), 0) % blk
    ko = lax.broadcasted_iota(jnp.int32, (1, nk), 1)
    kpos = (i - 1) * blk + ko
    valid = ((jnp.abs(ko - blk - qo) <= WINDOW) & (kpos >= 0) & (kpos < seq)) | (ko >= 3 * blk)
    _gqa(q, k_all, v_all, valid, sink_ref, o_ref)


def _attn_x(aq, ak, av, cos_t, sin_t, sink_row, n_ctx):
    b, l, wq = aq.shape
    wk = ak.shape[-1]
    blk = ATTN_BLOCK
    seq = l - n_ctx
    nb = seq // blk
    cb = n_ctx // blk
    kv = lambda f: pl.BlockSpec((1, blk, wk), lambda i, t: (i, f(t) + cb, 0))
    tab = lambda f: pl.BlockSpec((blk, LANE), lambda i, t: (f(t), 0))
    prev = lambda t: jnp.maximum(t - 1, 0)
    own = lambda t: t
    nxt = lambda t: jnp.minimum(t + 1, nb - 1)
    ctx = pl.BlockSpec((1, n_ctx, wk), lambda i, t: (i, 0, 0))
    return pl.pallas_call(
        functools.partial(_attn_kernel, seq=seq),
        grid=(b, nb),
        in_specs=[pl.BlockSpec((1, blk, wq), lambda i, t: (i, t + cb, 0)),
                  kv(prev), kv(own), kv(nxt), kv(prev), kv(own), kv(nxt), ctx, ctx,
                  tab(prev), tab(prev), tab(own), tab(own), tab(nxt), tab(nxt),
                  pl.BlockSpec((1, LANE), lambda i, t: (0, 0))],
        out_specs=pl.BlockSpec((1, blk, wq), lambda i, t: (i, t, 0)),
        out_shape=jax.ShapeDtypeStruct((b, seq, wq), F32),
        compiler_params=_cparams("parallel", "parallel"), name="attn_x",
    )(aq, ak, ak, ak, av, av, av, ak, av, cos_t, sin_t, cos_t, sin_t, cos_t, sin_t, sink_row)


def _attn_ctx_kernel(q_ref, k_ref, v_ref, sink_ref, o_ref):
    _gqa(q_ref[0] * (HEAD_DIM ** -0.5), k_ref[0], v_ref[0], None, sink_ref, o_ref)


def _attn_ctx(aq, ak, av, sink_row, n_ctx):
    b, _, wq = aq.shape
    wk = ak.shape[-1]
    spec = lambda w: pl.BlockSpec((1, n_ctx, w), lambda i: (i, 0, 0))
    return pl.pallas_call(
        _attn_ctx_kernel, grid=(b,),
        in_specs=[spec(wq), spec(wk), spec(wk), pl.BlockSpec((1, LANE), lambda i: (0, 0))],
        out_specs=spec(wq), out_shape=jax.ShapeDtypeStruct((b, n_ctx, wq), F32),
        compiler_params=_cparams("parallel"), name="attn_ctx",
    )(aq, ak, av, sink_row)


def _outproj_kernel(x_ref, ya_ref, yb_ref, yc_ref, w_ref, mod_ref, gain_ref, xo_ref, hx_ref):
    wa, wb = ya_ref.shape[-1], yb_ref.shape[-1]
    acc = (jnp.dot(ya_ref[0].astype(BF16), w_ref[0:wa, :], preferred_element_type=F32)
           + jnp.dot(yb_ref[0].astype(BF16), w_ref[wa:wa + wb, :], preferred_element_type=F32)
           + jnp.dot(yc_ref[0].astype(BF16), w_ref[wa + wb:, :], preferred_element_type=F32))
    x = x_ref[0] + mod_ref[0, 0, 2:3, :] * acc
    xo_ref[0] = x
    hx_ref[0] = _modnorm(x, gain_ref[...], mod_ref[0, 0, 4:5, :], mod_ref[0, 0, 3:4, :]).astype(BF16)


def _out_proj(h, ya, yb, yc, w, mod, gain, n_ctx, skip_ctx):
    b, l, d = h.shape
    nct = n_ctx // TOK_TILE
    off = nct if skip_ctx else 0
    nt = l // TOK_TILE - off
    full = lambda wd: pl.BlockSpec((1, TOK_TILE, wd), lambda i, t: (i, t + off, 0))
    own = lambda wd: pl.BlockSpec((1, TOK_TILE, wd), lambda i, t: (i, t, 0))
    return pl.pallas_call(
        _outproj_kernel, grid=(b, nt),
        in_specs=[full(d), full(ya.shape[-1]), full(yb.shape[-1]), own(yc.shape[-1]),
                  pl.BlockSpec(w.shape, lambda i, t: (0, 0)),
                  pl.BlockSpec((1, 1, 6, d), lambda i, t: (i, (t + off >= nct).astype(jnp.int32), 0, 0)),
                  pl.BlockSpec((1, d), lambda i, t: (0, 0))],
        out_specs=[own(d), own(d)],
        out_shape=[jax.ShapeDtypeStruct((b, nt * TOK_TILE, d), F32),
                   jax.ShapeDtypeStruct((b, nt * TOK_TILE, d), BF16)],
        compiler_params=_cparams("parallel", "parallel"), name="out_proj",
    )(h, ya, yb, yc, w, mod, gain)


def _route_kernel(h_ref, wq_ref, keys_ref, st_ref, stats_ref, qt_ref, top_ref, cand_ref):
    tm = h_ref.shape[0]
    k = PEER_TOPK
    neg = jnp.float32(-jnp.inf)
    qt_ref[...] = _dot_nt(wq_ref[...], h_ref[...]).astype(BF16)
    for hp in range(2 * PEER_HEADS):
        st = jnp.dot(keys_ref[hp], qt_ref[hp * N_KEYS:(hp + 1) * N_KEYS, :], preferred_element_type=F32)
        st_ref[hp * N_KEYS:(hp + 1) * N_KEYS, :] = st
        cur = st
        for r in range(k):
            m = jnp.max(cur, axis=0, keepdims=True)
            top_ref[hp * k + r:hp * k + r + 1, :] = m
            cur = jnp.where(cur == m, neg, cur)
    cand_ref[_N_CAND:_CAND_PAD, :] = jnp.full((_CAND_PAD - _N_CAND, tm), neg, F32)
    for h in range(PEER_HEADS):
        v1 = top_ref[2 * h * k:(2 * h + 1) * k, :]
        v2 = top_ref[(2 * h + 1) * k:(2 * h + 2) * k, :]
        row = 0
        for a, nb in enumerate(_CAND_ROWS):
            cand_ref[row:row + nb, :] = v1[a:a + 1, :] + v2[0:nb, :]
            row += nb
        cur = cand_ref[...]
        cmax = thr = zsum = None
        for r in range(k):
            m = jnp.max(cur, axis=0, keepdims=True)
            if r == 0:
                cmax, zsum = m, jnp.ones_like(m)
            else:
                zsum = zsum + jnp.exp(m - cmax)
            thr = m
            cur = jnp.where(cur == m, neg, cur)
        stats_ref[4 * h:4 * h + 4, :] = jnp.concatenate([thr, v1[0:1, :], v2[0:1, :], 1.0 / zsum], axis=0)


def _peer_route(hx, wq_t, keys):
    t, d = hx.shape
    nq = wq_t.shape[0]
    tm = PEER_ROUTE_TILE
    return pl.pallas_call(
        _route_kernel, grid=(t // tm,),
        in_specs=[pl.BlockSpec((tm, d), lambda i: (i, 0)),
                  pl.BlockSpec(wq_t.shape, lambda i: (0, 0)),
                  pl.BlockSpec(keys.shape, lambda i: (0, 0, 0))],
        out_specs=[pl.BlockSpec((nq, tm), lambda i: (0, i)),
                   pl.BlockSpec((4 * PEER_HEADS, tm), lambda i: (0, i))],
        out_shape=[jax.ShapeDtypeStruct((nq, t), F32), jax.ShapeDtypeStruct((4 * PEER_HEADS, t), F32)],
        scratch_shapes=[pltpu.VMEM((nq, tm), BF16), pltpu.VMEM((2 * PEER_HEADS * PEER_TOPK, tm), F32),
                        pltpu.VMEM((_CAND_PAD, tm), F32)],
        compiler_params=_cparams("parallel"), name="peer_route",
    )(hx, wq_t, keys)


def _peer_kernel(h_ref, st_ref, stats_ref, u_ref, vt_ref, o_ref, acc_ref, e2_ref, g_ref):
    e = pl.program_id(1)
    eb = u_ref.shape[0]
    nk = N_KEYS

    @pl.when(e == 0)
    def _():
        acc_ref[...] = jnp.zeros(acc_ref.shape, F32)
        for h in range(PEER_HEADS):
            s2 = st_ref[(2 * h + 1) * nk:(2 * h + 2) * nk, :]
            e2_ref[h * nk:(h + 1) * nk, :] = jnp.exp(s2 - stats_ref[4 * h + 2:4 * h + 3, :])

    act = lax.dot_general(u_ref[...], h_ref[...], (((1,), (1,)), ((), ())), preferred_element_type=F32)
    act = 0.5 * act * (1.0 + lax.erf(act * 0.7071067811865476))
    for il in range(eb // nk):
        i1 = e * (eb // nk) + il
        w = None
        for h in range(PEER_HEADS):
            s1 = st_ref[pl.ds(2 * h * nk + i1, 1), :]
            c = s1 + st_ref[(2 * h + 1) * nk:(2 * h + 2) * nk, :]
            e1z = jnp.exp(s1 - stats_ref[4 * h + 1:4 * h + 2, :]) * stats_ref[4 * h + 3:4 * h + 4, :]
            term = jnp.where(c >= stats_ref[4 * h:4 * h + 1, :], e1z * e2_ref[h * nk:(h + 1) * nk, :], 0.0)
            w = term if w is None else w + term
        g_ref[il * nk:(il + 1) * nk, :] = (w * act[il * nk:(il + 1) * nk, :]).astype(BF16)
    acc_ref[...] += jnp.dot(vt_ref[...], g_ref[...], preferred_element_type=F32)

    @pl.when(e == pl.num_programs(1) - 1)
    def _():
        o_ref[...] = acc_ref[...].T


def _peer_dense(hx, st, stats, u, vt):
    t, d = hx.shape
    n_exp = u.shape[0]
    tm, eb = PEER_TOK_TILE, PEER_EXP_TILE
    return pl.pallas_call(
        _peer_kernel, grid=(t // tm, n_exp // eb),
        in_specs=[pl.BlockSpec((tm, d), lambda i, e: (i, 0)),
                  pl.BlockSpec((st.shape[0], tm), lambda i, e: (0, i)),
                  pl.BlockSpec((stats.shape[0], tm), lambda i, e: (0, i)),
                  pl.BlockSpec((eb, d), lambda i, e: (e, 0)),
                  pl.BlockSpec((d, eb), lambda i, e: (0, e))],
        out_specs=pl.BlockSpec((tm, d), lambda i, e: (i, 0)),
        out_shape=jax.ShapeDtypeStruct((t, d), F32),
        scratch_shapes=[pltpu.VMEM((d, tm), F32), pltpu.VMEM((PEER_HEADS * N_KEYS, tm), F32),
                        pltpu.VMEM((eb, tm), BF16)],
        compiler_params=_cparams("parallel", "arbitrary"), name="peer_dense",
    )(hx, st, stats, u, vt)


def _final_kernel(x_ref, f_ref, mod_ref, gain_ref, o_ref):
    x = x_ref[0] + mod_ref[0, 0, 5:6, :] * f_ref[0]
    o_ref[0] = x * lax.rsqrt(jnp.mean(x * x, axis=-1, keepdims=True) + EPS) * gain_ref[...]


def _final(x, ffn, mod, gain):
    b, s, d = x.shape
    tok = pl.BlockSpec((1, TOK_TILE, d), lambda i, t: (i, t, 0))
    return pl.pallas_call(
        _final_kernel, grid=(b, s // TOK_TILE),
        in_specs=[tok, tok, pl.BlockSpec((1, 1, 6, d), lambda i, t: (i, 1, 0, 0)),
                  pl.BlockSpec((1, d), lambda i, t: (0, 0))],
        out_specs=tok, out_shape=jax.ShapeDtypeStruct((b, s, d), F32),
        compiler_params=_cparams("parallel", "parallel"), name="final_norm",
    )(x, ffn, mod, gain)


def _rope_tables(seq):
    pos = jnp.arange(seq)
    rc = jnp.stack([pos // GRID_W, pos % GRID_W], axis=-1).astype(F32)
    inv = jnp.power(ROPE_BASE, -jnp.arange(ROPE_PAIRS, dtype=F32) / ROPE_PAIRS)
    ang = rc[:, :, None] * inv
    cos = jnp.concatenate([jnp.cos(ang)] * 2, axis=-1).reshape(seq, HEAD_DIM)
    sin = jnp.concatenate([-jnp.sin(ang), jnp.sin(ang)], axis=-1).reshape(seq, HEAD_DIM)
    return jnp.concatenate([cos] * 2, axis=1), jnp.concatenate([sin] * 2, axis=1)


def _pair_lanes(cols, n_heads):
    rows = cols.shape[0]
    c = cols.reshape(rows, 4, n_heads // 2, 2).transpose(0, 2, 1, 3).reshape(rows, n_heads // 2, 8)
    return jnp.pad(c, ((0, 0), (0, 0), (0, LANE - 8))).reshape(rows, n_heads // 2 * LANE)


def kernel(x, c, ctx, c_ctx, w_ada, b_ada, norm1, norm2, w_in, conv_w, a_log, dt_bias, gdn_norm, w_pool,
           pool_scale, sink, w_out, peer_wq, peer_keys, peer_u, peer_v, norm_f):
    b, seq, d = x.shape
    n_ctx = ctx.shape[1]
    depth = w_ada.shape[0]
    pool_w = d // 4
    gdn_w = 3 * d // 8
    gdn_heads = gdn_w // HEAD_DIM
    attn_w = d - pool_w - gdn_w
    kv_w = KV_HEADS * HEAD_DIM
    assert n_ctx % TOK_TILE == 0 and seq % TOK_TILE == 0 and gdn_heads % 2 == 0
    assert (b * (n_ctx + seq)) % PEER_TOK_TILE == 0 and (b * seq) % PEER_TOK_TILE == 0

    rows = -(-(b + 1) // SUBLANE) * SUBLANE
    c_all = jnp.zeros((rows, d), F32).at[:b].set(c).at[b].set(c_ctx)
    mod = _ada_mod(c_all, w_ada, b_ada)
    mod_x = mod[:, :b].reshape(depth, b, 1, 6, d)
    mod_c = jnp.broadcast_to(mod[:, b].reshape(depth, 1, 1, 6, d), (depth, b, 1, 6, d))
    mod = jnp.concatenate([mod_c, mod_x], axis=2)

    cos_t, sin_t = _rope_tables(seq)
    splits = (pool_w, 3 * gdn_w, gdn_w, gdn_heads // 2 * LANE, attn_w, kv_w, kv_w)
    o_ba = pool_w + 4 * gdn_w
    n_ba = 4 * gdn_heads

    h = jnp.concatenate([ctx, x], axis=1)
    ffn = None
    for l in range(depth):
        last = l == depth - 1
        w_l = w_in[l]
        w_big = jnp.concatenate([w_l[:, :o_ba], _pair_lanes(w_l[:, o_ba:o_ba + n_ba], gdn_heads),
                                 w_l[:, o_ba + n_ba:]], axis=1).astype(BF16)
        pmod = mod[l - 1] if l > 0 else None
        outs = _in_proj(h, ffn, pmod, mod[l], norm1[l][None], w_big, splits, n_ctx)
        if l > 0:
            h, *outs = outs
        pa, qkv, z, ba, aq, ak, av = outs

        wbd = jax.scipy.linalg.block_diag(*[w_pool[l, g] for g in range(len(POOL_WINDOWS))])
        ya = _pool(pa, wbd, pool_scale[l][None], n_ctx)

        gparams = jnp.stack([_pair_lanes(jnp.concatenate([jnp.zeros_like(a_log[l]), a_log[l]]).reshape(1, -1),
                                         gdn_heads),
                             _pair_lanes(jnp.concatenate([jnp.zeros_like(dt_bias[l]), dt_bias[l]]).reshape(1, -1),
                                         gdn_heads)], axis=1)
        gparams = gparams.reshape(2, gdn_heads // 2, LANE).transpose(1, 0, 2)
        gparams = jnp.pad(gparams, ((0, 0), (0, SUBLANE - 2), (0, 0)))
        yb = _gdn(qkv, z, ba, conv_w[l], gparams, jnp.concatenate([gdn_norm[l]] * 2)[None], n_ctx)

        sink_row = jnp.pad(sink[l], (0, LANE - sink.shape[1]))[None]
        yc = _attn_x(aq, ak, av, cos_t, sin_t, sink_row, n_ctx)
        if not last:
            yc = jnp.concatenate([_attn_ctx(aq, ak, av, sink_row, n_ctx), yc], axis=1)

        x_new, hx = _out_proj(h, ya, yb, yc, w_out[l].astype(BF16), mod[l], norm2[l][None], n_ctx,
                              skip_ctx=last)
        t = hx.shape[0] * hx.shape[1]
        hx = hx.reshape(t, d)
        keys = peer_keys[l].reshape(2 * PEER_HEADS, N_KEYS, -1).astype(BF16)
        st, stats = _peer_route(hx, peer_wq[l].T.astype(BF16), keys)
        ffn = _peer_dense(hx, st, stats, peer_u[l].astype(BF16), peer_v[l].T.astype(BF16))
        ffn = ffn.reshape(x_new.shape)
        h = x_new
    return _final(h, ffn, mod[depth - 1], norm_f[None])
```

```python
import functools

import jax
import jax.numpy as jnp
from jax import lax
from jax.experimental import pallas as pl
from jax.experimental.pallas import tpu as pltpu

F32, BF16 = jnp.float32, jnp.bfloat16
HIGHEST = lax.Precision.HIGHEST

EPS = 1e-6
NEG_INF = -1e30
HEAD_DIM = 64
GRID_W = 64
POOL_WINDOWS = (2, 4, 8, 16)
GDN_CHUNK = 64
CONV_K = 5
KV_HEADS = 2
WINDOW = 128
ATTN_BLOCK = 128
ROPE_BASE = 10000.0
ROPE_PAIRS = HEAD_DIM // 4
PEER_HEADS = 8
N_KEYS = 128
PEER_TOPK = 16

LANE = 128
SUBLANE = 8
TOK_TILE = 256
ROW_CHUNK = 256
GDN_GROUP = 4
PEER_ROUTE_TILE = 256
PEER_TOK_TILE = 512
PEER_EXP_TILE = 512
PEER_ROW_TILE = 64
PEER_SUB = 4
VMEM_LIMIT = 48 * 1024 * 1024

_CAND_ROWS = tuple(PEER_TOPK // (a + 1) for a in range(PEER_TOPK))
_N_CAND = sum(_CAND_ROWS)
_CAND_PAD = -(-_N_CAND // SUBLANE) * SUBLANE


def _cparams(*sem):
    return pltpu.CompilerParams(dimension_semantics=sem, vmem_limit_bytes=VMEM_LIMIT)


def _dot(a, b):
    return jnp.dot(a.astype(BF16), b.astype(BF16), preferred_element_type=F32)


def _dot_nt(a, b):
    return lax.dot_general(a.astype(BF16), b.astype(BF16), (((1,), (1,)), ((), ())),
                           preferred_element_type=F32)


def _dot_tn(a, b):
    return lax.dot_general(a.astype(BF16), b.astype(BF16), (((0,), (0,)), ((), ())),
                           preferred_element_type=F32)


def _dot_hi(a, b):
    return jnp.dot(a, b, precision=HIGHEST, preferred_element_type=F32)


def _silu(x):
    return x * jax.nn.sigmoid(x)


def _softplus(x):
    return jnp.maximum(x, 0.0) + jnp.log1p(jnp.exp(-jnp.abs(x)))


def _modnorm(x, gain, scale, shift):
    y = x * lax.rsqrt(jnp.mean(x * x, axis=-1, keepdims=True) + EPS) * gain
    return y * (1.0 + scale) + shift


def _ada_kernel(c_ref, w_ref, b_ref, o_ref):
    o_ref[0] = _dot_hi(_silu(c_ref[...]), w_ref[0]) + b_ref[0]


def _ada_mod(c_all, w_ada, b_ada):
    depth, d, n = w_ada.shape
    rows = c_all.shape[0]
    tn = n // 4
    return pl.pallas_call(
        _ada_kernel,
        grid=(depth, n // tn),
        in_specs=[pl.BlockSpec((rows, d), lambda l, j: (0, 0)),
                  pl.BlockSpec((1, d, tn), lambda l, j: (l, 0, j)),
                  pl.BlockSpec((1, 1, tn), lambda l, j: (l, 0, j))],
        out_specs=pl.BlockSpec((1, rows, tn), lambda l, j: (l, 0, j)),
        out_shape=jax.ShapeDtypeStruct((depth, rows, n), F32),
        compiler_params=_cparams("parallel", "parallel"),
        name="ada_mod",
    )(c_all, w_ada, b_ada.reshape(depth, 1, n))


def _inproj_kernel(*refs, has_ffn, col_splits):
    if has_ffn:
        h_ref, f_ref, pmod_ref, mod_ref, gain_ref, w_ref, res_ref, *outs = refs
        x = h_ref[0] + pmod_ref[0, 0, 5:6, :] * f_ref[0]
        res_ref[0] = x
    else:
        h_ref, mod_ref, gain_ref, w_ref, *outs = refs
        x = h_ref[0]
    hb = _modnorm(x, gain_ref[...], mod_ref[0, 0, 1:2, :], mod_ref[0, 0, 0:1, :]).astype(BF16)
    off = 0
    for o_ref, width in zip(outs, col_splits):
        o_ref[0] = jnp.dot(hb, w_ref[:, off:off + width], preferred_element_type=F32)
        off += width


def _in_proj(h, ffn, pmod, mod, gain, w, col_splits, n_ctx):
    b, l, d = h.shape
    nt = l // TOK_TILE
    nct = n_ctx // TOK_TILE
    tok = pl.BlockSpec((1, TOK_TILE, d), lambda i, t: (i, t, 0))
    modspec = pl.BlockSpec((1, 1, 6, d), lambda i, t: (i, (t >= nct).astype(jnp.int32), 0, 0))
    has_ffn = ffn is not None
    in_specs = [tok] + ([tok, modspec] if has_ffn else []) + [
        modspec, pl.BlockSpec((1, d), lambda i, t: (0, 0)), pl.BlockSpec(w.shape, lambda i, t: (0, 0))]
    out_shape = [jax.ShapeDtypeStruct((b, l, wd), F32) for wd in col_splits]
    out_specs = [pl.BlockSpec((1, TOK_TILE, wd), lambda i, t: (i, t, 0)) for wd in col_splits]
    if has_ffn:
        out_shape = [jax.ShapeDtypeStruct((b, l, d), F32)] + out_shape
        out_specs = [tok] + out_specs
    args = (h, ffn, pmod, mod, gain, w) if has_ffn else (h, mod, gain, w)
    return pl.pallas_call(
        functools.partial(_inproj_kernel, has_ffn=has_ffn, col_splits=col_splits),
        grid=(b, nt), in_specs=in_specs, out_specs=out_specs, out_shape=out_shape,
        compiler_params=_cparams("parallel", "parallel"), name="in_proj",
    )(*args)


def _pool_kernel(a_ref, wbd_ref, scale_ref, o_ref, pad_ref, *, segments):
    c = a_ref.shape[-1]
    grp = lax.broadcasted_iota(jnp.int32, (1, c), 1) // (c // len(POOL_WINDOWS))
    zeros = jnp.zeros((SUBLANE, c), F32)
    for off, n in segments:
        pad_ref[0:SUBLANE, :] = zeros
        pad_ref[SUBLANE:SUBLANE + n, :] = a_ref[0, off:off + n, :]
        pad_ref[SUBLANE + n:2 * SUBLANE + n, :] = zeros

        def chunk(ci, carry, off=off, n=n):
            r0 = pl.multiple_of(ci * ROW_CHUNK, ROW_CHUNK)
            win = pad_ref[pl.ds(r0, ROW_CHUNK + 2 * SUBLANE), :]
            sh = lambda s: win[SUBLANE + s:SUBLANE + s + ROW_CHUNK, :]
            a0 = sh(0)
            s2 = sh(-1) + a0
            s4 = s2 + sh(-2) + sh(1)
            s8 = s4 + sh(-4) + sh(-3) + sh(2) + sh(3)
            s16 = s8 + (sh(-8) + sh(-7) + sh(-6) + sh(-5)) + (sh(4) + sh(5) + sh(6) + sh(7))
            t = r0 + lax.broadcasted_iota(jnp.int32, (ROW_CHUNK, 1), 0)

            def mean(s, w):
                cnt = jnp.minimum(t + (w - w // 2), n) - jnp.maximum(t - w // 2, 0)
                return s / cnt.astype(F32)

            pooled = jnp.where(grp == 0, mean(s2, 2), jnp.where(grp == 1, mean(s4, 4),
                               jnp.where(grp == 2, mean(s8, 8), mean(s16, 16)))) - a0
            o_ref[0, pl.ds(off + r0, ROW_CHUNK), :] = _dot_hi(pooled, wbd_ref[...]) * scale_ref[...]
            return carry

        lax.fori_loop(0, n // ROW_CHUNK, chunk, 0)


def _pool(a, wbd, scale, n_ctx):
    b, l, c = a.shape
    segments = ((0, n_ctx), (n_ctx, l - n_ctx))
    return pl.pallas_call(
        functools.partial(_pool_kernel, segments=segments),
        grid=(b,),
        in_specs=[pl.BlockSpec((1, l, c), lambda i: (i, 0, 0)),
                  pl.BlockSpec((c, c), lambda i: (0, 0)),
                  pl.BlockSpec((1, c), lambda i: (0, 0))],
        out_specs=pl.BlockSpec((1, l, c), lambda i: (i, 0, 0)),
        out_shape=jax.ShapeDtypeStruct((b, l, c), F32),
        scratch_shapes=[pltpu.VMEM((l - n_ctx + 2 * SUBLANE, c), F32)],
        compiler_params=_cparams("parallel"), name="pool",
    )(a, wbd, scale)


def _gdn_kernel(q_ref, k_ref, v_ref, z_ref, ba_ref, cwq_ref, cwk_ref, cwv_ref, gp_ref, gain_ref, o_ref,
                pad_ref, yq_ref, yk_ref, yv_ref, bg_ref, of_ref, ob_ref, s_ref,
                qe_ref, mn_ref, oc_ref, nn_ref, egl_ref, *, n_ctx):
    l = q_ref.shape[1]
    n_x = l - n_ctx
    hd = HEAD_DIM
    lane = lax.broadcasted_iota(jnp.int32, (1, LANE), 1)
    lo = lane < hd
    zeros = jnp.zeros((SUBLANE, LANE), F32)
    x_off = 2 * SUBLANE + n_ctx

    def head_scale(ss):
        s_lo = jnp.sum(jnp.where(lo, ss, 0.0), axis=-1, keepdims=True)
        s_hi = jnp.sum(jnp.where(lo, 0.0, ss), axis=-1, keepdims=True)
        return jnp.where(lo, s_lo, s_hi)

    def conv(u_ref, cw_ref, y_ref, post):
        pad_ref[0:SUBLANE, :] = zeros
        pad_ref[SUBLANE:SUBLANE + n_ctx, :] = u_ref[0, 0:n_ctx, :]
        pad_ref[SUBLANE + n_ctx:x_off, :] = zeros
        pad_ref[x_off:x_off + n_x, :] = u_ref[0, n_ctx:l, :]
        pad_ref[x_off + n_x:x_off + n_x + SUBLANE, :] = zeros
        cw = cw_ref[...]
        for poff, yoff, n in ((SUBLANE, 0, n_ctx), (x_off, n_ctx, n_x)):
            def chunk(ci, carry, poff=poff, yoff=yoff):
                r0 = pl.multiple_of(ci * ROW_CHUNK, ROW_CHUNK)
                win = pad_ref[pl.ds(poff - SUBLANE + r0, ROW_CHUNK + 2 * SUBLANE), :]
                acc = win[SUBLANE - 2:SUBLANE - 2 + ROW_CHUNK, :] * cw[0:1, :]
                for j in range(1, CONV_K):
                    acc = acc + win[SUBLANE - 2 + j:SUBLANE - 2 + j + ROW_CHUNK, :] * cw[j:j + 1, :]
                y_ref[pl.ds(yoff + r0, ROW_CHUNK), :] = post(_silu(acc))
                return carry
            lax.fori_loop(0, n // ROW_CHUNK, chunk, 0)

    l2 = lambda y: y * lax.rsqrt(head_scale(y * y) + EPS)
    conv(q_ref, cwq_ref, yq_ref, lambda y: l2(y) * (hd ** -0.5))
    conv(k_ref, cwk_ref, yk_ref, l2)
    conv(v_ref, cwv_ref, yv_ref, lambda y: y)

    ba = ba_ref[0]
    g = -jnp.exp(gp_ref[0, 0:1, :]) * _softplus(ba + gp_ref[0, 1:2, :])
    bg_ref[...] = jnp.where(lane < 4, jax.nn.sigmoid(ba), g)

    ck = GDN_CHUNK
    rr = lax.broadcasted_iota(jnp.int32, (ck, ck), 0)
    cc = lax.broadcasted_iota(jnp.int32, (ck, ck), 1)
    eye = (rr == cc).astype(F32)
    tril = (rr >= cc).astype(F32)
    n_chunks = l // ck
    nc_ctx = n_ctx // ck

    def phase1(ci, carry):
        probs = []
        for gi in range(GDN_GROUP):
            c = ci * GDN_GROUP + gi
            r0 = pl.multiple_of(c * ck, ck)
            qc = yq_ref[pl.ds(r0, ck), :]
            kc = yk_ref[pl.ds(r0, ck), :]
            vc = yv_ref[pl.ds(r0, ck), :]
            bgc = bg_ref[pl.ds(r0, ck), :]
            gcf = _dot_hi(tril, bgc)
            gcb = gcf[ck - 1:ck, :] - gcf + bgc
            gct = (gcf.T, gcb.T)
            for j in range(2):
                qh = qc[:, j * hd:(j + 1) * hd]
                kh = kc[:, j * hd:(j + 1) * hd]
                vh = vc[:, j * hd:(j + 1) * hd]
                kk = _dot_nt(kh, kh)
                qk = _dot_nt(qh, kh)
                for d in range(2):
                    mask = (rr >= cc) if d == 0 else (rr <= cc)
                    smask = (rr > cc) if d == 0 else (rr < cc)
                    lg, lb = 4 + 2 * d + j, 2 * d + j
                    gcol = (gcf, gcb)[d][:, lg:lg + 1]
                    grow = gct[d][lg:lg + 1, :]
                    bcol = bgc[:, lb:lb + 1]
                    decay = jnp.where(mask, jnp.exp(jnp.where(mask, gcol - grow, 0.0)), 0.0)
                    xp = jnp.where(smask, -(bcol * kk * decay), 0.0)
                    eg = jnp.exp(gcol)
                    last = ck - 1 if d == 0 else 0
                    glast = gcol[last:last + 1, :]
                    probs.append(dict(
                        idx=c * 4 + 2 * d + j, xp=xp, inv=eye + xp, attn=qk * decay, qd=qh * eg,
                        rhs=jnp.concatenate([vh * bcol, kh * bcol * eg], axis=1),
                        kd=kh * jnp.exp(glast - gcol), egl=jnp.exp(glast)))
        for _ in range(5):
            for p in probs:
                p["xp"] = _dot(p["xp"], p["xp"])
            for p in probs:
                p["inv"] = p["inv"] + _dot(p["inv"], p["xp"])
        for p in probs:
            p["sol"] = _dot(p["inv"], p["rhs"])
        for p in probs:
            p["as"] = _dot(p["attn"], p["sol"])
        for p in probs:
            p["ks"] = _dot_tn(p["kd"], p["sol"])
        for p in probs:
            i = p["idx"]
            qe_ref[i] = (p["qd"] - p["as"][:, hd:]).astype(BF16)
            oc_ref[i] = p["as"][:, :hd]
            mn_ref[i] = p["ks"][:, hd:].astype(BF16)
            nn_ref[i] = p["ks"][:, :hd]
            egl_ref[i] = jnp.broadcast_to(p["egl"], (ck, hd))
        return carry

    lax.fori_loop(0, n_chunks // GDN_GROUP, phase1, 0)

    s_ref[...] = jnp.zeros(s_ref.shape, F32)

    def phase2(i, carry):
        cb = jnp.where(i < nc_ctx, nc_ctx - 1 - i, n_chunks - 1 - (i - nc_ctx))
        work = [(d, j, cidx * 4 + 2 * d + j) for d, cidx in ((0, i), (1, cb)) for j in range(2)]
        states = [s_ref[2 * d + j] for d, j, _ in work]
        outs = [_dot(qe_ref[idx], s) + oc_ref[idx] for (_, _, idx), s in zip(work, states)]
        upd = [_dot(mn_ref[idx], s) for (_, _, idx), s in zip(work, states)]
        for (d, j, idx), s, m in zip(work, states, upd):
            s_ref[2 * d + j] = egl_ref[idx] * s - m + nn_ref[idx]
        of_ref[pl.ds(pl.multiple_of(i * ck, ck), ck), :] = jnp.concatenate(outs[0:2], axis=1)
        ob_ref[pl.ds(pl.multiple_of(cb * ck, ck), ck), :] = jnp.concatenate(outs[2:4], axis=1)
        return carry

    lax.fori_loop(0, n_chunks, phase2, 0)

    def finish(ci, carry):
        r0 = pl.multiple_of(ci * ROW_CHUNK, ROW_CHUNK)
        o = of_ref[pl.ds(r0, ROW_CHUNK), :] + ob_ref[pl.ds(r0, ROW_CHUNK), :]
        o = o * lax.rsqrt(head_scale(o * o) * (1.0 / hd) + EPS) * gain_ref[...]
        o_ref[0, pl.ds(r0, ROW_CHUNK), :] = o * _silu(z_ref[0, pl.ds(r0, ROW_CHUNK), :])
        return carry

    lax.fori_loop(0, l // ROW_CHUNK, finish, 0)


def _gdn(qkv, z, ba, conv_w, gparams, gain2, n_ctx):
    b, l, w3 = qkv.shape
    npair = w3 // (3 * LANE)
    seq = lambda off: pl.BlockSpec((1, l, LANE), lambda i, p: (i, 0, p + off))
    cw = lambda off: pl.BlockSpec((CONV_K, LANE), lambda i, p: (0, p + off))
    return pl.pallas_call(
        functools.partial(_gdn_kernel, n_ctx=n_ctx),
        grid=(b, npair),
        in_specs=[seq(0), seq(npair), seq(2 * npair), seq(0), seq(0),
                  cw(0), cw(npair), cw(2 * npair),
                  pl.BlockSpec((1, SUBLANE, LANE), lambda i, p: (p, 0, 0)),
                  pl.BlockSpec((1, LANE), lambda i, p: (0, 0))],
        out_specs=seq(0),
        out_shape=jax.ShapeDtypeStruct((b, l, npair * LANE), F32),
        scratch_shapes=[pltpu.VMEM((l + 3 * SUBLANE, LANE), F32)] + [pltpu.VMEM((l, LANE), F32)] * 6
        + [pltpu.VMEM((4, HEAD_DIM, HEAD_DIM), F32)]
        + [pltpu.VMEM((4 * l // GDN_CHUNK, GDN_CHUNK, HEAD_DIM), dt) for dt in (BF16, BF16, F32, F32, F32)],
        compiler_params=_cparams("parallel", "parallel"), name="gdn",
    )(qkv, qkv, qkv, z, ba, conv_w, conv_w, conv_w, gparams, gain2)


def _rope(x, cos, sin):
    w = x.shape[-1]
    reps = w // cos.shape[-1]
    if reps > 1:
        cos = jnp.concatenate([cos] * reps, axis=1)
        sin = jnp.concatenate([sin] * reps, axis=1)
    first = (lax.broadcasted_iota(jnp.int32, (1, w), 1) % (2 * ROPE_PAIRS)) < ROPE_PAIRS
    partner = jnp.where(first, pltpu.roll(x, w - ROPE_PAIRS, 1), pltpu.roll(x, ROPE_PAIRS, 1))
    return x * cos + partner * sin


def _attend(q, k, v, valid, sink_col):
    s = _dot_nt(q, k)
    if valid is not None:
        s = jnp.where(valid, s, NEG_INF)
    m = jnp.maximum(jnp.max(s, axis=-1, keepdims=True), sink_col)
    p = jnp.exp(s - m)
    den = jnp.sum(p, axis=-1, keepdims=True) + jnp.exp(sink_col - m)
    return _dot(p, v) / den


def _gqa(q, k_all, v_all, valid, sink_ref, o_ref):
    nq = q.shape[0]
    hd = HEAD_DIM
    group = q.shape[1] // hd // KV_HEADS
    outs = []
    for kvh in range(KV_HEADS):
        heads = [kvh * group + g for g in range(group)]
        qg = jnp.concatenate([q[:, h * hd:(h + 1) * hd] for h in heads], axis=0)
        sink_col = jnp.concatenate(
            [jnp.broadcast_to(sink_ref[0:1, h:h + 1], (nq, 1)) for h in heads], axis=0)
        o = _attend(qg, k_all[:, kvh * hd:(kvh + 1) * hd], v_all[:, kvh * hd:(kvh + 1) * hd],
                    valid, sink_col)
        outs += [o[g * nq:(g + 1) * nq, :] for g in range(group)]
    o_ref[0] = jnp.concatenate(outs, axis=1)


def _attn_kernel(q_ref, kp_ref, ko_ref, kn_ref, vp_ref, vo_ref, vn_ref, kc_ref, vc_ref,
                 cp_ref, sp_ref, co_ref, so_ref, cn_ref, sn_ref, sink_ref, o_ref, *, seq):
    i = pl.program_id(1)
    blk = ATTN_BLOCK
    n_ctx = kc_ref.shape[1]
    q = _rope(q_ref[0], co_ref[...], so_ref[...]) * (HEAD_DIM ** -0.5)
    k_all = jnp.concatenate([_rope(kp_ref[0], cp_ref[...], sp_ref[...]),
                             _rope(ko_ref[0], co_ref[...], so_ref[...]),
                             _rope(kn_ref[0], cn_ref[...], sn_ref[...]), kc_ref[0]], axis=0)
    v_all = jnp.concatenate([vp_ref[0], vo_ref[0], vn_ref[0], vc_ref[0]], axis=0)
    group = q.shape[1] // HEAD_DIM // KV_HEADS
    nk = 3 * blk + n_ctx
    qo = lax.broadcasted_iota(jnp.int32, (group * blk, 1), 0) % blk
    ko = lax.broadcasted_iota(jnp.int32, (1, nk), 1)
    kpos = (i - 1) * blk + ko
    valid = ((jnp.abs(ko - blk - qo) <= WINDOW) & (kpos >= 0) & (kpos < seq)) | (ko >= 3 * blk)
    _gqa(q, k_all, v_all, valid, sink_ref, o_ref)


def _attn_x(aq, ak, av, cos_t, sin_t, sink_row, n_ctx):
    b, l, wq = aq.shape
    wk = ak.shape[-1]
    blk = ATTN_BLOCK
    seq = l - n_ctx
    nb = seq // blk
    cb = n_ctx // blk
    kv = lambda f: pl.BlockSpec((1, blk, wk), lambda i, t: (i, f(t) + cb, 0))
    tab = lambda f: pl.BlockSpec((blk, LANE), lambda i, t: (f(t), 0))
    prev = lambda t: jnp.maximum(t - 1, 0)
    own = lambda t: t
    nxt = lambda t: jnp.minimum(t + 1, nb - 1)
    ctx = pl.BlockSpec((1, n_ctx, wk), lambda i, t: (i, 0, 0))
    return pl.pallas_call(
        functools.partial(_attn_kernel, seq=seq),
        grid=(b, nb),
        in_specs=[pl.BlockSpec((1, blk, wq), lambda i, t: (i, t + cb, 0)),
                  kv(prev), kv(own), kv(nxt), kv(prev), kv(own), kv(nxt), ctx, ctx,
                  tab(prev), tab(prev), tab(own), tab(own), tab(nxt), tab(nxt),
                  pl.BlockSpec((1, LANE), lambda i, t: (0, 0))],
        out_specs=pl.BlockSpec((1, blk, wq), lambda i, t: (i, t, 0)),
        out_shape=jax.ShapeDtypeStruct((b, seq, wq), F32),
        compiler_params=_cparams("parallel", "parallel"), name="attn_x",
    )(aq, ak, ak, ak, av, av, av, ak, av, cos_t, sin_t, cos_t, sin_t, cos_t, sin_t, sink_row)


def _attn_ctx_kernel(q_ref, k_ref, v_ref, sink_ref, o_ref):
    _gqa(q_ref[0] * (HEAD_DIM ** -0.5), k_ref[0], v_ref[0], None, sink_ref, o_ref)


def _attn_ctx(aq, ak, av, sink_row, n_ctx):
    b, _, wq = aq.shape
    wk = ak.shape[-1]
    spec = lambda w: pl.BlockSpec((1, n_ctx, w), lambda i: (i, 0, 0))
    return pl.pallas_call(
        _attn_ctx_kernel, grid=(b,),
        in_specs=[spec(wq), spec(wk), spec(wk), pl.BlockSpec((1, LANE), lambda i: (0, 0))],
        out_specs=spec(wq), out_shape=jax.ShapeDtypeStruct((b, n_ctx, wq), F32),
        compiler_params=_cparams("parallel"), name="attn_ctx",
    )(aq, ak, av, sink_row)


def _outproj_kernel(x_ref, ya_ref, yb_ref, yc_ref, w_ref, mod_ref, gain_ref, xo_ref, hx_ref):
    wa, wb = ya_ref.shape[-1], yb_ref.shape[-1]
    acc = (jnp.dot(ya_ref[0].astype(BF16), w_ref[0:wa, :], preferred_element_type=F32)
           + jnp.dot(yb_ref[0].astype(BF16), w_ref[wa:wa + wb, :], preferred_element_type=F32)
           + jnp.dot(yc_ref[0].astype(BF16), w_ref[wa + wb:, :], preferred_element_type=F32))
    x = x_ref[0] + mod_ref[0, 0, 2:3, :] * acc
    xo_ref[0] = x
    hx_ref[0] = _modnorm(x, gain_ref[...], mod_ref[0, 0, 4:5, :], mod_ref[0, 0, 3:4, :]).astype(BF16)


def _out_proj(h, ya, yb, yc, w, mod, gain, n_ctx, skip_ctx):
    b, l, d = h.shape
    nct = n_ctx // TOK_TILE
    off = nct if skip_ctx else 0
    nt = l // TOK_TILE - off
    full = lambda wd: pl.BlockSpec((1, TOK_TILE, wd), lambda i, t: (i, t + off, 0))
    own = lambda wd: pl.BlockSpec((1, TOK_TILE, wd), lambda i, t: (i, t, 0))
    return pl.pallas_call(
        _outproj_kernel, grid=(b, nt),
        in_specs=[full(d), full(ya.shape[-1]), full(yb.shape[-1]), own(yc.shape[-1]),
                  pl.BlockSpec(w.shape, lambda i, t: (0, 0)),
                  pl.BlockSpec((1, 1, 6, d), lambda i, t: (i, (t + off >= nct).astype(jnp.int32), 0, 0)),
                  pl.BlockSpec((1, d), lambda i, t: (0, 0))],
        out_specs=[own(d), own(d)],
        out_shape=[jax.ShapeDtypeStruct((b, nt * TOK_TILE, d), F32),
                   jax.ShapeDtypeStruct((b, nt * TOK_TILE, d), BF16)],
        compiler_params=_cparams("parallel", "parallel"), name="out_proj",
    )(h, ya, yb, yc, w, mod, gain)


def _route_kernel(ht_ref, wq_ref, keys_ref, s2_ref, e2_ref, tau_ref, e1_ref, qt_ref, top_ref, cand_ref):
    tm = ht_ref.shape[1]
    k = PEER_TOPK
    nk = N_KEYS
    neg = jnp.float32(-jnp.inf)
    pos = jnp.float32(jnp.inf)
    qt_ref[...] = _dot(wq_ref[...], ht_ref[...]).astype(BF16)
    cand_ref[_N_CAND:_CAND_PAD, :] = jnp.full((_CAND_PAD - _N_CAND, tm), neg, F32)
    for h in range(PEER_HEADS):
        st = []
        for p in range(2):
            hp = 2 * h + p
            s = jnp.dot(keys_ref[hp], qt_ref[hp * nk:(hp + 1) * nk, :], preferred_element_type=F32)
            st.append(s)
            cur = s
            for r in range(k):
                m = jnp.max(cur, axis=0, keepdims=True)
                top_ref[p * k + r:p * k + r + 1, :] = m
                cur = jnp.where(cur == m, neg, cur)
        v1 = top_ref[0:k, :]
        v2 = top_ref[k:2 * k, :]
        row = 0
        for a, nb in enumerate(_CAND_ROWS):
            cand_ref[row:row + nb, :] = v1[a:a + 1, :] + v2[0:nb, :]
            row += nb
        cur = cand_ref[...]
        cmax = thr = zsum = None
        for r in range(k):
            m = jnp.max(cur, axis=0, keepdims=True)
            if r == 0:
                cmax, zsum = m, jnp.ones_like(m)
            else:
                zsum = zsum + jnp.exp(m - cmax)
            thr = m
            cur = jnp.where(cur == m, neg, cur)
        tau = jnp.full((nk, tm), pos, F32)
        row = 0
        for a, nb in enumerate(_CAND_ROWS):
            sel = cand_ref[row:row + nb, :] >= thr
            tau_a = jnp.min(jnp.where(sel, v2[0:nb, :], pos), axis=0, keepdims=True)
            tau = jnp.where(st[0] == v1[a:a + 1, :], tau_a, tau)
            row += nb
        rows = slice(h * nk, (h + 1) * nk)
        e2 = jnp.exp(st[1] - v2[0:1, :])
        for tt in range(tm // LANE):
            s2_ref[tt, rows, :] = st[1][:, tt * LANE:(tt + 1) * LANE]
            e2_ref[tt, rows, :] = e2[:, tt * LANE:(tt + 1) * LANE]
        tau_ref[rows, :] = tau
        e1_ref[rows, :] = jnp.exp(st[0] - v1[0:1, :]) / zsum


def _peer_route(hxt, wq_t, keys):
    d, t = hxt.shape
    nq = wq_t.shape[0]
    tm = PEER_ROUTE_TILE
    rows = PEER_HEADS * N_KEYS
    return pl.pallas_call(
        _route_kernel, grid=(t // tm,),
        in_specs=[pl.BlockSpec((d, tm), lambda i: (0, i)),
                  pl.BlockSpec(wq_t.shape, lambda i: (0, 0)),
                  pl.BlockSpec(keys.shape, lambda i: (0, 0, 0))],
        out_specs=[pl.BlockSpec((tm // LANE, rows, LANE), lambda i: (i, 0, 0))] * 2
        + [pl.BlockSpec((rows, tm), lambda i: (0, i))] * 2,
        out_shape=[jax.ShapeDtypeStruct((t // LANE, rows, LANE), F32)] * 2
        + [jax.ShapeDtypeStruct((rows, t), F32)] * 2,
        scratch_shapes=[pltpu.VMEM((nq, tm), BF16), pltpu.VMEM((2 * PEER_TOPK, tm), F32),
                        pltpu.VMEM((_CAND_PAD, tm), F32)],
        compiler_params=_cparams("parallel"), name="peer_route",
    )(hxt, wq_t, keys)


def _peer_kernel(ht_ref, s2_ref, e2_ref, tau_ref, e1_ref, u_ref, vt_ref, o_ref,
                 acc_ref, act0_ref, act1_ref, g0_ref, g1_ref):
    e = pl.program_id(1)
    n_blk = pl.num_programs(1) - 2
    blk = jnp.clip(e - 1, 0, n_blk - 1)
    eb = u_ref.shape[0]
    tm = ht_ref.shape[1]
    nk = N_KEYS

    @pl.when(e == 0)
    def _():
        acc_ref[...] = jnp.zeros(acc_ref.shape, F32)
        act1_ref[...] = jnp.zeros(act1_ref.shape, F32)
        g0_ref[...] = jnp.zeros(g0_ref.shape, BF16)

    @pl.when(e % 2 == 0)
    def _():
        _peer_step(ht_ref, s2_ref, e2_ref, tau_ref, e1_ref, u_ref, vt_ref, acc_ref,
                   act0_ref, act1_ref, g1_ref, g0_ref, blk)

    @pl.when(e % 2 == 1)
    def _():
        _peer_step(ht_ref, s2_ref, e2_ref, tau_ref, e1_ref, u_ref, vt_ref, acc_ref,
                   act1_ref, act0_ref, g0_ref, g1_ref, blk)

    @pl.when(e == n_blk + 1)
    def _():
        o_ref[...] = acc_ref[...].T


def _peer_step(ht_ref, s2_ref, e2_ref, tau_ref, e1_ref, u_ref, vt_ref, acc_ref,
               act_w_ref, act_r_ref, g_w_ref, g_r_ref, blk):
    eb = u_ref.shape[0]
    tm = ht_ref.shape[1]
    nk = N_KEYS
    se = eb // PEER_SUB
    sa = acc_ref.shape[0] // PEER_SUB

    def sub(k, carry):
        ar = pl.multiple_of(k * sa, sa)
        er = pl.multiple_of(k * se, se)
        acc_ref[pl.ds(ar, sa), :] += jnp.dot(vt_ref[pl.ds(ar, sa), :], g_r_ref[...],
                                             preferred_element_type=F32)
        act_w_ref[pl.ds(er, se), :] = jnp.dot(u_ref[pl.ds(er, se), :], ht_ref[...],
                                              preferred_element_type=F32)
        act = act_r_ref[pl.ds(er, se), :]
        act = 0.5 * act * (1.0 + lax.erf(act * 0.7071067811865476))
        for il in range(se // nk):
            i1 = blk * (eb // nk) + k * (se // nk) + il
            tau = [tau_ref[pl.ds(h * nk + i1, 1), :] for h in range(PEER_HEADS)]
            e1 = [e1_ref[pl.ds(h * nk + i1, 1), :] for h in range(PEER_HEADS)]
            for tt in range(tm // LANE):
                ln = slice(tt * LANE, (tt + 1) * LANE)
                for r0 in range(0, nk, PEER_ROW_TILE):
                    w = None
                    for h in range(PEER_HEADS):
                        hk = slice(h * nk + r0, h * nk + r0 + PEER_ROW_TILE)
                        term = jnp.where(s2_ref[tt, hk, :] >= tau[h][:, ln], e2_ref[tt, hk, :], 0.0) * e1[h][:, ln]
                        w = term if w is None else w + term
                    lo = il * nk + r0
                    g_w_ref[pl.ds(er + lo, PEER_ROW_TILE), ln] = (
                        w * act[lo:lo + PEER_ROW_TILE, ln]).astype(BF16)
        return carry

    lax.fori_loop(0, PEER_SUB, sub, 0)


def _peer_dense(hxt, route, u, vt):
    d, t = hxt.shape
    n_exp = u.shape[0]
    tm, eb = PEER_TOK_TILE, PEER_EXP_TILE
    rows = route[2].shape[0]
    n_blk = n_exp // eb
    return pl.pallas_call(
        _peer_kernel, grid=(t // tm, n_blk + 2),
        in_specs=[pl.BlockSpec((d, tm), lambda i, e: (0, i))]
        + [pl.BlockSpec((tm // LANE, rows, LANE), lambda i, e: (i, 0, 0))] * 2
        + [pl.BlockSpec((rows, tm), lambda i, e: (0, i))] * 2
        + [pl.BlockSpec((eb, d), lambda i, e: (jnp.minimum(e, n_blk - 1), 0)),
           pl.BlockSpec((d, eb), lambda i, e: (0, jnp.clip(e - 2, 0, n_blk - 1)))],
        out_specs=pl.BlockSpec((tm, d), lambda i, e: (i, 0)),
        out_shape=jax.ShapeDtypeStruct((t, d), F32),
        scratch_shapes=[pltpu.VMEM((d, tm), F32), pltpu.VMEM((eb, tm), F32), pltpu.VMEM((eb, tm), F32),
                        pltpu.VMEM((eb, tm), BF16), pltpu.VMEM((eb, tm), BF16)],
        compiler_params=_cparams("parallel", "arbitrary"), name="peer_dense",
    )(hxt, *route, u, vt)


def _final_kernel(x_ref, f_ref, mod_ref, gain_ref, o_ref):
    x = x_ref[0] + mod_ref[0, 0, 5:6, :] * f_ref[0]
    o_ref[0] = x * lax.rsqrt(jnp.mean(x * x, axis=-1, keepdims=True) + EPS) * gain_ref[...]


def _final(x, ffn, mod, gain):
    b, s, d = x.shape
    tok = pl.BlockSpec((1, TOK_TILE, d), lambda i, t: (i, t, 0))
    return pl.pallas_call(
        _final_kernel, grid=(b, s // TOK_TILE),
        in_specs=[tok, tok, pl.BlockSpec((1, 1, 6, d), lambda i, t: (i, 1, 0, 0)),
                  pl.BlockSpec((1, d), lambda i, t: (0, 0))],
        out_specs=tok, out_shape=jax.ShapeDtypeStruct((b, s, d), F32),
        compiler_params=_cparams("parallel", "parallel"), name="final_norm",
    )(x, ffn, mod, gain)


def _rope_tables(seq):
    pos = jnp.arange(seq)
    rc = jnp.stack([pos // GRID_W, pos % GRID_W], axis=-1).astype(F32)
    inv = jnp.power(ROPE_BASE, -jnp.arange(ROPE_PAIRS, dtype=F32) / ROPE_PAIRS)
    ang = rc[:, :, None] * inv
    cos = jnp.concatenate([jnp.cos(ang)] * 2, axis=-1).reshape(seq, HEAD_DIM)
    sin = jnp.concatenate([-jnp.sin(ang), jnp.sin(ang)], axis=-1).reshape(seq, HEAD_DIM)
    return jnp.concatenate([cos] * 2, axis=1), jnp.concatenate([sin] * 2, axis=1)


def _pair_lanes(cols, n_heads):
    rows = cols.shape[0]
    c = cols.reshape(rows, 4, n_heads // 2, 2).transpose(0, 2, 1, 3).reshape(rows, n_heads // 2, 8)
    return jnp.pad(c, ((0, 0), (0, 0), (0, LANE - 8))).reshape(rows, n_heads // 2 * LANE)


def kernel(x, c, ctx, c_ctx, w_ada, b_ada, norm1, norm2, w_in, conv_w, a_log, dt_bias, gdn_norm, w_pool,
           pool_scale, sink, w_out, peer_wq, peer_keys, peer_u, peer_v, norm_f):
    b, seq, d = x.shape
    n_ctx = ctx.shape[1]
    depth = w_ada.shape[0]
    pool_w = d // 4
    gdn_w = 3 * d // 8
    gdn_heads = gdn_w // HEAD_DIM
    attn_w = d - pool_w - gdn_w
    kv_w = KV_HEADS * HEAD_DIM
    assert n_ctx % TOK_TILE == 0 and seq % TOK_TILE == 0 and gdn_heads % 2 == 0
    assert (b * (n_ctx + seq)) % PEER_TOK_TILE == 0 and (b * seq) % PEER_TOK_TILE == 0

    rows = -(-(b + 1) // SUBLANE) * SUBLANE
    c_all = jnp.zeros((rows, d), F32).at[:b].set(c).at[b].set(c_ctx)
    mod = _ada_mod(c_all, w_ada, b_ada)
    mod_x = mod[:, :b].reshape(depth, b, 1, 6, d)
    mod_c = jnp.broadcast_to(mod[:, b].reshape(depth, 1, 1, 6, d), (depth, b, 1, 6, d))
    mod = jnp.concatenate([mod_c, mod_x], axis=2)

    cos_t, sin_t = _rope_tables(seq)
    splits = (pool_w, 3 * gdn_w, gdn_w, gdn_heads // 2 * LANE, attn_w, kv_w, kv_w)
    o_ba = pool_w + 4 * gdn_w
    n_ba = 4 * gdn_heads

    h = jnp.concatenate([ctx, x], axis=1)
    ffn = None
    for l in range(depth):
        last = l == depth - 1
        w_l = w_in[l]
        w_big = jnp.concatenate([w_l[:, :o_ba], _pair_lanes(w_l[:, o_ba:o_ba + n_ba], gdn_heads),
                                 w_l[:, o_ba + n_ba:]], axis=1).astype(BF16)
        pmod = mod[l - 1] if l > 0 else None
        outs = _in_proj(h, ffn, pmod, mod[l], norm1[l][None], w_big, splits, n_ctx)
        if l > 0:
            h, *outs = outs
        pa, qkv, z, ba, aq, ak, av = outs

        wbd = jax.scipy.linalg.block_diag(*[w_pool[l, g] for g in range(len(POOL_WINDOWS))])
        ya = _pool(pa, wbd, pool_scale[l][None], n_ctx)

        gparams = jnp.stack([_pair_lanes(jnp.concatenate([jnp.zeros_like(a_log[l]), a_log[l]]).reshape(1, -1),
                                         gdn_heads),
                             _pair_lanes(jnp.concatenate([jnp.zeros_like(dt_bias[l]), dt_bias[l]]).reshape(1, -1),
                                         gdn_heads)], axis=1)
        gparams = gparams.reshape(2, gdn_heads // 2, LANE).transpose(1, 0, 2)
        gparams = jnp.pad(gparams, ((0, 0), (0, SUBLANE - 2), (0, 0)))
        yb = _gdn(qkv, z, ba, conv_w[l], gparams, jnp.concatenate([gdn_norm[l]] * 2)[None], n_ctx)

        sink_row = jnp.pad(sink[l], (0, LANE - sink.shape[1]))[None]
        yc = _attn_x(aq, ak, av, cos_t, sin_t, sink_row, n_ctx)
        if not last:
            yc = jnp.concatenate([_attn_ctx(aq, ak, av, sink_row, n_ctx), yc], axis=1)

        x_new, hx = _out_proj(h, ya, yb, yc, w_out[l].astype(BF16), mod[l], norm2[l][None], n_ctx,
                              skip_ctx=last)
        t = hx.shape[0] * hx.shape[1]
        hxt = hx.reshape(t, d).T
        keys = peer_keys[l].reshape(2 * PEER_HEADS, N_KEYS, -1).astype(BF16)
        route = _peer_route(hxt, peer_wq[l].T.astype(BF16), keys)
        ffn = _peer_dense(hxt, route, peer_u[l].astype(BF16), peer_v[l].T.astype(BF16))
        ffn = ffn.reshape(x_new.shape)
        h = x_new
    return _final(h, ffn, mod[depth - 1], norm_f[None])
```

```python
import functools

import jax
import jax.numpy as jnp
from jax import lax
from jax.experimental import pallas as pl
from jax.experimental.pallas import tpu as pltpu

F32, BF16 = jnp.float32, jnp.bfloat16
HIGHEST = lax.Precision.HIGHEST

EPS = 1e-6
NEG_INF = -1e30
HEAD_DIM = 64
GRID_W = 64
POOL_WINDOWS = (2, 4, 8, 16)
GDN_CHUNK = 64
CONV_K = 5
KV_HEADS = 2
WINDOW = 128
ATTN_BLOCK = 128
ROPE_BASE = 10000.0
ROPE_PAIRS = HEAD_DIM // 4
PEER_HEADS = 8
N_KEYS = 128
PEER_TOPK = 16

LANE = 128
SUBLANE = 8
TOK_TILE = 256
ROW_CHUNK = 256
GDN_GROUP = 4
PEER_ROUTE_TILE = 256
PEER_TOK_TILE = 512
PEER_EXP_TILE = 512
PEER_ROW_TILE = 64
PEER_SUB = 4
VMEM_LIMIT = 48 * 1024 * 1024

_CAND_ROWS = tuple(PEER_TOPK // (a + 1) for a in range(PEER_TOPK))
_N_CAND = sum(_CAND_ROWS)
_CAND_PAD = -(-_N_CAND // SUBLANE) * SUBLANE


def _cparams(*sem):
    return pltpu.CompilerParams(dimension_semantics=sem, vmem_limit_bytes=VMEM_LIMIT)


def _dot(a, b):
    return jnp.dot(a.astype(BF16), b.astype(BF16), preferred_element_type=F32)


def _dot_nt(a, b):
    return lax.dot_general(a.astype(BF16), b.astype(BF16), (((1,), (1,)), ((), ())),
                           preferred_element_type=F32)


def _dot_tn(a, b):
    return lax.dot_general(a.astype(BF16), b.astype(BF16), (((0,), (0,)), ((), ())),
                           preferred_element_type=F32)


def _dot_hi(a, b):
    return jnp.dot(a, b, precision=HIGHEST, preferred_element_type=F32)


def _silu(x):
    return x * jax.nn.sigmoid(x)


def _softplus(x):
    return jnp.maximum(x, 0.0) + jnp.log1p(jnp.exp(-jnp.abs(x)))


def _modnorm(x, gain, scale, shift):
    y = x * lax.rsqrt(jnp.mean(x * x, axis=-1, keepdims=True) + EPS) * gain
    return y * (1.0 + scale) + shift


def _ada_kernel(c_ref, w_ref, b_ref, o_ref):
    o_ref[0] = _dot_hi(_silu(c_ref[...]), w_ref[0]) + b_ref[0]


def _ada_mod(c_all, w_ada, b_ada):
    depth, d, n = w_ada.shape
    rows = c_all.shape[0]
    tn = n // 4
    return pl.pallas_call(
        _ada_kernel,
        grid=(depth, n // tn),
        in_specs=[pl.BlockSpec((rows, d), lambda l, j: (0, 0)),
                  pl.BlockSpec((1, d, tn), lambda l, j: (l, 0, j)),
                  pl.BlockSpec((1, 1, tn), lambda l, j: (l, 0, j))],
        out_specs=pl.BlockSpec((1, rows, tn), lambda l, j: (l, 0, j)),
        out_shape=jax.ShapeDtypeStruct((depth, rows, n), F32),
        compiler_params=_cparams("parallel", "parallel"),
        name="ada_mod",
    )(c_all, w_ada, b_ada.reshape(depth, 1, n))


def _inproj_kernel(*refs, has_ffn, col_splits):
    if has_ffn:
        h_ref, f_ref, pmod_ref, mod_ref, gain_ref, w_ref, res_ref, *outs = refs
        x = h_ref[0] + pmod_ref[0, 0, 5:6, :] * f_ref[0]
        res_ref[0] = x
    else:
        h_ref, mod_ref, gain_ref, w_ref, *outs = refs
        x = h_ref[0]
    hb = _modnorm(x, gain_ref[...], mod_ref[0, 0, 1:2, :], mod_ref[0, 0, 0:1, :]).astype(BF16)
    off = 0
    for o_ref, width in zip(outs, col_splits):
        o_ref[0] = jnp.dot(hb, w_ref[:, off:off + width], preferred_element_type=F32)
        off += width


def _in_proj(h, ffn, pmod, mod, gain, w, col_splits, n_ctx):
    b, l, d = h.shape
    nt = l // TOK_TILE
    nct = n_ctx // TOK_TILE
    tok = pl.BlockSpec((1, TOK_TILE, d), lambda i, t: (i, t, 0))
    modspec = pl.BlockSpec((1, 1, 6, d), lambda i, t: (i, (t >= nct).astype(jnp.int32), 0, 0))
    has_ffn = ffn is not None
    in_specs = [tok] + ([tok, modspec] if has_ffn else []) + [
        modspec, pl.BlockSpec((1, d), lambda i, t: (0, 0)), pl.BlockSpec(w.shape, lambda i, t: (0, 0))]
    out_shape = [jax.ShapeDtypeStruct((b, l, wd), F32) for wd in col_splits]
    out_specs = [pl.BlockSpec((1, TOK_TILE, wd), lambda i, t: (i, t, 0)) for wd in col_splits]
    if has_ffn:
        out_shape = [jax.ShapeDtypeStruct((b, l, d), F32)] + out_shape
        out_specs = [tok] + out_specs
    args = (h, ffn, pmod, mod, gain, w) if has_ffn else (h, mod, gain, w)
    return pl.pallas_call(
        functools.partial(_inproj_kernel, has_ffn=has_ffn, col_splits=col_splits),
        grid=(b, nt), in_specs=in_specs, out_specs=out_specs, out_shape=out_shape,
        compiler_params=_cparams("parallel", "parallel"), name="in_proj",
    )(*args)


def _pool_kernel(a_ref, wbd_ref, scale_ref, o_ref, pad_ref, *, segments):
    c = a_ref.shape[-1]
    grp = lax.broadcasted_iota(jnp.int32, (1, c), 1) // (c // len(POOL_WINDOWS))
    zeros = jnp.zeros((SUBLANE, c), F32)
    for off, n in segments:
        pad_ref[0:SUBLANE, :] = zeros
        pad_ref[SUBLANE:SUBLANE + n, :] = a_ref[0, off:off + n, :]
        pad_ref[SUBLANE + n:2 * SUBLANE + n, :] = zeros

        def chunk(ci, carry, off=off, n=n):
            r0 = pl.multiple_of(ci * ROW_CHUNK, ROW_CHUNK)
            win = pad_ref[pl.ds(r0, ROW_CHUNK + 2 * SUBLANE), :]
            sh = lambda s: win[SUBLANE + s:SUBLANE + s + ROW_CHUNK, :]
            a0 = sh(0)
            s2 = sh(-1) + a0
            s4 = s2 + sh(-2) + sh(1)
            s8 = s4 + sh(-4) + sh(-3) + sh(2) + sh(3)
            s16 = s8 + (sh(-8) + sh(-7) + sh(-6) + sh(-5)) + (sh(4) + sh(5) + sh(6) + sh(7))
            t = r0 + lax.broadcasted_iota(jnp.int32, (ROW_CHUNK, 1), 0)

            def mean(s, w):
                cnt = jnp.minimum(t + (w - w // 2), n) - jnp.maximum(t - w // 2, 0)
                return s / cnt.astype(F32)

            pooled = jnp.where(grp == 0, mean(s2, 2), jnp.where(grp == 1, mean(s4, 4),
                               jnp.where(grp == 2, mean(s8, 8), mean(s16, 16)))) - a0
            o_ref[0, pl.ds(off + r0, ROW_CHUNK), :] = _dot_hi(pooled, wbd_ref[...]) * scale_ref[...]
            return carry

        lax.fori_loop(0, n // ROW_CHUNK, chunk, 0)


def _pool(a, wbd, scale, n_ctx):
    b, l, c = a.shape
    segments = ((0, n_ctx), (n_ctx, l - n_ctx))
    return pl.pallas_call(
        functools.partial(_pool_kernel, segments=segments),
        grid=(b,),
        in_specs=[pl.BlockSpec((1, l, c), lambda i: (i, 0, 0)),
                  pl.BlockSpec((c, c), lambda i: (0, 0)),
                  pl.BlockSpec((1, c), lambda i: (0, 0))],
        out_specs=pl.BlockSpec((1, l, c), lambda i: (i, 0, 0)),
        out_shape=jax.ShapeDtypeStruct((b, l, c), F32),
        scratch_shapes=[pltpu.VMEM((l - n_ctx + 2 * SUBLANE, c), F32)],
        compiler_params=_cparams("parallel"), name="pool",
    )(a, wbd, scale)


def _gdn_kernel(q_ref, k_ref, v_ref, z_ref, ba_ref, cwq_ref, cwk_ref, cwv_ref, gp_ref, gain_ref, o_ref,
                pad_ref, yq_ref, yk_ref, yv_ref, bg_ref, of_ref, ob_ref, s_ref,
                qe_ref, mn_ref, oc_ref, nn_ref, egl_ref, *, n_ctx):
    l = q_ref.shape[1]
    n_x = l - n_ctx
    hd = HEAD_DIM
    lane = lax.broadcasted_iota(jnp.int32, (1, LANE), 1)
    lo = lane < hd
    zeros = jnp.zeros((SUBLANE, LANE), F32)
    x_off = 2 * SUBLANE + n_ctx

    def head_scale(ss):
        s_lo = jnp.sum(jnp.where(lo, ss, 0.0), axis=-1, keepdims=True)
        s_hi = jnp.sum(jnp.where(lo, 0.0, ss), axis=-1, keepdims=True)
        return jnp.where(lo, s_lo, s_hi)

    def conv(u_ref, cw_ref, y_ref, post):
        pad_ref[0:SUBLANE, :] = zeros
        pad_ref[SUBLANE:SUBLANE + n_ctx, :] = u_ref[0, 0:n_ctx, :]
        pad_ref[SUBLANE + n_ctx:x_off, :] = zeros
        pad_ref[x_off:x_off + n_x, :] = u_ref[0, n_ctx:l, :]
        pad_ref[x_off + n_x:x_off + n_x + SUBLANE, :] = zeros
        cw = cw_ref[...]
        for poff, yoff, n in ((SUBLANE, 0, n_ctx), (x_off, n_ctx, n_x)):
            def chunk(ci, carry, poff=poff, yoff=yoff):
                r0 = pl.multiple_of(ci * ROW_CHUNK, ROW_CHUNK)
                win = pad_ref[pl.ds(poff - SUBLANE + r0, ROW_CHUNK + 2 * SUBLANE), :]
                acc = win[SUBLANE - 2:SUBLANE - 2 + ROW_CHUNK, :] * cw[0:1, :]
                for j in range(1, CONV_K):
                    acc = acc + win[SUBLANE - 2 + j:SUBLANE - 2 + j + ROW_CHUNK, :] * cw[j:j + 1, :]
                y_ref[pl.ds(yoff + r0, ROW_CHUNK), :] = post(_silu(acc))
                return carry
            lax.fori_loop(0, n // ROW_CHUNK, chunk, 0)

    l2 = lambda y: y * lax.rsqrt(head_scale(y * y) + EPS)
    conv(q_ref, cwq_ref, yq_ref, lambda y: l2(y) * (hd ** -0.5))
    conv(k_ref, cwk_ref, yk_ref, l2)
    conv(v_ref, cwv_ref, yv_ref, lambda y: y)

    ba = ba_ref[0]
    g = -jnp.exp(gp_ref[0, 0:1, :]) * _softplus(ba + gp_ref[0, 1:2, :])
    bg_ref[...] = jnp.where(lane < 4, jax.nn.sigmoid(ba), g)

    ck = GDN_CHUNK
    rr = lax.broadcasted_iota(jnp.int32, (ck, ck), 0)
    cc = lax.broadcasted_iota(jnp.int32, (ck, ck), 1)
    eye = (rr == cc).astype(F32)
    tril = (rr >= cc).astype(F32)
    n_chunks = l // ck
    nc_ctx = n_ctx // ck

    def phase1(ci, carry):
        probs = []
        for gi in range(GDN_GROUP):
            c = ci * GDN_GROUP + gi
            r0 = pl.multiple_of(c * ck, ck)
            qc = yq_ref[pl.ds(r0, ck), :]
            kc = yk_ref[pl.ds(r0, ck), :]
            vc = yv_ref[pl.ds(r0, ck), :]
            bgc = bg_ref[pl.ds(r0, ck), :]
            gcf = _dot_hi(tril, bgc)
            gcb = gcf[ck - 1:ck, :] - gcf + bgc
            gct = (gcf.T, gcb.T)
            for j in range(2):
                qh = qc[:, j * hd:(j + 1) * hd]
                kh = kc[:, j * hd:(j + 1) * hd]
                vh = vc[:, j * hd:(j + 1) * hd]
                kk = _dot_nt(kh, kh)
                qk = _dot_nt(qh, kh)
                for d in range(2):
                    mask = (rr >= cc) if d == 0 else (rr <= cc)
                    smask = (rr > cc) if d == 0 else (rr < cc)
                    lg, lb = 4 + 2 * d + j, 2 * d + j
                    gcol = (gcf, gcb)[d][:, lg:lg + 1]
                    grow = gct[d][lg:lg + 1, :]
                    bcol = bgc[:, lb:lb + 1]
                    decay = jnp.where(mask, jnp.exp(jnp.where(mask, gcol - grow, 0.0)), 0.0)
                    xp = jnp.where(smask, -(bcol * kk * decay), 0.0)
                    eg = jnp.exp(gcol)
                    last = ck - 1 if d == 0 else 0
                    glast = gcol[last:last + 1, :]
                    probs.append(dict(
                        idx=c * 4 + 2 * d + j, xp=xp, inv=eye + xp, attn=qk * decay, qd=qh * eg,
                        rhs=jnp.concatenate([vh * bcol, kh * bcol * eg], axis=1),
                        kd=kh * jnp.exp(glast - gcol), egl=jnp.exp(glast)))
        for _ in range(5):
            for p in probs:
                p["xp"] = _dot(p["xp"], p["xp"])
            for p in probs:
                p["inv"] = p["inv"] + _dot(p["inv"], p["xp"])
        for p in probs:
            p["sol"] = _dot(p["inv"], p["rhs"])
        for p in probs:
            p["as"] = _dot(p["attn"], p["sol"])
        for p in probs:
            p["ks"] = _dot_tn(p["kd"], p["sol"])
        for p in probs:
            i = p["idx"]
            qe_ref[i] = (p["qd"] - p["as"][:, hd:]).astype(BF16)
            oc_ref[i] = p["as"][:, :hd]
            mn_ref[i] = p["ks"][:, hd:].astype(BF16)
            nn_ref[i] = p["ks"][:, :hd]
            egl_ref[i] = jnp.broadcast_to(p["egl"], (ck, hd))
        return carry

    lax.fori_loop(0, n_chunks // GDN_GROUP, phase1, 0)

    s_ref[...] = jnp.zeros(s_ref.shape, F32)

    def phase2(i, carry):
        cb = jnp.where(i < nc_ctx, nc_ctx - 1 - i, n_chunks - 1 - (i - nc_ctx))
        work = [(d, j, cidx * 4 + 2 * d + j) for d, cidx in ((0, i), (1, cb)) for j in range(2)]
        states = [s_ref[2 * d + j] for d, j, _ in work]
        outs = [_dot(qe_ref[idx], s) + oc_ref[idx] for (_, _, idx), s in zip(work, states)]
        upd = [_dot(mn_ref[idx], s) for (_, _, idx), s in zip(work, states)]
        for (d, j, idx), s, m in zip(work, states, upd):
            s_ref[2 * d + j] = egl_ref[idx] * s - m + nn_ref[idx]
        of_ref[pl.ds(pl.multiple_of(i * ck, ck), ck), :] = jnp.concatenate(outs[0:2], axis=1)
        ob_ref[pl.ds(pl.multiple_of(cb * ck, ck), ck), :] = jnp.concatenate(outs[2:4], axis=1)
        return carry

    lax.fori_loop(0, n_chunks, phase2, 0)

    def finish(ci, carry):
        r0 = pl.multiple_of(ci * ROW_CHUNK, ROW_CHUNK)
        o = of_ref[pl.ds(r0, ROW_CHUNK), :] + ob_ref[pl.ds(r0, ROW_CHUNK), :]
        o = o * lax.rsqrt(head_scale(o * o) * (1.0 / hd) + EPS) * gain_ref[...]
        o_ref[0, pl.ds(r0, ROW_CHUNK), :] = o * _silu(z_ref[0, pl.ds(r0, ROW_CHUNK), :])
        return carry

    lax.fori_loop(0, l // ROW_CHUNK, finish, 0)


def _gdn(qkv, z, ba, conv_w, gparams, gain2, n_ctx):
    b, l, w3 = qkv.shape
    npair = w3 // (3 * LANE)
    seq = lambda off: pl.BlockSpec((1, l, LANE), lambda i, p: (i, 0, p + off))
    cw = lambda off: pl.BlockSpec((CONV_K, LANE), lambda i, p: (0, p + off))
    return pl.pallas_call(
        functools.partial(_gdn_kernel, n_ctx=n_ctx),
        grid=(b, npair),
        in_specs=[seq(0), seq(npair), seq(2 * npair), seq(0), seq(0),
                  cw(0), cw(npair), cw(2 * npair),
                  pl.BlockSpec((1, SUBLANE, LANE), lambda i, p: (p, 0, 0)),
                  pl.BlockSpec((1, LANE), lambda i, p: (0, 0))],
        out_specs=seq(0),
        out_shape=jax.ShapeDtypeStruct((b, l, npair * LANE), F32),
        scratch_shapes=[pltpu.VMEM((l + 3 * SUBLANE, LANE), F32)] + [pltpu.VMEM((l, LANE), F32)] * 6
        + [pltpu.VMEM((4, HEAD_DIM, HEAD_DIM), F32)]
        + [pltpu.VMEM((4 * l // GDN_CHUNK, GDN_CHUNK, HEAD_DIM), dt) for dt in (BF16, BF16, F32, F32, F32)],
        compiler_params=_cparams("parallel", "parallel"), name="gdn",
    )(qkv, qkv, qkv, z, ba, conv_w, conv_w, conv_w, gparams, gain2)


def _rope(x, cos, sin):
    w = x.shape[-1]
    reps = w // cos.shape[-1]
    if reps > 1:
        cos = jnp.concatenate([cos] * reps, axis=1)
        sin = jnp.concatenate([sin] * reps, axis=1)
    first = (lax.broadcasted_iota(jnp.int32, (1, w), 1) % (2 * ROPE_PAIRS)) < ROPE_PAIRS
    partner = jnp.where(first, pltpu.roll(x, w - ROPE_PAIRS, 1), pltpu.roll(x, ROPE_PAIRS, 1))
    return x * cos + partner * sin


def _attend(q, k, v, valid, sink_col):
    s = _dot_nt(q, k)
    if valid is not None:
        s = jnp.where(valid, s, NEG_INF)
    m = jnp.maximum(jnp.max(s, axis=-1, keepdims=True), sink_col)
    p = jnp.exp(s - m)
    den = jnp.sum(p, axis=-1, keepdims=True) + jnp.exp(sink_col - m)
    return _dot(p, v) / den


def _gqa(q, k_all, v_all, valid, sink_ref, o_ref):
    nq = q.shape[0]
    hd = HEAD_DIM
    group = q.shape[1] // hd // KV_HEADS
    outs = []
    for kvh in range(KV_HEADS):
        heads = [kvh * group + g for g in range(group)]
        qg = jnp.concatenate([q[:, h * hd:(h + 1) * hd] for h in heads], axis=0)
        sink_col = jnp.concatenate(
            [jnp.broadcast_to(sink_ref[0:1, h:h + 1], (nq, 1)) for h in heads], axis=0)
        o = _attend(qg, k_all[:, kvh * hd:(kvh + 1) * hd], v_all[:, kvh * hd:(kvh + 1) * hd],
                    valid, sink_col)
        outs += [o[g * nq:(g + 1) * nq, :] for g in range(group)]
    o_ref[0] = jnp.concatenate(outs, axis=1)


def _attn_kernel(q_ref, kp_ref, ko_ref, kn_ref, vp_ref, vo_ref, vn_ref, kc_ref, vc_ref,
                 cp_ref, sp_ref, co_ref, so_ref, cn_ref, sn_ref, sink_ref, o_ref, *, seq):
    i = pl.program_id(1)
    blk = ATTN_BLOCK
    n_ctx = kc_ref.shape[1]
    q = _rope(q_ref[0], co_ref[...], so_ref[...]) * (HEAD_DIM ** -0.5)
    k_all = jnp.concatenate([_rope(kp_ref[0], cp_ref[...], sp_ref[...]),
                             _rope(ko_ref[0], co_ref[...], so_ref[...]),
                             _rope(kn_ref[0], cn_ref[...], sn_ref[...]), kc_ref[0]], axis=0)
    v_all = jnp.concatenate([vp_ref[0], vo_ref[0], vn_ref[0], vc_ref[0]], axis=0)
    group = q.shape[1] // HEAD_DIM // KV_HEADS
    nk = 3 * blk + n_ctx
    qo = lax.broadcasted_iota(jnp.int32, (group * blk, 1), 0) % blk
    ko = lax.broadcasted_iota(jnp.int32, (1, nk), 1)
    kpos = (i - 1) * blk + ko
    valid = ((jnp.abs(ko - blk - qo) <= WINDOW) & (kpos >= 0) & (kpos < seq)) | (ko >= 3 * blk)
    _gqa(q, k_all, v_all, valid, sink_ref, o_ref)


def _attn_x(aq, ak, av, cos_t, sin_t, sink_row, n_ctx):
    b, l, wq = aq.shape
    wk = ak.shape[-1]
    blk = ATTN_BLOCK
    seq = l - n_ctx
    nb = seq // blk
    cb = n_ctx // blk
    kv = lambda f: pl.BlockSpec((1, blk, wk), lambda i, t: (i, f(t) + cb, 0))
    tab = lambda f: pl.BlockSpec((blk, LANE), lambda i, t: (f(t), 0))
    prev = lambda t: jnp.maximum(t - 1, 0)
    own = lambda t: t
    nxt = lambda t: jnp.minimum(t + 1, nb - 1)
    ctx = pl.BlockSpec((1, n_ctx, wk), lambda i, t: (i, 0, 0))
    return pl.pallas_call(
        functools.partial(_attn_kernel, seq=seq),
        grid=(b, nb),
        in_specs=[pl.BlockSpec((1, blk, wq), lambda i, t: (i, t + cb, 0)),
                  kv(prev), kv(own), kv(nxt), kv(prev), kv(own), kv(nxt), ctx, ctx,
                  tab(prev), tab(prev), tab(own), tab(own), tab(nxt), tab(nxt),
                  pl.BlockSpec((1, LANE), lambda i, t: (0, 0))],
        out_specs=pl.BlockSpec((1, blk, wq), lambda i, t: (i, t, 0)),
        out_shape=jax.ShapeDtypeStruct((b, seq, wq), F32),
        compiler_params=_cparams("parallel", "parallel"), name="attn_x",
    )(aq, ak, ak, ak, av, av, av, ak, av, cos_t, sin_t, cos_t, sin_t, cos_t, sin_t, sink_row)


def _attn_ctx_kernel(q_ref, k_ref, v_ref, sink_ref, o_ref):
    _gqa(q_ref[0] * (HEAD_DIM ** -0.5), k_ref[0], v_ref[0], None, sink_ref, o_ref)


def _attn_ctx(aq, ak, av, sink_row, n_ctx):
    b, _, wq = aq.shape
    wk = ak.shape[-1]
    spec = lambda w: pl.BlockSpec((1, n_ctx, w), lambda i: (i, 0, 0))
    return pl.pallas_call(
        _attn_ctx_kernel, grid=(b,),
        in_specs=[spec(wq), spec(wk), spec(wk), pl.BlockSpec((1, LANE), lambda i: (0, 0))],
        out_specs=spec(wq), out_shape=jax.ShapeDtypeStruct((b, n_ctx, wq), F32),
        compiler_params=_cparams("parallel"), name="attn_ctx",
    )(aq, ak, av, sink_row)


def _outproj_kernel(x_ref, ya_ref, yb_ref, yc_ref, w_ref, mod_ref, gain_ref, xo_ref, hx_ref):
    wa, wb = ya_ref.shape[-1], yb_ref.shape[-1]
    acc = (jnp.dot(ya_ref[0].astype(BF16), w_ref[0:wa, :], preferred_element_type=F32)
           + jnp.dot(yb_ref[0].astype(BF16), w_ref[wa:wa + wb, :], preferred_element_type=F32)
           + jnp.dot(yc_ref[0].astype(BF16), w_ref[wa + wb:, :], preferred_element_type=F32))
    x = x_ref[0] + mod_ref[0, 0, 2:3, :] * acc
    xo_ref[0] = x
    hx_ref[0] = _modnorm(x, gain_ref[...], mod_ref[0, 0, 4:5, :], mod_ref[0, 0, 3:4, :]).astype(BF16)


def _out_proj(h, ya, yb, yc, w, mod, gain, n_ctx, skip_ctx):
    b, l, d = h.shape
    nct = n_ctx // TOK_TILE
    off = nct if skip_ctx else 0
    nt = l // TOK_TILE - off
    full = lambda wd: pl.BlockSpec((1, TOK_TILE, wd), lambda i, t: (i, t + off, 0))
    own = lambda wd: pl.BlockSpec((1, TOK_TILE, wd), lambda i, t: (i, t, 0))
    return pl.pallas_call(
        _outproj_kernel, grid=(b, nt),
        in_specs=[full(d), full(ya.shape[-1]), full(yb.shape[-1]), own(yc.shape[-1]),
                  pl.BlockSpec(w.shape, lambda i, t: (0, 0)),
                  pl.BlockSpec((1, 1, 6, d), lambda i, t: (i, (t + off >= nct).astype(jnp.int32), 0, 0)),
                  pl.BlockSpec((1, d), lambda i, t: (0, 0))],
        out_specs=[own(d), own(d)],
        out_shape=[jax.ShapeDtypeStruct((b, nt * TOK_TILE, d), F32),
                   jax.ShapeDtypeStruct((b, nt * TOK_TILE, d), BF16)],
        compiler_params=_cparams("parallel", "parallel"), name="out_proj",
    )(h, ya, yb, yc, w, mod, gain)


def _route_kernel(ht_ref, wq_ref, keys_ref, s2_ref, e2_ref, tau_ref, e1_ref, qt_ref, top_ref, cand_ref):
    tm = ht_ref.shape[1]
    k = PEER_TOPK
    nk = N_KEYS
    neg = jnp.float32(-jnp.inf)
    pos = jnp.float32(jnp.inf)
    qt_ref[...] = _dot(wq_ref[...], ht_ref[...]).astype(BF16)
    cand_ref[_N_CAND:_CAND_PAD, :] = jnp.full((_CAND_PAD - _N_CAND, tm), neg, F32)
    for h in range(PEER_HEADS):
        st = []
        for p in range(2):
            hp = 2 * h + p
            s = jnp.dot(keys_ref[hp], qt_ref[hp * nk:(hp + 1) * nk, :], preferred_element_type=F32)
            st.append(s)
            cur = s
            for r in range(k):
                m = jnp.max(cur, axis=0, keepdims=True)
                top_ref[p * k + r:p * k + r + 1, :] = m
                cur = jnp.where(cur == m, neg, cur)
        v1 = top_ref[0:k, :]
        v2 = top_ref[k:2 * k, :]
        row = 0
        for a, nb in enumerate(_CAND_ROWS):
            cand_ref[row:row + nb, :] = v1[a:a + 1, :] + v2[0:nb, :]
            row += nb
        cur = cand_ref[...]
        cmax = thr = zsum = None
        for r in range(k):
            m = jnp.max(cur, axis=0, keepdims=True)
            if r == 0:
                cmax, zsum = m, jnp.ones_like(m)
            else:
                zsum = zsum + jnp.exp(m - cmax)
            thr = m
            cur = jnp.where(cur == m, neg, cur)
        tau = jnp.full((nk, tm), pos, F32)
        row = 0
        for a, nb in enumerate(_CAND_ROWS):
            sel = cand_ref[row:row + nb, :] >= thr
            tau_a = jnp.min(jnp.where(sel, v2[0:nb, :], pos), axis=0, keepdims=True)
            tau = jnp.where(st[0] == v1[a:a + 1, :], tau_a, tau)
            row += nb
        rows = slice(h * nk, (h + 1) * nk)
        e2 = jnp.exp(st[1] - v2[0:1, :])
        for tt in range(tm // LANE):
            s2_ref[tt, rows, :] = st[1][:, tt * LANE:(tt + 1) * LANE]
            e2_ref[tt, rows, :] = e2[:, tt * LANE:(tt + 1) * LANE]
        tau_ref[rows, :] = tau
        e1_ref[rows, :] = jnp.exp(st[0] - v1[0:1, :]) / zsum


def _peer_route(hxt, wq_t, keys):
    d, t = hxt.shape
    nq = wq_t.shape[0]
    tm = PEER_ROUTE_TILE
    rows = PEER_HEADS * N_KEYS
    return pl.pallas_call(
        _route_kernel, grid=(t // tm,),
        in_specs=[pl.BlockSpec((d, tm), lambda i: (0, i)),
                  pl.BlockSpec(wq_t.shape, lambda i: (0, 0)),
                  pl.BlockSpec(keys.shape, lambda i: (0, 0, 0))],
        out_specs=[pl.BlockSpec((tm // LANE, rows, LANE), lambda i: (i, 0, 0))] * 2
        + [pl.BlockSpec((rows, tm), lambda i: (0, i))] * 2,
        out_shape=[jax.ShapeDtypeStruct((t // LANE, rows, LANE), F32)] * 2
        + [jax.ShapeDtypeStruct((rows, t), F32)] * 2,
        scratch_shapes=[pltpu.VMEM((nq, tm), BF16), pltpu.VMEM((2 * PEER_TOPK, tm), F32),
                        pltpu.VMEM((_CAND_PAD, tm), F32)],
        compiler_params=_cparams("parallel"), name="peer_route",
    )(hxt, wq_t, keys)


def _peer_kernel(ht_ref, s2_ref, e2_ref, tau_ref, e1_ref, u_ref, vt_ref, o_ref,
                 acc_ref, act0_ref, act1_ref, g0_ref, g1_ref):
    e = pl.program_id(1)
    n_blk = pl.num_programs(1) - 2
    blk = jnp.clip(e - 1, 0, n_blk - 1)
    eb = u_ref.shape[0]
    tm = ht_ref.shape[1]
    nk = N_KEYS

    @pl.when(e == 0)
    def _():
        acc_ref[...] = jnp.zeros(acc_ref.shape, F32)
        act1_ref[...] = jnp.zeros(act1_ref.shape, F32)
        g0_ref[...] = jnp.zeros(g0_ref.shape, BF16)

    @pl.when(e % 2 == 0)
    def _():
        _peer_step(ht_ref, s2_ref, e2_ref, tau_ref, e1_ref, u_ref, vt_ref, acc_ref,
                   act0_ref, act1_ref, g1_ref, g0_ref, blk)

    @pl.when(e % 2 == 1)
    def _():
        _peer_step(ht_ref, s2_ref, e2_ref, tau_ref, e1_ref, u_ref, vt_ref, acc_ref,
                   act1_ref, act0_ref, g0_ref, g1_ref, blk)

    @pl.when(e == n_blk + 1)
    def _():
        o_ref[...] = acc_ref[...].T


def _peer_step(ht_ref, s2_ref, e2_ref, tau_ref, e1_ref, u_ref, vt_ref, acc_ref,
               act_w_ref, act_r_ref, g_w_ref, g_r_ref, blk):
    eb = u_ref.shape[0]
    tm = ht_ref.shape[1]
    nk = N_KEYS
    se = eb // PEER_SUB
    sa = acc_ref.shape[0] // PEER_SUB

    def sub(k, carry):
        ar = pl.multiple_of(k * sa, sa)
        er = pl.multiple_of(k * se, se)
        acc_ref[pl.ds(ar, sa), :] += jnp.dot(vt_ref[0, pl.ds(ar, sa), :], g_r_ref[...],
                                             preferred_element_type=F32)
        act_w_ref[pl.ds(er, se), :] = jnp.dot(u_ref[pl.ds(er, se), :], ht_ref[...],
                                              preferred_element_type=F32)
        act = act_r_ref[pl.ds(er, se), :]
        act = 0.5 * act * (1.0 + lax.erf(act * 0.7071067811865476))
        for il in range(se // nk):
            i1 = blk * (eb // nk) + k * (se // nk) + il
            tau = [tau_ref[pl.ds(h * nk + i1, 1), :] for h in range(PEER_HEADS)]
            e1 = [e1_ref[pl.ds(h * nk + i1, 1), :] for h in range(PEER_HEADS)]
            for tt in range(tm // LANE):
                ln = slice(tt * LANE, (tt + 1) * LANE)
                for r0 in range(0, nk, PEER_ROW_TILE):
                    w = None
                    for h in range(PEER_HEADS):
                        hk = slice(h * nk + r0, h * nk + r0 + PEER_ROW_TILE)
                        term = jnp.where(s2_ref[tt, hk, :] >= tau[h][:, ln], e2_ref[tt, hk, :], 0.0) * e1[h][:, ln]
                        w = term if w is None else w + term
                    lo = il * nk + r0
                    g_w_ref[pl.ds(er + lo, PEER_ROW_TILE), ln] = (
                        w * act[lo:lo + PEER_ROW_TILE, ln]).astype(BF16)
        return carry

    lax.fori_loop(0, PEER_SUB, sub, 0)


def _peer_dense(hxt, route, u, vt):
    d, t = hxt.shape
    n_exp = u.shape[0]
    tm, eb = PEER_TOK_TILE, PEER_EXP_TILE
    rows = route[2].shape[0]
    n_blk = n_exp // eb
    return pl.pallas_call(
        _peer_kernel, grid=(t // tm, n_blk + 2),
        in_specs=[pl.BlockSpec((d, tm), lambda i, e: (0, i))]
        + [pl.BlockSpec((tm // LANE, rows, LANE), lambda i, e: (i, 0, 0))] * 2
        + [pl.BlockSpec((rows, tm), lambda i, e: (0, i))] * 2
        + [pl.BlockSpec((eb, d), lambda i, e: (jnp.minimum(e, n_blk - 1), 0)),
           pl.BlockSpec((1, d, eb), lambda i, e: (jnp.clip(e - 2, 0, n_blk - 1), 0, 0))],
        out_specs=pl.BlockSpec((tm, d), lambda i, e: (i, 0)),
        out_shape=jax.ShapeDtypeStruct((t, d), F32),
        scratch_shapes=[pltpu.VMEM((d, tm), F32), pltpu.VMEM((eb, tm), F32), pltpu.VMEM((eb, tm), F32),
                        pltpu.VMEM((eb, tm), BF16), pltpu.VMEM((eb, tm), BF16)],
        compiler_params=_cparams("parallel", "arbitrary"), name="peer_dense",
    )(hxt, *route, u, vt)


def _final_kernel(x_ref, f_ref, mod_ref, gain_ref, o_ref):
    x = x_ref[0] + mod_ref[0, 0, 5:6, :] * f_ref[0]
    o_ref[0] = x * lax.rsqrt(jnp.mean(x * x, axis=-1, keepdims=True) + EPS) * gain_ref[...]


def _final(x, ffn, mod, gain):
    b, s, d = x.shape
    tok = pl.BlockSpec((1, TOK_TILE, d), lambda i, t: (i, t, 0))
    return pl.pallas_call(
        _final_kernel, grid=(b, s // TOK_TILE),
        in_specs=[tok, tok, pl.BlockSpec((1, 1, 6, d), lambda i, t: (i, 1, 0, 0)),
                  pl.BlockSpec((1, d), lambda i, t: (0, 0))],
        out_specs=tok, out_shape=jax.ShapeDtypeStruct((b, s, d), F32),
        compiler_params=_cparams("parallel", "parallel"), name="final_norm",
    )(x, ffn, mod, gain)


def _rope_tables(seq):
    pos = jnp.arange(seq)
    rc = jnp.stack([pos // GRID_W, pos % GRID_W], axis=-1).astype(F32)
    inv = jnp.power(ROPE_BASE, -jnp.arange(ROPE_PAIRS, dtype=F32) / ROPE_PAIRS)
    ang = rc[:, :, None] * inv
    cos = jnp.concatenate([jnp.cos(ang)] * 2, axis=-1).reshape(seq, HEAD_DIM)
    sin = jnp.concatenate([-jnp.sin(ang), jnp.sin(ang)], axis=-1).reshape(seq, HEAD_DIM)
    return jnp.concatenate([cos] * 2, axis=1), jnp.concatenate([sin] * 2, axis=1)


def _pair_lanes(cols, n_heads):
    rows = cols.shape[0]
    c = cols.reshape(rows, 4, n_heads // 2, 2).transpose(0, 2, 1, 3).reshape(rows, n_heads // 2, 8)
    return jnp.pad(c, ((0, 0), (0, 0), (0, LANE - 8))).reshape(rows, n_heads // 2 * LANE)


def kernel(x, c, ctx, c_ctx, w_ada, b_ada, norm1, norm2, w_in, conv_w, a_log, dt_bias, gdn_norm, w_pool,
           pool_scale, sink, w_out, peer_wq, peer_keys, peer_u, peer_v, norm_f):
    b, seq, d = x.shape
    n_ctx = ctx.shape[1]
    depth = w_ada.shape[0]
    pool_w = d // 4
    gdn_w = 3 * d // 8
    gdn_heads = gdn_w // HEAD_DIM
    attn_w = d - pool_w - gdn_w
    kv_w = KV_HEADS * HEAD_DIM
    assert n_ctx % TOK_TILE == 0 and seq % TOK_TILE == 0 and gdn_heads % 2 == 0
    assert (b * (n_ctx + seq)) % PEER_TOK_TILE == 0 and (b * seq) % PEER_TOK_TILE == 0

    rows = -(-(b + 1) // SUBLANE) * SUBLANE
    c_all = jnp.zeros((rows, d), F32).at[:b].set(c).at[b].set(c_ctx)
    mod = _ada_mod(c_all, w_ada, b_ada)
    mod_x = mod[:, :b].reshape(depth, b, 1, 6, d)
    mod_c = jnp.broadcast_to(mod[:, b].reshape(depth, 1, 1, 6, d), (depth, b, 1, 6, d))
    mod = jnp.concatenate([mod_c, mod_x], axis=2)

    cos_t, sin_t = _rope_tables(seq)
    splits = (pool_w, 3 * gdn_w, gdn_w, gdn_heads // 2 * LANE, attn_w, kv_w, kv_w)
    o_ba = pool_w + 4 * gdn_w
    n_ba = 4 * gdn_heads

    h = jnp.concatenate([ctx, x], axis=1)
    ffn = None
    for l in range(depth):
        last = l == depth - 1
        w_l = w_in[l]
        w_big = jnp.concatenate([w_l[:, :o_ba], _pair_lanes(w_l[:, o_ba:o_ba + n_ba], gdn_heads),
                                 w_l[:, o_ba + n_ba:]], axis=1).astype(BF16)
        pmod = mod[l - 1] if l > 0 else None
        outs = _in_proj(h, ffn, pmod, mod[l], norm1[l][None], w_big, splits, n_ctx)
        if l > 0:
            h, *outs = outs
        pa, qkv, z, ba, aq, ak, av = outs

        wbd = jax.scipy.linalg.block_diag(*[w_pool[l, g] for g in range(len(POOL_WINDOWS))])
        ya = _pool(pa, wbd, pool_scale[l][None], n_ctx)

        gparams = jnp.stack([_pair_lanes(jnp.concatenate([jnp.zeros_like(a_log[l]), a_log[l]]).reshape(1, -1),
                                         gdn_heads),
                             _pair_lanes(jnp.concatenate([jnp.zeros_like(dt_bias[l]), dt_bias[l]]).reshape(1, -1),
                                         gdn_heads)], axis=1)
        gparams = gparams.reshape(2, gdn_heads // 2, LANE).transpose(1, 0, 2)
        gparams = jnp.pad(gparams, ((0, 0), (0, SUBLANE - 2), (0, 0)))
        yb = _gdn(qkv, z, ba, conv_w[l], gparams, jnp.concatenate([gdn_norm[l]] * 2)[None], n_ctx)

        sink_row = jnp.pad(sink[l], (0, LANE - sink.shape[1]))[None]
        yc = _attn_x(aq, ak, av, cos_t, sin_t, sink_row, n_ctx)
        if not last:
            yc = jnp.concatenate([_attn_ctx(aq, ak, av, sink_row, n_ctx), yc], axis=1)

        x_new, hx = _out_proj(h, ya, yb, yc, w_out[l].astype(BF16), mod[l], norm2[l][None], n_ctx,
                              skip_ctx=last)
        t = hx.shape[0] * hx.shape[1]
        hxt = hx.reshape(t, d).T
        keys = peer_keys[l].reshape(2 * PEER_HEADS, N_KEYS, -1).astype(BF16)
        route = _peer_route(hxt, peer_wq[l].T.astype(BF16), keys)
        vt = peer_v[l].reshape(-1, PEER_EXP_TILE, d).transpose(0, 2, 1).astype(BF16)
        ffn = _peer_dense(hxt, route, peer_u[l].astype(BF16), vt)
        ffn = ffn.reshape(x_new.shape)
        h = x_new
    return _final(h, ffn, mod[depth - 1], norm_f[None])
```

```python
import functools

import jax
import jax.numpy as jnp
from jax import lax
from jax.experimental import pallas as pl
from jax.experimental.pallas import tpu as pltpu

F32, BF16 = jnp.float32, jnp.bfloat16
HIGHEST = lax.Precision.HIGHEST

EPS = 1e-6
NEG_INF = -1e30
HEAD_DIM = 64
GRID_W = 64
POOL_WINDOWS = (2, 4, 8, 16)
GDN_CHUNK = 64
CONV_K = 5
KV_HEADS = 2
WINDOW = 128
ATTN_BLOCK = 128
ROPE_BASE = 10000.0
ROPE_PAIRS = HEAD_DIM // 4
PEER_HEADS = 8
N_KEYS = 128
PEER_TOPK = 16

LANE = 128
SUBLANE = 8
TOK_TILE = 256
ROW_CHUNK = 256
GDN_GROUP = 4
PEER_ROUTE_TILE = 256
PEER_GATE_TILE = 128
PEER_GATE_UNROLL = 16
PEER_TOK_TILE = 1024
PEER_EXP_TILE = 512
PEER_SUB = 2
VMEM_LIMIT = 48 * 1024 * 1024

_CAND_ROWS = tuple(PEER_TOPK // (a + 1) for a in range(PEER_TOPK))
_N_CAND = sum(_CAND_ROWS)
_CAND_PAD = -(-_N_CAND // SUBLANE) * SUBLANE


def _cparams(*sem):
    return pltpu.CompilerParams(dimension_semantics=sem, vmem_limit_bytes=VMEM_LIMIT)


def _dot(a, b):
    return jnp.dot(a.astype(BF16), b.astype(BF16), preferred_element_type=F32)


def _dot_nt(a, b):
    return lax.dot_general(a.astype(BF16), b.astype(BF16), (((1,), (1,)), ((), ())),
                           preferred_element_type=F32)


def _dot_tn(a, b):
    return lax.dot_general(a.astype(BF16), b.astype(BF16), (((0,), (0,)), ((), ())),
                           preferred_element_type=F32)


def _dot_hi(a, b):
    return jnp.dot(a, b, precision=HIGHEST, preferred_element_type=F32)


def _silu(x):
    return x * jax.nn.sigmoid(x)


def _softplus(x):
    return jnp.maximum(x, 0.0) + jnp.log1p(jnp.exp(-jnp.abs(x)))


def _modnorm(x, gain, scale, shift):
    y = x * lax.rsqrt(jnp.mean(x * x, axis=-1, keepdims=True) + EPS) * gain
    return y * (1.0 + scale) + shift


def _ada_kernel(c_ref, w_ref, b_ref, o_ref):
    o_ref[0] = _dot_hi(_silu(c_ref[...]), w_ref[0]) + b_ref[0]


def _ada_mod(c_all, w_ada, b_ada):
    depth, d, n = w_ada.shape
    rows = c_all.shape[0]
    tn = n // 4
    return pl.pallas_call(
        _ada_kernel,
        grid=(depth, n // tn),
        in_specs=[pl.BlockSpec((rows, d), lambda l, j: (0, 0)),
                  pl.BlockSpec((1, d, tn), lambda l, j: (l, 0, j)),
                  pl.BlockSpec((1, 1, tn), lambda l, j: (l, 0, j))],
        out_specs=pl.BlockSpec((1, rows, tn), lambda l, j: (l, 0, j)),
        out_shape=jax.ShapeDtypeStruct((depth, rows, n), F32),
        compiler_params=_cparams("parallel", "parallel"),
        name="ada_mod",
    )(c_all, w_ada, b_ada.reshape(depth, 1, n))


def _inproj_kernel(*refs, has_ffn, col_splits):
    if has_ffn:
        h_ref, f_ref, pmod_ref, mod_ref, gain_ref, w_ref, res_ref, *outs = refs
        x = h_ref[0] + pmod_ref[0, 0, 5:6, :] * f_ref[0]
        res_ref[0] = x
    else:
        h_ref, mod_ref, gain_ref, w_ref, *outs = refs
        x = h_ref[0]
    hb = _modnorm(x, gain_ref[...], mod_ref[0, 0, 1:2, :], mod_ref[0, 0, 0:1, :]).astype(BF16)
    off = 0
    for o_ref, width in zip(outs, col_splits):
        o_ref[0] = jnp.dot(hb, w_ref[:, off:off + width], preferred_element_type=F32)
        off += width


def _in_proj(h, ffn, pmod, mod, gain, w, col_splits, n_ctx):
    b, l, d = h.shape
    nt = l // TOK_TILE
    nct = n_ctx // TOK_TILE
    tok = pl.BlockSpec((1, TOK_TILE, d), lambda i, t: (i, t, 0))
    modspec = pl.BlockSpec((1, 1, 6, d), lambda i, t: (i, (t >= nct).astype(jnp.int32), 0, 0))
    has_ffn = ffn is not None
    in_specs = [tok] + ([tok, modspec] if has_ffn else []) + [
        modspec, pl.BlockSpec((1, d), lambda i, t: (0, 0)), pl.BlockSpec(w.shape, lambda i, t: (0, 0))]
    out_shape = [jax.ShapeDtypeStruct((b, l, wd), F32) for wd in col_splits]
    out_specs = [pl.BlockSpec((1, TOK_TILE, wd), lambda i, t: (i, t, 0)) for wd in col_splits]
    if has_ffn:
        out_shape = [jax.ShapeDtypeStruct((b, l, d), F32)] + out_shape
        out_specs = [tok] + out_specs
    args = (h, ffn, pmod, mod, gain, w) if has_ffn else (h, mod, gain, w)
    return pl.pallas_call(
        functools.partial(_inproj_kernel, has_ffn=has_ffn, col_splits=col_splits),
        grid=(b, nt), in_specs=in_specs, out_specs=out_specs, out_shape=out_shape,
        compiler_params=_cparams("parallel", "parallel"), name="in_proj",
    )(*args)


def _pool_kernel(a_ref, wbd_ref, scale_ref, o_ref, pad_ref, *, segments):
    c = a_ref.shape[-1]
    grp = lax.broadcasted_iota(jnp.int32, (1, c), 1) // (c // len(POOL_WINDOWS))
    zeros = jnp.zeros((SUBLANE, c), F32)
    for off, n in segments:
        pad_ref[0:SUBLANE, :] = zeros
        pad_ref[SUBLANE:SUBLANE + n, :] = a_ref[0, off:off + n, :]
        pad_ref[SUBLANE + n:2 * SUBLANE + n, :] = zeros

        def chunk(ci, carry, off=off, n=n):
            r0 = pl.multiple_of(ci * ROW_CHUNK, ROW_CHUNK)
            win = pad_ref[pl.ds(r0, ROW_CHUNK + 2 * SUBLANE), :]
            sh = lambda s: win[SUBLANE + s:SUBLANE + s + ROW_CHUNK, :]
            a0 = sh(0)
            s2 = sh(-1) + a0
            s4 = s2 + sh(-2) + sh(1)
            s8 = s4 + sh(-4) + sh(-3) + sh(2) + sh(3)
            s16 = s8 + (sh(-8) + sh(-7) + sh(-6) + sh(-5)) + (sh(4) + sh(5) + sh(6) + sh(7))
            t = r0 + lax.broadcasted_iota(jnp.int32, (ROW_CHUNK, 1), 0)

            def mean(s, w):
                cnt = jnp.minimum(t + (w - w // 2), n) - jnp.maximum(t - w // 2, 0)
                return s / cnt.astype(F32)

            pooled = jnp.where(grp == 0, mean(s2, 2), jnp.where(grp == 1, mean(s4, 4),
                               jnp.where(grp == 2, mean(s8, 8), mean(s16, 16)))) - a0
            o_ref[0, pl.ds(off + r0, ROW_CHUNK), :] = _dot_hi(pooled, wbd_ref[...]) * scale_ref[...]
            return carry

        lax.fori_loop(0, n // ROW_CHUNK, chunk, 0)


def _pool(a, wbd, scale, n_ctx):
    b, l, c = a.shape
    segments = ((0, n_ctx), (n_ctx, l - n_ctx))
    return pl.pallas_call(
        functools.partial(_pool_kernel, segments=segments),
        grid=(b,),
        in_specs=[pl.BlockSpec((1, l, c), lambda i: (i, 0, 0)),
                  pl.BlockSpec((c, c), lambda i: (0, 0)),
                  pl.BlockSpec((1, c), lambda i: (0, 0))],
        out_specs=pl.BlockSpec((1, l, c), lambda i: (i, 0, 0)),
        out_shape=jax.ShapeDtypeStruct((b, l, c), F32),
        scratch_shapes=[pltpu.VMEM((l - n_ctx + 2 * SUBLANE, c), F32)],
        compiler_params=_cparams("parallel"), name="pool",
    )(a, wbd, scale)


def _gdn_kernel(q_ref, k_ref, v_ref, z_ref, ba_ref, cwq_ref, cwk_ref, cwv_ref, gp_ref, gain_ref, o_ref,
                pad_ref, yq_ref, yk_ref, yv_ref, bg_ref, of_ref, ob_ref, s_ref,
                qe_ref, mn_ref, oc_ref, nn_ref, egl_ref, *, n_ctx):
    l = q_ref.shape[1]
    n_x = l - n_ctx
    hd = HEAD_DIM
    lane = lax.broadcasted_iota(jnp.int32, (1, LANE), 1)
    lo = lane < hd
    zeros = jnp.zeros((SUBLANE, LANE), F32)
    x_off = 2 * SUBLANE + n_ctx

    def head_scale(ss):
        s_lo = jnp.sum(jnp.where(lo, ss, 0.0), axis=-1, keepdims=True)
        s_hi = jnp.sum(jnp.where(lo, 0.0, ss), axis=-1, keepdims=True)
        return jnp.where(lo, s_lo, s_hi)

    def conv(u_ref, cw_ref, y_ref, post):
        pad_ref[0:SUBLANE, :] = zeros
        pad_ref[SUBLANE:SUBLANE + n_ctx, :] = u_ref[0, 0:n_ctx, :]
        pad_ref[SUBLANE + n_ctx:x_off, :] = zeros
        pad_ref[x_off:x_off + n_x, :] = u_ref[0, n_ctx:l, :]
        pad_ref[x_off + n_x:x_off + n_x + SUBLANE, :] = zeros
        cw = cw_ref[...]
        for poff, yoff, n in ((SUBLANE, 0, n_ctx), (x_off, n_ctx, n_x)):
            def chunk(ci, carry, poff=poff, yoff=yoff):
                r0 = pl.multiple_of(ci * ROW_CHUNK, ROW_CHUNK)
                win = pad_ref[pl.ds(poff - SUBLANE + r0, ROW_CHUNK + 2 * SUBLANE), :]
                acc = win[SUBLANE - 2:SUBLANE - 2 + ROW_CHUNK, :] * cw[0:1, :]
                for j in range(1, CONV_K):
                    acc = acc + win[SUBLANE - 2 + j:SUBLANE - 2 + j + ROW_CHUNK, :] * cw[j:j + 1, :]
                y_ref[pl.ds(yoff + r0, ROW_CHUNK), :] = post(_silu(acc))
                return carry
            lax.fori_loop(0, n // ROW_CHUNK, chunk, 0)

    l2 = lambda y: y * lax.rsqrt(head_scale(y * y) + EPS)
    conv(q_ref, cwq_ref, yq_ref, lambda y: l2(y) * (hd ** -0.5))
    conv(k_ref, cwk_ref, yk_ref, l2)
    conv(v_ref, cwv_ref, yv_ref, lambda y: y)

    ba = ba_ref[0]
    g = -jnp.exp(gp_ref[0, 0:1, :]) * _softplus(ba + gp_ref[0, 1:2, :])
    bg_ref[...] = jnp.where(lane < 4, jax.nn.sigmoid(ba), g)

    ck = GDN_CHUNK
    rr = lax.broadcasted_iota(jnp.int32, (ck, ck), 0)
    cc = lax.broadcasted_iota(jnp.int32, (ck, ck), 1)
    eye = (rr == cc).astype(F32)
    tril = (rr >= cc).astype(F32)
    n_chunks = l // ck
    nc_ctx = n_ctx // ck

    def phase1(ci, carry):
        probs = []
        for gi in range(GDN_GROUP):
            c = ci * GDN_GROUP + gi
            r0 = pl.multiple_of(c * ck, ck)
            qc = yq_ref[pl.ds(r0, ck), :]
            kc = yk_ref[pl.ds(r0, ck), :]
            vc = yv_ref[pl.ds(r0, ck), :]
            bgc = bg_ref[pl.ds(r0, ck), :]
            gcf = _dot_hi(tril, bgc)
            gcb = gcf[ck - 1:ck, :] - gcf + bgc
            gct = (gcf.T, gcb.T)
            for j in range(2):
                qh = qc[:, j * hd:(j + 1) * hd]
                kh = kc[:, j * hd:(j + 1) * hd]
                vh = vc[:, j * hd:(j + 1) * hd]
                kk = _dot_nt(kh, kh)
                qk = _dot_nt(qh, kh)
                for d in range(2):
                    mask = (rr >= cc) if d == 0 else (rr <= cc)
                    smask = (rr > cc) if d == 0 else (rr < cc)
                    lg, lb = 4 + 2 * d + j, 2 * d + j
                    gcol = (gcf, gcb)[d][:, lg:lg + 1]
                    grow = gct[d][lg:lg + 1, :]
                    bcol = bgc[:, lb:lb + 1]
                    decay = jnp.where(mask, jnp.exp(jnp.where(mask, gcol - grow, 0.0)), 0.0)
                    xp = jnp.where(smask, -(bcol * kk * decay), 0.0)
                    eg = jnp.exp(gcol)
                    last = ck - 1 if d == 0 else 0
                    glast = gcol[last:last + 1, :]
                    probs.append(dict(
                        idx=c * 4 + 2 * d + j, xp=xp, inv=eye + xp, attn=qk * decay, qd=qh * eg,
                        rhs=jnp.concatenate([vh * bcol, kh * bcol * eg], axis=1),
                        kd=kh * jnp.exp(glast - gcol), egl=jnp.exp(glast)))
        for _ in range(5):
            for p in probs:
                p["xp"] = _dot(p["xp"], p["xp"])
            for p in probs:
                p["inv"] = p["inv"] + _dot(p["inv"], p["xp"])
        for p in probs:
            p["sol"] = _dot(p["inv"], p["rhs"])
        for p in probs:
            p["as"] = _dot(p["attn"], p["sol"])
        for p in probs:
            p["ks"] = _dot_tn(p["kd"], p["sol"])
        for p in probs:
            i = p["idx"]
            qe_ref[i] = (p["qd"] - p["as"][:, hd:]).astype(BF16)
            oc_ref[i] = p["as"][:, :hd]
            mn_ref[i] = p["ks"][:, hd:].astype(BF16)
            nn_ref[i] = p["ks"][:, :hd]
            egl_ref[i] = jnp.broadcast_to(p["egl"], (ck, hd))
        return carry

    lax.fori_loop(0, n_chunks // GDN_GROUP, phase1, 0)

    s_ref[...] = jnp.zeros(s_ref.shape, F32)

    def phase2(i, carry):
        cb = jnp.where(i < nc_ctx, nc_ctx - 1 - i, n_chunks - 1 - (i - nc_ctx))
        work = [(d, j, cidx * 4 + 2 * d + j) for d, cidx in ((0, i), (1, cb)) for j in range(2)]
        states = [s_ref[2 * d + j] for d, j, _ in work]
        outs = [_dot(qe_ref[idx], s) + oc_ref[idx] for (_, _, idx), s in zip(work, states)]
        upd = [_dot(mn_ref[idx], s) for (_, _, idx), s in zip(work, states)]
        for (d, j, idx), s, m in zip(work, states, upd):
            s_ref[2 * d + j] = egl_ref[idx] * s - m + nn_ref[idx]
        of_ref[pl.ds(pl.multiple_of(i * ck, ck), ck), :] = jnp.concatenate(outs[0:2], axis=1)
        ob_ref[pl.ds(pl.multiple_of(cb * ck, ck), ck), :] = jnp.concatenate(outs[2:4], axis=1)
        return carry

    lax.fori_loop(0, n_chunks, phase2, 0)

    def finish(ci, carry):
        r0 = pl.multiple_of(ci * ROW_CHUNK, ROW_CHUNK)
        o = of_ref[pl.ds(r0, ROW_CHUNK), :] + ob_ref[pl.ds(r0, ROW_CHUNK), :]
        o = o * lax.rsqrt(head_scale(o * o) * (1.0 / hd) + EPS) * gain_ref[...]
        o_ref[0, pl.ds(r0, ROW_CHUNK), :] = o * _silu(z_ref[0, pl.ds(r0, ROW_CHUNK), :])
        return carry

    lax.fori_loop(0, l // ROW_CHUNK, finish, 0)


def _gdn(qkv, z, ba, conv_w, gparams, gain2, n_ctx):
    b, l, w3 = qkv.shape
    npair = w3 // (3 * LANE)
    seq = lambda off: pl.BlockSpec((1, l, LANE), lambda i, p: (i, 0, p + off))
    cw = lambda off: pl.BlockSpec((CONV_K, LANE), lambda i, p: (0, p + off))
    return pl.pallas_call(
        functools.partial(_gdn_kernel, n_ctx=n_ctx),
        grid=(b, npair),
        in_specs=[seq(0), seq(npair), seq(2 * npair), seq(0), seq(0),
                  cw(0), cw(npair), cw(2 * npair),
                  pl.BlockSpec((1, SUBLANE, LANE), lambda i, p: (p, 0, 0)),
                  pl.BlockSpec((1, LANE), lambda i, p: (0, 0))],
        out_specs=seq(0),
        out_shape=jax.ShapeDtypeStruct((b, l, npair * LANE), F32),
        scratch_shapes=[pltpu.VMEM((l + 3 * SUBLANE, LANE), F32)] + [pltpu.VMEM((l, LANE), F32)] * 6
        + [pltpu.VMEM((4, HEAD_DIM, HEAD_DIM), F32)]
        + [pltpu.VMEM((4 * l // GDN_CHUNK, GDN_CHUNK, HEAD_DIM), dt) for dt in (BF16, BF16, F32, F32, F32)],
        compiler_params=_cparams("parallel", "parallel"), name="gdn",
    )(qkv, qkv, qkv, z, ba, conv_w, conv_w, conv_w, gparams, gain2)


def _rope(x, cos, sin):
    w = x.shape[-1]
    reps = w // cos.shape[-1]
    if reps > 1:
        cos = jnp.concatenate([cos] * reps, axis=1)
        sin = jnp.concatenate([sin] * reps, axis=1)
    first = (lax.broadcasted_iota(jnp.int32, (1, w), 1) % (2 * ROPE_PAIRS)) < ROPE_PAIRS
    partner = jnp.where(first, pltpu.roll(x, w - ROPE_PAIRS, 1), pltpu.roll(x, ROPE_PAIRS, 1))
    return x * cos + partner * sin


def _attend(q, k, v, valid, sink_col):
    s = _dot_nt(q, k)
    if valid is not None:
        s = jnp.where(valid, s, NEG_INF)
    m = jnp.maximum(jnp.max(s, axis=-1, keepdims=True), sink_col)
    p = jnp.exp(s - m)
    den = jnp.sum(p, axis=-1, keepdims=True) + jnp.exp(sink_col - m)
    return _dot(p, v) / den


def _gqa(q, k_all, v_all, valid, sink_ref, o_ref):
    nq = q.shape[0]
    hd = HEAD_DIM
    group = q.shape[1] // hd // KV_HEADS
    outs = []
    for kvh in range(KV_HEADS):
        heads = [kvh * group + g for g in range(group)]
        qg = jnp.concatenate([q[:, h * hd:(h + 1) * hd] for h in heads], axis=0)
        sink_col = jnp.concatenate(
            [jnp.broadcast_to(sink_ref[0:1, h:h + 1], (nq, 1)) for h in heads], axis=0)
        o = _attend(qg, k_all[:, kvh * hd:(kvh + 1) * hd], v_all[:, kvh * hd:(kvh + 1) * hd],
                    valid, sink_col)
        outs += [o[g * nq:(g + 1) * nq, :] for g in range(group)]
    o_ref[0] = jnp.concatenate(outs, axis=1)


def _attn_kernel(q_ref, kp_ref, ko_ref, kn_ref, vp_ref, vo_ref, vn_ref, kc_ref, vc_ref,
                 cp_ref, sp_ref, co_ref, so_ref, cn_ref, sn_ref, sink_ref, o_ref, *, seq):
    i = pl.program_id(1)
    blk = ATTN_BLOCK
    n_ctx = kc_ref.shape[1]
    q = _rope(q_ref[0], co_ref[...], so_ref[...]) * (HEAD_DIM ** -0.5)
    k_all = jnp.concatenate([_rope(kp_ref[0], cp_ref[...], sp_ref[...]),
                             _rope(ko_ref[0], co_ref[...], so_ref[...]),
                             _rope(kn_ref[0], cn_ref[...], sn_ref[...]), kc_ref[0]], axis=0)
    v_all = jnp.concatenate([vp_ref[0], vo_ref[0], vn_ref[0], vc_ref[0]], axis=0)
    group = q.shape[1] // HEAD_DIM // KV_HEADS
    nk = 3 * blk + n_ctx
    qo = lax.broadcasted_iota(jnp.int32, (group * blk, 1), 0) % blk
    ko = lax.broadcasted_iota(jnp.int32, (1, nk), 1)
    kpos = (i - 1) * blk + ko
    valid = ((jnp.abs(ko - blk - qo) <= WINDOW) & (kpos >= 0) & (kpos < seq)) | (ko >= 3 * blk)
    _gqa(q, k_all, v_all, valid, sink_ref, o_ref)


def _attn_x(aq, ak, av, cos_t, sin_t, sink_row, n_ctx):
    b, l, wq = aq.shape
    wk = ak.shape[-1]
    blk = ATTN_BLOCK
    seq = l - n_ctx
    nb = seq // blk
    cb = n_ctx // blk
    kv = lambda f: pl.BlockSpec((1, blk, wk), lambda i, t: (i, f(t) + cb, 0))
    tab = lambda f: pl.BlockSpec((blk, LANE), lambda i, t: (f(t), 0))
    prev = lambda t: jnp.maximum(t - 1, 0)
    own = lambda t: t
    nxt = lambda t: jnp.minimum(t + 1, nb - 1)
    ctx = pl.BlockSpec((1, n_ctx, wk), lambda i, t: (i, 0, 0))
    return pl.pallas_call(
        functools.partial(_attn_kernel, seq=seq),
        grid=(b, nb),
        in_specs=[pl.BlockSpec((1, blk, wq), lambda i, t: (i, t + cb, 0)),
                  kv(prev), kv(own), kv(nxt), kv(prev), kv(own), kv(nxt), ctx, ctx,
                  tab(prev), tab(prev), tab(own), tab(own), tab(nxt), tab(nxt),
                  pl.BlockSpec((1, LANE), lambda i, t: (0, 0))],
        out_specs=pl.BlockSpec((1, blk, wq), lambda i, t: (i, t, 0)),
        out_shape=jax.ShapeDtypeStruct((b, seq, wq), F32),
        compiler_params=_cparams("parallel", "parallel"), name="attn_x",
    )(aq, ak, ak, ak, av, av, av, ak, av, cos_t, sin_t, cos_t, sin_t, cos_t, sin_t, sink_row)


def _attn_ctx_kernel(q_ref, k_ref, v_ref, sink_ref, o_ref):
    _gqa(q_ref[0] * (HEAD_DIM ** -0.5), k_ref[0], v_ref[0], None, sink_ref, o_ref)


def _attn_ctx(aq, ak, av, sink_row, n_ctx):
    b, _, wq = aq.shape
    wk = ak.shape[-1]
    spec = lambda w: pl.BlockSpec((1, n_ctx, w), lambda i: (i, 0, 0))
    return pl.pallas_call(
        _attn_ctx_kernel, grid=(b,),
        in_specs=[spec(wq), spec(wk), spec(wk), pl.BlockSpec((1, LANE), lambda i: (0, 0))],
        out_specs=spec(wq), out_shape=jax.ShapeDtypeStruct((b, n_ctx, wq), F32),
        compiler_params=_cparams("parallel"), name="attn_ctx",
    )(aq, ak, av, sink_row)


def _outproj_kernel(x_ref, ya_ref, yb_ref, yc_ref, w_ref, mod_ref, gain_ref, xo_ref, hx_ref):
    wa, wb = ya_ref.shape[-1], yb_ref.shape[-1]
    acc = (jnp.dot(ya_ref[0].astype(BF16), w_ref[0:wa, :], preferred_element_type=F32)
           + jnp.dot(yb_ref[0].astype(BF16), w_ref[wa:wa + wb, :], preferred_element_type=F32)
           + jnp.dot(yc_ref[0].astype(BF16), w_ref[wa + wb:, :], preferred_element_type=F32))
    x = x_ref[0] + mod_ref[0, 0, 2:3, :] * acc
    xo_ref[0] = x
    hx_ref[0] = _modnorm(x, gain_ref[...], mod_ref[0, 0, 4:5, :], mod_ref[0, 0, 3:4, :]).astype(BF16)


def _out_proj(h, ya, yb, yc, w, mod, gain, n_ctx, skip_ctx):
    b, l, d = h.shape
    nct = n_ctx // TOK_TILE
    off = nct if skip_ctx else 0
    nt = l // TOK_TILE - off
    full = lambda wd: pl.BlockSpec((1, TOK_TILE, wd), lambda i, t: (i, t + off, 0))
    own = lambda wd: pl.BlockSpec((1, TOK_TILE, wd), lambda i, t: (i, t, 0))
    return pl.pallas_call(
        _outproj_kernel, grid=(b, nt),
        in_specs=[full(d), full(ya.shape[-1]), full(yb.shape[-1]), own(yc.shape[-1]),
                  pl.BlockSpec(w.shape, lambda i, t: (0, 0)),
                  pl.BlockSpec((1, 1, 6, d), lambda i, t: (i, (t + off >= nct).astype(jnp.int32), 0, 0)),
                  pl.BlockSpec((1, d), lambda i, t: (0, 0))],
        out_specs=[own(d), own(d)],
        out_shape=[jax.ShapeDtypeStruct((b, nt * TOK_TILE, d), F32),
                   jax.ShapeDtypeStruct((b, nt * TOK_TILE, d), BF16)],
        compiler_params=_cparams("parallel", "parallel"), name="out_proj",
    )(h, ya, yb, yc, w, mod, gain)


def _route_kernel(ht_ref, wq_ref, keys_ref, r2_ref, e2_ref, n1_ref, e1_ref, qt_ref, top_ref, cand_ref):
    tm = ht_ref.shape[1]
    k = PEER_TOPK
    nk = N_KEYS
    neg = jnp.float32(-jnp.inf)
    qt_ref[...] = _dot(wq_ref[...], ht_ref[...]).astype(BF16)
    cand_ref[_N_CAND:_CAND_PAD, :] = jnp.full((_CAND_PAD - _N_CAND, tm), neg, F32)
    for h in range(PEER_HEADS):
        st = []
        rank2 = jnp.full((nk, tm), float(k), F32)
        for p in range(2):
            hp = 2 * h + p
            s = jnp.dot(keys_ref[hp], qt_ref[hp * nk:(hp + 1) * nk, :], preferred_element_type=F32)
            st.append(s)
            cur = s
            for r in range(k):
                m = jnp.max(cur, axis=0, keepdims=True)
                top_ref[p * k + r:p * k + r + 1, :] = m
                hit = cur == m
                if p == 1:
                    rank2 = jnp.where(hit, float(r), rank2)
                cur = jnp.where(hit, neg, cur)
        v1 = top_ref[0:k, :]
        v2 = top_ref[k:2 * k, :]
        row = 0
        for a, nb in enumerate(_CAND_ROWS):
            cand_ref[row:row + nb, :] = v1[a:a + 1, :] + v2[0:nb, :]
            row += nb
        cur = cand_ref[...]
        cmax = thr = zsum = None
        for r in range(k):
            m = jnp.max(cur, axis=0, keepdims=True)
            if r == 0:
                cmax, zsum = m, jnp.ones_like(m)
            else:
                zsum = zsum + jnp.exp(m - cmax)
            thr = m
            cur = jnp.where(cur == m, neg, cur)
        n1 = jnp.zeros((nk, tm), F32)
        row = 0
        for a, nb in enumerate(_CAND_ROWS):
            sel = cand_ref[row:row + nb, :] >= thr
            n_a = jnp.sum(sel.astype(F32), axis=0, keepdims=True)
            n1 = jnp.where(st[0] == v1[a:a + 1, :], n_a, n1)
            row += nb
        rows = slice(h * nk, (h + 1) * nk)
        r2_ref[rows, :] = rank2
        e2_ref[rows, :] = jnp.exp(st[1] - v2[0:1, :])
        n1_ref[rows, :] = n1
        e1_ref[rows, :] = jnp.exp(st[0] - v1[0:1, :]) / zsum


def _peer_route(hxt, wq_t, keys):
    d, t = hxt.shape
    nq = wq_t.shape[0]
    tm = PEER_ROUTE_TILE
    rows = PEER_HEADS * N_KEYS
    return pl.pallas_call(
        _route_kernel, grid=(t // tm,),
        in_specs=[pl.BlockSpec((d, tm), lambda i: (0, i)),
                  pl.BlockSpec(wq_t.shape, lambda i: (0, 0)),
                  pl.BlockSpec(keys.shape, lambda i: (0, 0, 0))],
        out_specs=[pl.BlockSpec((rows, tm), lambda i: (0, i))] * 4,
        out_shape=[jax.ShapeDtypeStruct((rows, t), F32)] * 4,
        scratch_shapes=[pltpu.VMEM((nq, tm), BF16), pltpu.VMEM((2 * PEER_TOPK, tm), F32),
                        pltpu.VMEM((_CAND_PAD, tm), F32)],
        compiler_params=_cparams("parallel"), name="peer_route",
    )(hxt, wq_t, keys)


def _gates_kernel(r2_ref, e2_ref, n1_ref, e1_ref, w_ref, r2t_ref, e2t_ref, n1t_ref, e1t_ref):
    tb = w_ref.shape[0]
    nk = N_KEYS
    k = PEER_TOPK
    for src, dst in ((r2_ref, r2t_ref), (e2_ref, e2t_ref), (n1_ref, n1t_ref), (e1_ref, e1t_ref)):
        dst[...] = src[...].T
    rank = lax.broadcasted_iota(jnp.int32, (k, nk), 0).astype(F32)

    def tok(i, carry):
        for j in range(PEER_GATE_UNROLL):
            t = i * PEER_GATE_UNROLL + j
            r2, e2, n1, e1 = (ref[pl.ds(t, 1), :] for ref in (r2t_ref, e2t_ref, n1t_ref, e1t_ref))
            a, b = [], []
            for h in range(PEER_HEADS):
                ks = slice(h * nk, (h + 1) * nk)
                a.append(jnp.where(n1[:, ks] > rank, e1[:, ks], 0.0))
                b.append(jnp.where(r2[:, ks] == rank, e2[:, ks], 0.0))
            w_ref[t] = _dot_tn(jnp.concatenate(a, axis=0), jnp.concatenate(b, axis=0)).astype(BF16)
        return carry

    lax.fori_loop(0, tb // PEER_GATE_UNROLL, tok, 0)


def _peer_gates(route):
    rows, t = route[0].shape
    tb = PEER_GATE_TILE
    return pl.pallas_call(
        _gates_kernel, grid=(t // tb,),
        in_specs=[pl.BlockSpec((rows, tb), lambda i: (0, i))] * 4,
        out_specs=pl.BlockSpec((tb, N_KEYS, N_KEYS), lambda i: (i, 0, 0)),
        out_shape=jax.ShapeDtypeStruct((t, N_KEYS, N_KEYS), BF16),
        scratch_shapes=[pltpu.VMEM((tb, rows), F32)] * 4,
        compiler_params=_cparams("parallel"), name="peer_gates",
    )(*route)


def _peer_kernel(h_ref, w_ref, ut_ref, v_ref, o_ref, acc_ref, g0_ref, g1_ref):
    e = pl.program_id(1)
    n_blk = pl.num_programs(1) - 1

    @pl.when(e == 0)
    def _():
        acc_ref[...] = jnp.zeros(acc_ref.shape, F32)
        g1_ref[...] = jnp.zeros(g1_ref.shape, BF16)

    @pl.when(e % 2 == 0)
    def _():
        _peer_step(h_ref, w_ref, ut_ref, v_ref, acc_ref, g0_ref, g1_ref)

    @pl.when(e % 2 == 1)
    def _():
        _peer_step(h_ref, w_ref, ut_ref, v_ref, acc_ref, g1_ref, g0_ref)

    @pl.when(e == n_blk)
    def _():
        o_ref[...] = acc_ref[...]


def _peer_step(h_ref, w_ref, ut_ref, v_ref, acc_ref, g_w_ref, g_r_ref):
    sr = h_ref.shape[0] // PEER_SUB

    def sub(k, carry):
        rows = pl.ds(pl.multiple_of(k * sr, sr), sr)
        acc_ref[rows, :] += jnp.dot(g_r_ref[rows, :], v_ref[...], preferred_element_type=F32)
        act = jnp.dot(h_ref[rows, :], ut_ref[0], preferred_element_type=F32)
        act = 0.5 * act * (1.0 + lax.erf(act * 0.7071067811865476))
        g_w_ref[rows, :] = (w_ref[rows, :].astype(F32) * act).astype(BF16)
        return carry

    lax.fori_loop(0, PEER_SUB, sub, 0)


def _peer_dense(hx, w, ut, v):
    t, d = hx.shape
    n_blk, _, eb = ut.shape
    tm = PEER_TOK_TILE
    last = n_blk - 1
    return pl.pallas_call(
        _peer_kernel, grid=(t // tm, n_blk + 1),
        in_specs=[pl.BlockSpec((tm, d), lambda i, e: (i, 0)),
                  pl.BlockSpec((tm, eb), lambda i, e: (i, jnp.minimum(e, last))),
                  pl.BlockSpec((1, d, eb), lambda i, e: (jnp.minimum(e, last), 0, 0)),
                  pl.BlockSpec((eb, d), lambda i, e: (jnp.maximum(e - 1, 0), 0))],
        out_specs=pl.BlockSpec((tm, d), lambda i, e: (i, 0)),
        out_shape=jax.ShapeDtypeStruct((t, d), F32),
        scratch_shapes=[pltpu.VMEM((tm, d), F32), pltpu.VMEM((tm, eb), BF16), pltpu.VMEM((tm, eb), BF16)],
        compiler_params=_cparams("parallel", "arbitrary"), name="peer_dense",
    )(hx, w, ut, v)


def _final_kernel(x_ref, f_ref, mod_ref, gain_ref, o_ref):
    x = x_ref[0] + mod_ref[0, 0, 5:6, :] * f_ref[0]
    o_ref[0] = x * lax.rsqrt(jnp.mean(x * x, axis=-1, keepdims=True) + EPS) * gain_ref[...]


def _final(x, ffn, mod, gain):
    b, s, d = x.shape
    tok = pl.BlockSpec((1, TOK_TILE, d), lambda i, t: (i, t, 0))
    return pl.pallas_call(
        _final_kernel, grid=(b, s // TOK_TILE),
        in_specs=[tok, tok, pl.BlockSpec((1, 1, 6, d), lambda i, t: (i, 1, 0, 0)),
                  pl.BlockSpec((1, d), lambda i, t: (0, 0))],
        out_specs=tok, out_shape=jax.ShapeDtypeStruct((b, s, d), F32),
        compiler_params=_cparams("parallel", "parallel"), name="final_norm",
    )(x, ffn, mod, gain)


def _rope_tables(seq):
    pos = jnp.arange(seq)
    rc = jnp.stack([pos // GRID_W, pos % GRID_W], axis=-1).astype(F32)
    inv = jnp.power(ROPE_BASE, -jnp.arange(ROPE_PAIRS, dtype=F32) / ROPE_PAIRS)
    ang = rc[:, :, None] * inv
    cos = jnp.concatenate([jnp.cos(ang)] * 2, axis=-1).reshape(seq, HEAD_DIM)
    sin = jnp.concatenate([-jnp.sin(ang), jnp.sin(ang)], axis=-1).reshape(seq, HEAD_DIM)
    return jnp.concatenate([cos] * 2, axis=1), jnp.concatenate([sin] * 2, axis=1)


def _pair_lanes(cols, n_heads):
    rows = cols.shape[0]
    c = cols.reshape(rows, 4, n_heads // 2, 2).transpose(0, 2, 1, 3).reshape(rows, n_heads // 2, 8)
    return jnp.pad(c, ((0, 0), (0, 0), (0, LANE - 8))).reshape(rows, n_heads // 2 * LANE)


def kernel(x, c, ctx, c_ctx, w_ada, b_ada, norm1, norm2, w_in, conv_w, a_log, dt_bias, gdn_norm, w_pool,
           pool_scale, sink, w_out, peer_wq, peer_keys, peer_u, peer_v, norm_f):
    b, seq, d = x.shape
    n_ctx = ctx.shape[1]
    depth = w_ada.shape[0]
    pool_w = d // 4
    gdn_w = 3 * d // 8
    gdn_heads = gdn_w // HEAD_DIM
    attn_w = d - pool_w - gdn_w
    kv_w = KV_HEADS * HEAD_DIM
    assert n_ctx % TOK_TILE == 0 and seq % TOK_TILE == 0 and gdn_heads % 2 == 0
    assert (b * (n_ctx + seq)) % PEER_TOK_TILE == 0 and (b * seq) % PEER_TOK_TILE == 0

    rows = -(-(b + 1) // SUBLANE) * SUBLANE
    c_all = jnp.zeros((rows, d), F32).at[:b].set(c).at[b].set(c_ctx)
    mod = _ada_mod(c_all, w_ada, b_ada)
    mod_x = mod[:, :b].reshape(depth, b, 1, 6, d)
    mod_c = jnp.broadcast_to(mod[:, b].reshape(depth, 1, 1, 6, d), (depth, b, 1, 6, d))
    mod = jnp.concatenate([mod_c, mod_x], axis=2)

    cos_t, sin_t = _rope_tables(seq)
    splits = (pool_w, 3 * gdn_w, gdn_w, gdn_heads // 2 * LANE, attn_w, kv_w, kv_w)
    o_ba = pool_w + 4 * gdn_w
    n_ba = 4 * gdn_heads

    h = jnp.concatenate([ctx, x], axis=1)
    ffn = None
    for l in range(depth):
        last = l == depth - 1
        w_l = w_in[l]
        w_big = jnp.concatenate([w_l[:, :o_ba], _pair_lanes(w_l[:, o_ba:o_ba + n_ba], gdn_heads),
                                 w_l[:, o_ba + n_ba:]], axis=1).astype(BF16)
        pmod = mod[l - 1] if l > 0 else None
        outs = _in_proj(h, ffn, pmod, mod[l], norm1[l][None], w_big, splits, n_ctx)
        if l > 0:
            h, *outs = outs
        pa, qkv, z, ba, aq, ak, av = outs

        wbd = jax.scipy.linalg.block_diag(*[w_pool[l, g] for g in range(len(POOL_WINDOWS))])
        ya = _pool(pa, wbd, pool_scale[l][None], n_ctx)

        gparams = jnp.stack([_pair_lanes(jnp.concatenate([jnp.zeros_like(a_log[l]), a_log[l]]).reshape(1, -1),
                                         gdn_heads),
                             _pair_lanes(jnp.concatenate([jnp.zeros_like(dt_bias[l]), dt_bias[l]]).reshape(1, -1),
                                         gdn_heads)], axis=1)
        gparams = gparams.reshape(2, gdn_heads // 2, LANE).transpose(1, 0, 2)
        gparams = jnp.pad(gparams, ((0, 0), (0, SUBLANE - 2), (0, 0)))
        yb = _gdn(qkv, z, ba, conv_w[l], gparams, jnp.concatenate([gdn_norm[l]] * 2)[None], n_ctx)

        sink_row = jnp.pad(sink[l], (0, LANE - sink.shape[1]))[None]
        yc = _attn_x(aq, ak, av, cos_t, sin_t, sink_row, n_ctx)
        if not last:
            yc = jnp.concatenate([_attn_ctx(aq, ak, av, sink_row, n_ctx), yc], axis=1)

        x_new, hx = _out_proj(h, ya, yb, yc, w_out[l].astype(BF16), mod[l], norm2[l][None], n_ctx,
                              skip_ctx=last)
        t = hx.shape[0] * hx.shape[1]
        hx = hx.reshape(t, d)
        keys = peer_keys[l].reshape(2 * PEER_HEADS, N_KEYS, -1).astype(BF16)
        route = _peer_route(hx.T, peer_wq[l].T.astype(BF16), keys)
        gates = _peer_gates(route).reshape(t, N_KEYS * N_KEYS)
        ut = peer_u[l].reshape(-1, PEER_EXP_TILE, d).transpose(0, 2, 1).astype(BF16)
        ffn = _peer_dense(hx, gates, ut, peer_v[l].astype(BF16))
        ffn = ffn.reshape(x_new.shape)
        h = x_new
    return _final(h, ffn, mod[depth - 1], norm_f[None])
```

```python
import functools

import jax
import jax.numpy as jnp
from jax import lax
from jax.experimental import pallas as pl
from jax.experimental.pallas import tpu as pltpu

F32, BF16 = jnp.float32, jnp.bfloat16
HIGHEST = lax.Precision.HIGHEST

EPS = 1e-6
NEG_INF = -1e30
HEAD_DIM = 64
GRID_W = 64
POOL_WINDOWS = (2, 4, 8, 16)
GDN_CHUNK = 64
CONV_K = 5
KV_HEADS = 2
WINDOW = 128
ATTN_BLOCK = 128
ROPE_BASE = 10000.0
ROPE_PAIRS = HEAD_DIM // 4
PEER_HEADS = 8
N_KEYS = 128
PEER_TOPK = 16

LANE = 128
SUBLANE = 8
TOK_TILE = 256
ROW_CHUNK = 256
GDN_GROUP = 4
PEER_ROUTE_TILE = 256
PEER_GATE_TILE = 128
PEER_GATE_UNROLL = 16
PEER_TOK_TILE = 1024
PEER_EXP_TILE = 1024
PEER_SUB = 2
VMEM_LIMIT = 48 * 1024 * 1024

_CAND_ROWS = tuple(PEER_TOPK // (a + 1) for a in range(PEER_TOPK))
_N_CAND = sum(_CAND_ROWS)
_CAND_PAD = -(-_N_CAND // SUBLANE) * SUBLANE


def _cparams(*sem):
    return pltpu.CompilerParams(dimension_semantics=sem, vmem_limit_bytes=VMEM_LIMIT)


def _dot(a, b):
    return jnp.dot(a.astype(BF16), b.astype(BF16), preferred_element_type=F32)


def _dot_nt(a, b):
    return lax.dot_general(a.astype(BF16), b.astype(BF16), (((1,), (1,)), ((), ())),
                           preferred_element_type=F32)


def _dot_tn(a, b):
    return lax.dot_general(a.astype(BF16), b.astype(BF16), (((0,), (0,)), ((), ())),
                           preferred_element_type=F32)


def _dot_hi(a, b):
    return jnp.dot(a, b, precision=HIGHEST, preferred_element_type=F32)


def _silu(x):
    return x * jax.nn.sigmoid(x)


def _softplus(x):
    return jnp.maximum(x, 0.0) + jnp.log1p(jnp.exp(-jnp.abs(x)))


def _modnorm(x, gain, scale, shift):
    y = x * lax.rsqrt(jnp.mean(x * x, axis=-1, keepdims=True) + EPS) * gain
    return y * (1.0 + scale) + shift


def _ada_kernel(c_ref, w_ref, b_ref, o_ref):
    o_ref[0] = _dot_hi(_silu(c_ref[...]), w_ref[0]) + b_ref[0]


def _ada_mod(c_all, w_ada, b_ada):
    depth, d, n = w_ada.shape
    rows = c_all.shape[0]
    tn = n // 4
    return pl.pallas_call(
        _ada_kernel,
        grid=(depth, n // tn),
        in_specs=[pl.BlockSpec((rows, d), lambda l, j: (0, 0)),
                  pl.BlockSpec((1, d, tn), lambda l, j: (l, 0, j)),
                  pl.BlockSpec((1, 1, tn), lambda l, j: (l, 0, j))],
        out_specs=pl.BlockSpec((1, rows, tn), lambda l, j: (l, 0, j)),
        out_shape=jax.ShapeDtypeStruct((depth, rows, n), F32),
        compiler_params=_cparams("parallel", "parallel"),
        name="ada_mod",
    )(c_all, w_ada, b_ada.reshape(depth, 1, n))


def _inproj_kernel(*refs, has_ffn, col_splits):
    if has_ffn:
        h_ref, f_ref, pmod_ref, mod_ref, gain_ref, w_ref, res_ref, *outs = refs
        x = h_ref[0] + pmod_ref[0, 0, 5:6, :] * f_ref[0]
        res_ref[0] = x
    else:
        h_ref, mod_ref, gain_ref, w_ref, *outs = refs
        x = h_ref[0]
    hb = _modnorm(x, gain_ref[...], mod_ref[0, 0, 1:2, :], mod_ref[0, 0, 0:1, :]).astype(BF16)
    off = 0
    for o_ref, width in zip(outs, col_splits):
        o_ref[0] = jnp.dot(hb, w_ref[:, off:off + width], preferred_element_type=F32)
        off += width


def _in_proj(h, ffn, pmod, mod, gain, w, col_splits, n_ctx):
    b, l, d = h.shape
    nt = l // TOK_TILE
    nct = n_ctx // TOK_TILE
    tok = pl.BlockSpec((1, TOK_TILE, d), lambda i, t: (i, t, 0))
    modspec = pl.BlockSpec((1, 1, 6, d), lambda i, t: (i, (t >= nct).astype(jnp.int32), 0, 0))
    has_ffn = ffn is not None
    in_specs = [tok] + ([tok, modspec] if has_ffn else []) + [
        modspec, pl.BlockSpec((1, d), lambda i, t: (0, 0)), pl.BlockSpec(w.shape, lambda i, t: (0, 0))]
    out_shape = [jax.ShapeDtypeStruct((b, l, wd), F32) for wd in col_splits]
    out_specs = [pl.BlockSpec((1, TOK_TILE, wd), lambda i, t: (i, t, 0)) for wd in col_splits]
    if has_ffn:
        out_shape = [jax.ShapeDtypeStruct((b, l, d), F32)] + out_shape
        out_specs = [tok] + out_specs
    args = (h, ffn, pmod, mod, gain, w) if has_ffn else (h, mod, gain, w)
    return pl.pallas_call(
        functools.partial(_inproj_kernel, has_ffn=has_ffn, col_splits=col_splits),
        grid=(b, nt), in_specs=in_specs, out_specs=out_specs, out_shape=out_shape,
        compiler_params=_cparams("parallel", "parallel"), name="in_proj",
    )(*args)


def _pool_kernel(a_ref, wbd_ref, scale_ref, o_ref, pad_ref, *, segments):
    c = a_ref.shape[-1]
    grp = lax.broadcasted_iota(jnp.int32, (1, c), 1) // (c // len(POOL_WINDOWS))
    zeros = jnp.zeros((SUBLANE, c), F32)
    for off, n in segments:
        pad_ref[0:SUBLANE, :] = zeros
        pad_ref[SUBLANE:SUBLANE + n, :] = a_ref[0, off:off + n, :]
        pad_ref[SUBLANE + n:2 * SUBLANE + n, :] = zeros

        def chunk(ci, carry, off=off, n=n):
            r0 = pl.multiple_of(ci * ROW_CHUNK, ROW_CHUNK)
            win = pad_ref[pl.ds(r0, ROW_CHUNK + 2 * SUBLANE), :]
            sh = lambda s: win[SUBLANE + s:SUBLANE + s + ROW_CHUNK, :]
            a0 = sh(0)
            s2 = sh(-1) + a0
            s4 = s2 + sh(-2) + sh(1)
            s8 = s4 + sh(-4) + sh(-3) + sh(2) + sh(3)
            s16 = s8 + (sh(-8) + sh(-7) + sh(-6) + sh(-5)) + (sh(4) + sh(5) + sh(6) + sh(7))
            t = r0 + lax.broadcasted_iota(jnp.int32, (ROW_CHUNK, 1), 0)

            def mean(s, w):
                cnt = jnp.minimum(t + (w - w // 2), n) - jnp.maximum(t - w // 2, 0)
                return s / cnt.astype(F32)

            pooled = jnp.where(grp == 0, mean(s2, 2), jnp.where(grp == 1, mean(s4, 4),
                               jnp.where(grp == 2, mean(s8, 8), mean(s16, 16)))) - a0
            o_ref[0, pl.ds(off + r0, ROW_CHUNK), :] = _dot_hi(pooled, wbd_ref[...]) * scale_ref[...]
            return carry

        lax.fori_loop(0, n // ROW_CHUNK, chunk, 0)


def _pool(a, wbd, scale, n_ctx):
    b, l, c = a.shape
    segments = ((0, n_ctx), (n_ctx, l - n_ctx))
    return pl.pallas_call(
        functools.partial(_pool_kernel, segments=segments),
        grid=(b,),
        in_specs=[pl.BlockSpec((1, l, c), lambda i: (i, 0, 0)),
                  pl.BlockSpec((c, c), lambda i: (0, 0)),
                  pl.BlockSpec((1, c), lambda i: (0, 0))],
        out_specs=pl.BlockSpec((1, l, c), lambda i: (i, 0, 0)),
        out_shape=jax.ShapeDtypeStruct((b, l, c), F32),
        scratch_shapes=[pltpu.VMEM((l - n_ctx + 2 * SUBLANE, c), F32)],
        compiler_params=_cparams("parallel"), name="pool",
    )(a, wbd, scale)


def _gdn_kernel(q_ref, k_ref, v_ref, z_ref, ba_ref, cwq_ref, cwk_ref, cwv_ref, gp_ref, gain_ref, o_ref,
                pad_ref, yq_ref, yk_ref, yv_ref, bg_ref, of_ref, ob_ref, s_ref,
                qe_ref, mn_ref, oc_ref, nn_ref, egl_ref, *, n_ctx):
    l = q_ref.shape[1]
    n_x = l - n_ctx
    hd = HEAD_DIM
    lane = lax.broadcasted_iota(jnp.int32, (1, LANE), 1)
    lo = lane < hd
    zeros = jnp.zeros((SUBLANE, LANE), F32)
    x_off = 2 * SUBLANE + n_ctx

    def head_scale(ss):
        s_lo = jnp.sum(jnp.where(lo, ss, 0.0), axis=-1, keepdims=True)
        s_hi = jnp.sum(jnp.where(lo, 0.0, ss), axis=-1, keepdims=True)
        return jnp.where(lo, s_lo, s_hi)

    def conv(u_ref, cw_ref, y_ref, post):
        pad_ref[0:SUBLANE, :] = zeros
        pad_ref[SUBLANE:SUBLANE + n_ctx, :] = u_ref[0, 0:n_ctx, :]
        pad_ref[SUBLANE + n_ctx:x_off, :] = zeros
        pad_ref[x_off:x_off + n_x, :] = u_ref[0, n_ctx:l, :]
        pad_ref[x_off + n_x:x_off + n_x + SUBLANE, :] = zeros
        cw = cw_ref[...]
        for poff, yoff, n in ((SUBLANE, 0, n_ctx), (x_off, n_ctx, n_x)):
            def chunk(ci, carry, poff=poff, yoff=yoff):
                r0 = pl.multiple_of(ci * ROW_CHUNK, ROW_CHUNK)
                win = pad_ref[pl.ds(poff - SUBLANE + r0, ROW_CHUNK + 2 * SUBLANE), :]
                acc = win[SUBLANE - 2:SUBLANE - 2 + ROW_CHUNK, :] * cw[0:1, :]
                for j in range(1, CONV_K):
                    acc = acc + win[SUBLANE - 2 + j:SUBLANE - 2 + j + ROW_CHUNK, :] * cw[j:j + 1, :]
                y_ref[pl.ds(yoff + r0, ROW_CHUNK), :] = post(_silu(acc))
                return carry
            lax.fori_loop(0, n // ROW_CHUNK, chunk, 0)

    l2 = lambda y: y * lax.rsqrt(head_scale(y * y) + EPS)
    conv(q_ref, cwq_ref, yq_ref, lambda y: l2(y) * (hd ** -0.5))
    conv(k_ref, cwk_ref, yk_ref, l2)
    conv(v_ref, cwv_ref, yv_ref, lambda y: y)

    ba = ba_ref[0]
    g = -jnp.exp(gp_ref[0, 0:1, :]) * _softplus(ba + gp_ref[0, 1:2, :])
    bg_ref[...] = jnp.where(lane < 4, jax.nn.sigmoid(ba), g)

    ck = GDN_CHUNK
    rr = lax.broadcasted_iota(jnp.int32, (ck, ck), 0)
    cc = lax.broadcasted_iota(jnp.int32, (ck, ck), 1)
    eye = (rr == cc).astype(F32)
    tril = (rr >= cc).astype(F32)
    n_chunks = l // ck
    nc_ctx = n_ctx // ck

    def phase1(ci, carry):
        probs = []
        for gi in range(GDN_GROUP):
            c = ci * GDN_GROUP + gi
            r0 = pl.multiple_of(c * ck, ck)
            qc = yq_ref[pl.ds(r0, ck), :]
            kc = yk_ref[pl.ds(r0, ck), :]
            vc = yv_ref[pl.ds(r0, ck), :]
            bgc = bg_ref[pl.ds(r0, ck), :]
            gcf = _dot_hi(tril, bgc)
            gcb = gcf[ck - 1:ck, :] - gcf + bgc
            gct = (gcf.T, gcb.T)
            for j in range(2):
                qh = qc[:, j * hd:(j + 1) * hd]
                kh = kc[:, j * hd:(j + 1) * hd]
                vh = vc[:, j * hd:(j + 1) * hd]
                kk = _dot_nt(kh, kh)
                qk = _dot_nt(qh, kh)
                for d in range(2):
                    mask = (rr >= cc) if d == 0 else (rr <= cc)
                    smask = (rr > cc) if d == 0 else (rr < cc)
                    lg, lb = 4 + 2 * d + j, 2 * d + j
                    gcol = (gcf, gcb)[d][:, lg:lg + 1]
                    grow = gct[d][lg:lg + 1, :]
                    bcol = bgc[:, lb:lb + 1]
                    decay = jnp.where(mask, jnp.exp(jnp.where(mask, gcol - grow, 0.0)), 0.0)
                    xp = jnp.where(smask, -(bcol * kk * decay), 0.0)
                    eg = jnp.exp(gcol)
                    last = ck - 1 if d == 0 else 0
                    glast = gcol[last:last + 1, :]
                    probs.append(dict(
                        idx=c * 4 + 2 * d + j, xp=xp, inv=eye + xp, attn=qk * decay, qd=qh * eg,
                        rhs=jnp.concatenate([vh * bcol, kh * bcol * eg], axis=1),
                        kd=kh * jnp.exp(glast - gcol), egl=jnp.exp(glast)))
        for _ in range(5):
            for p in probs:
                p["xp"] = _dot(p["xp"], p["xp"])
            for p in probs:
                p["inv"] = p["inv"] + _dot(p["inv"], p["xp"])
        for p in probs:
            p["sol"] = _dot(p["inv"], p["rhs"])
        for p in probs:
            p["as"] = _dot(p["attn"], p["sol"])
        for p in probs:
            p["ks"] = _dot_tn(p["kd"], p["sol"])
        for p in probs:
            i = p["idx"]
            qe_ref[i] = (p["qd"] - p["as"][:, hd:]).astype(BF16)
            oc_ref[i] = p["as"][:, :hd]
            mn_ref[i] = p["ks"][:, hd:].astype(BF16)
            nn_ref[i] = p["ks"][:, :hd]
            egl_ref[i] = jnp.broadcast_to(p["egl"], (ck, hd))
        return carry

    lax.fori_loop(0, n_chunks // GDN_GROUP, phase1, 0)

    s_ref[...] = jnp.zeros(s_ref.shape, F32)

    def phase2(i, carry):
        cb = jnp.where(i < nc_ctx, nc_ctx - 1 - i, n_chunks - 1 - (i - nc_ctx))
        work = [(d, j, cidx * 4 + 2 * d + j) for d, cidx in ((0, i), (1, cb)) for j in range(2)]
        states = [s_ref[2 * d + j] for d, j, _ in work]
        outs = [_dot(qe_ref[idx], s) + oc_ref[idx] for (_, _, idx), s in zip(work, states)]
        upd = [_dot(mn_ref[idx], s) for (_, _, idx), s in zip(work, states)]
        for (d, j, idx), s, m in zip(work, states, upd):
            s_ref[2 * d + j] = egl_ref[idx] * s - m + nn_ref[idx]
        of_ref[pl.ds(pl.multiple_of(i * ck, ck), ck), :] = jnp.concatenate(outs[0:2], axis=1)
        ob_ref[pl.ds(pl.multiple_of(cb * ck, ck), ck), :] = jnp.concatenate(outs[2:4], axis=1)
        return carry

    lax.fori_loop(0, n_chunks, phase2, 0)

    def finish(ci, carry):
        r0 = pl.multiple_of(ci * ROW_CHUNK, ROW_CHUNK)
        o = of_ref[pl.ds(r0, ROW_CHUNK), :] + ob_ref[pl.ds(r0, ROW_CHUNK), :]
        o = o * lax.rsqrt(head_scale(o * o) * (1.0 / hd) + EPS) * gain_ref[...]
        o_ref[0, pl.ds(r0, ROW_CHUNK), :] = o * _silu(z_ref[0, pl.ds(r0, ROW_CHUNK), :])
        return carry

    lax.fori_loop(0, l // ROW_CHUNK, finish, 0)


def _gdn(qkv, z, ba, conv_w, gparams, gain2, n_ctx):
    b, l, w3 = qkv.shape
    npair = w3 // (3 * LANE)
    seq = lambda off: pl.BlockSpec((1, l, LANE), lambda i, p: (i, 0, p + off))
    cw = lambda off: pl.BlockSpec((CONV_K, LANE), lambda i, p: (0, p + off))
    return pl.pallas_call(
        functools.partial(_gdn_kernel, n_ctx=n_ctx),
        grid=(b, npair),
        in_specs=[seq(0), seq(npair), seq(2 * npair), seq(0), seq(0),
                  cw(0), cw(npair), cw(2 * npair),
                  pl.BlockSpec((1, SUBLANE, LANE), lambda i, p: (p, 0, 0)),
                  pl.BlockSpec((1, LANE), lambda i, p: (0, 0))],
        out_specs=seq(0),
        out_shape=jax.ShapeDtypeStruct((b, l, npair * LANE), F32),
        scratch_shapes=[pltpu.VMEM((l + 3 * SUBLANE, LANE), F32)] + [pltpu.VMEM((l, LANE), F32)] * 6
        + [pltpu.VMEM((4, HEAD_DIM, HEAD_DIM), F32)]
        + [pltpu.VMEM((4 * l // GDN_CHUNK, GDN_CHUNK, HEAD_DIM), dt) for dt in (BF16, BF16, F32, F32, F32)],
        compiler_params=_cparams("parallel", "parallel"), name="gdn",
    )(qkv, qkv, qkv, z, ba, conv_w, conv_w, conv_w, gparams, gain2)


def _rope(x, cos, sin):
    w = x.shape[-1]
    reps = w // cos.shape[-1]
    if reps > 1:
        cos = jnp.concatenate([cos] * reps, axis=1)
        sin = jnp.concatenate([sin] * reps, axis=1)
    first = (lax.broadcasted_iota(jnp.int32, (1, w), 1) % (2 * ROPE_PAIRS)) < ROPE_PAIRS
    partner = jnp.where(first, pltpu.roll(x, w - ROPE_PAIRS, 1), pltpu.roll(x, ROPE_PAIRS, 1))
    return x * cos + partner * sin


def _attend(q, k, v, valid, sink_col):
    s = _dot_nt(q, k)
    if valid is not None:
        s = jnp.where(valid, s, NEG_INF)
    m = jnp.maximum(jnp.max(s, axis=-1, keepdims=True), sink_col)
    p = jnp.exp(s - m)
    den = jnp.sum(p, axis=-1, keepdims=True) + jnp.exp(sink_col - m)
    return _dot(p, v) / den


def _gqa(q, k_all, v_all, valid, sink_ref, o_ref):
    nq = q.shape[0]
    hd = HEAD_DIM
    group = q.shape[1] // hd // KV_HEADS
    outs = []
    for kvh in range(KV_HEADS):
        heads = [kvh * group + g for g in range(group)]
        qg = jnp.concatenate([q[:, h * hd:(h + 1) * hd] for h in heads], axis=0)
        sink_col = jnp.concatenate(
            [jnp.broadcast_to(sink_ref[0:1, h:h + 1], (nq, 1)) for h in heads], axis=0)
        o = _attend(qg, k_all[:, kvh * hd:(kvh + 1) * hd], v_all[:, kvh * hd:(kvh + 1) * hd],
                    valid, sink_col)
        outs += [o[g * nq:(g + 1) * nq, :] for g in range(group)]
    o_ref[0] = jnp.concatenate(outs, axis=1)


def _attn_kernel(q_ref, kp_ref, ko_ref, kn_ref, vp_ref, vo_ref, vn_ref, kc_ref, vc_ref,
                 cp_ref, sp_ref, co_ref, so_ref, cn_ref, sn_ref, sink_ref, o_ref, *, seq):
    i = pl.program_id(1)
    blk = ATTN_BLOCK
    n_ctx = kc_ref.shape[1]
    q = _rope(q_ref[0], co_ref[...], so_ref[...]) * (HEAD_DIM ** -0.5)
    k_all = jnp.concatenate([_rope(kp_ref[0], cp_ref[...], sp_ref[...]),
                             _rope(ko_ref[0], co_ref[...], so_ref[...]),
                             _rope(kn_ref[0], cn_ref[...], sn_ref[...]), kc_ref[0]], axis=0)
    v_all = jnp.concatenate([vp_ref[0], vo_ref[0], vn_ref[0], vc_ref[0]], axis=0)
    group = q.shape[1] // HEAD_DIM // KV_HEADS
    nk = 3 * blk + n_ctx
    qo = lax.broadcasted_iota(jnp.int32, (group * blk, 1), 0) % blk
    ko = lax.broadcasted_iota(jnp.int32, (1, nk), 1)
    kpos = (i - 1) * blk + ko
    valid = ((jnp.abs(ko - blk - qo) <= WINDOW) & (kpos >= 0) & (kpos < seq)) | (ko >= 3 * blk)
    _gqa(q, k_all, v_all, valid, sink_ref, o_ref)


def _attn_x(aq, ak, av, cos_t, sin_t, sink_row, n_ctx):
    b, l, wq = aq.shape
    wk = ak.shape[-1]
    blk = ATTN_BLOCK
    seq = l - n_ctx
    nb = seq // blk
    cb = n_ctx // blk
    kv = lambda f: pl.BlockSpec((1, blk, wk), lambda i, t: (i, f(t) + cb, 0))
    tab = lambda f: pl.BlockSpec((blk, LANE), lambda i, t: (f(t), 0))
    prev = lambda t: jnp.maximum(t - 1, 0)
    own = lambda t: t
    nxt = lambda t: jnp.minimum(t + 1, nb - 1)
    ctx = pl.BlockSpec((1, n_ctx, wk), lambda i, t: (i, 0, 0))
    return pl.pallas_call(
        functools.partial(_attn_kernel, seq=seq),
        grid=(b, nb),
        in_specs=[pl.BlockSpec((1, blk, wq), lambda i, t: (i, t + cb, 0)),
                  kv(prev), kv(own), kv(nxt), kv(prev), kv(own), kv(nxt), ctx, ctx,
                  tab(prev), tab(prev), tab(own), tab(own), tab(nxt), tab(nxt),
                  pl.BlockSpec((1, LANE), lambda i, t: (0, 0))],
        out_specs=pl.BlockSpec((1, blk, wq), lambda i, t: (i, t, 0)),
        out_shape=jax.ShapeDtypeStruct((b, seq, wq), F32),
        compiler_params=_cparams("parallel", "parallel"), name="attn_x",
    )(aq, ak, ak, ak, av, av, av, ak, av, cos_t, sin_t, cos_t, sin_t, cos_t, sin_t, sink_row)


def _attn_ctx_kernel(q_ref, k_ref, v_ref, sink_ref, o_ref):
    _gqa(q_ref[0] * (HEAD_DIM ** -0.5), k_ref[0], v_ref[0], None, sink_ref, o_ref)


def _attn_ctx(aq, ak, av, sink_row, n_ctx):
    b, _, wq = aq.shape
    wk = ak.shape[-1]
    spec = lambda w: pl.BlockSpec((1, n_ctx, w), lambda i: (i, 0, 0))
    return pl.pallas_call(
        _attn_ctx_kernel, grid=(b,),
        in_specs=[spec(wq), spec(wk), spec(wk), pl.BlockSpec((1, LANE), lambda i: (0, 0))],
        out_specs=spec(wq), out_shape=jax.ShapeDtypeStruct((b, n_ctx, wq), F32),
        compiler_params=_cparams("parallel"), name="attn_ctx",
    )(aq, ak, av, sink_row)


def _outproj_kernel(x_ref, ya_ref, yb_ref, yc_ref, w_ref, mod_ref, gain_ref, xo_ref, hx_ref):
    wa, wb = ya_ref.shape[-1], yb_ref.shape[-1]
    acc = (jnp.dot(ya_ref[0].astype(BF16), w_ref[0:wa, :], preferred_element_type=F32)
           + jnp.dot(yb_ref[0].astype(BF16), w_ref[wa:wa + wb, :], preferred_element_type=F32)
           + jnp.dot(yc_ref[0].astype(BF16), w_ref[wa + wb:, :], preferred_element_type=F32))
    x = x_ref[0] + mod_ref[0, 0, 2:3, :] * acc
    xo_ref[0] = x
    hx_ref[0] = _modnorm(x, gain_ref[...], mod_ref[0, 0, 4:5, :], mod_ref[0, 0, 3:4, :]).astype(BF16)


def _out_proj(h, ya, yb, yc, w, mod, gain, n_ctx, skip_ctx):
    b, l, d = h.shape
    nct = n_ctx // TOK_TILE
    off = nct if skip_ctx else 0
    nt = l // TOK_TILE - off
    full = lambda wd: pl.BlockSpec((1, TOK_TILE, wd), lambda i, t: (i, t + off, 0))
    own = lambda wd: pl.BlockSpec((1, TOK_TILE, wd), lambda i, t: (i, t, 0))
    return pl.pallas_call(
        _outproj_kernel, grid=(b, nt),
        in_specs=[full(d), full(ya.shape[-1]), full(yb.shape[-1]), own(yc.shape[-1]),
                  pl.BlockSpec(w.shape, lambda i, t: (0, 0)),
                  pl.BlockSpec((1, 1, 6, d), lambda i, t: (i, (t + off >= nct).astype(jnp.int32), 0, 0)),
                  pl.BlockSpec((1, d), lambda i, t: (0, 0))],
        out_specs=[own(d), own(d)],
        out_shape=[jax.ShapeDtypeStruct((b, nt * TOK_TILE, d), F32),
                   jax.ShapeDtypeStruct((b, nt * TOK_TILE, d), BF16)],
        compiler_params=_cparams("parallel", "parallel"), name="out_proj",
    )(h, ya, yb, yc, w, mod, gain)


def _route_kernel(ht_ref, wq_ref, keys_ref, r2_ref, e2_ref, n1_ref, e1_ref, qt_ref, top_ref, cand_ref):
    tm = ht_ref.shape[1]
    k = PEER_TOPK
    nk = N_KEYS
    neg = jnp.float32(-jnp.inf)
    qt_ref[...] = _dot(wq_ref[...], ht_ref[...]).astype(BF16)
    cand_ref[_N_CAND:_CAND_PAD, :] = jnp.full((_CAND_PAD - _N_CAND, tm), neg, F32)
    for h in range(PEER_HEADS):
        st = []
        rank2 = jnp.full((nk, tm), float(k), F32)
        for p in range(2):
            hp = 2 * h + p
            s = jnp.dot(keys_ref[hp], qt_ref[hp * nk:(hp + 1) * nk, :], preferred_element_type=F32)
            st.append(s)
            cur = s
            for r in range(k):
                m = jnp.max(cur, axis=0, keepdims=True)
                top_ref[p * k + r:p * k + r + 1, :] = m
                hit = cur == m
                if p == 1:
                    rank2 = jnp.where(hit, float(r), rank2)
                cur = jnp.where(hit, neg, cur)
        v1 = top_ref[0:k, :]
        v2 = top_ref[k:2 * k, :]
        row = 0
        for a, nb in enumerate(_CAND_ROWS):
            cand_ref[row:row + nb, :] = v1[a:a + 1, :] + v2[0:nb, :]
            row += nb
        cur = cand_ref[...]
        cmax = thr = zsum = None
        for r in range(k):
            m = jnp.max(cur, axis=0, keepdims=True)
            if r == 0:
                cmax, zsum = m, jnp.ones_like(m)
            else:
                zsum = zsum + jnp.exp(m - cmax)
            thr = m
            cur = jnp.where(cur == m, neg, cur)
        n1 = jnp.zeros((nk, tm), F32)
        row = 0
        for a, nb in enumerate(_CAND_ROWS):
            sel = cand_ref[row:row + nb, :] >= thr
            n_a = jnp.sum(sel.astype(F32), axis=0, keepdims=True)
            n1 = jnp.where(st[0] == v1[a:a + 1, :], n_a, n1)
            row += nb
        rows = slice(h * nk, (h + 1) * nk)
        r2_ref[rows, :] = rank2
        e2_ref[rows, :] = jnp.exp(st[1] - v2[0:1, :])
        n1_ref[rows, :] = n1
        e1_ref[rows, :] = jnp.exp(st[0] - v1[0:1, :]) / zsum


def _peer_route(hxt, wq_t, keys):
    d, t = hxt.shape
    nq = wq_t.shape[0]
    tm = PEER_ROUTE_TILE
    rows = PEER_HEADS * N_KEYS
    return pl.pallas_call(
        _route_kernel, grid=(t // tm,),
        in_specs=[pl.BlockSpec((d, tm), lambda i: (0, i)),
                  pl.BlockSpec(wq_t.shape, lambda i: (0, 0)),
                  pl.BlockSpec(keys.shape, lambda i: (0, 0, 0))],
        out_specs=[pl.BlockSpec((rows, tm), lambda i: (0, i))] * 4,
        out_shape=[jax.ShapeDtypeStruct((rows, t), F32)] * 4,
        scratch_shapes=[pltpu.VMEM((nq, tm), BF16), pltpu.VMEM((2 * PEER_TOPK, tm), F32),
                        pltpu.VMEM((_CAND_PAD, tm), F32)],
        compiler_params=_cparams("parallel"), name="peer_route",
    )(hxt, wq_t, keys)


def _gates_kernel(r2_ref, e2_ref, n1_ref, e1_ref, w_ref, r2t_ref, e2t_ref, n1t_ref, e1t_ref):
    tb = w_ref.shape[1]
    nk = N_KEYS
    k = PEER_TOPK
    for src, dst in ((r2_ref, r2t_ref), (e2_ref, e2t_ref), (n1_ref, n1t_ref), (e1_ref, e1t_ref)):
        dst[...] = src[...].T
    rank = lax.broadcasted_iota(jnp.int32, (k, nk), 0).astype(F32)

    def tok(i, carry):
        for j in range(PEER_GATE_UNROLL):
            t = i * PEER_GATE_UNROLL + j
            r2, e2, n1, e1 = (ref[pl.ds(t, 1), :] for ref in (r2t_ref, e2t_ref, n1t_ref, e1t_ref))
            a, b = [], []
            for h in range(PEER_HEADS):
                ks = slice(h * nk, (h + 1) * nk)
                a.append(jnp.where(n1[:, ks] > rank, e1[:, ks], 0.0))
                b.append(jnp.where(r2[:, ks] == rank, e2[:, ks], 0.0))
            w = _dot_tn(jnp.concatenate(a, axis=0), jnp.concatenate(b, axis=0))
            for g in range(nk // SUBLANE):
                w_ref[g, t] = w[g * SUBLANE:(g + 1) * SUBLANE, :]
        return carry

    lax.fori_loop(0, tb // PEER_GATE_UNROLL, tok, 0)


def _peer_gates(route):
    rows, t = route[0].shape
    tb = PEER_GATE_TILE
    return pl.pallas_call(
        _gates_kernel, grid=(t // tb,),
        in_specs=[pl.BlockSpec((rows, tb), lambda i: (0, i))] * 4,
        out_specs=pl.BlockSpec((N_KEYS // SUBLANE, tb, SUBLANE, N_KEYS), lambda i: (0, i, 0, 0)),
        out_shape=jax.ShapeDtypeStruct((N_KEYS // SUBLANE, t, SUBLANE, N_KEYS), F32),
        scratch_shapes=[pltpu.VMEM((tb, rows), F32)] * 4,
        compiler_params=_cparams("parallel"), name="peer_gates",
    )(*route)


def _peer_kernel(h_ref, w_ref, ut_ref, v_ref, o_ref, acc_ref, g0_ref, g1_ref):
    e = pl.program_id(1)
    n_blk = pl.num_programs(1) - 1

    @pl.when(e == 0)
    def _():
        acc_ref[...] = jnp.zeros(acc_ref.shape, F32)
        g1_ref[...] = jnp.zeros(g1_ref.shape, BF16)

    @pl.when(e % 2 == 0)
    def _():
        _peer_step(h_ref, w_ref, ut_ref, v_ref, acc_ref, g0_ref, g1_ref)

    @pl.when(e % 2 == 1)
    def _():
        _peer_step(h_ref, w_ref, ut_ref, v_ref, acc_ref, g1_ref, g0_ref)

    @pl.when(e == n_blk)
    def _():
        o_ref[...] = acc_ref[...]


def _peer_step(h_ref, w_ref, ut_ref, v_ref, acc_ref, g_w_ref, g_r_ref):
    sr = h_ref.shape[0] // PEER_SUB

    def sub(k, carry):
        rows = pl.ds(pl.multiple_of(k * sr, sr), sr)
        acc_ref[rows, :] += jnp.dot(g_r_ref[rows, :], v_ref[...], preferred_element_type=F32)
        act = jnp.dot(h_ref[rows, :], ut_ref[0], preferred_element_type=F32)
        act = 0.5 * act * (1.0 + lax.erf(act * 0.7071067811865476))
        for j in range(SUBLANE):
            ln = slice(j * N_KEYS, (j + 1) * N_KEYS)
            wj = w_ref[pl.ds(k * (sr * SUBLANE) + j, sr, stride=SUBLANE), :]
            g_w_ref[rows, ln] = (wj * act[:, ln]).astype(BF16)
        return carry

    lax.fori_loop(0, PEER_SUB, sub, 0)


def _peer_dense(hx, w, ut, v):
    t, d = hx.shape
    n_blk, _, eb = ut.shape
    tm = PEER_TOK_TILE
    last = n_blk - 1
    return pl.pallas_call(
        _peer_kernel, grid=(t // tm, n_blk + 1),
        in_specs=[pl.BlockSpec((tm, d), lambda i, e: (i, 0)),
                  pl.BlockSpec((tm * SUBLANE, N_KEYS), lambda i, e: (jnp.minimum(e, last) * (t // tm) + i, 0)),
                  pl.BlockSpec((1, d, eb), lambda i, e: (jnp.minimum(e, last), 0, 0)),
                  pl.BlockSpec((eb, d), lambda i, e: (jnp.maximum(e - 1, 0), 0))],
        out_specs=pl.BlockSpec((tm, d), lambda i, e: (i, 0)),
        out_shape=jax.ShapeDtypeStruct((t, d), F32),
        scratch_shapes=[pltpu.VMEM((tm, d), F32), pltpu.VMEM((tm, eb), BF16), pltpu.VMEM((tm, eb), BF16)],
        compiler_params=_cparams("parallel", "arbitrary"), name="peer_dense",
    )(hx, w, ut, v)


def _final_kernel(x_ref, f_ref, mod_ref, gain_ref, o_ref):
    x = x_ref[0] + mod_ref[0, 0, 5:6, :] * f_ref[0]
    o_ref[0] = x * lax.rsqrt(jnp.mean(x * x, axis=-1, keepdims=True) + EPS) * gain_ref[...]


def _final(x, ffn, mod, gain):
    b, s, d = x.shape
    tok = pl.BlockSpec((1, TOK_TILE, d), lambda i, t: (i, t, 0))
    return pl.pallas_call(
        _final_kernel, grid=(b, s // TOK_TILE),
        in_specs=[tok, tok, pl.BlockSpec((1, 1, 6, d), lambda i, t: (i, 1, 0, 0)),
                  pl.BlockSpec((1, d), lambda i, t: (0, 0))],
        out_specs=tok, out_shape=jax.ShapeDtypeStruct((b, s, d), F32),
        compiler_params=_cparams("parallel", "parallel"), name="final_norm",
    )(x, ffn, mod, gain)


def _rope_tables(seq):
    pos = jnp.arange(seq)
    rc = jnp.stack([pos // GRID_W, pos % GRID_W], axis=-1).astype(F32)
    inv = jnp.power(ROPE_BASE, -jnp.arange(ROPE_PAIRS, dtype=F32) / ROPE_PAIRS)
    ang = rc[:, :, None] * inv
    cos = jnp.concatenate([jnp.cos(ang)] * 2, axis=-1).reshape(seq, HEAD_DIM)
    sin = jnp.concatenate([-jnp.sin(ang), jnp.sin(ang)], axis=-1).reshape(seq, HEAD_DIM)
    return jnp.concatenate([cos] * 2, axis=1), jnp.concatenate([sin] * 2, axis=1)


def _pair_lanes(cols, n_heads):
    rows = cols.shape[0]
    c = cols.reshape(rows, 4, n_heads // 2, 2).transpose(0, 2, 1, 3).reshape(rows, n_heads // 2, 8)
    return jnp.pad(c, ((0, 0), (0, 0), (0, LANE - 8))).reshape(rows, n_heads // 2 * LANE)


def kernel(x, c, ctx, c_ctx, w_ada, b_ada, norm1, norm2, w_in, conv_w, a_log, dt_bias, gdn_norm, w_pool,
           pool_scale, sink, w_out, peer_wq, peer_keys, peer_u, peer_v, norm_f):
    b, seq, d = x.shape
    n_ctx = ctx.shape[1]
    depth = w_ada.shape[0]
    pool_w = d // 4
    gdn_w = 3 * d // 8
    gdn_heads = gdn_w // HEAD_DIM
    attn_w = d - pool_w - gdn_w
    kv_w = KV_HEADS * HEAD_DIM
    assert n_ctx % TOK_TILE == 0 and seq % TOK_TILE == 0 and gdn_heads % 2 == 0
    assert (b * (n_ctx + seq)) % PEER_TOK_TILE == 0 and (b * seq) % PEER_TOK_TILE == 0

    rows = -(-(b + 1) // SUBLANE) * SUBLANE
    c_all = jnp.zeros((rows, d), F32).at[:b].set(c).at[b].set(c_ctx)
    mod = _ada_mod(c_all, w_ada, b_ada)
    mod_x = mod[:, :b].reshape(depth, b, 1, 6, d)
    mod_c = jnp.broadcast_to(mod[:, b].reshape(depth, 1, 1, 6, d), (depth, b, 1, 6, d))
    mod = jnp.concatenate([mod_c, mod_x], axis=2)

    cos_t, sin_t = _rope_tables(seq)
    splits = (pool_w, 3 * gdn_w, gdn_w, gdn_heads // 2 * LANE, attn_w, kv_w, kv_w)
    o_ba = pool_w + 4 * gdn_w
    n_ba = 4 * gdn_heads

    h = jnp.concatenate([ctx, x], axis=1)
    ffn = None
    for l in range(depth):
        last = l == depth - 1
        w_l = w_in[l]
        w_big = jnp.concatenate([w_l[:, :o_ba], _pair_lanes(w_l[:, o_ba:o_ba + n_ba], gdn_heads),
                                 w_l[:, o_ba + n_ba:]], axis=1).astype(BF16)
        pmod = mod[l - 1] if l > 0 else None
        outs = _in_proj(h, ffn, pmod, mod[l], norm1[l][None], w_big, splits, n_ctx)
        if l > 0:
            h, *outs = outs
        pa, qkv, z, ba, aq, ak, av = outs

        wbd = jax.scipy.linalg.block_diag(*[w_pool[l, g] for g in range(len(POOL_WINDOWS))])
        ya = _pool(pa, wbd, pool_scale[l][None], n_ctx)

        gparams = jnp.stack([_pair_lanes(jnp.concatenate([jnp.zeros_like(a_log[l]), a_log[l]]).reshape(1, -1),
                                         gdn_heads),
                             _pair_lanes(jnp.concatenate([jnp.zeros_like(dt_bias[l]), dt_bias[l]]).reshape(1, -1),
                                         gdn_heads)], axis=1)
        gparams = gparams.reshape(2, gdn_heads // 2, LANE).transpose(1, 0, 2)
        gparams = jnp.pad(gparams, ((0, 0), (0, SUBLANE - 2), (0, 0)))
        yb = _gdn(qkv, z, ba, conv_w[l], gparams, jnp.concatenate([gdn_norm[l]] * 2)[None], n_ctx)

        sink_row = jnp.pad(sink[l], (0, LANE - sink.shape[1]))[None]
        yc = _attn_x(aq, ak, av, cos_t, sin_t, sink_row, n_ctx)
        if not last:
            yc = jnp.concatenate([_attn_ctx(aq, ak, av, sink_row, n_ctx), yc], axis=1)

        x_new, hx = _out_proj(h, ya, yb, yc, w_out[l].astype(BF16), mod[l], norm2[l][None], n_ctx,
                              skip_ctx=last)
        t = hx.shape[0] * hx.shape[1]
        hx = hx.reshape(t, d)
        keys = peer_keys[l].reshape(2 * PEER_HEADS, N_KEYS, -1).astype(BF16)
        route = _peer_route(hx.T, peer_wq[l].T.astype(BF16), keys)
        gates = _peer_gates(route).reshape(-1, N_KEYS)
        ut = peer_u[l].reshape(-1, PEER_EXP_TILE, d).transpose(0, 2, 1).astype(BF16)
        ffn = _peer_dense(hx, gates, ut, peer_v[l].astype(BF16))
        ffn = ffn.reshape(x_new.shape)
        h = x_new
    return _final(h, ffn, mod[depth - 1], norm_f[None])
```

```python
import functools

import jax
import jax.numpy as jnp
from jax import lax
from jax.experimental import pallas as pl
from jax.experimental.pallas import tpu as pltpu

F32, BF16 = jnp.float32, jnp.bfloat16
HIGHEST = lax.Precision.HIGHEST

EPS = 1e-6
NEG_INF = -1e30
HEAD_DIM = 64
GRID_W = 64
POOL_WINDOWS = (2, 4, 8, 16)
GDN_CHUNK = 64
CONV_K = 5
KV_HEADS = 2
WINDOW = 128
ATTN_BLOCK = 128
ROPE_BASE = 10000.0
ROPE_PAIRS = HEAD_DIM // 4
PEER_HEADS = 8
N_KEYS = 128
PEER_TOPK = 16

LANE = 128
SUBLANE = 8
TOK_TILE = 256
PROJ_TILE = 768
ROW_CHUNK = 256
GDN_GROUP = 4
PEER_ROUTE_TILE = 256
PEER_GATE_TILE = 128
PEER_GATE_UNROLL = 16
PEER_TOK_TILE = 1024
PEER_EXP_TILE = 1024
PEER_SUB = 2
VMEM_LIMIT = 48 * 1024 * 1024

_CAND_ROWS = tuple(PEER_TOPK // (a + 1) for a in range(PEER_TOPK))
_N_CAND = sum(_CAND_ROWS)
_CAND_PAD = -(-_N_CAND // SUBLANE) * SUBLANE


def _cparams(*sem):
    return pltpu.CompilerParams(dimension_semantics=sem, vmem_limit_bytes=VMEM_LIMIT)


def _dot(a, b):
    return jnp.dot(a.astype(BF16), b.astype(BF16), preferred_element_type=F32)


def _dot_nt(a, b):
    return lax.dot_general(a.astype(BF16), b.astype(BF16), (((1,), (1,)), ((), ())),
                           preferred_element_type=F32)


def _dot_tn(a, b):
    return lax.dot_general(a.astype(BF16), b.astype(BF16), (((0,), (0,)), ((), ())),
                           preferred_element_type=F32)


def _dot_hi(a, b):
    return jnp.dot(a, b, precision=HIGHEST, preferred_element_type=F32)


def _silu(x):
    return x * jax.nn.sigmoid(x)


def _softplus(x):
    return jnp.maximum(x, 0.0) + jnp.log1p(jnp.exp(-jnp.abs(x)))


def _modnorm(x, gain, scale, shift):
    y = x * lax.rsqrt(jnp.mean(x * x, axis=-1, keepdims=True) + EPS) * gain
    return y * (1.0 + scale) + shift


def _ada_kernel(c_ref, w_ref, b_ref, o_ref):
    o_ref[0] = _dot_hi(_silu(c_ref[...]), w_ref[0]) + b_ref[0]


def _ada_mod(c_all, w_ada, b_ada):
    depth, d, n = w_ada.shape
    rows = c_all.shape[0]
    tn = n // 4
    return pl.pallas_call(
        _ada_kernel,
        grid=(depth, n // tn),
        in_specs=[pl.BlockSpec((rows, d), lambda l, j: (0, 0)),
                  pl.BlockSpec((1, d, tn), lambda l, j: (l, 0, j)),
                  pl.BlockSpec((1, 1, tn), lambda l, j: (l, 0, j))],
        out_specs=pl.BlockSpec((1, rows, tn), lambda l, j: (l, 0, j)),
        out_shape=jax.ShapeDtypeStruct((depth, rows, n), F32),
        compiler_params=_cparams("parallel", "parallel"),
        name="ada_mod",
    )(c_all, w_ada, b_ada.reshape(depth, 1, n))


def _row_mod(mod_ref, first_row, n_rows, n_ctx):
    is_x = first_row + lax.broadcasted_iota(jnp.int32, (n_rows, 1), 0) >= n_ctx
    return lambda k: jnp.where(is_x, mod_ref[0, 1, k:k + 1, :], mod_ref[0, 0, k:k + 1, :])


def _inproj_kernel(*refs, has_ffn, col_splits, n_ctx):
    tile = refs[0].shape[1]
    first_row = pl.program_id(1) * tile
    if has_ffn:
        h_ref, f_ref, pmod_ref, mod_ref, gain_ref, w_ref, res_ref, *outs = refs
        x = h_ref[0] + _row_mod(pmod_ref, first_row, tile, n_ctx)(5) * f_ref[0]
        res_ref[0] = x
    else:
        h_ref, mod_ref, gain_ref, w_ref, *outs = refs
        x = h_ref[0]
    mod = _row_mod(mod_ref, first_row, tile, n_ctx)
    hb = _modnorm(x, gain_ref[...], mod(1), mod(0)).astype(BF16)
    off = 0
    for o_ref, width in zip(outs, col_splits):
        o_ref[0] = jnp.dot(hb, w_ref[:, off:off + width], preferred_element_type=F32)
        off += width


def _in_proj(h, ffn, pmod, mod, gain, w, col_splits, n_ctx):
    b, l, d = h.shape
    tile = PROJ_TILE
    tok = pl.BlockSpec((1, tile, d), lambda i, t: (i, t, 0))
    modspec = pl.BlockSpec((1, 2, 6, d), lambda i, t: (i, 0, 0, 0))
    has_ffn = ffn is not None
    in_specs = [tok] + ([tok, modspec] if has_ffn else []) + [
        modspec, pl.BlockSpec((1, d), lambda i, t: (0, 0)), pl.BlockSpec(w.shape, lambda i, t: (0, 0))]
    out_shape = [jax.ShapeDtypeStruct((b, l, wd), F32) for wd in col_splits]
    out_specs = [pl.BlockSpec((1, tile, wd), lambda i, t: (i, t, 0)) for wd in col_splits]
    if has_ffn:
        out_shape = [jax.ShapeDtypeStruct((b, l, d), F32)] + out_shape
        out_specs = [tok] + out_specs
    args = (h, ffn, pmod, mod, gain, w) if has_ffn else (h, mod, gain, w)
    return pl.pallas_call(
        functools.partial(_inproj_kernel, has_ffn=has_ffn, col_splits=col_splits, n_ctx=n_ctx),
        grid=(b, l // tile), in_specs=in_specs, out_specs=out_specs, out_shape=out_shape,
        compiler_params=_cparams("parallel", "parallel"), name="in_proj",
    )(*args)


def _pool_kernel(a_ref, wbd_ref, scale_ref, o_ref, pad_ref, *, segments):
    c = a_ref.shape[-1]
    grp = lax.broadcasted_iota(jnp.int32, (1, c), 1) // (c // len(POOL_WINDOWS))
    zeros = jnp.zeros((SUBLANE, c), F32)
    for off, n in segments:
        pad_ref[0:SUBLANE, :] = zeros
        pad_ref[SUBLANE:SUBLANE + n, :] = a_ref[0, off:off + n, :]
        pad_ref[SUBLANE + n:2 * SUBLANE + n, :] = zeros

        def chunk(ci, carry, off=off, n=n):
            r0 = pl.multiple_of(ci * ROW_CHUNK, ROW_CHUNK)
            win = pad_ref[pl.ds(r0, ROW_CHUNK + 2 * SUBLANE), :]
            sh = lambda s: win[SUBLANE + s:SUBLANE + s + ROW_CHUNK, :]
            a0 = sh(0)
            s2 = sh(-1) + a0
            s4 = s2 + sh(-2) + sh(1)
            s8 = s4 + sh(-4) + sh(-3) + sh(2) + sh(3)
            s16 = s8 + (sh(-8) + sh(-7) + sh(-6) + sh(-5)) + (sh(4) + sh(5) + sh(6) + sh(7))
            t = r0 + lax.broadcasted_iota(jnp.int32, (ROW_CHUNK, 1), 0)

            def mean(s, w):
                cnt = jnp.minimum(t + (w - w // 2), n) - jnp.maximum(t - w // 2, 0)
                return s / cnt.astype(F32)

            pooled = jnp.where(grp == 0, mean(s2, 2), jnp.where(grp == 1, mean(s4, 4),
                               jnp.where(grp == 2, mean(s8, 8), mean(s16, 16)))) - a0
            o_ref[0, pl.ds(off + r0, ROW_CHUNK), :] = _dot_hi(pooled, wbd_ref[...]) * scale_ref[...]
            return carry

        lax.fori_loop(0, n // ROW_CHUNK, chunk, 0)


def _pool(a, wbd, scale, n_ctx):
    b, l, c = a.shape
    segments = ((0, n_ctx), (n_ctx, l - n_ctx))
    return pl.pallas_call(
        functools.partial(_pool_kernel, segments=segments),
        grid=(b,),
        in_specs=[pl.BlockSpec((1, l, c), lambda i: (i, 0, 0)),
                  pl.BlockSpec((c, c), lambda i: (0, 0)),
                  pl.BlockSpec((1, c), lambda i: (0, 0))],
        out_specs=pl.BlockSpec((1, l, c), lambda i: (i, 0, 0)),
        out_shape=jax.ShapeDtypeStruct((b, l, c), F32),
        scratch_shapes=[pltpu.VMEM((l - n_ctx + 2 * SUBLANE, c), F32)],
        compiler_params=_cparams("parallel"), name="pool",
    )(a, wbd, scale)


def _gdn_kernel(q_ref, k_ref, v_ref, z_ref, ba_ref, cwq_ref, cwk_ref, cwv_ref, gp_ref, gain_ref, o_ref,
                pad_ref, yq_ref, yk_ref, yv_ref, bg_ref, of_ref, ob_ref, s_ref,
                qe_ref, mn_ref, oc_ref, nn_ref, egl_ref, *, n_ctx):
    l = q_ref.shape[1]
    n_x = l - n_ctx
    hd = HEAD_DIM
    lane = lax.broadcasted_iota(jnp.int32, (1, LANE), 1)
    lo = lane < hd
    zeros = jnp.zeros((SUBLANE, LANE), F32)
    x_off = 2 * SUBLANE + n_ctx

    def head_scale(ss):
        s_lo = jnp.sum(jnp.where(lo, ss, 0.0), axis=-1, keepdims=True)
        s_hi = jnp.sum(jnp.where(lo, 0.0, ss), axis=-1, keepdims=True)
        return jnp.where(lo, s_lo, s_hi)

    def conv(u_ref, cw_ref, y_ref, post):
        pad_ref[0:SUBLANE, :] = zeros
        pad_ref[SUBLANE:SUBLANE + n_ctx, :] = u_ref[0, 0:n_ctx, :]
        pad_ref[SUBLANE + n_ctx:x_off, :] = zeros
        pad_ref[x_off:x_off + n_x, :] = u_ref[0, n_ctx:l, :]
        pad_ref[x_off + n_x:x_off + n_x + SUBLANE, :] = zeros
        cw = cw_ref[...]
        for poff, yoff, n in ((SUBLANE, 0, n_ctx), (x_off, n_ctx, n_x)):
            def chunk(ci, carry, poff=poff, yoff=yoff):
                r0 = pl.multiple_of(ci * ROW_CHUNK, ROW_CHUNK)
                win = pad_ref[pl.ds(poff - SUBLANE + r0, ROW_CHUNK + 2 * SUBLANE), :]
                acc = win[SUBLANE - 2:SUBLANE - 2 + ROW_CHUNK, :] * cw[0:1, :]
                for j in range(1, CONV_K):
                    acc = acc + win[SUBLANE - 2 + j:SUBLANE - 2 + j + ROW_CHUNK, :] * cw[j:j + 1, :]
                y_ref[pl.ds(yoff + r0, ROW_CHUNK), :] = post(_silu(acc))
                return carry
            lax.fori_loop(0, n // ROW_CHUNK, chunk, 0)

    l2 = lambda y: y * lax.rsqrt(head_scale(y * y) + EPS)
    conv(q_ref, cwq_ref, yq_ref, lambda y: l2(y) * (hd ** -0.5))
    conv(k_ref, cwk_ref, yk_ref, l2)
    conv(v_ref, cwv_ref, yv_ref, lambda y: y)

    ba = ba_ref[0]
    g = -jnp.exp(gp_ref[0, 0:1, :]) * _softplus(ba + gp_ref[0, 1:2, :])
    bg_ref[...] = jnp.where(lane < 4, jax.nn.sigmoid(ba), g)

    ck = GDN_CHUNK
    rr = lax.broadcasted_iota(jnp.int32, (ck, ck), 0)
    cc = lax.broadcasted_iota(jnp.int32, (ck, ck), 1)
    eye = (rr == cc).astype(F32)
    tril = (rr >= cc).astype(F32)
    n_chunks = l // ck
    nc_ctx = n_ctx // ck

    def phase1(ci, carry):
        probs = []
        for gi in range(GDN_GROUP):
            c = ci * GDN_GROUP + gi
            r0 = pl.multiple_of(c * ck, ck)
            qc = yq_ref[pl.ds(r0, ck), :]
            kc = yk_ref[pl.ds(r0, ck), :]
            vc = yv_ref[pl.ds(r0, ck), :]
            bgc = bg_ref[pl.ds(r0, ck), :]
            gcf = _dot_hi(tril, bgc)
            gcb = gcf[ck - 1:ck, :] - gcf + bgc
            gct = (gcf.T, gcb.T)
            for j in range(2):
                qh = qc[:, j * hd:(j + 1) * hd]
                kh = kc[:, j * hd:(j + 1) * hd]
                vh = vc[:, j * hd:(j + 1) * hd]
                kk = _dot_nt(kh, kh)
                qk = _dot_nt(qh, kh)
                for d in range(2):
                    mask = (rr >= cc) if d == 0 else (rr <= cc)
                    smask = (rr > cc) if d == 0 else (rr < cc)
                    lg, lb = 4 + 2 * d + j, 2 * d + j
                    gcol = jnp.broadcast_to((gcf, gcb)[d][:, lg:lg + 1], (ck, hd))
                    grow = gct[d][lg:lg + 1, :]
                    bcol = jnp.broadcast_to(bgc[:, lb:lb + 1], (ck, hd))
                    decay = jnp.where(mask, jnp.exp(jnp.where(mask, gcol - grow, 0.0)), 0.0)
                    xp = jnp.where(smask, -(bcol * kk * decay), 0.0)
                    eg = jnp.exp(gcol)
                    last = ck - 1 if d == 0 else 0
                    glast = gcol[last:last + 1, :]
                    probs.append(dict(
                        idx=c * 4 + 2 * d + j, xp=xp, inv=eye + xp, attn=qk * decay, qd=qh * eg,
                        rhs=jnp.concatenate([vh * bcol, kh * bcol * eg], axis=1),
                        kd=kh * jnp.exp(glast - gcol), egl=jnp.exp(glast)))
        for _ in range(5):
            for p in probs:
                p["xp"] = _dot(p["xp"], p["xp"])
            for p in probs:
                p["inv"] = p["inv"] + _dot(p["inv"], p["xp"])
        for p in probs:
            p["sol"] = _dot(p["inv"], p["rhs"])
        for p in probs:
            p["as"] = _dot(p["attn"], p["sol"])
        for p in probs:
            p["ks"] = _dot_tn(p["kd"], p["sol"])
        for p in probs:
            i = p["idx"]
            qe_ref[i] = (p["qd"] - p["as"][:, hd:]).astype(BF16)
            oc_ref[i] = p["as"][:, :hd]
            mn_ref[i] = p["ks"][:, hd:].astype(BF16)
            nn_ref[i] = p["ks"][:, :hd]
            egl_ref[i] = jnp.broadcast_to(p["egl"], (ck, hd))
        return carry

    lax.fori_loop(0, n_chunks // GDN_GROUP, phase1, 0)

    s_ref[...] = jnp.zeros(s_ref.shape, F32)

    def phase2(i, carry):
        cb = jnp.where(i < nc_ctx, nc_ctx - 1 - i, n_chunks - 1 - (i - nc_ctx))
        work = [(d, j, cidx * 4 + 2 * d + j) for d, cidx in ((0, i), (1, cb)) for j in range(2)]
        states = [s_ref[2 * d + j] for d, j, _ in work]
        outs = [_dot(qe_ref[idx], s) + oc_ref[idx] for (_, _, idx), s in zip(work, states)]
        upd = [_dot(mn_ref[idx], s) for (_, _, idx), s in zip(work, states)]
        for (d, j, idx), s, m in zip(work, states, upd):
            s_ref[2 * d + j] = egl_ref[idx] * s - m + nn_ref[idx]
        of_ref[pl.ds(pl.multiple_of(i * ck, ck), ck), :] = jnp.concatenate(outs[0:2], axis=1)
        ob_ref[pl.ds(pl.multiple_of(cb * ck, ck), ck), :] = jnp.concatenate(outs[2:4], axis=1)
        return carry

    lax.fori_loop(0, n_chunks, phase2, 0)

    def finish(ci, carry):
        r0 = pl.multiple_of(ci * ROW_CHUNK, ROW_CHUNK)
        o = of_ref[pl.ds(r0, ROW_CHUNK), :] + ob_ref[pl.ds(r0, ROW_CHUNK), :]
        o = o * lax.rsqrt(head_scale(o * o) * (1.0 / hd) + EPS) * gain_ref[...]
        o_ref[0, pl.ds(r0, ROW_CHUNK), :] = o * _silu(z_ref[0, pl.ds(r0, ROW_CHUNK), :])
        return carry

    lax.fori_loop(0, l // ROW_CHUNK, finish, 0)


def _gdn(qkv, z, ba, conv_w, gparams, gain2, n_ctx):
    b, l, w3 = qkv.shape
    npair = w3 // (3 * LANE)
    seq = lambda off: pl.BlockSpec((1, l, LANE), lambda i, p: (i, 0, p + off))
    cw = lambda off: pl.BlockSpec((CONV_K, LANE), lambda i, p: (0, p + off))
    return pl.pallas_call(
        functools.partial(_gdn_kernel, n_ctx=n_ctx),
        grid=(b, npair),
        in_specs=[seq(0), seq(npair), seq(2 * npair), seq(0), seq(0),
                  cw(0), cw(npair), cw(2 * npair),
                  pl.BlockSpec((1, SUBLANE, LANE), lambda i, p: (p, 0, 0)),
                  pl.BlockSpec((1, LANE), lambda i, p: (0, 0))],
        out_specs=seq(0),
        out_shape=jax.ShapeDtypeStruct((b, l, npair * LANE), F32),
        scratch_shapes=[pltpu.VMEM((l + 3 * SUBLANE, LANE), F32)] + [pltpu.VMEM((l, LANE), F32)] * 6
        + [pltpu.VMEM((4, HEAD_DIM, HEAD_DIM), F32)]
        + [pltpu.VMEM((4 * l // GDN_CHUNK, GDN_CHUNK, HEAD_DIM), dt) for dt in (BF16, BF16, F32, F32, F32)],
        compiler_params=_cparams("parallel", "parallel"), name="gdn",
    )(qkv, qkv, qkv, z, ba, conv_w, conv_w, conv_w, gparams, gain2)


def _rope(x, cos, sin):
    w = x.shape[-1]
    reps = w // cos.shape[-1]
    if reps > 1:
        cos = jnp.concatenate([cos] * reps, axis=1)
        sin = jnp.concatenate([sin] * reps, axis=1)
    first = (lax.broadcasted_iota(jnp.int32, (1, w), 1) % (2 * ROPE_PAIRS)) < ROPE_PAIRS
    partner = jnp.where(first, pltpu.roll(x, w - ROPE_PAIRS, 1), pltpu.roll(x, ROPE_PAIRS, 1))
    return x * cos + partner * sin


def _attend(q, k, v, valid, sink_col):
    s = _dot_nt(q, k)
    if valid is not None:
        s = jnp.where(valid, s, NEG_INF)
    m = jnp.maximum(jnp.max(s, axis=-1, keepdims=True), sink_col)
    p = jnp.exp(s - m)
    den = jnp.sum(p, axis=-1, keepdims=True) + jnp.exp(sink_col - m)
    return _dot(p, v) / den


def _gqa(q, k_all, v_all, valid, sink_ref, o_ref):
    nq = q.shape[0]
    hd = HEAD_DIM
    group = q.shape[1] // hd // KV_HEADS
    outs = []
    for kvh in range(KV_HEADS):
        heads = [kvh * group + g for g in range(group)]
        qg = jnp.concatenate([q[:, h * hd:(h + 1) * hd] for h in heads], axis=0)
        sink_col = jnp.concatenate(
            [jnp.broadcast_to(sink_ref[0:1, h:h + 1], (nq, 1)) for h in heads], axis=0)
        o = _attend(qg, k_all[:, kvh * hd:(kvh + 1) * hd], v_all[:, kvh * hd:(kvh + 1) * hd],
                    valid, sink_col)
        outs += [o[g * nq:(g + 1) * nq, :] for g in range(group)]
    o_ref[0] = jnp.concatenate(outs, axis=1)


def _attn_kernel(q_ref, kp_ref, ko_ref, kn_ref, vp_ref, vo_ref, vn_ref, kc_ref, vc_ref,
                 cp_ref, sp_ref, co_ref, so_ref, cn_ref, sn_ref, sink_ref, o_ref, *, seq):
    i = pl.program_id(1)
    blk = ATTN_BLOCK
    n_ctx = kc_ref.shape[1]
    q = _rope(q_ref[0], co_ref[...], so_ref[...]) * (HEAD_DIM ** -0.5)
    k_all = jnp.concatenate([_rope(kp_ref[0], cp_ref[...], sp_ref[...]),
                             _rope(ko_ref[0], co_ref[...], so_ref[...]),
                             _rope(kn_ref[0], cn_ref[...], sn_ref[...]), kc_ref[0]], axis=0)
    v_all = jnp.concatenate([vp_ref[0], vo_ref[0], vn_ref[0], vc_ref[0]], axis=0)
    group = q.shape[1] // HEAD_DIM // KV_HEADS
    nk = 3 * blk + n_ctx
    qo = lax.broadcasted_iota(jnp.int32, (group * blk, 1), 0) % blk
    ko = lax.broadcasted_iota(jnp.int32, (1, nk), 1)
    kpos = (i - 1) * blk + ko
    valid = ((jnp.abs(ko - blk - qo) <= WINDOW) & (kpos >= 0) & (kpos < seq)) | (ko >= 3 * blk)
    _gqa(q, k_all, v_all, valid, sink_ref, o_ref)


def _attn_x(aq, ak, av, cos_t, sin_t, sink_row, n_ctx):
    b, l, wq = aq.shape
    wk = ak.shape[-1]
    blk = ATTN_BLOCK
    seq = l - n_ctx
    nb = seq // blk
    cb = n_ctx // blk
    kv = lambda f: pl.BlockSpec((1, blk, wk), lambda i, t: (i, f(t) + cb, 0))
    tab = lambda f: pl.BlockSpec((blk, LANE), lambda i, t: (f(t), 0))
    prev = lambda t: jnp.maximum(t - 1, 0)
    own = lambda t: t
    nxt = lambda t: jnp.minimum(t + 1, nb - 1)
    ctx = pl.BlockSpec((1, n_ctx, wk), lambda i, t: (i, 0, 0))
    return pl.pallas_call(
        functools.partial(_attn_kernel, seq=seq),
        grid=(b, nb),
        in_specs=[pl.BlockSpec((1, blk, wq), lambda i, t: (i, t + cb, 0)),
                  kv(prev), kv(own), kv(nxt), kv(prev), kv(own), kv(nxt), ctx, ctx,
                  tab(prev), tab(prev), tab(own), tab(own), tab(nxt), tab(nxt),
                  pl.BlockSpec((1, LANE), lambda i, t: (0, 0))],
        out_specs=pl.BlockSpec((1, blk, wq), lambda i, t: (i, t, 0)),
        out_shape=jax.ShapeDtypeStruct((b, seq, wq), F32),
        compiler_params=_cparams("parallel", "parallel"), name="attn_x",
    )(aq, ak, ak, ak, av, av, av, ak, av, cos_t, sin_t, cos_t, sin_t, cos_t, sin_t, sink_row)


def _attn_ctx_kernel(q_ref, k_ref, v_ref, sink_ref, o_ref):
    _gqa(q_ref[0] * (HEAD_DIM ** -0.5), k_ref[0], v_ref[0], None, sink_ref, o_ref)


def _attn_ctx(aq, ak, av, sink_row, n_ctx):
    b, _, wq = aq.shape
    wk = ak.shape[-1]
    spec = lambda w: pl.BlockSpec((1, n_ctx, w), lambda i: (i, 0, 0))
    return pl.pallas_call(
        _attn_ctx_kernel, grid=(b,),
        in_specs=[spec(wq), spec(wk), spec(wk), pl.BlockSpec((1, LANE), lambda i: (0, 0))],
        out_specs=spec(wq), out_shape=jax.ShapeDtypeStruct((b, n_ctx, wq), F32),
        compiler_params=_cparams("parallel"), name="attn_ctx",
    )(aq, ak, av, sink_row)


def _outproj_kernel(x_ref, ya_ref, yb_ref, yc_ref, w_ref, mod_ref, gain_ref, xo_ref, hx_ref, *, row_off, n_ctx):
    wa, wb = ya_ref.shape[-1], yb_ref.shape[-1]
    tile = x_ref.shape[1]
    acc = (jnp.dot(ya_ref[0].astype(BF16), w_ref[0:wa, :], preferred_element_type=F32)
           + jnp.dot(yb_ref[0].astype(BF16), w_ref[wa:wa + wb, :], preferred_element_type=F32)
           + jnp.dot(yc_ref[0].astype(BF16), w_ref[wa + wb:, :], preferred_element_type=F32))
    mod = _row_mod(mod_ref, row_off + pl.program_id(1) * tile, tile, n_ctx)
    x = x_ref[0] + mod(2) * acc
    xo_ref[0] = x
    hx_ref[0] = _modnorm(x, gain_ref[...], mod(4), mod(3)).astype(BF16)


def _out_proj(h, ya, yb, yc, w, mod, gain, n_ctx, skip_ctx):
    b, l, d = h.shape
    tile = TOK_TILE if skip_ctx else PROJ_TILE
    off = n_ctx // tile if skip_ctx else 0
    nt = l // tile - off
    full = lambda wd: pl.BlockSpec((1, tile, wd), lambda i, t: (i, t + off, 0))
    own = lambda wd: pl.BlockSpec((1, tile, wd), lambda i, t: (i, t, 0))
    return pl.pallas_call(
        functools.partial(_outproj_kernel, row_off=off * tile, n_ctx=n_ctx), grid=(b, nt),
        in_specs=[full(d), full(ya.shape[-1]), full(yb.shape[-1]), own(yc.shape[-1]),
                  pl.BlockSpec(w.shape, lambda i, t: (0, 0)),
                  pl.BlockSpec((1, 2, 6, d), lambda i, t: (i, 0, 0, 0)),
                  pl.BlockSpec((1, d), lambda i, t: (0, 0))],
        out_specs=[own(d), own(d)],
        out_shape=[jax.ShapeDtypeStruct((b, nt * tile, d), F32),
                   jax.ShapeDtypeStruct((b, nt * tile, d), BF16)],
        compiler_params=_cparams("parallel", "parallel"), name="out_proj",
    )(h, ya, yb, yc, w, mod, gain)


def _route_kernel(ht_ref, wq_ref, keys_ref, r2_ref, e2_ref, n1_ref, e1_ref, qt_ref, top_ref, cand_ref):
    tm = ht_ref.shape[1]
    k = PEER_TOPK
    nk = N_KEYS
    neg = jnp.float32(-jnp.inf)
    qt_ref[...] = _dot(wq_ref[...], ht_ref[...]).astype(BF16)
    cand_ref[_N_CAND:_CAND_PAD, :] = jnp.full((_CAND_PAD - _N_CAND, tm), neg, F32)
    for h in range(PEER_HEADS):
        st = []
        rank2 = jnp.full((nk, tm), float(k), F32)
        for p in range(2):
            hp = 2 * h + p
            s = jnp.dot(keys_ref[hp], qt_ref[hp * nk:(hp + 1) * nk, :], preferred_element_type=F32)
            st.append(s)
            cur = s
            for r in range(k):
                m = jnp.max(cur, axis=0, keepdims=True)
                top_ref[p * k + r:p * k + r + 1, :] = m
                hit = cur == m
                if p == 1:
                    rank2 = jnp.where(hit, float(r), rank2)
                cur = jnp.where(hit, neg, cur)
        v1 = top_ref[0:k, :]
        v2 = top_ref[k:2 * k, :]
        row = 0
        for a, nb in enumerate(_CAND_ROWS):
            cand_ref[row:row + nb, :] = v1[a:a + 1, :] + v2[0:nb, :]
            row += nb
        cur = cand_ref[...]
        cmax = thr = zsum = None
        for r in range(k):
            m = jnp.max(cur, axis=0, keepdims=True)
            if r == 0:
                cmax, zsum = m, jnp.ones_like(m)
            else:
                zsum = zsum + jnp.exp(m - cmax)
            thr = m
            cur = jnp.where(cur == m, neg, cur)
        n1 = jnp.zeros((nk, tm), F32)
        row = 0
        for a, nb in enumerate(_CAND_ROWS):
            sel = cand_ref[row:row + nb, :] >= thr
            n_a = jnp.sum(sel.astype(F32), axis=0, keepdims=True)
            n1 = jnp.where(st[0] == v1[a:a + 1, :], n_a, n1)
            row += nb
        rows = slice(h * nk, (h + 1) * nk)
        r2_ref[rows, :] = rank2
        e2_ref[rows, :] = jnp.exp(st[1] - v2[0:1, :])
        n1_ref[rows, :] = n1
        e1_ref[rows, :] = jnp.exp(st[0] - v1[0:1, :]) / zsum


def _peer_route(hxt, wq_t, keys):
    d, t = hxt.shape
    nq = wq_t.shape[0]
    tm = PEER_ROUTE_TILE
    rows = PEER_HEADS * N_KEYS
    return pl.pallas_call(
        _route_kernel, grid=(t // tm,),
        in_specs=[pl.BlockSpec((d, tm), lambda i: (0, i)),
                  pl.BlockSpec(wq_t.shape, lambda i: (0, 0)),
                  pl.BlockSpec(keys.shape, lambda i: (0, 0, 0))],
        out_specs=[pl.BlockSpec((rows, tm), lambda i: (0, i))] * 4,
        out_shape=[jax.ShapeDtypeStruct((rows, t), F32)] * 4,
        scratch_shapes=[pltpu.VMEM((nq, tm), BF16), pltpu.VMEM((2 * PEER_TOPK, tm), F32),
                        pltpu.VMEM((_CAND_PAD, tm), F32)],
        compiler_params=_cparams("parallel"), name="peer_route",
    )(hxt, wq_t, keys)


def _gates_kernel(r2_ref, e2_ref, n1_ref, e1_ref, w_ref, r2t_ref, e2t_ref, n1t_ref, e1t_ref):
    tb = w_ref.shape[1]
    nk = N_KEYS
    k = PEER_TOPK
    for src, dst in ((r2_ref, r2t_ref), (e2_ref, e2t_ref), (n1_ref, n1t_ref), (e1_ref, e1t_ref)):
        dst[...] = src[...].T
    rank = lax.broadcasted_iota(jnp.int32, (k, nk), 0).astype(F32)

    def tok(i, carry):
        for j in range(PEER_GATE_UNROLL):
            t = i * PEER_GATE_UNROLL + j
            r2, e2, n1, e1 = (ref[pl.ds(t, 1), :] for ref in (r2t_ref, e2t_ref, n1t_ref, e1t_ref))
            a, b = [], []
            for h in range(PEER_HEADS):
                ks = slice(h * nk, (h + 1) * nk)
                a.append(jnp.where(n1[:, ks] > rank, e1[:, ks], 0.0))
                b.append(jnp.where(r2[:, ks] == rank, e2[:, ks], 0.0))
            w = _dot_tn(jnp.concatenate(a, axis=0), jnp.concatenate(b, axis=0))
            for g in range(nk // SUBLANE):
                w_ref[g, t] = w[g * SUBLANE:(g + 1) * SUBLANE, :]
        return carry

    lax.fori_loop(0, tb // PEER_GATE_UNROLL, tok, 0)


def _peer_gates(route):
    rows, t = route[0].shape
    tb = PEER_GATE_TILE
    return pl.pallas_call(
        _gates_kernel, grid=(t // tb,),
        in_specs=[pl.BlockSpec((rows, tb), lambda i: (0, i))] * 4,
        out_specs=pl.BlockSpec((N_KEYS // SUBLANE, tb, SUBLANE, N_KEYS), lambda i: (0, i, 0, 0)),
        out_shape=jax.ShapeDtypeStruct((N_KEYS // SUBLANE, t, SUBLANE, N_KEYS), F32),
        scratch_shapes=[pltpu.VMEM((tb, rows), F32)] * 4,
        compiler_params=_cparams("parallel"), name="peer_gates",
    )(*route)


def _peer_kernel(h_ref, w_ref, ut_ref, v_ref, o_ref, acc_ref, g0_ref, g1_ref):
    e = pl.program_id(1)
    n_blk = pl.num_programs(1) - 1

    @pl.when(e == 0)
    def _():
        acc_ref[...] = jnp.zeros(acc_ref.shape, F32)
        g1_ref[...] = jnp.zeros(g1_ref.shape, BF16)

    @pl.when(e % 2 == 0)
    def _():
        _peer_step(h_ref, w_ref, ut_ref, v_ref, acc_ref, g0_ref, g1_ref)

    @pl.when(e % 2 == 1)
    def _():
        _peer_step(h_ref, w_ref, ut_ref, v_ref, acc_ref, g1_ref, g0_ref)

    @pl.when(e == n_blk)
    def _():
        o_ref[...] = acc_ref[...]


def _peer_step(h_ref, w_ref, ut_ref, v_ref, acc_ref, g_w_ref, g_r_ref):
    sr = h_ref.shape[0] // PEER_SUB

    def sub(k, carry):
        rows = pl.ds(pl.multiple_of(k * sr, sr), sr)
        acc_ref[rows, :] += jnp.dot(g_r_ref[rows, :], v_ref[...].astype(BF16), preferred_element_type=F32)
        act = jnp.dot(h_ref[rows, :], ut_ref[0], preferred_element_type=F32)
        act = 0.5 * act * (1.0 + lax.erf(act * 0.7071067811865476))
        for j in range(SUBLANE):
            ln = slice(j * N_KEYS, (j + 1) * N_KEYS)
            wj = w_ref[pl.ds(k * (sr * SUBLANE) + j, sr, stride=SUBLANE), :]
            g_w_ref[rows, ln] = (wj * act[:, ln]).astype(BF16)
        return carry

    lax.fori_loop(0, PEER_SUB, sub, 0)


def _peer_dense(hx, w, ut, v):
    t, d = hx.shape
    n_blk, _, eb = ut.shape
    tm = PEER_TOK_TILE
    last = n_blk - 1
    return pl.pallas_call(
        _peer_kernel, grid=(t // tm, n_blk + 1),
        in_specs=[pl.BlockSpec((tm, d), lambda i, e: (i, 0)),
                  pl.BlockSpec((tm * SUBLANE, N_KEYS), lambda i, e: (jnp.minimum(e, last) * (t // tm) + i, 0)),
                  pl.BlockSpec((1, d, eb), lambda i, e: (jnp.minimum(e, last), 0, 0)),
                  pl.BlockSpec((eb, d), lambda i, e: (jnp.maximum(e - 1, 0), 0))],
        out_specs=pl.BlockSpec((tm, d), lambda i, e: (i, 0)),
        out_shape=jax.ShapeDtypeStruct((t, d), F32),
        scratch_shapes=[pltpu.VMEM((tm, d), F32), pltpu.VMEM((tm, eb), BF16), pltpu.VMEM((tm, eb), BF16)],
        compiler_params=_cparams("parallel", "arbitrary"), name="peer_dense",
    )(hx, w, ut, v)


def _final_kernel(x_ref, f_ref, mod_ref, gain_ref, o_ref):
    x = x_ref[0] + mod_ref[0, 0, 5:6, :] * f_ref[0]
    o_ref[0] = x * lax.rsqrt(jnp.mean(x * x, axis=-1, keepdims=True) + EPS) * gain_ref[...]


def _final(x, ffn, mod, gain):
    b, s, d = x.shape
    tok = pl.BlockSpec((1, TOK_TILE, d), lambda i, t: (i, t, 0))
    return pl.pallas_call(
        _final_kernel, grid=(b, s // TOK_TILE),
        in_specs=[tok, tok, pl.BlockSpec((1, 1, 6, d), lambda i, t: (i, 1, 0, 0)),
                  pl.BlockSpec((1, d), lambda i, t: (0, 0))],
        out_specs=tok, out_shape=jax.ShapeDtypeStruct((b, s, d), F32),
        compiler_params=_cparams("parallel", "parallel"), name="final_norm",
    )(x, ffn, mod, gain)


def _rope_tables(seq):
    pos = jnp.arange(seq)
    rc = jnp.stack([pos // GRID_W, pos % GRID_W], axis=-1).astype(F32)
    inv = jnp.power(ROPE_BASE, -jnp.arange(ROPE_PAIRS, dtype=F32) / ROPE_PAIRS)
    ang = rc[:, :, None] * inv
    cos = jnp.concatenate([jnp.cos(ang)] * 2, axis=-1).reshape(seq, HEAD_DIM)
    sin = jnp.concatenate([-jnp.sin(ang), jnp.sin(ang)], axis=-1).reshape(seq, HEAD_DIM)
    return jnp.concatenate([cos] * 2, axis=1), jnp.concatenate([sin] * 2, axis=1)


def _pair_lanes(cols, n_heads):
    rows = cols.shape[0]
    c = cols.reshape(rows, 4, n_heads // 2, 2).transpose(0, 2, 1, 3).reshape(rows, n_heads // 2, 8)
    return jnp.pad(c, ((0, 0), (0, 0), (0, LANE - 8))).reshape(rows, n_heads // 2 * LANE)


def kernel(x, c, ctx, c_ctx, w_ada, b_ada, norm1, norm2, w_in, conv_w, a_log, dt_bias, gdn_norm, w_pool,
           pool_scale, sink, w_out, peer_wq, peer_keys, peer_u, peer_v, norm_f):
    b, seq, d = x.shape
    n_ctx = ctx.shape[1]
    depth = w_ada.shape[0]
    pool_w = d // 4
    gdn_w = 3 * d // 8
    gdn_heads = gdn_w // HEAD_DIM
    attn_w = d - pool_w - gdn_w
    kv_w = KV_HEADS * HEAD_DIM
    assert n_ctx % TOK_TILE == 0 and seq % TOK_TILE == 0 and gdn_heads % 2 == 0
    assert (n_ctx + seq) % PROJ_TILE == 0 and PEER_EXP_TILE == SUBLANE * N_KEYS
    assert (b * (n_ctx + seq)) % PEER_TOK_TILE == 0 and (b * seq) % PEER_TOK_TILE == 0

    rows = -(-(b + 1) // SUBLANE) * SUBLANE
    c_all = jnp.zeros((rows, d), F32).at[:b].set(c).at[b].set(c_ctx)
    mod = _ada_mod(c_all, w_ada, b_ada)
    mod_x = mod[:, :b].reshape(depth, b, 1, 6, d)
    mod_c = jnp.broadcast_to(mod[:, b].reshape(depth, 1, 1, 6, d), (depth, b, 1, 6, d))
    mod = jnp.concatenate([mod_c, mod_x], axis=2)

    cos_t, sin_t = _rope_tables(seq)
    splits = (pool_w, 3 * gdn_w, gdn_w, gdn_heads // 2 * LANE, attn_w, kv_w, kv_w)
    o_ba = pool_w + 4 * gdn_w
    n_ba = 4 * gdn_heads

    h = jnp.concatenate([ctx, x], axis=1)
    ffn = None
    for l in range(depth):
        last = l == depth - 1
        w_l = w_in[l]
        w_big = jnp.concatenate([w_l[:, :o_ba], _pair_lanes(w_l[:, o_ba:o_ba + n_ba], gdn_heads),
                                 w_l[:, o_ba + n_ba:]], axis=1).astype(BF16)
        pmod = mod[l - 1] if l > 0 else None
        outs = _in_proj(h, ffn, pmod, mod[l], norm1[l][None], w_big, splits, n_ctx)
        if l > 0:
            h, *outs = outs
        pa, qkv, z, ba, aq, ak, av = outs

        wbd = jax.scipy.linalg.block_diag(*[w_pool[l, g] for g in range(len(POOL_WINDOWS))])
        ya = _pool(pa, wbd, pool_scale[l][None], n_ctx)

        gparams = jnp.stack([_pair_lanes(jnp.concatenate([jnp.zeros_like(a_log[l]), a_log[l]]).reshape(1, -1),
                                         gdn_heads),
                             _pair_lanes(jnp.concatenate([jnp.zeros_like(dt_bias[l]), dt_bias[l]]).reshape(1, -1),
                                         gdn_heads)], axis=1)
        gparams = gparams.reshape(2, gdn_heads // 2, LANE).transpose(1, 0, 2)
        gparams = jnp.pad(gparams, ((0, 0), (0, SUBLANE - 2), (0, 0)))
        yb = _gdn(qkv, z, ba, conv_w[l], gparams, jnp.concatenate([gdn_norm[l]] * 2)[None], n_ctx)

        sink_row = jnp.pad(sink[l], (0, LANE - sink.shape[1]))[None]
        yc = _attn_x(aq, ak, av, cos_t, sin_t, sink_row, n_ctx)
        if not last:
            yc = jnp.concatenate([_attn_ctx(aq, ak, av, sink_row, n_ctx), yc], axis=1)

        x_new, hx = _out_proj(h, ya, yb, yc, w_out[l].astype(BF16), mod[l], norm2[l][None], n_ctx,
                              skip_ctx=last)
        t = hx.shape[0] * hx.shape[1]
        hx = hx.reshape(t, d)
        keys = peer_keys[l].reshape(2 * PEER_HEADS, N_KEYS, -1).astype(BF16)
        route = _peer_route(hx.T, peer_wq[l].T.astype(BF16), keys)
        gates = _peer_gates(route).reshape(-1, N_KEYS)
        ut = peer_u[l].reshape(-1, PEER_EXP_TILE, d).transpose(0, 2, 1).astype(BF16)
        ffn = _peer_dense(hx, gates, ut, peer_v[l])
        ffn = ffn.reshape(x_new.shape)
        h = x_new
    return _final(h, ffn, mod[depth - 1], norm_f[None])
```

```python
import functools

import jax
import jax.numpy as jnp
from jax import lax
from jax.experimental import pallas as pl
from jax.experimental.pallas import tpu as pltpu

F32, BF16 = jnp.float32, jnp.bfloat16
HIGHEST = lax.Precision.HIGHEST

EPS = 1e-6
NEG_INF = -1e30
HEAD_DIM = 64
GRID_W = 64
POOL_WINDOWS = (2, 4, 8, 16)
GDN_CHUNK = 64
CONV_K = 5
KV_HEADS = 2
WINDOW = 128
ATTN_BLOCK = 128
ROPE_BASE = 10000.0
ROPE_PAIRS = HEAD_DIM // 4
PEER_HEADS = 8
N_KEYS = 128
PEER_TOPK = 16

LANE = 128
SUBLANE = 8
TOK_TILE = 256
PROJ_TILE = 768
ROW_CHUNK = 256
GDN_GROUP = 4
PEER_ROUTE_TILE = 256
PEER_GATE_TILE = 128
PEER_GATE_UNROLL = 16
PEER_TOK_TILE = 1024
PEER_EXP_TILE = 1024
PEER_SUB = 2
VMEM_LIMIT = 48 * 1024 * 1024

_CAND_ROWS = tuple(PEER_TOPK // (a + 1) for a in range(PEER_TOPK))
_N_CAND = sum(_CAND_ROWS)
_CAND_PAD = -(-_N_CAND // SUBLANE) * SUBLANE


def _cparams(*sem):
    return pltpu.CompilerParams(dimension_semantics=sem, vmem_limit_bytes=VMEM_LIMIT)


def _dot(a, b):
    return jnp.dot(a.astype(BF16), b.astype(BF16), preferred_element_type=F32)


def _dot_nt(a, b):
    return lax.dot_general(a.astype(BF16), b.astype(BF16), (((1,), (1,)), ((), ())),
                           preferred_element_type=F32)


def _dot_tn(a, b):
    return lax.dot_general(a.astype(BF16), b.astype(BF16), (((0,), (0,)), ((), ())),
                           preferred_element_type=F32)


def _dot_hi(a, b):
    return jnp.dot(a, b, precision=HIGHEST, preferred_element_type=F32)


def _silu(x):
    return x * jax.nn.sigmoid(x)


def _softplus(x):
    return jnp.maximum(x, 0.0) + jnp.log1p(jnp.exp(-jnp.abs(x)))


def _modnorm(x, gain, scale, shift):
    y = x * lax.rsqrt(jnp.mean(x * x, axis=-1, keepdims=True) + EPS) * gain
    return y * (1.0 + scale) + shift


def _ada_kernel(c_ref, w_ref, b_ref, o_ref):
    o_ref[0] = _dot_hi(_silu(c_ref[...]), w_ref[0]) + b_ref[0]


def _ada_mod(c_all, w_ada, b_ada):
    depth, d, n = w_ada.shape
    rows = c_all.shape[0]
    tn = n // 4
    return pl.pallas_call(
        _ada_kernel,
        grid=(depth, n // tn),
        in_specs=[pl.BlockSpec((rows, d), lambda l, j: (0, 0)),
                  pl.BlockSpec((1, d, tn), lambda l, j: (l, 0, j)),
                  pl.BlockSpec((1, 1, tn), lambda l, j: (l, 0, j))],
        out_specs=pl.BlockSpec((1, rows, tn), lambda l, j: (l, 0, j)),
        out_shape=jax.ShapeDtypeStruct((depth, rows, n), F32),
        compiler_params=_cparams("parallel", "parallel"),
        name="ada_mod",
    )(c_all, w_ada, b_ada.reshape(depth, 1, n))


def _row_mod(mod_ref, first_row, n_rows, n_ctx):
    is_x = first_row + lax.broadcasted_iota(jnp.int32, (n_rows, 1), 0) >= n_ctx
    return lambda k: jnp.where(is_x, mod_ref[0, 1, k:k + 1, :], mod_ref[0, 0, k:k + 1, :])


def _inproj_kernel(*refs, has_ffn, col_splits, n_ctx):
    tile = refs[0].shape[1]
    first_row = pl.program_id(1) * tile
    if has_ffn:
        h_ref, f_ref, pmod_ref, mod_ref, gain_ref, w_ref, res_ref, *outs = refs
        x = h_ref[0] + _row_mod(pmod_ref, first_row, tile, n_ctx)(5) * f_ref[0]
        res_ref[0] = x
    else:
        h_ref, mod_ref, gain_ref, w_ref, *outs = refs
        x = h_ref[0]
    mod = _row_mod(mod_ref, first_row, tile, n_ctx)
    hb = _modnorm(x, gain_ref[...], mod(1), mod(0)).astype(BF16)
    off = 0
    for o_ref, width in zip(outs, col_splits):
        o_ref[0] = jnp.dot(hb, w_ref[:, off:off + width], preferred_element_type=F32)
        off += width


def _in_proj(h, ffn, pmod, mod, gain, w, col_splits, n_ctx):
    b, l, d = h.shape
    tile = PROJ_TILE
    tok = pl.BlockSpec((1, tile, d), lambda i, t: (i, t, 0))
    modspec = pl.BlockSpec((1, 2, 6, d), lambda i, t: (i, 0, 0, 0))
    has_ffn = ffn is not None
    in_specs = [tok] + ([tok, modspec] if has_ffn else []) + [
        modspec, pl.BlockSpec((1, d), lambda i, t: (0, 0)), pl.BlockSpec(w.shape, lambda i, t: (0, 0))]
    out_shape = [jax.ShapeDtypeStruct((b, l, wd), F32) for wd in col_splits]
    out_specs = [pl.BlockSpec((1, tile, wd), lambda i, t: (i, t, 0)) for wd in col_splits]
    if has_ffn:
        out_shape = [jax.ShapeDtypeStruct((b, l, d), F32)] + out_shape
        out_specs = [tok] + out_specs
    args = (h, ffn, pmod, mod, gain, w) if has_ffn else (h, mod, gain, w)
    return pl.pallas_call(
        functools.partial(_inproj_kernel, has_ffn=has_ffn, col_splits=col_splits, n_ctx=n_ctx),
        grid=(b, l // tile), in_specs=in_specs, out_specs=out_specs, out_shape=out_shape,
        compiler_params=_cparams("parallel", "parallel"), name="in_proj",
    )(*args)


def _pool_kernel(a_ref, wbd_ref, scale_ref, o_ref, pad_ref, *, segments):
    c = a_ref.shape[-1]
    grp = lax.broadcasted_iota(jnp.int32, (1, c), 1) // (c // len(POOL_WINDOWS))
    zeros = jnp.zeros((SUBLANE, c), F32)
    for off, n in segments:
        pad_ref[0:SUBLANE, :] = zeros
        pad_ref[SUBLANE:SUBLANE + n, :] = a_ref[0, off:off + n, :]
        pad_ref[SUBLANE + n:2 * SUBLANE + n, :] = zeros

        def chunk(ci, carry, off=off, n=n):
            r0 = pl.multiple_of(ci * ROW_CHUNK, ROW_CHUNK)
            win = pad_ref[pl.ds(r0, ROW_CHUNK + 2 * SUBLANE), :]
            sh = lambda s: win[SUBLANE + s:SUBLANE + s + ROW_CHUNK, :]
            a0 = sh(0)
            s2 = sh(-1) + a0
            s4 = s2 + sh(-2) + sh(1)
            s8 = s4 + sh(-4) + sh(-3) + sh(2) + sh(3)
            s16 = s8 + (sh(-8) + sh(-7) + sh(-6) + sh(-5)) + (sh(4) + sh(5) + sh(6) + sh(7))
            t = r0 + lax.broadcasted_iota(jnp.int32, (ROW_CHUNK, 1), 0)

            def mean(s, w):
                cnt = jnp.minimum(t + (w - w // 2), n) - jnp.maximum(t - w // 2, 0)
                return s / cnt.astype(F32)

            pooled = jnp.where(grp == 0, mean(s2, 2), jnp.where(grp == 1, mean(s4, 4),
                               jnp.where(grp == 2, mean(s8, 8), mean(s16, 16)))) - a0
            o_ref[0, pl.ds(off + r0, ROW_CHUNK), :] = _dot_hi(pooled, wbd_ref[...]) * scale_ref[...]
            return carry

        lax.fori_loop(0, n // ROW_CHUNK, chunk, 0)


def _pool(a, wbd, scale, n_ctx):
    b, l, c = a.shape
    segments = ((0, n_ctx), (n_ctx, l - n_ctx))
    return pl.pallas_call(
        functools.partial(_pool_kernel, segments=segments),
        grid=(b,),
        in_specs=[pl.BlockSpec((1, l, c), lambda i: (i, 0, 0)),
                  pl.BlockSpec((c, c), lambda i: (0, 0)),
                  pl.BlockSpec((1, c), lambda i: (0, 0))],
        out_specs=pl.BlockSpec((1, l, c), lambda i: (i, 0, 0)),
        out_shape=jax.ShapeDtypeStruct((b, l, c), F32),
        scratch_shapes=[pltpu.VMEM((l - n_ctx + 2 * SUBLANE, c), F32)],
        compiler_params=_cparams("parallel"), name="pool",
    )(a, wbd, scale)


def _gdn_kernel(q_ref, k_ref, v_ref, z_ref, ba_ref, cwq_ref, cwk_ref, cwv_ref, gp_ref, gain_ref, o_ref,
                pad_ref, yq_ref, yk_ref, yv_ref, bg_ref, of_ref, ob_ref, s_ref,
                qe_ref, mn_ref, oc_ref, nn_ref, egl_ref, *, n_ctx):
    l = q_ref.shape[1]
    n_x = l - n_ctx
    hd = HEAD_DIM
    lane = lax.broadcasted_iota(jnp.int32, (1, LANE), 1)
    lo = lane < hd
    zeros = jnp.zeros((SUBLANE, LANE), F32)
    x_off = 2 * SUBLANE + n_ctx

    def head_scale(ss):
        s_lo = jnp.sum(jnp.where(lo, ss, 0.0), axis=-1, keepdims=True)
        s_hi = jnp.sum(jnp.where(lo, 0.0, ss), axis=-1, keepdims=True)
        return jnp.where(lo, s_lo, s_hi)

    def conv(u_ref, cw_ref, y_ref, post):
        pad_ref[0:SUBLANE, :] = zeros
        pad_ref[SUBLANE:SUBLANE + n_ctx, :] = u_ref[0, 0:n_ctx, :]
        pad_ref[SUBLANE + n_ctx:x_off, :] = zeros
        pad_ref[x_off:x_off + n_x, :] = u_ref[0, n_ctx:l, :]
        pad_ref[x_off + n_x:x_off + n_x + SUBLANE, :] = zeros
        cw = cw_ref[...]
        for poff, yoff, n in ((SUBLANE, 0, n_ctx), (x_off, n_ctx, n_x)):
            def chunk(ci, carry, poff=poff, yoff=yoff):
                r0 = pl.multiple_of(ci * ROW_CHUNK, ROW_CHUNK)
                win = pad_ref[pl.ds(poff - SUBLANE + r0, ROW_CHUNK + 2 * SUBLANE), :]
                acc = win[SUBLANE - 2:SUBLANE - 2 + ROW_CHUNK, :] * cw[0:1, :]
                for j in range(1, CONV_K):
                    acc = acc + win[SUBLANE - 2 + j:SUBLANE - 2 + j + ROW_CHUNK, :] * cw[j:j + 1, :]
                y_ref[pl.ds(yoff + r0, ROW_CHUNK), :] = post(_silu(acc))
                return carry
            lax.fori_loop(0, n // ROW_CHUNK, chunk, 0)

    l2 = lambda y: y * lax.rsqrt(head_scale(y * y) + EPS)
    conv(q_ref, cwq_ref, yq_ref, lambda y: l2(y) * (hd ** -0.5))
    conv(k_ref, cwk_ref, yk_ref, l2)
    conv(v_ref, cwv_ref, yv_ref, lambda y: y)

    ba = ba_ref[0]
    g = -jnp.exp(gp_ref[0, 0:1, :]) * _softplus(ba + gp_ref[0, 1:2, :])
    bg_ref[...] = jnp.where(lane < 4, jax.nn.sigmoid(ba), g)

    ck = GDN_CHUNK
    rr = lax.broadcasted_iota(jnp.int32, (ck, ck), 0)
    cc = lax.broadcasted_iota(jnp.int32, (ck, ck), 1)
    eye = (rr == cc).astype(F32)
    tril = (rr >= cc).astype(F32)
    n_chunks = l // ck
    nc_ctx = n_ctx // ck

    def phase1(ci, carry):
        probs = []
        for gi in range(GDN_GROUP):
            c = ci * GDN_GROUP + gi
            r0 = pl.multiple_of(c * ck, ck)
            qc = yq_ref[pl.ds(r0, ck), :]
            kc = yk_ref[pl.ds(r0, ck), :]
            vc = yv_ref[pl.ds(r0, ck), :]
            bgc = bg_ref[pl.ds(r0, ck), :]
            gcf = _dot_hi(tril, bgc)
            gcb = gcf[ck - 1:ck, :] - gcf + bgc
            gct = (gcf.T, gcb.T)
            for j in range(2):
                qh = qc[:, j * hd:(j + 1) * hd]
                kh = kc[:, j * hd:(j + 1) * hd]
                vh = vc[:, j * hd:(j + 1) * hd]
                kk = _dot_nt(kh, kh)
                qk = _dot_nt(qh, kh)
                for d in range(2):
                    mask = (rr >= cc) if d == 0 else (rr <= cc)
                    smask = (rr > cc) if d == 0 else (rr < cc)
                    lg, lb = 4 + 2 * d + j, 2 * d + j
                    gcol = jnp.broadcast_to((gcf, gcb)[d][:, lg:lg + 1], (ck, hd))
                    grow = gct[d][lg:lg + 1, :]
                    bcol = jnp.broadcast_to(bgc[:, lb:lb + 1], (ck, hd))
                    decay = jnp.where(mask, jnp.exp(jnp.where(mask, gcol - grow, 0.0)), 0.0)
                    xp = jnp.where(smask, -(bcol * kk * decay), 0.0)
                    eg = jnp.exp(gcol)
                    last = ck - 1 if d == 0 else 0
                    glast = gcol[last:last + 1, :]
                    probs.append(dict(
                        idx=c * 4 + 2 * d + j, xp=xp, inv=eye + xp, attn=qk * decay, qd=qh * eg,
                        rhs=jnp.concatenate([vh * bcol, kh * bcol * eg], axis=1),
                        kd=kh * jnp.exp(glast - gcol), egl=jnp.exp(glast)))
        for _ in range(5):
            for p in probs:
                p["xp"] = _dot(p["xp"], p["xp"])
            for p in probs:
                p["inv"] = p["inv"] + _dot(p["inv"], p["xp"])
        for p in probs:
            p["sol"] = _dot(p["inv"], p["rhs"])
        for p in probs:
            p["as"] = _dot(p["attn"], p["sol"])
        for p in probs:
            p["ks"] = _dot_tn(p["kd"], p["sol"])
        for p in probs:
            i = p["idx"]
            qe_ref[i] = (p["qd"] - p["as"][:, hd:]).astype(BF16)
            oc_ref[i] = p["as"][:, :hd]
            mn_ref[i] = p["ks"][:, hd:].astype(BF16)
            nn_ref[i] = p["ks"][:, :hd]
            egl_ref[i] = jnp.broadcast_to(p["egl"], (ck, hd))
        return carry

    lax.fori_loop(0, n_chunks // GDN_GROUP, phase1, 0)

    s_ref[...] = jnp.zeros(s_ref.shape, F32)

    def phase2(i, carry):
        cb = jnp.where(i < nc_ctx, nc_ctx - 1 - i, n_chunks - 1 - (i - nc_ctx))
        work = [(d, j, cidx * 4 + 2 * d + j) for d, cidx in ((0, i), (1, cb)) for j in range(2)]
        states = [s_ref[2 * d + j] for d, j, _ in work]
        outs = [_dot(qe_ref[idx], s) + oc_ref[idx] for (_, _, idx), s in zip(work, states)]
        upd = [_dot(mn_ref[idx], s) for (_, _, idx), s in zip(work, states)]
        for (d, j, idx), s, m in zip(work, states, upd):
            s_ref[2 * d + j] = egl_ref[idx] * s - m + nn_ref[idx]
        of_ref[pl.ds(pl.multiple_of(i * ck, ck), ck), :] = jnp.concatenate(outs[0:2], axis=1)
        ob_ref[pl.ds(pl.multiple_of(cb * ck, ck), ck), :] = jnp.concatenate(outs[2:4], axis=1)
        return carry

    lax.fori_loop(0, n_chunks, phase2, 0)

    def finish(ci, carry):
        r0 = pl.multiple_of(ci * ROW_CHUNK, ROW_CHUNK)
        o = of_ref[pl.ds(r0, ROW_CHUNK), :] + ob_ref[pl.ds(r0, ROW_CHUNK), :]
        o = o * lax.rsqrt(head_scale(o * o) * (1.0 / hd) + EPS) * gain_ref[...]
        o_ref[0, pl.ds(r0, ROW_CHUNK), :] = o * _silu(z_ref[0, pl.ds(r0, ROW_CHUNK), :])
        return carry

    lax.fori_loop(0, l // ROW_CHUNK, finish, 0)


def _gdn(qkv, z, ba, conv_w, gparams, gain2, n_ctx):
    b, l, w3 = qkv.shape
    npair = w3 // (3 * LANE)
    seq = lambda off: pl.BlockSpec((1, l, LANE), lambda i, p: (i, 0, p + off))
    cw = lambda off: pl.BlockSpec((CONV_K, LANE), lambda i, p: (0, p + off))
    return pl.pallas_call(
        functools.partial(_gdn_kernel, n_ctx=n_ctx),
        grid=(b, npair),
        in_specs=[seq(0), seq(npair), seq(2 * npair), seq(0), seq(0),
                  cw(0), cw(npair), cw(2 * npair),
                  pl.BlockSpec((1, SUBLANE, LANE), lambda i, p: (p, 0, 0)),
                  pl.BlockSpec((1, LANE), lambda i, p: (0, 0))],
        out_specs=seq(0),
        out_shape=jax.ShapeDtypeStruct((b, l, npair * LANE), F32),
        scratch_shapes=[pltpu.VMEM((l + 3 * SUBLANE, LANE), F32)] + [pltpu.VMEM((l, LANE), F32)] * 6
        + [pltpu.VMEM((4, HEAD_DIM, HEAD_DIM), F32)]
        + [pltpu.VMEM((4 * l // GDN_CHUNK, GDN_CHUNK, HEAD_DIM), dt) for dt in (BF16, BF16, F32, F32, F32)],
        compiler_params=_cparams("parallel", "parallel"), name="gdn",
    )(qkv, qkv, qkv, z, ba, conv_w, conv_w, conv_w, gparams, gain2)


def _rope(x, cos, sin):
    w = x.shape[-1]
    reps = w // cos.shape[-1]
    if reps > 1:
        cos = jnp.concatenate([cos] * reps, axis=1)
        sin = jnp.concatenate([sin] * reps, axis=1)
    first = (lax.broadcasted_iota(jnp.int32, (1, w), 1) % (2 * ROPE_PAIRS)) < ROPE_PAIRS
    partner = jnp.where(first, pltpu.roll(x, w - ROPE_PAIRS, 1), pltpu.roll(x, ROPE_PAIRS, 1))
    return x * cos + partner * sin


def _attend(q, k, v, valid, sink_col):
    s = _dot_nt(q, k)
    if valid is not None:
        s = jnp.where(valid, s, NEG_INF)
    m = jnp.maximum(jnp.max(s, axis=-1, keepdims=True), sink_col)
    p = jnp.exp(s - m)
    den = jnp.sum(p, axis=-1, keepdims=True) + jnp.exp(sink_col - m)
    return _dot(p, v) / den


def _gqa(q, k_all, v_all, valid, sink_ref, o_ref):
    nq = q.shape[0]
    hd = HEAD_DIM
    group = q.shape[1] // hd // KV_HEADS
    outs = []
    for kvh in range(KV_HEADS):
        heads = [kvh * group + g for g in range(group)]
        qg = jnp.concatenate([q[:, h * hd:(h + 1) * hd] for h in heads], axis=0)
        sink_col = jnp.concatenate(
            [jnp.broadcast_to(sink_ref[0:1, h:h + 1], (nq, 1)) for h in heads], axis=0)
        o = _attend(qg, k_all[:, kvh * hd:(kvh + 1) * hd], v_all[:, kvh * hd:(kvh + 1) * hd],
                    valid, sink_col)
        outs += [o[g * nq:(g + 1) * nq, :] for g in range(group)]
    o_ref[0] = jnp.concatenate(outs, axis=1)


def _attn_kernel(q_ref, kp_ref, ko_ref, kn_ref, vp_ref, vo_ref, vn_ref, kc_ref, vc_ref,
                 cp_ref, sp_ref, co_ref, so_ref, cn_ref, sn_ref, sink_ref, o_ref, *, seq):
    i = pl.program_id(1)
    blk = ATTN_BLOCK
    n_ctx = kc_ref.shape[1]
    q = _rope(q_ref[0], co_ref[...], so_ref[...]) * (HEAD_DIM ** -0.5)
    k_all = jnp.concatenate([_rope(kp_ref[0], cp_ref[...], sp_ref[...]),
                             _rope(ko_ref[0], co_ref[...], so_ref[...]),
                             _rope(kn_ref[0], cn_ref[...], sn_ref[...]), kc_ref[0]], axis=0)
    v_all = jnp.concatenate([vp_ref[0], vo_ref[0], vn_ref[0], vc_ref[0]], axis=0)
    group = q.shape[1] // HEAD_DIM // KV_HEADS
    nk = 3 * blk + n_ctx
    qo = lax.broadcasted_iota(jnp.int32, (group * blk, 1), 0) % blk
    ko = lax.broadcasted_iota(jnp.int32, (1, nk), 1)
    kpos = (i - 1) * blk + ko
    valid = ((jnp.abs(ko - blk - qo) <= WINDOW) & (kpos >= 0) & (kpos < seq)) | (ko >= 3 * blk)
    _gqa(q, k_all, v_all, valid, sink_ref, o_ref)


def _attn_x(aq, ak, av, cos_t, sin_t, sink_row, n_ctx):
    b, l, wq = aq.shape
    wk = ak.shape[-1]
    blk = ATTN_BLOCK
    seq = l - n_ctx
    nb = seq // blk
    cb = n_ctx // blk
    kv = lambda f: pl.BlockSpec((1, blk, wk), lambda i, t: (i, f(t) + cb, 0))
    tab = lambda f: pl.BlockSpec((blk, LANE), lambda i, t: (f(t), 0))
    prev = lambda t: jnp.maximum(t - 1, 0)
    own = lambda t: t
    nxt = lambda t: jnp.minimum(t + 1, nb - 1)
    ctx = pl.BlockSpec((1, n_ctx, wk), lambda i, t: (i, 0, 0))
    return pl.pallas_call(
        functools.partial(_attn_kernel, seq=seq),
        grid=(b, nb),
        in_specs=[pl.BlockSpec((1, blk, wq), lambda i, t: (i, t + cb, 0)),
                  kv(prev), kv(own), kv(nxt), kv(prev), kv(own), kv(nxt), ctx, ctx,
                  tab(prev), tab(prev), tab(own), tab(own), tab(nxt), tab(nxt),
                  pl.BlockSpec((1, LANE), lambda i, t: (0, 0))],
        out_specs=pl.BlockSpec((1, blk, wq), lambda i, t: (i, t, 0)),
        out_shape=jax.ShapeDtypeStruct((b, seq, wq), F32),
        compiler_params=_cparams("parallel", "parallel"), name="attn_x",
    )(aq, ak, ak, ak, av, av, av, ak, av, cos_t, sin_t, cos_t, sin_t, cos_t, sin_t, sink_row)


def _attn_ctx_kernel(q_ref, k_ref, v_ref, sink_ref, o_ref):
    _gqa(q_ref[0] * (HEAD_DIM ** -0.5), k_ref[0], v_ref[0], None, sink_ref, o_ref)


def _attn_ctx(aq, ak, av, sink_row, n_ctx):
    b, _, wq = aq.shape
    wk = ak.shape[-1]
    spec = lambda w: pl.BlockSpec((1, n_ctx, w), lambda i: (i, 0, 0))
    return pl.pallas_call(
        _attn_ctx_kernel, grid=(b,),
        in_specs=[spec(wq), spec(wk), spec(wk), pl.BlockSpec((1, LANE), lambda i: (0, 0))],
        out_specs=spec(wq), out_shape=jax.ShapeDtypeStruct((b, n_ctx, wq), F32),
        compiler_params=_cparams("parallel"), name="attn_ctx",
    )(aq, ak, av, sink_row)


def _outproj_kernel(x_ref, ya_ref, yb_ref, yc_ref, w_ref, mod_ref, gain_ref, xo_ref, hx_ref, *, row_off, n_ctx):
    wa, wb = ya_ref.shape[-1], yb_ref.shape[-1]
    tile = x_ref.shape[1]
    acc = (jnp.dot(ya_ref[0].astype(BF16), w_ref[0:wa, :], preferred_element_type=F32)
           + jnp.dot(yb_ref[0].astype(BF16), w_ref[wa:wa + wb, :], preferred_element_type=F32)
           + jnp.dot(yc_ref[0].astype(BF16), w_ref[wa + wb:, :], preferred_element_type=F32))
    mod = _row_mod(mod_ref, row_off + pl.program_id(1) * tile, tile, n_ctx)
    x = x_ref[0] + mod(2) * acc
    xo_ref[0] = x
    hx_ref[0] = _modnorm(x, gain_ref[...], mod(4), mod(3)).astype(BF16)


def _out_proj(h, ya, yb, yc, w, mod, gain, n_ctx, skip_ctx):
    b, l, d = h.shape
    tile = TOK_TILE if skip_ctx else PROJ_TILE
    off = n_ctx // tile if skip_ctx else 0
    nt = l // tile - off
    full = lambda wd: pl.BlockSpec((1, tile, wd), lambda i, t: (i, t + off, 0))
    own = lambda wd: pl.BlockSpec((1, tile, wd), lambda i, t: (i, t, 0))
    return pl.pallas_call(
        functools.partial(_outproj_kernel, row_off=off * tile, n_ctx=n_ctx), grid=(b, nt),
        in_specs=[full(d), full(ya.shape[-1]), full(yb.shape[-1]), own(yc.shape[-1]),
                  pl.BlockSpec(w.shape, lambda i, t: (0, 0)),
                  pl.BlockSpec((1, 2, 6, d), lambda i, t: (i, 0, 0, 0)),
                  pl.BlockSpec((1, d), lambda i, t: (0, 0))],
        out_specs=[own(d), own(d)],
        out_shape=[jax.ShapeDtypeStruct((b, nt * tile, d), F32),
                   jax.ShapeDtypeStruct((b, nt * tile, d), BF16)],
        compiler_params=_cparams("parallel", "parallel"), name="out_proj",
    )(h, ya, yb, yc, w, mod, gain)


def _sort_pairs(n):
    pairs, p = [], 1
    while p < n:
        k = p
        while k >= 1:
            for j in range(k % p, n - k, 2 * k):
                for i in range(min(k, n - j - k)):
                    if (i + j) // (2 * p) == (i + j + k) // (2 * p):
                        pairs.append((i + j, i + j + k))
            k //= 2
        p *= 2
    return pairs


def _topk_rows(v):
    n = len(v)
    v = list(v)

    def cmpx(i, j):
        v[i], v[j] = jnp.maximum(v[i], v[j]), jnp.minimum(v[i], v[j])

    for i, j in _sort_pairs(n):
        cmpx(i, j)
    shift = SUBLANE // 2
    while shift >= 1:
        partner = [pltpu.roll(x, shift, 0) for x in v]
        v = [jnp.maximum(v[j], partner[n - 1 - j]) for j in range(n)]
        stride = n // 2
        while stride >= 1:
            for i in range(n):
                if not i & stride:
                    cmpx(i, i + stride)
            stride //= 2
        shift //= 2
    return v


def _route_kernel(ht_ref, wq_ref, keys_ref, r2_ref, e2_ref, n1_ref, e1_ref, qt_ref, top_ref, cand_ref):
    tm = ht_ref.shape[1]
    k = PEER_TOPK
    nk = N_KEYS
    neg = jnp.float32(-jnp.inf)
    qt_ref[...] = _dot(wq_ref[...], ht_ref[...]).astype(BF16)
    cand_ref[_N_CAND:_CAND_PAD, :] = jnp.full((_CAND_PAD - _N_CAND, tm), neg, F32)
    for h in range(PEER_HEADS):
        st = []
        for p in range(2):
            hp = 2 * h + p
            s = jnp.dot(keys_ref[hp], qt_ref[hp * nk:(hp + 1) * nk, :], preferred_element_type=F32)
            st.append(s)
            top = _topk_rows([s[g * SUBLANE:(g + 1) * SUBLANE, :] for g in range(nk // SUBLANE)])
            for r in range(k):
                top_ref[p * k + r:p * k + r + 1, :] = top[r][0:1, :]
        v1 = top_ref[0:k, :]
        v2 = top_ref[k:2 * k, :]
        rank2 = jnp.full((nk, tm), float(k), F32)
        for r in range(k):
            rank2 = jnp.where(st[1] == v2[r:r + 1, :], float(r), rank2)
        row = 0
        for a, nb in enumerate(_CAND_ROWS):
            cand_ref[row:row + nb, :] = v1[a:a + 1, :] + v2[0:nb, :]
            row += nb
        cur = cand_ref[...]
        cmax = thr = zsum = None
        for r in range(k):
            m = jnp.max(cur, axis=0, keepdims=True)
            if r == 0:
                cmax, zsum = m, jnp.ones_like(m)
            else:
                zsum = zsum + jnp.exp(m - cmax)
            thr = m
            cur = jnp.where(cur == m, neg, cur)
        n1 = jnp.zeros((nk, tm), F32)
        row = 0
        for a, nb in enumerate(_CAND_ROWS):
            sel = cand_ref[row:row + nb, :] >= thr
            n_a = jnp.sum(sel.astype(F32), axis=0, keepdims=True)
            n1 = jnp.where(st[0] == v1[a:a + 1, :], n_a, n1)
            row += nb
        rows = slice(h * nk, (h + 1) * nk)
        r2_ref[rows, :] = rank2
        e2_ref[rows, :] = jnp.exp(st[1] - v2[0:1, :])
        n1_ref[rows, :] = n1
        e1_ref[rows, :] = jnp.exp(st[0] - v1[0:1, :]) / zsum


def _peer_route(hxt, wq_t, keys):
    d, t = hxt.shape
    nq = wq_t.shape[0]
    tm = PEER_ROUTE_TILE
    rows = PEER_HEADS * N_KEYS
    return pl.pallas_call(
        _route_kernel, grid=(t // tm,),
        in_specs=[pl.BlockSpec((d, tm), lambda i: (0, i)),
                  pl.BlockSpec(wq_t.shape, lambda i: (0, 0)),
                  pl.BlockSpec(keys.shape, lambda i: (0, 0, 0))],
        out_specs=[pl.BlockSpec((rows, tm), lambda i: (0, i))] * 4,
        out_shape=[jax.ShapeDtypeStruct((rows, t), F32)] * 4,
        scratch_shapes=[pltpu.VMEM((nq, tm), BF16), pltpu.VMEM((2 * PEER_TOPK, tm), F32),
                        pltpu.VMEM((_CAND_PAD, tm), F32)],
        compiler_params=_cparams("parallel"), name="peer_route",
    )(hxt, wq_t, keys)


def _gates_kernel(r2_ref, e2_ref, n1_ref, e1_ref, w_ref, r2t_ref, e2t_ref, n1t_ref, e1t_ref):
    tb = w_ref.shape[1]
    nk = N_KEYS
    k = PEER_TOPK
    for src, dst in ((r2_ref, r2t_ref), (e2_ref, e2t_ref), (n1_ref, n1t_ref), (e1_ref, e1t_ref)):
        dst[...] = src[...].T
    rank = lax.broadcasted_iota(jnp.int32, (k, nk), 0).astype(F32)

    def tok(i, carry):
        for j in range(PEER_GATE_UNROLL):
            t = i * PEER_GATE_UNROLL + j
            r2, e2, n1, e1 = (ref[pl.ds(t, 1), :] for ref in (r2t_ref, e2t_ref, n1t_ref, e1t_ref))
            a, b = [], []
            for h in range(PEER_HEADS):
                ks = slice(h * nk, (h + 1) * nk)
                a.append(jnp.where(n1[:, ks] > rank, e1[:, ks], 0.0))
                b.append(jnp.where(r2[:, ks] == rank, e2[:, ks], 0.0))
            w = _dot_tn(jnp.concatenate(a, axis=0), jnp.concatenate(b, axis=0))
            for g in range(nk // SUBLANE):
                w_ref[g, t] = w[g * SUBLANE:(g + 1) * SUBLANE, :]
        return carry

    lax.fori_loop(0, tb // PEER_GATE_UNROLL, tok, 0)


def _peer_gates(route):
    rows, t = route[0].shape
    tb = PEER_GATE_TILE
    return pl.pallas_call(
        _gates_kernel, grid=(t // tb,),
        in_specs=[pl.BlockSpec((rows, tb), lambda i: (0, i))] * 4,
        out_specs=pl.BlockSpec((N_KEYS // SUBLANE, tb, SUBLANE, N_KEYS), lambda i: (0, i, 0, 0)),
        out_shape=jax.ShapeDtypeStruct((N_KEYS // SUBLANE, t, SUBLANE, N_KEYS), F32),
        scratch_shapes=[pltpu.VMEM((tb, rows), F32)] * 4,
        compiler_params=_cparams("parallel"), name="peer_gates",
    )(*route)


def _peer_kernel(h_ref, w_ref, ut_ref, v_ref, o_ref, acc_ref, g0_ref, g1_ref):
    e = pl.program_id(1)
    n_blk = pl.num_programs(1) - 1

    @pl.when(e == 0)
    def _():
        acc_ref[...] = jnp.zeros(acc_ref.shape, F32)
        g1_ref[...] = jnp.zeros(g1_ref.shape, BF16)

    @pl.when(e % 2 == 0)
    def _():
        _peer_step(h_ref, w_ref, ut_ref, v_ref, acc_ref, g0_ref, g1_ref)

    @pl.when(e % 2 == 1)
    def _():
        _peer_step(h_ref, w_ref, ut_ref, v_ref, acc_ref, g1_ref, g0_ref)

    @pl.when(e == n_blk)
    def _():
        o_ref[...] = acc_ref[...]


def _peer_step(h_ref, w_ref, ut_ref, v_ref, acc_ref, g_w_ref, g_r_ref):
    sr = h_ref.shape[0] // PEER_SUB

    def sub(k, carry):
        rows = pl.ds(pl.multiple_of(k * sr, sr), sr)
        acc_ref[rows, :] += jnp.dot(g_r_ref[rows, :], v_ref[...].astype(BF16), preferred_element_type=F32)
        act = jnp.dot(h_ref[rows, :], ut_ref[0], preferred_element_type=F32)
        act = 0.5 * act * (1.0 + lax.erf(act * 0.7071067811865476))
        for j in range(SUBLANE):
            ln = slice(j * N_KEYS, (j + 1) * N_KEYS)
            wj = w_ref[pl.ds(k * (sr * SUBLANE) + j, sr, stride=SUBLANE), :]
            g_w_ref[rows, ln] = (wj * act[:, ln]).astype(BF16)
        return carry

    lax.fori_loop(0, PEER_SUB, sub, 0)


def _peer_dense(hx, w, ut, v):
    t, d = hx.shape
    n_blk, _, eb = ut.shape
    tm = PEER_TOK_TILE
    last = n_blk - 1
    return pl.pallas_call(
        _peer_kernel, grid=(t // tm, n_blk + 1),
        in_specs=[pl.BlockSpec((tm, d), lambda i, e: (i, 0)),
                  pl.BlockSpec((tm * SUBLANE, N_KEYS), lambda i, e: (jnp.minimum(e, last) * (t // tm) + i, 0)),
                  pl.BlockSpec((1, d, eb), lambda i, e: (jnp.minimum(e, last), 0, 0)),
                  pl.BlockSpec((eb, d), lambda i, e: (jnp.maximum(e - 1, 0), 0))],
        out_specs=pl.BlockSpec((tm, d), lambda i, e: (i, 0)),
        out_shape=jax.ShapeDtypeStruct((t, d), F32),
        scratch_shapes=[pltpu.VMEM((tm, d), F32), pltpu.VMEM((tm, eb), BF16), pltpu.VMEM((tm, eb), BF16)],
        compiler_params=_cparams("parallel", "arbitrary"), name="peer_dense",
    )(hx, w, ut, v)


def _final_kernel(x_ref, f_ref, mod_ref, gain_ref, o_ref):
    x = x_ref[0] + mod_ref[0, 0, 5:6, :] * f_ref[0]
    o_ref[0] = x * lax.rsqrt(jnp.mean(x * x, axis=-1, keepdims=True) + EPS) * gain_ref[...]


def _final(x, ffn, mod, gain):
    b, s, d = x.shape
    tok = pl.BlockSpec((1, TOK_TILE, d), lambda i, t: (i, t, 0))
    return pl.pallas_call(
        _final_kernel, grid=(b, s // TOK_TILE),
        in_specs=[tok, tok, pl.BlockSpec((1, 1, 6, d), lambda i, t: (i, 1, 0, 0)),
                  pl.BlockSpec((1, d), lambda i, t: (0, 0))],
        out_specs=tok, out_shape=jax.ShapeDtypeStruct((b, s, d), F32),
        compiler_params=_cparams("parallel", "parallel"), name="final_norm",
    )(x, ffn, mod, gain)


def _rope_tables(seq):
    pos = jnp.arange(seq)
    rc = jnp.stack([pos // GRID_W, pos % GRID_W], axis=-1).astype(F32)
    inv = jnp.power(ROPE_BASE, -jnp.arange(ROPE_PAIRS, dtype=F32) / ROPE_PAIRS)
    ang = rc[:, :, None] * inv
    cos = jnp.concatenate([jnp.cos(ang)] * 2, axis=-1).reshape(seq, HEAD_DIM)
    sin = jnp.concatenate([-jnp.sin(ang), jnp.sin(ang)], axis=-1).reshape(seq, HEAD_DIM)
    return jnp.concatenate([cos] * 2, axis=1), jnp.concatenate([sin] * 2, axis=1)


def _pair_lanes(cols, n_heads):
    rows = cols.shape[0]
    c = cols.reshape(rows, 4, n_heads // 2, 2).transpose(0, 2, 1, 3).reshape(rows, n_heads // 2, 8)
    return jnp.pad(c, ((0, 0), (0, 0), (0, LANE - 8))).reshape(rows, n_heads // 2 * LANE)


def kernel(x, c, ctx, c_ctx, w_ada, b_ada, norm1, norm2, w_in, conv_w, a_log, dt_bias, gdn_norm, w_pool,
           pool_scale, sink, w_out, peer_wq, peer_keys, peer_u, peer_v, norm_f):
    b, seq, d = x.shape
    n_ctx = ctx.shape[1]
    depth = w_ada.shape[0]
    pool_w = d // 4
    gdn_w = 3 * d // 8
    gdn_heads = gdn_w // HEAD_DIM
    attn_w = d - pool_w - gdn_w
    kv_w = KV_HEADS * HEAD_DIM
    assert n_ctx % TOK_TILE == 0 and seq % TOK_TILE == 0 and gdn_heads % 2 == 0
    assert (n_ctx + seq) % PROJ_TILE == 0 and PEER_EXP_TILE == SUBLANE * N_KEYS
    assert (b * (n_ctx + seq)) % PEER_TOK_TILE == 0 and (b * seq) % PEER_TOK_TILE == 0

    rows = -(-(b + 1) // SUBLANE) * SUBLANE
    c_all = jnp.zeros((rows, d), F32).at[:b].set(c).at[b].set(c_ctx)
    mod = _ada_mod(c_all, w_ada, b_ada)
    mod_x = mod[:, :b].reshape(depth, b, 1, 6, d)
    mod_c = jnp.broadcast_to(mod[:, b].reshape(depth, 1, 1, 6, d), (depth, b, 1, 6, d))
    mod = jnp.concatenate([mod_c, mod_x], axis=2)

    cos_t, sin_t = _rope_tables(seq)
    splits = (pool_w, 3 * gdn_w, gdn_w, gdn_heads // 2 * LANE, attn_w, kv_w, kv_w)
    o_ba = pool_w + 4 * gdn_w
    n_ba = 4 * gdn_heads

    h = jnp.concatenate([ctx, x], axis=1)
    ffn = None
    for l in range(depth):
        last = l == depth - 1
        w_l = w_in[l]
        w_big = jnp.concatenate([w_l[:, :o_ba], _pair_lanes(w_l[:, o_ba:o_ba + n_ba], gdn_heads),
                                 w_l[:, o_ba + n_ba:]], axis=1).astype(BF16)
        pmod = mod[l - 1] if l > 0 else None
        outs = _in_proj(h, ffn, pmod, mod[l], norm1[l][None], w_big, splits, n_ctx)
        if l > 0:
            h, *outs = outs
        pa, qkv, z, ba, aq, ak, av = outs

        wbd = jax.scipy.linalg.block_diag(*[w_pool[l, g] for g in range(len(POOL_WINDOWS))])
        ya = _pool(pa, wbd, pool_scale[l][None], n_ctx)

        gparams = jnp.stack([_pair_lanes(jnp.concatenate([jnp.zeros_like(a_log[l]), a_log[l]]).reshape(1, -1),
                                         gdn_heads),
                             _pair_lanes(jnp.concatenate([jnp.zeros_like(dt_bias[l]), dt_bias[l]]).reshape(1, -1),
                                         gdn_heads)], axis=1)
        gparams = gparams.reshape(2, gdn_heads // 2, LANE).transpose(1, 0, 2)
        gparams = jnp.pad(gparams, ((0, 0), (0, SUBLANE - 2), (0, 0)))
        yb = _gdn(qkv, z, ba, conv_w[l], gparams, jnp.concatenate([gdn_norm[l]] * 2)[None], n_ctx)

        sink_row = jnp.pad(sink[l], (0, LANE - sink.shape[1]))[None]
        yc = _attn_x(aq, ak, av, cos_t, sin_t, sink_row, n_ctx)
        if not last:
            yc = jnp.concatenate([_attn_ctx(aq, ak, av, sink_row, n_ctx), yc], axis=1)

        x_new, hx = _out_proj(h, ya, yb, yc, w_out[l].astype(BF16), mod[l], norm2[l][None], n_ctx,
                              skip_ctx=last)
        t = hx.shape[0] * hx.shape[1]
        hx = hx.reshape(t, d)
        keys = peer_keys[l].reshape(2 * PEER_HEADS, N_KEYS, -1).astype(BF16)
        route = _peer_route(hx.T, peer_wq[l].T.astype(BF16), keys)
        gates = _peer_gates(route).reshape(-1, N_KEYS)
        ut = peer_u[l].reshape(-1, PEER_EXP_TILE, d).transpose(0, 2, 1).astype(BF16)
        ffn = _peer_dense(hx, gates, ut, peer_v[l])
        ffn = ffn.reshape(x_new.shape)
        h = x_new
    return _final(h, ffn, mod[depth - 1], norm_f[None])
```

```python
import functools

import jax
import jax.numpy as jnp
from jax import lax
from jax.experimental import pallas as pl
from jax.experimental.pallas import tpu as pltpu

F32, BF16 = jnp.float32, jnp.bfloat16
HIGHEST = lax.Precision.HIGHEST

EPS = 1e-6
NEG_INF = -1e30
HEAD_DIM = 64
GRID_W = 64
POOL_WINDOWS = (2, 4, 8, 16)
GDN_CHUNK = 64
CONV_K = 5
KV_HEADS = 2
WINDOW = 128
ATTN_BLOCK = 128
ROPE_BASE = 10000.0
ROPE_PAIRS = HEAD_DIM // 4
PEER_HEADS = 8
N_KEYS = 128
PEER_TOPK = 16

LANE = 128
SUBLANE = 8
TOK_TILE = 256
PROJ_TILE = 768
ROW_CHUNK = 256
ATTN_QBLOCKS = 2
GDN_GROUP = 4
PEER_ROUTE_TILE = 256
PEER_GATE_TILE = 128
PEER_GATE_UNROLL = 16
PEER_TOK_TILE = 1024
PEER_EXP_TILE = 1024
PEER_SUB = 2
VMEM_LIMIT = 48 * 1024 * 1024

_CAND_ROWS = tuple(PEER_TOPK // (a + 1) for a in range(PEER_TOPK))
_N_CAND = sum(_CAND_ROWS)
_CAND_PAD = -(-_N_CAND // SUBLANE) * SUBLANE


def _cparams(*sem):
    return pltpu.CompilerParams(dimension_semantics=sem, vmem_limit_bytes=VMEM_LIMIT)


def _dot(a, b):
    return jnp.dot(a.astype(BF16), b.astype(BF16), preferred_element_type=F32)


def _dot_nt(a, b):
    return lax.dot_general(a.astype(BF16), b.astype(BF16), (((1,), (1,)), ((), ())),
                           preferred_element_type=F32)


def _dot_tn(a, b):
    return lax.dot_general(a.astype(BF16), b.astype(BF16), (((0,), (0,)), ((), ())),
                           preferred_element_type=F32)


def _dot_hi(a, b):
    return jnp.dot(a, b, precision=HIGHEST, preferred_element_type=F32)


def _silu(x):
    return x * jax.nn.sigmoid(x)


def _softplus(x):
    return jnp.maximum(x, 0.0) + jnp.log1p(jnp.exp(-jnp.abs(x)))


def _modnorm(x, gain, scale, shift):
    y = x * lax.rsqrt(jnp.mean(x * x, axis=-1, keepdims=True) + EPS) * gain
    return y * (1.0 + scale) + shift


def _ada_kernel(c_ref, w_ref, b_ref, o_ref):
    o_ref[0] = _dot_hi(_silu(c_ref[...]), w_ref[0]) + b_ref[0]


def _ada_mod(c_all, w_ada, b_ada):
    depth, d, n = w_ada.shape
    rows = c_all.shape[0]
    tn = n // 4
    return pl.pallas_call(
        _ada_kernel,
        grid=(depth, n // tn),
        in_specs=[pl.BlockSpec((rows, d), lambda l, j: (0, 0)),
                  pl.BlockSpec((1, d, tn), lambda l, j: (l, 0, j)),
                  pl.BlockSpec((1, 1, tn), lambda l, j: (l, 0, j))],
        out_specs=pl.BlockSpec((1, rows, tn), lambda l, j: (l, 0, j)),
        out_shape=jax.ShapeDtypeStruct((depth, rows, n), F32),
        compiler_params=_cparams("parallel", "parallel"),
        name="ada_mod",
    )(c_all, w_ada, b_ada.reshape(depth, 1, n))


def _row_mod(mod_ref, first_row, n_rows, n_ctx):
    is_x = first_row + lax.broadcasted_iota(jnp.int32, (n_rows, 1), 0) >= n_ctx
    return lambda k: jnp.where(is_x, mod_ref[0, 1, k:k + 1, :], mod_ref[0, 0, k:k + 1, :])


def _inproj_kernel(*refs, has_ffn, col_splits, n_ctx):
    tile = refs[0].shape[1]
    first_row = pl.program_id(1) * tile
    if has_ffn:
        h_ref, f_ref, pmod_ref, mod_ref, gain_ref, w_ref, res_ref, *outs = refs
        x = h_ref[0] + _row_mod(pmod_ref, first_row, tile, n_ctx)(5) * f_ref[0]
        res_ref[0] = x
    else:
        h_ref, mod_ref, gain_ref, w_ref, *outs = refs
        x = h_ref[0]
    mod = _row_mod(mod_ref, first_row, tile, n_ctx)
    hb = _modnorm(x, gain_ref[...], mod(1), mod(0)).astype(BF16)
    off = 0
    for o_ref, width in zip(outs, col_splits):
        o_ref[0] = jnp.dot(hb, w_ref[:, off:off + width], preferred_element_type=F32)
        off += width


def _in_proj(h, ffn, pmod, mod, gain, w, col_splits, n_ctx):
    b, l, d = h.shape
    tile = PROJ_TILE
    tok = pl.BlockSpec((1, tile, d), lambda i, t: (i, t, 0))
    modspec = pl.BlockSpec((1, 2, 6, d), lambda i, t: (i, 0, 0, 0))
    has_ffn = ffn is not None
    in_specs = [tok] + ([tok, modspec] if has_ffn else []) + [
        modspec, pl.BlockSpec((1, d), lambda i, t: (0, 0)), pl.BlockSpec(w.shape, lambda i, t: (0, 0))]
    out_shape = [jax.ShapeDtypeStruct((b, l, wd), F32) for wd in col_splits]
    out_specs = [pl.BlockSpec((1, tile, wd), lambda i, t: (i, t, 0)) for wd in col_splits]
    if has_ffn:
        out_shape = [jax.ShapeDtypeStruct((b, l, d), F32)] + out_shape
        out_specs = [tok] + out_specs
    args = (h, ffn, pmod, mod, gain, w) if has_ffn else (h, mod, gain, w)
    return pl.pallas_call(
        functools.partial(_inproj_kernel, has_ffn=has_ffn, col_splits=col_splits, n_ctx=n_ctx),
        grid=(b, l // tile), in_specs=in_specs, out_specs=out_specs, out_shape=out_shape,
        compiler_params=_cparams("parallel", "parallel"), name="in_proj",
    )(*args)


def _pool_kernel(a_ref, wbd_ref, scale_ref, o_ref, pad_ref, *, segments):
    c = a_ref.shape[-1]
    grp = lax.broadcasted_iota(jnp.int32, (1, c), 1) // (c // len(POOL_WINDOWS))
    zeros = jnp.zeros((SUBLANE, c), F32)
    for off, n in segments:
        pad_ref[0:SUBLANE, :] = zeros
        pad_ref[SUBLANE:SUBLANE + n, :] = a_ref[0, off:off + n, :]
        pad_ref[SUBLANE + n:2 * SUBLANE + n, :] = zeros

        def chunk(ci, carry, off=off, n=n):
            r0 = pl.multiple_of(ci * ROW_CHUNK, ROW_CHUNK)
            win = pad_ref[pl.ds(r0, ROW_CHUNK + 2 * SUBLANE), :]
            sh = lambda s: win[SUBLANE + s:SUBLANE + s + ROW_CHUNK, :]
            a0 = sh(0)
            s2 = sh(-1) + a0
            s4 = s2 + sh(-2) + sh(1)
            s8 = s4 + sh(-4) + sh(-3) + sh(2) + sh(3)
            s16 = s8 + (sh(-8) + sh(-7) + sh(-6) + sh(-5)) + (sh(4) + sh(5) + sh(6) + sh(7))
            t = r0 + lax.broadcasted_iota(jnp.int32, (ROW_CHUNK, 1), 0)

            def mean(s, w):
                cnt = jnp.minimum(t + (w - w // 2), n) - jnp.maximum(t - w // 2, 0)
                return s / cnt.astype(F32)

            pooled = jnp.where(grp == 0, mean(s2, 2), jnp.where(grp == 1, mean(s4, 4),
                               jnp.where(grp == 2, mean(s8, 8), mean(s16, 16)))) - a0
            o_ref[0, pl.ds(off + r0, ROW_CHUNK), :] = _dot_hi(pooled, wbd_ref[...]) * scale_ref[...]
            return carry

        lax.fori_loop(0, n // ROW_CHUNK, chunk, 0)


def _pool(a, wbd, scale, n_ctx):
    b, l, c = a.shape
    segments = ((0, n_ctx), (n_ctx, l - n_ctx))
    return pl.pallas_call(
        functools.partial(_pool_kernel, segments=segments),
        grid=(b,),
        in_specs=[pl.BlockSpec((1, l, c), lambda i: (i, 0, 0)),
                  pl.BlockSpec((c, c), lambda i: (0, 0)),
                  pl.BlockSpec((1, c), lambda i: (0, 0))],
        out_specs=pl.BlockSpec((1, l, c), lambda i: (i, 0, 0)),
        out_shape=jax.ShapeDtypeStruct((b, l, c), F32),
        scratch_shapes=[pltpu.VMEM((l - n_ctx + 2 * SUBLANE, c), F32)],
        compiler_params=_cparams("parallel"), name="pool",
    )(a, wbd, scale)


def _gdn_kernel(q_ref, k_ref, v_ref, z_ref, ba_ref, cwq_ref, cwk_ref, cwv_ref, gp_ref, gain_ref, o_ref,
                pad_ref, yq_ref, yk_ref, yv_ref, bg_ref, of_ref, ob_ref, s_ref,
                qe_ref, mn_ref, oc_ref, nn_ref, egl_ref, *, n_ctx):
    l = q_ref.shape[1]
    n_x = l - n_ctx
    hd = HEAD_DIM
    lane = lax.broadcasted_iota(jnp.int32, (1, LANE), 1)
    lo = lane < hd
    zeros = jnp.zeros((SUBLANE, LANE), F32)
    x_off = 2 * SUBLANE + n_ctx

    def head_scale(ss):
        s_lo = jnp.sum(jnp.where(lo, ss, 0.0), axis=-1, keepdims=True)
        s_hi = jnp.sum(jnp.where(lo, 0.0, ss), axis=-1, keepdims=True)
        return jnp.where(lo, s_lo, s_hi)

    def conv(u_ref, cw_ref, y_ref, post):
        pad_ref[0:SUBLANE, :] = zeros
        pad_ref[SUBLANE:SUBLANE + n_ctx, :] = u_ref[0, 0:n_ctx, :]
        pad_ref[SUBLANE + n_ctx:x_off, :] = zeros
        pad_ref[x_off:x_off + n_x, :] = u_ref[0, n_ctx:l, :]
        pad_ref[x_off + n_x:x_off + n_x + SUBLANE, :] = zeros
        cw = cw_ref[...]
        for poff, yoff, n in ((SUBLANE, 0, n_ctx), (x_off, n_ctx, n_x)):
            def chunk(ci, carry, poff=poff, yoff=yoff):
                r0 = pl.multiple_of(ci * ROW_CHUNK, ROW_CHUNK)
                win = pad_ref[pl.ds(poff - SUBLANE + r0, ROW_CHUNK + 2 * SUBLANE), :]
                acc = win[SUBLANE - 2:SUBLANE - 2 + ROW_CHUNK, :] * cw[0:1, :]
                for j in range(1, CONV_K):
                    acc = acc + win[SUBLANE - 2 + j:SUBLANE - 2 + j + ROW_CHUNK, :] * cw[j:j + 1, :]
                y_ref[pl.ds(yoff + r0, ROW_CHUNK), :] = post(_silu(acc))
                return carry
            lax.fori_loop(0, n // ROW_CHUNK, chunk, 0)

    l2 = lambda y: y * lax.rsqrt(head_scale(y * y) + EPS)
    conv(q_ref, cwq_ref, yq_ref, lambda y: l2(y) * (hd ** -0.5))
    conv(k_ref, cwk_ref, yk_ref, l2)
    conv(v_ref, cwv_ref, yv_ref, lambda y: y)

    ba = ba_ref[0]
    g = -jnp.exp(gp_ref[0, 0:1, :]) * _softplus(ba + gp_ref[0, 1:2, :])
    bg_ref[...] = jnp.where(lane < 4, jax.nn.sigmoid(ba), g)

    ck = GDN_CHUNK
    rr = lax.broadcasted_iota(jnp.int32, (ck, ck), 0)
    cc = lax.broadcasted_iota(jnp.int32, (ck, ck), 1)
    eye = (rr == cc).astype(F32)
    tril = (rr >= cc).astype(F32)
    n_chunks = l // ck
    nc_ctx = n_ctx // ck

    def phase1(ci, carry):
        probs = []
        for gi in range(GDN_GROUP):
            c = ci * GDN_GROUP + gi
            r0 = pl.multiple_of(c * ck, ck)
            qc = yq_ref[pl.ds(r0, ck), :]
            kc = yk_ref[pl.ds(r0, ck), :]
            vc = yv_ref[pl.ds(r0, ck), :]
            bgc = bg_ref[pl.ds(r0, ck), :]
            gcf = _dot_hi(tril, bgc)
            gcb = gcf[ck - 1:ck, :] - gcf + bgc
            gct = (gcf.T, gcb.T)
            for j in range(2):
                qh = qc[:, j * hd:(j + 1) * hd]
                kh = kc[:, j * hd:(j + 1) * hd]
                vh = vc[:, j * hd:(j + 1) * hd]
                kk = _dot_nt(kh, kh)
                qk = _dot_nt(qh, kh)
                for d in range(2):
                    mask = (rr >= cc) if d == 0 else (rr <= cc)
                    smask = (rr > cc) if d == 0 else (rr < cc)
                    lg, lb = 4 + 2 * d + j, 2 * d + j
                    gcol = jnp.broadcast_to((gcf, gcb)[d][:, lg:lg + 1], (ck, hd))
                    grow = gct[d][lg:lg + 1, :]
                    bcol = jnp.broadcast_to(bgc[:, lb:lb + 1], (ck, hd))
                    decay = jnp.where(mask, jnp.exp(jnp.where(mask, gcol - grow, 0.0)), 0.0)
                    xp = jnp.where(smask, -(bcol * kk * decay), 0.0)
                    eg = jnp.exp(gcol)
                    last = ck - 1 if d == 0 else 0
                    glast = gcol[last:last + 1, :]
                    probs.append(dict(
                        idx=c * 4 + 2 * d + j, xp=xp, inv=eye + xp, attn=qk * decay, qd=qh * eg,
                        rhs=jnp.concatenate([vh * bcol, kh * bcol * eg], axis=1),
                        kd=kh * jnp.exp(glast - gcol), egl=jnp.exp(glast)))
        for _ in range(5):
            for p in probs:
                p["xp"] = _dot(p["xp"], p["xp"])
            for p in probs:
                p["inv"] = p["inv"] + _dot(p["inv"], p["xp"])
        for p in probs:
            p["sol"] = _dot(p["inv"], p["rhs"])
        for p in probs:
            p["as"] = _dot(p["attn"], p["sol"])
        for p in probs:
            p["ks"] = _dot_tn(p["kd"], p["sol"])
        for p in probs:
            i = p["idx"]
            qe_ref[i] = (p["qd"] - p["as"][:, hd:]).astype(BF16)
            oc_ref[i] = p["as"][:, :hd]
            mn_ref[i] = p["ks"][:, hd:].astype(BF16)
            nn_ref[i] = p["ks"][:, :hd]
            egl_ref[i] = jnp.broadcast_to(p["egl"], (ck, hd))
        return carry

    lax.fori_loop(0, n_chunks // GDN_GROUP, phase1, 0)

    s_ref[...] = jnp.zeros(s_ref.shape, F32)

    def phase2(i, carry):
        cb = jnp.where(i < nc_ctx, nc_ctx - 1 - i, n_chunks - 1 - (i - nc_ctx))
        work = [(d, j, cidx * 4 + 2 * d + j) for d, cidx in ((0, i), (1, cb)) for j in range(2)]
        states = [s_ref[2 * d + j] for d, j, _ in work]
        outs = [_dot(qe_ref[idx], s) + oc_ref[idx] for (_, _, idx), s in zip(work, states)]
        upd = [_dot(mn_ref[idx], s) for (_, _, idx), s in zip(work, states)]
        for (d, j, idx), s, m in zip(work, states, upd):
            s_ref[2 * d + j] = egl_ref[idx] * s - m + nn_ref[idx]
        of_ref[pl.ds(pl.multiple_of(i * ck, ck), ck), :] = jnp.concatenate(outs[0:2], axis=1)
        ob_ref[pl.ds(pl.multiple_of(cb * ck, ck), ck), :] = jnp.concatenate(outs[2:4], axis=1)
        return carry

    lax.fori_loop(0, n_chunks, phase2, 0)

    def finish(ci, carry):
        r0 = pl.multiple_of(ci * ROW_CHUNK, ROW_CHUNK)
        o = of_ref[pl.ds(r0, ROW_CHUNK), :] + ob_ref[pl.ds(r0, ROW_CHUNK), :]
        o = o * lax.rsqrt(head_scale(o * o) * (1.0 / hd) + EPS) * gain_ref[...]
        o_ref[0, pl.ds(r0, ROW_CHUNK), :] = o * _silu(z_ref[0, pl.ds(r0, ROW_CHUNK), :])
        return carry

    lax.fori_loop(0, l // ROW_CHUNK, finish, 0)


def _gdn(qkv, z, ba, conv_w, gparams, gain2, n_ctx):
    b, l, w3 = qkv.shape
    npair = w3 // (3 * LANE)
    seq = lambda off: pl.BlockSpec((1, l, LANE), lambda i, p: (i, 0, p + off))
    cw = lambda off: pl.BlockSpec((CONV_K, LANE), lambda i, p: (0, p + off))
    return pl.pallas_call(
        functools.partial(_gdn_kernel, n_ctx=n_ctx),
        grid=(b, npair),
        in_specs=[seq(0), seq(npair), seq(2 * npair), seq(0), seq(0),
                  cw(0), cw(npair), cw(2 * npair),
                  pl.BlockSpec((1, SUBLANE, LANE), lambda i, p: (p, 0, 0)),
                  pl.BlockSpec((1, LANE), lambda i, p: (0, 0))],
        out_specs=seq(0),
        out_shape=jax.ShapeDtypeStruct((b, l, npair * LANE), F32),
        scratch_shapes=[pltpu.VMEM((l + 3 * SUBLANE, LANE), F32)] + [pltpu.VMEM((l, LANE), F32)] * 6
        + [pltpu.VMEM((4, HEAD_DIM, HEAD_DIM), F32)]
        + [pltpu.VMEM((4 * l // GDN_CHUNK, GDN_CHUNK, HEAD_DIM), dt) for dt in (BF16, BF16, F32, F32, F32)],
        compiler_params=_cparams("parallel", "parallel"), name="gdn",
    )(qkv, qkv, qkv, z, ba, conv_w, conv_w, conv_w, gparams, gain2)


def _rope(x, cos, sin):
    w = x.shape[-1]
    reps = w // cos.shape[-1]
    if reps > 1:
        cos = jnp.concatenate([cos] * reps, axis=1)
        sin = jnp.concatenate([sin] * reps, axis=1)
    first = (lax.broadcasted_iota(jnp.int32, (1, w), 1) % (2 * ROPE_PAIRS)) < ROPE_PAIRS
    partner = jnp.where(first, pltpu.roll(x, w - ROPE_PAIRS, 1), pltpu.roll(x, ROPE_PAIRS, 1))
    return x * cos + partner * sin


def _attend(q, k, v, valid, sink_col):
    s = _dot_nt(q, k)
    if valid is not None:
        s = jnp.where(valid, s, NEG_INF)
    m = jnp.maximum(jnp.max(s, axis=-1, keepdims=True), sink_col)
    p = jnp.exp(s - m)
    den = jnp.sum(p, axis=-1, keepdims=True) + jnp.exp(sink_col - m)
    return _dot(p, v) / den


def _gqa(q, k_all, v_all, valid, sink_ref):
    nq = q.shape[0]
    hd = HEAD_DIM
    group = q.shape[1] // hd // KV_HEADS
    outs = []
    for kvh in range(KV_HEADS):
        heads = [kvh * group + g for g in range(group)]
        qg = jnp.concatenate([q[:, h * hd:(h + 1) * hd] for h in heads], axis=0)
        sink_col = jnp.concatenate(
            [jnp.broadcast_to(sink_ref[0:1, h:h + 1], (nq, 1)) for h in heads], axis=0)
        o = _attend(qg, k_all[:, kvh * hd:(kvh + 1) * hd], v_all[:, kvh * hd:(kvh + 1) * hd],
                    valid, sink_col)
        outs += [o[g * nq:(g + 1) * nq, :] for g in range(group)]
    return jnp.concatenate(outs, axis=1)


def _attn_kernel(q_ref, *refs, seq):
    nkb = ATTN_QBLOCKS + 2
    k_refs, v_refs = refs[0:nkb], refs[nkb:2 * nkb]
    kc_ref, vc_ref = refs[2 * nkb:2 * nkb + 2]
    cos_refs = refs[2 * nkb + 2:3 * nkb + 2]
    sin_refs = refs[3 * nkb + 2:4 * nkb + 2]
    sink_ref, o_ref = refs[4 * nkb + 2:]
    blk = ATTN_BLOCK
    n_ctx = kc_ref.shape[1]
    group = q_ref.shape[2] // HEAD_DIM // KV_HEADS
    k_rot = [_rope(k_refs[j][0], cos_refs[j][...], sin_refs[j][...]) for j in range(nkb)]
    nk = 3 * blk + n_ctx
    qo = lax.broadcasted_iota(jnp.int32, (group * blk, 1), 0) % blk
    ko = lax.broadcasted_iota(jnp.int32, (1, nk), 1)
    in_window = jnp.abs(ko - blk - qo) <= WINDOW
    for s in range(ATTN_QBLOCKS):
        i = pl.program_id(1) * ATTN_QBLOCKS + s
        rows = slice(s * blk, (s + 1) * blk)
        q = _rope(q_ref[0, rows, :], cos_refs[s + 1][...], sin_refs[s + 1][...]) * (HEAD_DIM ** -0.5)
        k_all = jnp.concatenate(k_rot[s:s + 3] + [kc_ref[0]], axis=0)
        v_all = jnp.concatenate([v_refs[j][0] for j in range(s, s + 3)] + [vc_ref[0]], axis=0)
        kpos = (i - 1) * blk + ko
        valid = (in_window & (kpos >= 0) & (kpos < seq)) | (ko >= 3 * blk)
        o_ref[0, rows, :] = _gqa(q, k_all, v_all, valid, sink_ref)


def _attn_x(aq, ak, av, cos_t, sin_t, sink_row, n_ctx):
    b, l, wq = aq.shape
    wk = ak.shape[-1]
    blk = ATTN_BLOCK
    seq = l - n_ctx
    nb = seq // blk
    cb = n_ctx // blk
    nq = ATTN_QBLOCKS
    assert nb % nq == 0 and cb % nq == 0
    blocks = [lambda t, j=j: jnp.clip(t * nq + j - 1, 0, nb - 1) for j in range(nq + 2)]
    kv = [pl.BlockSpec((1, blk, wk), lambda i, t, f=f: (i, f(t) + cb, 0)) for f in blocks]
    tab = [pl.BlockSpec((blk, LANE), lambda i, t, f=f: (f(t), 0)) for f in blocks]
    ctx = pl.BlockSpec((1, n_ctx, wk), lambda i, t: (i, 0, 0))
    qspec = pl.BlockSpec((1, nq * blk, wq), lambda i, t: (i, t + cb // nq, 0))
    return pl.pallas_call(
        functools.partial(_attn_kernel, seq=seq),
        grid=(b, nb // nq),
        in_specs=[qspec] + kv + kv + [ctx, ctx] + tab + tab + [pl.BlockSpec((1, LANE), lambda i, t: (0, 0))],
        out_specs=pl.BlockSpec((1, nq * blk, wq), lambda i, t: (i, t, 0)),
        out_shape=jax.ShapeDtypeStruct((b, seq, wq), F32),
        compiler_params=_cparams("parallel", "parallel"), name="attn_x",
    )(aq, *([ak] * (nq + 2)), *([av] * (nq + 2)), ak, av, *([cos_t] * (nq + 2)), *([sin_t] * (nq + 2)),
      sink_row)


def _attn_ctx_kernel(q_ref, k_ref, v_ref, sink_ref, o_ref):
    o_ref[0] = _gqa(q_ref[0] * (HEAD_DIM ** -0.5), k_ref[0], v_ref[0], None, sink_ref)


def _attn_ctx(aq, ak, av, sink_row, n_ctx):
    b, _, wq = aq.shape
    wk = ak.shape[-1]
    spec = lambda w: pl.BlockSpec((1, n_ctx, w), lambda i: (i, 0, 0))
    return pl.pallas_call(
        _attn_ctx_kernel, grid=(b,),
        in_specs=[spec(wq), spec(wk), spec(wk), pl.BlockSpec((1, LANE), lambda i: (0, 0))],
        out_specs=spec(wq), out_shape=jax.ShapeDtypeStruct((b, n_ctx, wq), F32),
        compiler_params=_cparams("parallel"), name="attn_ctx",
    )(aq, ak, av, sink_row)


def _outproj_kernel(x_ref, ya_ref, yb_ref, yc_ref, w_ref, mod_ref, gain_ref, xo_ref, hx_ref, *, row_off, n_ctx):
    wa, wb = ya_ref.shape[-1], yb_ref.shape[-1]
    tile = x_ref.shape[1]
    acc = (jnp.dot(ya_ref[0].astype(BF16), w_ref[0:wa, :], preferred_element_type=F32)
           + jnp.dot(yb_ref[0].astype(BF16), w_ref[wa:wa + wb, :], preferred_element_type=F32)
           + jnp.dot(yc_ref[0].astype(BF16), w_ref[wa + wb:, :], preferred_element_type=F32))
    mod = _row_mod(mod_ref, row_off + pl.program_id(1) * tile, tile, n_ctx)
    x = x_ref[0] + mod(2) * acc
    xo_ref[0] = x
    hx_ref[0] = _modnorm(x, gain_ref[...], mod(4), mod(3)).astype(BF16)


def _out_proj(h, ya, yb, yc, w, mod, gain, n_ctx, skip_ctx):
    b, l, d = h.shape
    tile = TOK_TILE if skip_ctx else PROJ_TILE
    off = n_ctx // tile if skip_ctx else 0
    nt = l // tile - off
    full = lambda wd: pl.BlockSpec((1, tile, wd), lambda i, t: (i, t + off, 0))
    own = lambda wd: pl.BlockSpec((1, tile, wd), lambda i, t: (i, t, 0))
    return pl.pallas_call(
        functools.partial(_outproj_kernel, row_off=off * tile, n_ctx=n_ctx), grid=(b, nt),
        in_specs=[full(d), full(ya.shape[-1]), full(yb.shape[-1]), own(yc.shape[-1]),
                  pl.BlockSpec(w.shape, lambda i, t: (0, 0)),
                  pl.BlockSpec((1, 2, 6, d), lambda i, t: (i, 0, 0, 0)),
                  pl.BlockSpec((1, d), lambda i, t: (0, 0))],
        out_specs=[own(d), own(d)],
        out_shape=[jax.ShapeDtypeStruct((b, nt * tile, d), F32),
                   jax.ShapeDtypeStruct((b, nt * tile, d), BF16)],
        compiler_params=_cparams("parallel", "parallel"), name="out_proj",
    )(h, ya, yb, yc, w, mod, gain)


def _sort_pairs(n):
    pairs, p = [], 1
    while p < n:
        k = p
        while k >= 1:
            for j in range(k % p, n - k, 2 * k):
                for i in range(min(k, n - j - k)):
                    if (i + j) // (2 * p) == (i + j + k) // (2 * p):
                        pairs.append((i + j, i + j + k))
            k //= 2
        p *= 2
    return pairs


def _topk_rows(v):
    n = len(v)
    v = list(v)

    def cmpx(i, j):
        v[i], v[j] = jnp.maximum(v[i], v[j]), jnp.minimum(v[i], v[j])

    for i, j in _sort_pairs(n):
        cmpx(i, j)
    shift = SUBLANE // 2
    while shift >= 1:
        partner = [pltpu.roll(x, shift, 0) for x in v]
        v = [jnp.maximum(v[j], partner[n - 1 - j]) for j in range(n)]
        stride = n // 2
        while stride >= 1:
            for i in range(n):
                if not i & stride:
                    cmpx(i, i + stride)
            stride //= 2
        shift //= 2
    return v


def _route_kernel(ht_ref, wq_ref, keys_ref, r2_ref, e2_ref, n1_ref, e1_ref, qt_ref, top_ref, cand_ref):
    tm = ht_ref.shape[1]
    k = PEER_TOPK
    nk = N_KEYS
    neg = jnp.float32(-jnp.inf)
    qt_ref[...] = _dot(wq_ref[...], ht_ref[...]).astype(BF16)
    cand_ref[_N_CAND:_CAND_PAD, :] = jnp.full((_CAND_PAD - _N_CAND, tm), neg, F32)
    for h in range(PEER_HEADS):
        st = []
        for p in range(2):
            hp = 2 * h + p
            s = jnp.dot(keys_ref[hp], qt_ref[hp * nk:(hp + 1) * nk, :], preferred_element_type=F32)
            st.append(s)
            top = _topk_rows([s[g * SUBLANE:(g + 1) * SUBLANE, :] for g in range(nk // SUBLANE)])
            for r in range(k):
                top_ref[p * k + r:p * k + r + 1, :] = top[r][0:1, :]
        v1 = top_ref[0:k, :]
        v2 = top_ref[k:2 * k, :]
        rank2 = jnp.full((nk, tm), float(k), F32)
        for r in range(k):
            rank2 = jnp.where(st[1] == v2[r:r + 1, :], float(r), rank2)
        row = 0
        for a, nb in enumerate(_CAND_ROWS):
            cand_ref[row:row + nb, :] = v1[a:a + 1, :] + v2[0:nb, :]
            row += nb
        cur = cand_ref[...]
        cmax = thr = zsum = None
        for r in range(k):
            m = jnp.max(cur, axis=0, keepdims=True)
            if r == 0:
                cmax, zsum = m, jnp.ones_like(m)
            else:
                zsum = zsum + jnp.exp(m - cmax)
            thr = m
            cur = jnp.where(cur == m, neg, cur)
        n1 = jnp.zeros((nk, tm), F32)
        row = 0
        for a, nb in enumerate(_CAND_ROWS):
            sel = cand_ref[row:row + nb, :] >= thr
            n_a = jnp.sum(sel.astype(F32), axis=0, keepdims=True)
            n1 = jnp.where(st[0] == v1[a:a + 1, :], n_a, n1)
            row += nb
        rows = slice(h * nk, (h + 1) * nk)
        r2_ref[rows, :] = rank2
        e2_ref[rows, :] = jnp.exp(st[1] - v2[0:1, :])
        n1_ref[rows, :] = n1
        e1_ref[rows, :] = jnp.exp(st[0] - v1[0:1, :]) / zsum


def _peer_route(hxt, wq_t, keys):
    d, t = hxt.shape
    nq = wq_t.shape[0]
    tm = PEER_ROUTE_TILE
    rows = PEER_HEADS * N_KEYS
    return pl.pallas_call(
        _route_kernel, grid=(t // tm,),
        in_specs=[pl.BlockSpec((d, tm), lambda i: (0, i)),
                  pl.BlockSpec(wq_t.shape, lambda i: (0, 0)),
                  pl.BlockSpec(keys.shape, lambda i: (0, 0, 0))],
        out_specs=[pl.BlockSpec((rows, tm), lambda i: (0, i))] * 4,
        out_shape=[jax.ShapeDtypeStruct((rows, t), F32)] * 4,
        scratch_shapes=[pltpu.VMEM((nq, tm), BF16), pltpu.VMEM((2 * PEER_TOPK, tm), F32),
                        pltpu.VMEM((_CAND_PAD, tm), F32)],
        compiler_params=_cparams("parallel"), name="peer_route",
    )(hxt, wq_t, keys)


def _gates_kernel(r2_ref, e2_ref, n1_ref, e1_ref, w_ref, r2t_ref, e2t_ref, n1t_ref, e1t_ref):
    tb = w_ref.shape[1]
    nk = N_KEYS
    k = PEER_TOPK
    for src, dst in ((r2_ref, r2t_ref), (e2_ref, e2t_ref), (n1_ref, n1t_ref), (e1_ref, e1t_ref)):
        dst[...] = src[...].T
    rank = lax.broadcasted_iota(jnp.int32, (k, nk), 0).astype(F32)

    def tok(i, carry):
        for j in range(PEER_GATE_UNROLL):
            t = i * PEER_GATE_UNROLL + j
            r2, e2, n1, e1 = (ref[pl.ds(t, 1), :] for ref in (r2t_ref, e2t_ref, n1t_ref, e1t_ref))
            a, b = [], []
            for h in range(PEER_HEADS):
                ks = slice(h * nk, (h + 1) * nk)
                a.append(jnp.where(n1[:, ks] > rank, e1[:, ks], 0.0))
                b.append(jnp.where(r2[:, ks] == rank, e2[:, ks], 0.0))
            w = _dot_tn(jnp.concatenate(a, axis=0), jnp.concatenate(b, axis=0))
            for g in range(nk // SUBLANE):
                w_ref[g, t] = w[g * SUBLANE:(g + 1) * SUBLANE, :]
        return carry

    lax.fori_loop(0, tb // PEER_GATE_UNROLL, tok, 0)


def _peer_gates(route):
    rows, t = route[0].shape
    tb = PEER_GATE_TILE
    return pl.pallas_call(
        _gates_kernel, grid=(t // tb,),
        in_specs=[pl.BlockSpec((rows, tb), lambda i: (0, i))] * 4,
        out_specs=pl.BlockSpec((N_KEYS // SUBLANE, tb, SUBLANE, N_KEYS), lambda i: (0, i, 0, 0)),
        out_shape=jax.ShapeDtypeStruct((N_KEYS // SUBLANE, t, SUBLANE, N_KEYS), F32),
        scratch_shapes=[pltpu.VMEM((tb, rows), F32)] * 4,
        compiler_params=_cparams("parallel"), name="peer_gates",
    )(*route)


def _peer_kernel(h_ref, w_ref, ut_ref, v_ref, o_ref, acc_ref, g0_ref, g1_ref):
    e = pl.program_id(1)
    n_blk = pl.num_programs(1) - 1

    @pl.when(e == 0)
    def _():
        acc_ref[...] = jnp.zeros(acc_ref.shape, F32)
        g1_ref[...] = jnp.zeros(g1_ref.shape, BF16)

    @pl.when(e % 2 == 0)
    def _():
        _peer_step(h_ref, w_ref, ut_ref, v_ref, acc_ref, g0_ref, g1_ref)

    @pl.when(e % 2 == 1)
    def _():
        _peer_step(h_ref, w_ref, ut_ref, v_ref, acc_ref, g1_ref, g0_ref)

    @pl.when(e == n_blk)
    def _():
        o_ref[...] = acc_ref[...]


def _peer_step(h_ref, w_ref, ut_ref, v_ref, acc_ref, g_w_ref, g_r_ref):
    sr = h_ref.shape[0] // PEER_SUB

    def sub(k, carry):
        rows = pl.ds(pl.multiple_of(k * sr, sr), sr)
        acc_ref[rows, :] += jnp.dot(g_r_ref[rows, :], v_ref[...].astype(BF16), preferred_element_type=F32)
        act = jnp.dot(h_ref[rows, :], ut_ref[0], preferred_element_type=F32)
        act = 0.5 * act * (1.0 + lax.erf(act * 0.7071067811865476))
        for j in range(SUBLANE):
            ln = slice(j * N_KEYS, (j + 1) * N_KEYS)
            wj = w_ref[pl.ds(k * (sr * SUBLANE) + j, sr, stride=SUBLANE), :]
            g_w_ref[rows, ln] = (wj * act[:, ln]).astype(BF16)
        return carry

    lax.fori_loop(0, PEER_SUB, sub, 0)


def _peer_dense(hx, w, ut, v_all, layer):
    t, d = hx.shape
    n_blk, _, eb = ut.shape
    tm = PEER_TOK_TILE
    last = n_blk - 1
    return pl.pallas_call(
        _peer_kernel, grid=(t // tm, n_blk + 1),
        in_specs=[pl.BlockSpec((tm, d), lambda i, e: (i, 0)),
                  pl.BlockSpec((tm * SUBLANE, N_KEYS), lambda i, e: (jnp.minimum(e, last) * (t // tm) + i, 0)),
                  pl.BlockSpec((1, d, eb), lambda i, e: (jnp.minimum(e, last), 0, 0)),
                  pl.BlockSpec((None, eb, d), lambda i, e: (layer, jnp.maximum(e - 1, 0), 0))],
        out_specs=pl.BlockSpec((tm, d), lambda i, e: (i, 0)),
        out_shape=jax.ShapeDtypeStruct((t, d), F32),
        scratch_shapes=[pltpu.VMEM((tm, d), F32), pltpu.VMEM((tm, eb), BF16), pltpu.VMEM((tm, eb), BF16)],
        compiler_params=_cparams("parallel", "arbitrary"), name="peer_dense",
    )(hx, w, ut, v_all)


def _final_kernel(x_ref, f_ref, mod_ref, gain_ref, o_ref):
    x = x_ref[0] + mod_ref[0, 0, 5:6, :] * f_ref[0]
    o_ref[0] = x * lax.rsqrt(jnp.mean(x * x, axis=-1, keepdims=True) + EPS) * gain_ref[...]


def _final(x, ffn, mod, gain):
    b, s, d = x.shape
    tok = pl.BlockSpec((1, TOK_TILE, d), lambda i, t: (i, t, 0))
    return pl.pallas_call(
        _final_kernel, grid=(b, s // TOK_TILE),
        in_specs=[tok, tok, pl.BlockSpec((1, 1, 6, d), lambda i, t: (i, 1, 0, 0)),
                  pl.BlockSpec((1, d), lambda i, t: (0, 0))],
        out_specs=tok, out_shape=jax.ShapeDtypeStruct((b, s, d), F32),
        compiler_params=_cparams("parallel", "parallel"), name="final_norm",
    )(x, ffn, mod, gain)


def _rope_tables(seq):
    pos = jnp.arange(seq)
    rc = jnp.stack([pos // GRID_W, pos % GRID_W], axis=-1).astype(F32)
    inv = jnp.power(ROPE_BASE, -jnp.arange(ROPE_PAIRS, dtype=F32) / ROPE_PAIRS)
    ang = rc[:, :, None] * inv
    cos = jnp.concatenate([jnp.cos(ang)] * 2, axis=-1).reshape(seq, HEAD_DIM)
    sin = jnp.concatenate([-jnp.sin(ang), jnp.sin(ang)], axis=-1).reshape(seq, HEAD_DIM)
    return jnp.concatenate([cos] * 2, axis=1), jnp.concatenate([sin] * 2, axis=1)


def _pair_lanes(cols, n_heads):
    rows = cols.shape[0]
    c = cols.reshape(rows, 4, n_heads // 2, 2).transpose(0, 2, 1, 3).reshape(rows, n_heads // 2, 8)
    return jnp.pad(c, ((0, 0), (0, 0), (0, LANE - 8))).reshape(rows, n_heads // 2 * LANE)


def kernel(x, c, ctx, c_ctx, w_ada, b_ada, norm1, norm2, w_in, conv_w, a_log, dt_bias, gdn_norm, w_pool,
           pool_scale, sink, w_out, peer_wq, peer_keys, peer_u, peer_v, norm_f):
    b, seq, d = x.shape
    n_ctx = ctx.shape[1]
    depth = w_ada.shape[0]
    pool_w = d // 4
    gdn_w = 3 * d // 8
    gdn_heads = gdn_w // HEAD_DIM
    attn_w = d - pool_w - gdn_w
    kv_w = KV_HEADS * HEAD_DIM
    assert n_ctx % TOK_TILE == 0 and seq % TOK_TILE == 0 and gdn_heads % 2 == 0
    assert (n_ctx + seq) % PROJ_TILE == 0 and PEER_EXP_TILE == SUBLANE * N_KEYS
    assert (b * (n_ctx + seq)) % PEER_TOK_TILE == 0 and (b * seq) % PEER_TOK_TILE == 0

    rows = -(-(b + 1) // SUBLANE) * SUBLANE
    c_all = jnp.zeros((rows, d), F32).at[:b].set(c).at[b].set(c_ctx)
    mod = _ada_mod(c_all, w_ada, b_ada)
    mod_x = mod[:, :b].reshape(depth, b, 1, 6, d)
    mod_c = jnp.broadcast_to(mod[:, b].reshape(depth, 1, 1, 6, d), (depth, b, 1, 6, d))
    mod = jnp.concatenate([mod_c, mod_x], axis=2)

    cos_t, sin_t = _rope_tables(seq)
    splits = (pool_w, 3 * gdn_w, gdn_w, gdn_heads // 2 * LANE, attn_w, kv_w, kv_w)
    o_ba = pool_w + 4 * gdn_w
    n_ba = 4 * gdn_heads

    h = jnp.concatenate([ctx, x], axis=1)
    ffn = None
    for l in range(depth):
        last = l == depth - 1
        w_l = w_in[l]
        w_big = jnp.concatenate([w_l[:, :o_ba], _pair_lanes(w_l[:, o_ba:o_ba + n_ba], gdn_heads),
                                 w_l[:, o_ba + n_ba:]], axis=1).astype(BF16)
        pmod = mod[l - 1] if l > 0 else None
        outs = _in_proj(h, ffn, pmod, mod[l], norm1[l][None], w_big, splits, n_ctx)
        if l > 0:
            h, *outs = outs
        pa, qkv, z, ba, aq, ak, av = outs

        wbd = jax.scipy.linalg.block_diag(*[w_pool[l, g] for g in range(len(POOL_WINDOWS))])
        ya = _pool(pa, wbd, pool_scale[l][None], n_ctx)

        gparams = jnp.stack([_pair_lanes(jnp.concatenate([jnp.zeros_like(a_log[l]), a_log[l]]).reshape(1, -1),
                                         gdn_heads),
                             _pair_lanes(jnp.concatenate([jnp.zeros_like(dt_bias[l]), dt_bias[l]]).reshape(1, -1),
                                         gdn_heads)], axis=1)
        gparams = gparams.reshape(2, gdn_heads // 2, LANE).transpose(1, 0, 2)
        gparams = jnp.pad(gparams, ((0, 0), (0, SUBLANE - 2), (0, 0)))
        yb = _gdn(qkv, z, ba, conv_w[l], gparams, jnp.concatenate([gdn_norm[l]] * 2)[None], n_ctx)

        sink_row = jnp.pad(sink[l], (0, LANE - sink.shape[1]))[None]
        yc = _attn_x(aq, ak, av, cos_t, sin_t, sink_row, n_ctx)
        if not last:
            yc = jnp.concatenate([_attn_ctx(aq, ak, av, sink_row, n_ctx), yc], axis=1)

        x_new, hx = _out_proj(h, ya, yb, yc, w_out[l].astype(BF16), mod[l], norm2[l][None], n_ctx,
                              skip_ctx=last)
        t = hx.shape[0] * hx.shape[1]
        hx = hx.reshape(t, d)
        keys = peer_keys[l].reshape(2 * PEER_HEADS, N_KEYS, -1).astype(BF16)
        route = _peer_route(hx.T, peer_wq[l].T.astype(BF16), keys)
        gates = _peer_gates(route).reshape(-1, N_KEYS)
        ut = peer_u[l].reshape(-1, PEER_EXP_TILE, d).transpose(0, 2, 1).astype(BF16)
        ffn = _peer_dense(hx, gates, ut, peer_v, l)
        ffn = ffn.reshape(x_new.shape)
        h = x_new
    return _final(h, ffn, mod[depth - 1], norm_f[None])
```

```python
import functools

import jax
import jax.numpy as jnp
from jax import lax
from jax.experimental import pallas as pl
from jax.experimental.pallas import tpu as pltpu

F32, BF16 = jnp.float32, jnp.bfloat16
HIGHEST = lax.Precision.HIGHEST

EPS = 1e-6
NEG_INF = -1e30
HEAD_DIM = 64
GRID_W = 64
POOL_WINDOWS = (2, 4, 8, 16)
GDN_CHUNK = 64
CONV_K = 5
KV_HEADS = 2
WINDOW = 128
ATTN_BLOCK = 128
ROPE_BASE = 10000.0
ROPE_PAIRS = HEAD_DIM // 4
PEER_HEADS = 8
N_KEYS = 128
PEER_TOPK = 16

LANE = 128
SUBLANE = 8
TOK_TILE = 256
PROJ_TILE = 768
ROW_CHUNK = 256
ATTN_QBLOCKS = 2
GDN_GROUP = 4
PEER_ROUTE_TILE = 256
PEER_GATE_TILE = 128
PEER_GATE_UNROLL = 16
PEER_TOK_TILE = 1024
PEER_EXP_TILE = 1024
PEER_SUB = 2
VMEM_LIMIT = 48 * 1024 * 1024

_CAND_ROWS = tuple(PEER_TOPK // (a + 1) for a in range(PEER_TOPK))
_N_CAND = sum(_CAND_ROWS)
_CAND_PAD = -(-_N_CAND // SUBLANE) * SUBLANE


def _cparams(*sem):
    return pltpu.CompilerParams(dimension_semantics=sem, vmem_limit_bytes=VMEM_LIMIT)


def _dot(a, b):
    return jnp.dot(a.astype(BF16), b.astype(BF16), preferred_element_type=F32)


def _dot_nt(a, b):
    return lax.dot_general(a.astype(BF16), b.astype(BF16), (((1,), (1,)), ((), ())),
                           preferred_element_type=F32)


def _dot_tn(a, b):
    return lax.dot_general(a.astype(BF16), b.astype(BF16), (((0,), (0,)), ((), ())),
                           preferred_element_type=F32)


def _dot_hi(a, b):
    return jnp.dot(a, b, precision=HIGHEST, preferred_element_type=F32)


def _silu(x):
    return x * jax.nn.sigmoid(x)


def _softplus(x):
    return jnp.maximum(x, 0.0) + jnp.log1p(jnp.exp(-jnp.abs(x)))


def _modnorm(x, gain, scale, shift):
    y = x * lax.rsqrt(jnp.mean(x * x, axis=-1, keepdims=True) + EPS) * gain
    return y * (1.0 + scale) + shift


def _ada_kernel(c_ref, w_ref, b_ref, o_ref):
    o_ref[0] = _dot_hi(_silu(c_ref[...]), w_ref[0]) + b_ref[0]


def _ada_mod(c_all, w_ada, b_ada):
    depth, d, n = w_ada.shape
    rows = c_all.shape[0]
    tn = n // 4
    return pl.pallas_call(
        _ada_kernel,
        grid=(depth, n // tn),
        in_specs=[pl.BlockSpec((rows, d), lambda l, j: (0, 0)),
                  pl.BlockSpec((1, d, tn), lambda l, j: (l, 0, j)),
                  pl.BlockSpec((1, 1, tn), lambda l, j: (l, 0, j))],
        out_specs=pl.BlockSpec((1, rows, tn), lambda l, j: (l, 0, j)),
        out_shape=jax.ShapeDtypeStruct((depth, rows, n), F32),
        compiler_params=_cparams("parallel", "parallel"),
        name="ada_mod",
    )(c_all, w_ada, b_ada.reshape(depth, 1, n))


def _row_mod(mod_ref, first_row, n_rows, n_ctx):
    is_x = first_row + lax.broadcasted_iota(jnp.int32, (n_rows, 1), 0) >= n_ctx
    return lambda k: jnp.where(is_x, mod_ref[0, 1, k:k + 1, :], mod_ref[0, 0, k:k + 1, :])


def _inproj_kernel(*refs, has_ffn, col_splits, n_ctx):
    tile = refs[0].shape[1]
    first_row = pl.program_id(1) * tile
    if has_ffn:
        h_ref, f_ref, pmod_ref, mod_ref, gain_ref, w_ref, res_ref, *outs = refs
        x = h_ref[0] + _row_mod(pmod_ref, first_row, tile, n_ctx)(5) * f_ref[0]
        res_ref[0] = x
    else:
        h_ref, mod_ref, gain_ref, w_ref, *outs = refs
        x = h_ref[0]
    mod = _row_mod(mod_ref, first_row, tile, n_ctx)
    hb = _modnorm(x, gain_ref[...], mod(1), mod(0)).astype(BF16)
    off = 0
    for o_ref, width in zip(outs, col_splits):
        o_ref[0] = jnp.dot(hb, w_ref[:, off:off + width], preferred_element_type=F32)
        off += width


def _in_proj(h, ffn, pmod, mod, gain, w, col_splits, n_ctx):
    b, l, d = h.shape
    tile = PROJ_TILE
    tok = pl.BlockSpec((1, tile, d), lambda i, t: (i, t, 0))
    modspec = pl.BlockSpec((1, 2, 6, d), lambda i, t: (i, 0, 0, 0))
    has_ffn = ffn is not None
    in_specs = [tok] + ([tok, modspec] if has_ffn else []) + [
        modspec, pl.BlockSpec((1, d), lambda i, t: (0, 0)), pl.BlockSpec(w.shape, lambda i, t: (0, 0))]
    out_shape = [jax.ShapeDtypeStruct((b, l, wd), F32) for wd in col_splits]
    out_specs = [pl.BlockSpec((1, tile, wd), lambda i, t: (i, t, 0)) for wd in col_splits]
    if has_ffn:
        out_shape = [jax.ShapeDtypeStruct((b, l, d), F32)] + out_shape
        out_specs = [tok] + out_specs
    args = (h, ffn, pmod, mod, gain, w) if has_ffn else (h, mod, gain, w)
    return pl.pallas_call(
        functools.partial(_inproj_kernel, has_ffn=has_ffn, col_splits=col_splits, n_ctx=n_ctx),
        grid=(b, l // tile), in_specs=in_specs, out_specs=out_specs, out_shape=out_shape,
        compiler_params=_cparams("parallel", "parallel"), name="in_proj",
    )(*args)


def _pool_kernel(a_ref, wbd_ref, scale_ref, o_ref, pad_ref, *, segments):
    c = a_ref.shape[-1]
    grp = lax.broadcasted_iota(jnp.int32, (1, c), 1) // (c // len(POOL_WINDOWS))
    zeros = jnp.zeros((SUBLANE, c), F32)
    for off, n in segments:
        pad_ref[0:SUBLANE, :] = zeros
        pad_ref[SUBLANE:SUBLANE + n, :] = a_ref[0, off:off + n, :]
        pad_ref[SUBLANE + n:2 * SUBLANE + n, :] = zeros

        def chunk(ci, carry, off=off, n=n):
            r0 = pl.multiple_of(ci * ROW_CHUNK, ROW_CHUNK)
            win = pad_ref[pl.ds(r0, ROW_CHUNK + 2 * SUBLANE), :]
            sh = lambda s: win[SUBLANE + s:SUBLANE + s + ROW_CHUNK, :]
            a0 = sh(0)
            s2 = sh(-1) + a0
            s4 = s2 + sh(-2) + sh(1)
            s8 = s4 + sh(-4) + sh(-3) + sh(2) + sh(3)
            s16 = s8 + (sh(-8) + sh(-7) + sh(-6) + sh(-5)) + (sh(4) + sh(5) + sh(6) + sh(7))
            t = r0 + lax.broadcasted_iota(jnp.int32, (ROW_CHUNK, 1), 0)

            def mean(s, w):
                cnt = jnp.minimum(t + (w - w // 2), n) - jnp.maximum(t - w // 2, 0)
                return s / cnt.astype(F32)

            pooled = jnp.where(grp == 0, mean(s2, 2), jnp.where(grp == 1, mean(s4, 4),
                               jnp.where(grp == 2, mean(s8, 8), mean(s16, 16)))) - a0
            o_ref[0, pl.ds(off + r0, ROW_CHUNK), :] = _dot_hi(pooled, wbd_ref[...]) * scale_ref[...]
            return carry

        lax.fori_loop(0, n // ROW_CHUNK, chunk, 0)


def _pool(a, wbd, scale, n_ctx):
    b, l, c = a.shape
    segments = ((0, n_ctx), (n_ctx, l - n_ctx))
    return pl.pallas_call(
        functools.partial(_pool_kernel, segments=segments),
        grid=(b,),
        in_specs=[pl.BlockSpec((1, l, c), lambda i: (i, 0, 0)),
                  pl.BlockSpec((c, c), lambda i: (0, 0)),
                  pl.BlockSpec((1, c), lambda i: (0, 0))],
        out_specs=pl.BlockSpec((1, l, c), lambda i: (i, 0, 0)),
        out_shape=jax.ShapeDtypeStruct((b, l, c), F32),
        scratch_shapes=[pltpu.VMEM((l - n_ctx + 2 * SUBLANE, c), F32)],
        compiler_params=_cparams("parallel"), name="pool",
    )(a, wbd, scale)


def _gdn_kernel(q_ref, k_ref, v_ref, z_ref, ba_ref, cwq_ref, cwk_ref, cwv_ref, gp_ref, gain_ref, o_ref,
                pad_ref, yq_ref, yk_ref, yv_ref, bg_ref, of_ref, ob_ref, s_ref,
                qe_ref, mn_ref, oc_ref, nn_ref, egl_ref, *, n_ctx):
    l = q_ref.shape[1]
    n_x = l - n_ctx
    hd = HEAD_DIM
    lane = lax.broadcasted_iota(jnp.int32, (1, LANE), 1)
    lo = lane < hd
    zeros = jnp.zeros((SUBLANE, LANE), F32)
    x_off = 2 * SUBLANE + n_ctx

    def head_scale(ss):
        s_lo = jnp.sum(jnp.where(lo, ss, 0.0), axis=-1, keepdims=True)
        s_hi = jnp.sum(jnp.where(lo, 0.0, ss), axis=-1, keepdims=True)
        return jnp.where(lo, s_lo, s_hi)

    def conv(u_ref, cw_ref, y_ref, post):
        pad_ref[0:SUBLANE, :] = zeros
        pad_ref[SUBLANE:SUBLANE + n_ctx, :] = u_ref[0, 0:n_ctx, :]
        pad_ref[SUBLANE + n_ctx:x_off, :] = zeros
        pad_ref[x_off:x_off + n_x, :] = u_ref[0, n_ctx:l, :]
        pad_ref[x_off + n_x:x_off + n_x + SUBLANE, :] = zeros
        cw = cw_ref[...]
        for poff, yoff, n in ((SUBLANE, 0, n_ctx), (x_off, n_ctx, n_x)):
            def chunk(ci, carry, poff=poff, yoff=yoff):
                r0 = pl.multiple_of(ci * ROW_CHUNK, ROW_CHUNK)
                win = pad_ref[pl.ds(poff - SUBLANE + r0, ROW_CHUNK + 2 * SUBLANE), :]
                acc = win[SUBLANE - 2:SUBLANE - 2 + ROW_CHUNK, :] * cw[0:1, :]
                for j in range(1, CONV_K):
                    acc = acc + win[SUBLANE - 2 + j:SUBLANE - 2 + j + ROW_CHUNK, :] * cw[j:j + 1, :]
                y_ref[pl.ds(yoff + r0, ROW_CHUNK), :] = post(_silu(acc))
                return carry
            lax.fori_loop(0, n // ROW_CHUNK, chunk, 0)

    l2 = lambda y: y * lax.rsqrt(head_scale(y * y) + EPS)
    conv(q_ref, cwq_ref, yq_ref, lambda y: l2(y) * (hd ** -0.5))
    conv(k_ref, cwk_ref, yk_ref, l2)
    conv(v_ref, cwv_ref, yv_ref, lambda y: y)

    ba = ba_ref[0]
    g = -jnp.exp(gp_ref[0, 0:1, :]) * _softplus(ba + gp_ref[0, 1:2, :])
    bg_ref[...] = jnp.where(lane < 4, jax.nn.sigmoid(ba), g)

    ck = GDN_CHUNK
    rr = lax.broadcasted_iota(jnp.int32, (ck, ck), 0)
    cc = lax.broadcasted_iota(jnp.int32, (ck, ck), 1)
    eye = (rr == cc).astype(F32)
    tril = (rr >= cc).astype(F32)
    n_chunks = l // ck
    nc_ctx = n_ctx // ck

    def phase1(ci, carry):
        probs = []
        for gi in range(GDN_GROUP):
            c = ci * GDN_GROUP + gi
            r0 = pl.multiple_of(c * ck, ck)
            qc = yq_ref[pl.ds(r0, ck), :]
            kc = yk_ref[pl.ds(r0, ck), :]
            vc = yv_ref[pl.ds(r0, ck), :]
            bgc = bg_ref[pl.ds(r0, ck), :]
            gcf = _dot_hi(tril, bgc)
            gcb = gcf[ck - 1:ck, :] - gcf + bgc
            gct = (gcf.T, gcb.T)
            for j in range(2):
                qh = qc[:, j * hd:(j + 1) * hd]
                kh = kc[:, j * hd:(j + 1) * hd]
                vh = vc[:, j * hd:(j + 1) * hd]
                kk = _dot_nt(kh, kh)
                qk = _dot_nt(qh, kh)
                for d in range(2):
                    mask = (rr >= cc) if d == 0 else (rr <= cc)
                    smask = (rr > cc) if d == 0 else (rr < cc)
                    lg, lb = 4 + 2 * d + j, 2 * d + j
                    gcol = jnp.broadcast_to((gcf, gcb)[d][:, lg:lg + 1], (ck, hd))
                    grow = gct[d][lg:lg + 1, :]
                    bcol = jnp.broadcast_to(bgc[:, lb:lb + 1], (ck, hd))
                    decay = jnp.where(mask, jnp.exp(jnp.where(mask, gcol - grow, 0.0)), 0.0)
                    xp = jnp.where(smask, -(bcol * kk * decay), 0.0)
                    eg = jnp.exp(gcol)
                    last = ck - 1 if d == 0 else 0
                    glast = gcol[last:last + 1, :]
                    probs.append(dict(
                        idx=c * 4 + 2 * d + j, xp=xp, inv=eye + xp, attn=qk * decay, qd=qh * eg,
                        rhs=jnp.concatenate([vh * bcol, kh * bcol * eg], axis=1),
                        kd=kh * jnp.exp(glast - gcol), egl=jnp.exp(glast)))
        for _ in range(5):
            for p in probs:
                p["xp"] = _dot(p["xp"], p["xp"])
            for p in probs:
                p["inv"] = p["inv"] + _dot(p["inv"], p["xp"])
        for p in probs:
            p["sol"] = _dot(p["inv"], p["rhs"])
        for p in probs:
            p["as"] = _dot(p["attn"], p["sol"])
        for p in probs:
            p["ks"] = _dot_tn(p["kd"], p["sol"])
        for p in probs:
            i = p["idx"]
            qe_ref[i] = (p["qd"] - p["as"][:, hd:]).astype(BF16)
            oc_ref[i] = p["as"][:, :hd]
            mn_ref[i] = p["ks"][:, hd:].astype(BF16)
            nn_ref[i] = p["ks"][:, :hd]
            egl_ref[i] = jnp.broadcast_to(p["egl"], (ck, hd))
        return carry

    lax.fori_loop(0, n_chunks // GDN_GROUP, phase1, 0)

    s_ref[...] = jnp.zeros(s_ref.shape, F32)

    def phase2(i, carry):
        cb = jnp.where(i < nc_ctx, nc_ctx - 1 - i, n_chunks - 1 - (i - nc_ctx))
        work = [(d, j, cidx * 4 + 2 * d + j) for d, cidx in ((0, i), (1, cb)) for j in range(2)]
        states = [s_ref[2 * d + j] for d, j, _ in work]
        outs = [_dot(qe_ref[idx], s) + oc_ref[idx] for (_, _, idx), s in zip(work, states)]
        upd = [_dot(mn_ref[idx], s) for (_, _, idx), s in zip(work, states)]
        for (d, j, idx), s, m in zip(work, states, upd):
            s_ref[2 * d + j] = egl_ref[idx] * s - m + nn_ref[idx]
        of_ref[pl.ds(pl.multiple_of(i * ck, ck), ck), :] = jnp.concatenate(outs[0:2], axis=1)
        ob_ref[pl.ds(pl.multiple_of(cb * ck, ck), ck), :] = jnp.concatenate(outs[2:4], axis=1)
        return carry

    lax.fori_loop(0, n_chunks, phase2, 0)

    def finish(ci, carry):
        r0 = pl.multiple_of(ci * ROW_CHUNK, ROW_CHUNK)
        o = of_ref[pl.ds(r0, ROW_CHUNK), :] + ob_ref[pl.ds(r0, ROW_CHUNK), :]
        o = o * lax.rsqrt(head_scale(o * o) * (1.0 / hd) + EPS) * gain_ref[...]
        o_ref[0, pl.ds(r0, ROW_CHUNK), :] = o * _silu(z_ref[0, pl.ds(r0, ROW_CHUNK), :])
        return carry

    lax.fori_loop(0, l // ROW_CHUNK, finish, 0)


def _gdn(qkv, z, ba, conv_w, gparams, gain2, n_ctx):
    b, l, w3 = qkv.shape
    npair = w3 // (3 * LANE)
    seq = lambda off: pl.BlockSpec((1, l, LANE), lambda i, p: (i, 0, p + off))
    cw = lambda off: pl.BlockSpec((CONV_K, LANE), lambda i, p: (0, p + off))
    return pl.pallas_call(
        functools.partial(_gdn_kernel, n_ctx=n_ctx),
        grid=(b, npair),
        in_specs=[seq(0), seq(npair), seq(2 * npair), seq(0), seq(0),
                  cw(0), cw(npair), cw(2 * npair),
                  pl.BlockSpec((1, SUBLANE, LANE), lambda i, p: (p, 0, 0)),
                  pl.BlockSpec((1, LANE), lambda i, p: (0, 0))],
        out_specs=seq(0),
        out_shape=jax.ShapeDtypeStruct((b, l, npair * LANE), F32),
        scratch_shapes=[pltpu.VMEM((l + 3 * SUBLANE, LANE), F32)] + [pltpu.VMEM((l, LANE), F32)] * 6
        + [pltpu.VMEM((4, HEAD_DIM, HEAD_DIM), F32)]
        + [pltpu.VMEM((4 * l // GDN_CHUNK, GDN_CHUNK, HEAD_DIM), dt) for dt in (BF16, BF16, F32, F32, F32)],
        compiler_params=_cparams("parallel", "parallel"), name="gdn",
    )(qkv, qkv, qkv, z, ba, conv_w, conv_w, conv_w, gparams, gain2)


def _rope(x, cos, sin):
    w = x.shape[-1]
    reps = w // cos.shape[-1]
    if reps > 1:
        cos = jnp.concatenate([cos] * reps, axis=1)
        sin = jnp.concatenate([sin] * reps, axis=1)
    first = (lax.broadcasted_iota(jnp.int32, (1, w), 1) % (2 * ROPE_PAIRS)) < ROPE_PAIRS
    partner = jnp.where(first, pltpu.roll(x, w - ROPE_PAIRS, 1), pltpu.roll(x, ROPE_PAIRS, 1))
    return x * cos + partner * sin


def _attend(q, k, v, valid, sink_col):
    s = _dot_nt(q, k)
    if valid is not None:
        s = jnp.where(valid, s, NEG_INF)
    m = jnp.maximum(jnp.max(s, axis=-1, keepdims=True), sink_col)
    p = jnp.exp(s - m)
    den = jnp.sum(p, axis=-1, keepdims=True) + jnp.exp(sink_col - m)
    return _dot(p, v) / den


def _gqa(q, k_all, v_all, valid, sink_ref):
    nq = q.shape[0]
    hd = HEAD_DIM
    group = q.shape[1] // hd // KV_HEADS
    outs = []
    for kvh in range(KV_HEADS):
        heads = [kvh * group + g for g in range(group)]
        qg = jnp.concatenate([q[:, h * hd:(h + 1) * hd] for h in heads], axis=0)
        sink_col = jnp.concatenate(
            [jnp.broadcast_to(sink_ref[0:1, h:h + 1], (nq, 1)) for h in heads], axis=0)
        o = _attend(qg, k_all[:, kvh * hd:(kvh + 1) * hd], v_all[:, kvh * hd:(kvh + 1) * hd],
                    valid, sink_col)
        outs += [o[g * nq:(g + 1) * nq, :] for g in range(group)]
    return jnp.concatenate(outs, axis=1)


def _attn_kernel(q_ref, *refs, seq):
    nkb = ATTN_QBLOCKS + 2
    k_refs, v_refs = refs[0:nkb], refs[nkb:2 * nkb]
    kc_ref, vc_ref = refs[2 * nkb:2 * nkb + 2]
    cos_refs = refs[2 * nkb + 2:3 * nkb + 2]
    sin_refs = refs[3 * nkb + 2:4 * nkb + 2]
    sink_ref, o_ref = refs[4 * nkb + 2:]
    blk = ATTN_BLOCK
    n_ctx = kc_ref.shape[1]
    group = q_ref.shape[2] // HEAD_DIM // KV_HEADS
    k_rot = [_rope(k_refs[j][0], cos_refs[j][...], sin_refs[j][...]) for j in range(nkb)]
    nk = 3 * blk + n_ctx
    qo = lax.broadcasted_iota(jnp.int32, (group * blk, 1), 0) % blk
    ko = lax.broadcasted_iota(jnp.int32, (1, nk), 1)
    in_window = jnp.abs(ko - blk - qo) <= WINDOW
    for s in range(ATTN_QBLOCKS):
        i = pl.program_id(1) * ATTN_QBLOCKS + s
        rows = slice(s * blk, (s + 1) * blk)
        q = _rope(q_ref[0, rows, :], cos_refs[s + 1][...], sin_refs[s + 1][...]) * (HEAD_DIM ** -0.5)
        k_all = jnp.concatenate(k_rot[s:s + 3] + [kc_ref[0]], axis=0)
        v_all = jnp.concatenate([v_refs[j][0] for j in range(s, s + 3)] + [vc_ref[0]], axis=0)
        kpos = (i - 1) * blk + ko
        valid = (in_window & (kpos >= 0) & (kpos < seq)) | (ko >= 3 * blk)
        o_ref[0, rows, :] = _gqa(q, k_all, v_all, valid, sink_ref)


def _attn_x(aq, ak, av, cos_t, sin_t, sink_row, n_ctx):
    b, l, wq = aq.shape
    wk = ak.shape[-1]
    blk = ATTN_BLOCK
    seq = l - n_ctx
    nb = seq // blk
    cb = n_ctx // blk
    nq = ATTN_QBLOCKS
    assert nb % nq == 0 and cb % nq == 0
    blocks = [lambda t, j=j: jnp.clip(t * nq + j - 1, 0, nb - 1) for j in range(nq + 2)]
    kv = [pl.BlockSpec((1, blk, wk), lambda i, t, f=f: (i, f(t) + cb, 0)) for f in blocks]
    tab = [pl.BlockSpec((blk, LANE), lambda i, t, f=f: (f(t), 0)) for f in blocks]
    ctx = pl.BlockSpec((1, n_ctx, wk), lambda i, t: (i, 0, 0))
    qspec = pl.BlockSpec((1, nq * blk, wq), lambda i, t: (i, t + cb // nq, 0))
    return pl.pallas_call(
        functools.partial(_attn_kernel, seq=seq),
        grid=(b, nb // nq),
        in_specs=[qspec] + kv + kv + [ctx, ctx] + tab + tab + [pl.BlockSpec((1, LANE), lambda i, t: (0, 0))],
        out_specs=pl.BlockSpec((1, nq * blk, wq), lambda i, t: (i, t, 0)),
        out_shape=jax.ShapeDtypeStruct((b, seq, wq), F32),
        compiler_params=_cparams("parallel", "parallel"), name="attn_x",
    )(aq, *([ak] * (nq + 2)), *([av] * (nq + 2)), ak, av, *([cos_t] * (nq + 2)), *([sin_t] * (nq + 2)),
      sink_row)


def _attn_ctx_kernel(q_ref, k_ref, v_ref, sink_ref, o_ref):
    o_ref[0] = _gqa(q_ref[0] * (HEAD_DIM ** -0.5), k_ref[0], v_ref[0], None, sink_ref)


def _attn_ctx(aq, ak, av, sink_row, n_ctx):
    b, _, wq = aq.shape
    wk = ak.shape[-1]
    spec = lambda w: pl.BlockSpec((1, n_ctx, w), lambda i: (i, 0, 0))
    return pl.pallas_call(
        _attn_ctx_kernel, grid=(b,),
        in_specs=[spec(wq), spec(wk), spec(wk), pl.BlockSpec((1, LANE), lambda i: (0, 0))],
        out_specs=spec(wq), out_shape=jax.ShapeDtypeStruct((b, n_ctx, wq), F32),
        compiler_params=_cparams("parallel"), name="attn_ctx",
    )(aq, ak, av, sink_row)


def _outproj_kernel(x_ref, ya_ref, yb_ref, yc_ref, w_ref, mod_ref, gain_ref, xo_ref, hx_ref, *, row_off, n_ctx):
    wa, wb = ya_ref.shape[-1], yb_ref.shape[-1]
    tile = x_ref.shape[1]
    acc = (jnp.dot(ya_ref[0].astype(BF16), w_ref[0:wa, :], preferred_element_type=F32)
           + jnp.dot(yb_ref[0].astype(BF16), w_ref[wa:wa + wb, :], preferred_element_type=F32)
           + jnp.dot(yc_ref[0].astype(BF16), w_ref[wa + wb:, :], preferred_element_type=F32))
    mod = _row_mod(mod_ref, row_off + pl.program_id(1) * tile, tile, n_ctx)
    x = x_ref[0] + mod(2) * acc
    xo_ref[0] = x
    hx_ref[0] = _modnorm(x, gain_ref[...], mod(4), mod(3)).astype(BF16)


def _out_proj(h, ya, yb, yc, w, mod, gain, n_ctx, skip_ctx):
    b, l, d = h.shape
    tile = TOK_TILE if skip_ctx else PROJ_TILE
    off = n_ctx // tile if skip_ctx else 0
    nt = l // tile - off
    full = lambda wd: pl.BlockSpec((1, tile, wd), lambda i, t: (i, t + off, 0))
    own = lambda wd: pl.BlockSpec((1, tile, wd), lambda i, t: (i, t, 0))
    return pl.pallas_call(
        functools.partial(_outproj_kernel, row_off=off * tile, n_ctx=n_ctx), grid=(b, nt),
        in_specs=[full(d), full(ya.shape[-1]), full(yb.shape[-1]), own(yc.shape[-1]),
                  pl.BlockSpec(w.shape, lambda i, t: (0, 0)),
                  pl.BlockSpec((1, 2, 6, d), lambda i, t: (i, 0, 0, 0)),
                  pl.BlockSpec((1, d), lambda i, t: (0, 0))],
        out_specs=[own(d), own(d)],
        out_shape=[jax.ShapeDtypeStruct((b, nt * tile, d), F32),
                   jax.ShapeDtypeStruct((b, nt * tile, d), BF16)],
        compiler_params=_cparams("parallel", "parallel"), name="out_proj",
    )(h, ya, yb, yc, w, mod, gain)


def _sort_pairs(n):
    pairs, p = [], 1
    while p < n:
        k = p
        while k >= 1:
            for j in range(k % p, n - k, 2 * k):
                for i in range(min(k, n - j - k)):
                    if (i + j) // (2 * p) == (i + j + k) // (2 * p):
                        pairs.append((i + j, i + j + k))
            k //= 2
        p *= 2
    return pairs


def _topk_rows(v):
    n = len(v)
    v = list(v)

    def cmpx(i, j):
        v[i], v[j] = jnp.maximum(v[i], v[j]), jnp.minimum(v[i], v[j])

    for i, j in _sort_pairs(n):
        cmpx(i, j)
    shift = SUBLANE // 2
    while shift >= 1:
        partner = [pltpu.roll(x, shift, 0) for x in v]
        v = [jnp.maximum(v[j], partner[n - 1 - j]) for j in range(n)]
        stride = n // 2
        while stride >= 1:
            for i in range(n):
                if not i & stride:
                    cmpx(i, i + stride)
            stride //= 2
        shift //= 2
    return v


def _route_kernel(ht_ref, wq_ref, keys_ref, r2_ref, e2_ref, n1_ref, e1_ref, qt_ref, top_ref, cand_ref):
    tm = ht_ref.shape[1]
    k = PEER_TOPK
    nk = N_KEYS
    neg = jnp.float32(-jnp.inf)
    qt_ref[...] = _dot(wq_ref[...], ht_ref[...]).astype(BF16)
    cand_ref[_N_CAND:_CAND_PAD, :] = jnp.full((_CAND_PAD - _N_CAND, tm), neg, F32)
    for h in range(PEER_HEADS):
        st = []
        for p in range(2):
            hp = 2 * h + p
            s = jnp.dot(keys_ref[hp], qt_ref[hp * nk:(hp + 1) * nk, :], preferred_element_type=F32)
            st.append(s)
            top = _topk_rows([s[g * SUBLANE:(g + 1) * SUBLANE, :] for g in range(nk // SUBLANE)])
            for r in range(k):
                top_ref[p * k + r:p * k + r + 1, :] = top[r][0:1, :]
        v1 = top_ref[0:k, :]
        v2 = top_ref[k:2 * k, :]
        rank2 = jnp.full((nk, tm), float(k), F32)
        for r in range(k):
            rank2 = jnp.where(st[1] == v2[r:r + 1, :], float(r), rank2)
        row = 0
        for a, nb in enumerate(_CAND_ROWS):
            cand_ref[row:row + nb, :] = v1[a:a + 1, :] + v2[0:nb, :]
            row += nb
        cur = cand_ref[...]
        cmax = thr = zsum = None
        for r in range(k):
            m = jnp.max(cur, axis=0, keepdims=True)
            if r == 0:
                cmax, zsum = m, jnp.ones_like(m)
            else:
                zsum = zsum + jnp.exp(m - cmax)
            thr = m
            cur = jnp.where(cur == m, neg, cur)
        n1 = jnp.zeros((nk, tm), F32)
        row = 0
        for a, nb in enumerate(_CAND_ROWS):
            sel = cand_ref[row:row + nb, :] >= thr
            n_a = jnp.sum(sel.astype(F32), axis=0, keepdims=True)
            n1 = jnp.where(st[0] == v1[a:a + 1, :], n_a, n1)
            row += nb
        r2_ref[h] = rank2.T
        e2_ref[h] = jnp.exp(st[1] - v2[0:1, :]).T
        n1_ref[h] = n1.T
        e1_ref[h] = (jnp.exp(st[0] - v1[0:1, :]) / zsum).T


def _peer_route(hxt, wq_t, keys):
    d, t = hxt.shape
    nq = wq_t.shape[0]
    tm = PEER_ROUTE_TILE
    rows = PEER_HEADS * N_KEYS
    return pl.pallas_call(
        _route_kernel, grid=(t // tm,),
        in_specs=[pl.BlockSpec((d, tm), lambda i: (0, i)),
                  pl.BlockSpec(wq_t.shape, lambda i: (0, 0)),
                  pl.BlockSpec(keys.shape, lambda i: (0, 0, 0))],
        out_specs=[pl.BlockSpec((PEER_HEADS, tm, N_KEYS), lambda i: (0, i, 0))] * 4,
        out_shape=[jax.ShapeDtypeStruct((PEER_HEADS, t, N_KEYS), F32)] * 4,
        scratch_shapes=[pltpu.VMEM((nq, tm), BF16), pltpu.VMEM((2 * PEER_TOPK, tm), F32),
                        pltpu.VMEM((_CAND_PAD, tm), F32)],
        compiler_params=_cparams("parallel"), name="peer_route",
    )(hxt, wq_t, keys)


def _gates_kernel(r2t_ref, e2t_ref, n1t_ref, e1t_ref, w_ref):
    tb = w_ref.shape[1]
    nk = N_KEYS
    k = PEER_TOPK
    rank = lax.broadcasted_iota(jnp.int32, (k, nk), 0).astype(F32)

    def tok(i, carry):
        for j in range(PEER_GATE_UNROLL):
            t = i * PEER_GATE_UNROLL + j
            a, b = [], []
            for h in range(PEER_HEADS):
                r2, e2, n1, e1 = (ref[h, pl.ds(t, k, stride=0), :]
                                  for ref in (r2t_ref, e2t_ref, n1t_ref, e1t_ref))
                a.append(jnp.where(n1 > rank, e1, 0.0))
                b.append(jnp.where(r2 == rank, e2, 0.0))
            w = _dot_tn(jnp.concatenate(a, axis=0), jnp.concatenate(b, axis=0))
            for g in range(nk // SUBLANE):
                w_ref[g, t] = w[g * SUBLANE:(g + 1) * SUBLANE, :]
        return carry

    lax.fori_loop(0, tb // PEER_GATE_UNROLL, tok, 0)


def _peer_gates(route):
    t = route[0].shape[1]
    tb = PEER_GATE_TILE
    return pl.pallas_call(
        _gates_kernel, grid=(t // tb,),
        in_specs=[pl.BlockSpec((PEER_HEADS, tb, N_KEYS), lambda i: (0, i, 0))] * 4,
        out_specs=pl.BlockSpec((N_KEYS // SUBLANE, tb, SUBLANE, N_KEYS), lambda i: (0, i, 0, 0)),
        out_shape=jax.ShapeDtypeStruct((N_KEYS // SUBLANE, t, SUBLANE, N_KEYS), F32),
        compiler_params=_cparams("parallel"), name="peer_gates",
    )(*route)


def _peer_kernel(h_ref, w_ref, ut_ref, v_ref, o_ref, acc_ref, g0_ref, g1_ref):
    e = pl.program_id(1)
    n_blk = pl.num_programs(1) - 1

    @pl.when(e == 0)
    def _():
        acc_ref[...] = jnp.zeros(acc_ref.shape, F32)
        g1_ref[...] = jnp.zeros(g1_ref.shape, BF16)

    @pl.when(e % 2 == 0)
    def _():
        _peer_step(h_ref, w_ref, ut_ref, v_ref, acc_ref, g0_ref, g1_ref)

    @pl.when(e % 2 == 1)
    def _():
        _peer_step(h_ref, w_ref, ut_ref, v_ref, acc_ref, g1_ref, g0_ref)

    @pl.when(e == n_blk)
    def _():
        o_ref[...] = acc_ref[...]


def _peer_step(h_ref, w_ref, ut_ref, v_ref, acc_ref, g_w_ref, g_r_ref):
    sr = h_ref.shape[0] // PEER_SUB

    def sub(k, carry):
        rows = pl.ds(pl.multiple_of(k * sr, sr), sr)
        acc_ref[rows, :] += jnp.dot(g_r_ref[rows, :], v_ref[...].astype(BF16), preferred_element_type=F32)
        act = jnp.dot(h_ref[rows, :], ut_ref[0], preferred_element_type=F32)
        act = 0.5 * act * (1.0 + lax.erf(act * 0.7071067811865476))
        for j in range(SUBLANE):
            ln = slice(j * N_KEYS, (j + 1) * N_KEYS)
            wj = w_ref[pl.ds(k * (sr * SUBLANE) + j, sr, stride=SUBLANE), :]
            g_w_ref[rows, ln] = (wj * act[:, ln]).astype(BF16)
        return carry

    lax.fori_loop(0, PEER_SUB, sub, 0)


def _peer_dense(hx, w, ut, v_all, layer):
    t, d = hx.shape
    n_blk, _, eb = ut.shape
    tm = PEER_TOK_TILE
    last = n_blk - 1
    return pl.pallas_call(
        _peer_kernel, grid=(t // tm, n_blk + 1),
        in_specs=[pl.BlockSpec((tm, d), lambda i, e: (i, 0)),
                  pl.BlockSpec((tm * SUBLANE, N_KEYS), lambda i, e: (jnp.minimum(e, last) * (t // tm) + i, 0)),
                  pl.BlockSpec((1, d, eb), lambda i, e: (jnp.minimum(e, last), 0, 0)),
                  pl.BlockSpec((None, eb, d), lambda i, e: (layer, jnp.maximum(e - 1, 0), 0))],
        out_specs=pl.BlockSpec((tm, d), lambda i, e: (i, 0)),
        out_shape=jax.ShapeDtypeStruct((t, d), F32),
        scratch_shapes=[pltpu.VMEM((tm, d), F32), pltpu.VMEM((tm, eb), BF16), pltpu.VMEM((tm, eb), BF16)],
        compiler_params=_cparams("parallel", "arbitrary"), name="peer_dense",
    )(hx, w, ut, v_all)


def _final_kernel(x_ref, f_ref, mod_ref, gain_ref, o_ref):
    x = x_ref[0] + mod_ref[0, 0, 5:6, :] * f_ref[0]
    o_ref[0] = x * lax.rsqrt(jnp.mean(x * x, axis=-1, keepdims=True) + EPS) * gain_ref[...]


def _final(x, ffn, mod, gain):
    b, s, d = x.shape
    tok = pl.BlockSpec((1, TOK_TILE, d), lambda i, t: (i, t, 0))
    return pl.pallas_call(
        _final_kernel, grid=(b, s // TOK_TILE),
        in_specs=[tok, tok, pl.BlockSpec((1, 1, 6, d), lambda i, t: (i, 1, 0, 0)),
                  pl.BlockSpec((1, d), lambda i, t: (0, 0))],
        out_specs=tok, out_shape=jax.ShapeDtypeStruct((b, s, d), F32),
        compiler_params=_cparams("parallel", "parallel"), name="final_norm",
    )(x, ffn, mod, gain)


def _rope_tables(seq):
    pos = jnp.arange(seq)
    rc = jnp.stack([pos // GRID_W, pos % GRID_W], axis=-1).astype(F32)
    inv = jnp.power(ROPE_BASE, -jnp.arange(ROPE_PAIRS, dtype=F32) / ROPE_PAIRS)
    ang = rc[:, :, None] * inv
    cos = jnp.concatenate([jnp.cos(ang)] * 2, axis=-1).reshape(seq, HEAD_DIM)
    sin = jnp.concatenate([-jnp.sin(ang), jnp.sin(ang)], axis=-1).reshape(seq, HEAD_DIM)
    return jnp.concatenate([cos] * 2, axis=1), jnp.concatenate([sin] * 2, axis=1)


def _pair_lanes(cols, n_heads):
    rows = cols.shape[0]
    c = cols.reshape(rows, 4, n_heads // 2, 2).transpose(0, 2, 1, 3).reshape(rows, n_heads // 2, 8)
    return jnp.pad(c, ((0, 0), (0, 0), (0, LANE - 8))).reshape(rows, n_heads // 2 * LANE)


def kernel(x, c, ctx, c_ctx, w_ada, b_ada, norm1, norm2, w_in, conv_w, a_log, dt_bias, gdn_norm, w_pool,
           pool_scale, sink, w_out, peer_wq, peer_keys, peer_u, peer_v, norm_f):
    b, seq, d = x.shape
    n_ctx = ctx.shape[1]
    depth = w_ada.shape[0]
    pool_w = d // 4
    gdn_w = 3 * d // 8
    gdn_heads = gdn_w // HEAD_DIM
    attn_w = d - pool_w - gdn_w
    kv_w = KV_HEADS * HEAD_DIM
    assert n_ctx % TOK_TILE == 0 and seq % TOK_TILE == 0 and gdn_heads % 2 == 0
    assert (n_ctx + seq) % PROJ_TILE == 0 and PEER_EXP_TILE == SUBLANE * N_KEYS
    assert (b * (n_ctx + seq)) % PEER_TOK_TILE == 0 and (b * seq) % PEER_TOK_TILE == 0

    rows = -(-(b + 1) // SUBLANE) * SUBLANE
    c_all = jnp.zeros((rows, d), F32).at[:b].set(c).at[b].set(c_ctx)
    mod = _ada_mod(c_all, w_ada, b_ada)
    mod_x = mod[:, :b].reshape(depth, b, 1, 6, d)
    mod_c = jnp.broadcast_to(mod[:, b].reshape(depth, 1, 1, 6, d), (depth, b, 1, 6, d))
    mod = jnp.concatenate([mod_c, mod_x], axis=2)

    cos_t, sin_t = _rope_tables(seq)
    splits = (pool_w, 3 * gdn_w, gdn_w, gdn_heads // 2 * LANE, attn_w, kv_w, kv_w)
    o_ba = pool_w + 4 * gdn_w
    n_ba = 4 * gdn_heads

    h = jnp.concatenate([ctx, x], axis=1)
    ffn = None
    for l in range(depth):
        last = l == depth - 1
        w_l = w_in[l]
        w_big = jnp.concatenate([w_l[:, :o_ba], _pair_lanes(w_l[:, o_ba:o_ba + n_ba], gdn_heads),
                                 w_l[:, o_ba + n_ba:]], axis=1).astype(BF16)
        pmod = mod[l - 1] if l > 0 else None
        outs = _in_proj(h, ffn, pmod, mod[l], norm1[l][None], w_big, splits, n_ctx)
        if l > 0:
            h, *outs = outs
        pa, qkv, z, ba, aq, ak, av = outs

        wbd = jax.scipy.linalg.block_diag(*[w_pool[l, g] for g in range(len(POOL_WINDOWS))])
        ya = _pool(pa, wbd, pool_scale[l][None], n_ctx)

        gparams = jnp.stack([_pair_lanes(jnp.concatenate([jnp.zeros_like(a_log[l]), a_log[l]]).reshape(1, -1),
                                         gdn_heads),
                             _pair_lanes(jnp.concatenate([jnp.zeros_like(dt_bias[l]), dt_bias[l]]).reshape(1, -1),
                                         gdn_heads)], axis=1)
        gparams = gparams.reshape(2, gdn_heads // 2, LANE).transpose(1, 0, 2)
        gparams = jnp.pad(gparams, ((0, 0), (0, SUBLANE - 2), (0, 0)))
        yb = _gdn(qkv, z, ba, conv_w[l], gparams, jnp.concatenate([gdn_norm[l]] * 2)[None], n_ctx)

        sink_row = jnp.pad(sink[l], (0, LANE - sink.shape[1]))[None]
        yc = _attn_x(aq, ak, av, cos_t, sin_t, sink_row, n_ctx)
        if not last:
            yc = jnp.concatenate([_attn_ctx(aq, ak, av, sink_row, n_ctx), yc], axis=1)

        x_new, hx = _out_proj(h, ya, yb, yc, w_out[l].astype(BF16), mod[l], norm2[l][None], n_ctx,
                              skip_ctx=last)
        t = hx.shape[0] * hx.shape[1]
        hx = hx.reshape(t, d)
        keys = peer_keys[l].reshape(2 * PEER_HEADS, N_KEYS, -1).astype(BF16)
        route = _peer_route(hx.T, peer_wq[l].T.astype(BF16), keys)
        gates = _peer_gates(route).reshape(-1, N_KEYS)
        ut = peer_u[l].reshape(-1, PEER_EXP_TILE, d).transpose(0, 2, 1).astype(BF16)
        ffn = _peer_dense(hx, gates, ut, peer_v, l)
        ffn = ffn.reshape(x_new.shape)
        h = x_new
    return _final(h, ffn, mod[depth - 1], norm_f[None])
```

```python
import functools

import jax
import jax.numpy as jnp
from jax import lax
from jax.experimental import pallas as pl
from jax.experimental.pallas import tpu as pltpu

F32, BF16 = jnp.float32, jnp.bfloat16
HIGHEST = lax.Precision.HIGHEST

EPS = 1e-6
NEG_INF = -1e30
HEAD_DIM = 64
GRID_W = 64
POOL_WINDOWS = (2, 4, 8, 16)
GDN_CHUNK = 64
CONV_K = 5
KV_HEADS = 2
WINDOW = 128
ATTN_BLOCK = 128
ROPE_BASE = 10000.0
ROPE_PAIRS = HEAD_DIM // 4
PEER_HEADS = 8
N_KEYS = 128
PEER_TOPK = 16

LANE = 128
SUBLANE = 8
TOK_TILE = 256
PROJ_TILE = 768
ROW_CHUNK = 256
ATTN_QBLOCKS = 2
GDN_GROUP = 4
PEER_ROUTE_TILE = 512
PEER_GATE_TILE = 128
PEER_GATE_UNROLL = 32
PEER_TOK_TILE = 1024
PEER_EXP_TILE = 1024
PEER_SUB = 2
VMEM_LIMIT = 48 * 1024 * 1024

_CAND_ROWS = tuple(PEER_TOPK // (a + 1) for a in range(PEER_TOPK))
_N_CAND = sum(_CAND_ROWS)
_CAND_PAD = -(-_N_CAND // SUBLANE) * SUBLANE


def _cparams(*sem):
    return pltpu.CompilerParams(dimension_semantics=sem, vmem_limit_bytes=VMEM_LIMIT)


def _dot(a, b):
    return jnp.dot(a.astype(BF16), b.astype(BF16), preferred_element_type=F32)


def _dot_nt(a, b):
    return lax.dot_general(a.astype(BF16), b.astype(BF16), (((1,), (1,)), ((), ())),
                           preferred_element_type=F32)


def _dot_tn(a, b):
    return lax.dot_general(a.astype(BF16), b.astype(BF16), (((0,), (0,)), ((), ())),
                           preferred_element_type=F32)


def _dot_hi(a, b):
    return jnp.dot(a, b, precision=HIGHEST, preferred_element_type=F32)


def _silu(x):
    return x * jax.nn.sigmoid(x)


def _softplus(x):
    return jnp.maximum(x, 0.0) + jnp.log1p(jnp.exp(-jnp.abs(x)))


def _modnorm(x, gain, scale, shift):
    y = x * lax.rsqrt(jnp.mean(x * x, axis=-1, keepdims=True) + EPS) * gain
    return y * (1.0 + scale) + shift


def _ada_kernel(c_ref, w_ref, b_ref, o_ref):
    o_ref[0] = _dot_hi(_silu(c_ref[...]), w_ref[0]) + b_ref[0]


def _ada_mod(c_all, w_ada, b_ada):
    depth, d, n = w_ada.shape
    rows = c_all.shape[0]
    tn = n // 4
    return pl.pallas_call(
        _ada_kernel,
        grid=(depth, n // tn),
        in_specs=[pl.BlockSpec((rows, d), lambda l, j: (0, 0)),
                  pl.BlockSpec((1, d, tn), lambda l, j: (l, 0, j)),
                  pl.BlockSpec((1, 1, tn), lambda l, j: (l, 0, j))],
        out_specs=pl.BlockSpec((1, rows, tn), lambda l, j: (l, 0, j)),
        out_shape=jax.ShapeDtypeStruct((depth, rows, n), F32),
        compiler_params=_cparams("parallel", "parallel"),
        name="ada_mod",
    )(c_all, w_ada, b_ada.reshape(depth, 1, n))


def _row_mod(mod_ref, first_row, n_rows, n_ctx):
    is_x = first_row + lax.broadcasted_iota(jnp.int32, (n_rows, 1), 0) >= n_ctx
    return lambda k: jnp.where(is_x, mod_ref[0, 1, k:k + 1, :], mod_ref[0, 0, k:k + 1, :])


def _inproj_kernel(*refs, has_ffn, col_splits, n_ctx):
    tile = refs[0].shape[1]
    first_row = pl.program_id(1) * tile
    if has_ffn:
        h_ref, f_ref, pmod_ref, mod_ref, gain_ref, w_ref, res_ref, *outs = refs
        x = h_ref[0] + _row_mod(pmod_ref, first_row, tile, n_ctx)(5) * f_ref[0]
        res_ref[0] = x
    else:
        h_ref, mod_ref, gain_ref, w_ref, *outs = refs
        x = h_ref[0]
    mod = _row_mod(mod_ref, first_row, tile, n_ctx)
    hb = _modnorm(x, gain_ref[...], mod(1), mod(0)).astype(BF16)
    off = 0
    for o_ref, width in zip(outs, col_splits):
        o_ref[0] = jnp.dot(hb, w_ref[:, off:off + width], preferred_element_type=F32)
        off += width


def _in_proj(h, ffn, pmod, mod, gain, w, col_splits, n_ctx):
    b, l, d = h.shape
    tile = PROJ_TILE
    tok = pl.BlockSpec((1, tile, d), lambda i, t: (i, t, 0))
    modspec = pl.BlockSpec((1, 2, 6, d), lambda i, t: (i, 0, 0, 0))
    has_ffn = ffn is not None
    in_specs = [tok] + ([tok, modspec] if has_ffn else []) + [
        modspec, pl.BlockSpec((1, d), lambda i, t: (0, 0)), pl.BlockSpec(w.shape, lambda i, t: (0, 0))]
    out_shape = [jax.ShapeDtypeStruct((b, l, wd), F32) for wd in col_splits]
    out_specs = [pl.BlockSpec((1, tile, wd), lambda i, t: (i, t, 0)) for wd in col_splits]
    if has_ffn:
        out_shape = [jax.ShapeDtypeStruct((b, l, d), F32)] + out_shape
        out_specs = [tok] + out_specs
    args = (h, ffn, pmod, mod, gain, w) if has_ffn else (h, mod, gain, w)
    return pl.pallas_call(
        functools.partial(_inproj_kernel, has_ffn=has_ffn, col_splits=col_splits, n_ctx=n_ctx),
        grid=(b, l // tile), in_specs=in_specs, out_specs=out_specs, out_shape=out_shape,
        compiler_params=_cparams("parallel", "parallel"), name="in_proj",
    )(*args)


def _pool_kernel(a_ref, wbd_ref, scale_ref, o_ref, pad_ref, *, segments):
    c = a_ref.shape[-1]
    grp = lax.broadcasted_iota(jnp.int32, (1, c), 1) // (c // len(POOL_WINDOWS))
    zeros = jnp.zeros((SUBLANE, c), F32)
    for off, n in segments:
        pad_ref[0:SUBLANE, :] = zeros
        pad_ref[SUBLANE:SUBLANE + n, :] = a_ref[0, off:off + n, :]
        pad_ref[SUBLANE + n:2 * SUBLANE + n, :] = zeros

        def chunk(ci, carry, off=off, n=n):
            r0 = pl.multiple_of(ci * ROW_CHUNK, ROW_CHUNK)
            win = pad_ref[pl.ds(r0, ROW_CHUNK + 2 * SUBLANE), :]
            sh = lambda s: win[SUBLANE + s:SUBLANE + s + ROW_CHUNK, :]
            a0 = sh(0)
            s2 = sh(-1) + a0
            s4 = s2 + sh(-2) + sh(1)
            s8 = s4 + sh(-4) + sh(-3) + sh(2) + sh(3)
            s16 = s8 + (sh(-8) + sh(-7) + sh(-6) + sh(-5)) + (sh(4) + sh(5) + sh(6) + sh(7))
            t = r0 + lax.broadcasted_iota(jnp.int32, (ROW_CHUNK, 1), 0)

            def mean(s, w):
                cnt = jnp.minimum(t + (w - w // 2), n) - jnp.maximum(t - w // 2, 0)
                return s / cnt.astype(F32)

            pooled = jnp.where(grp == 0, mean(s2, 2), jnp.where(grp == 1, mean(s4, 4),
                               jnp.where(grp == 2, mean(s8, 8), mean(s16, 16)))) - a0
            o_ref[0, pl.ds(off + r0, ROW_CHUNK), :] = _dot_hi(pooled, wbd_ref[...]) * scale_ref[...]
            return carry

        lax.fori_loop(0, n // ROW_CHUNK, chunk, 0)


def _pool(a, wbd, scale, n_ctx):
    b, l, c = a.shape
    segments = ((0, n_ctx), (n_ctx, l - n_ctx))
    return pl.pallas_call(
        functools.partial(_pool_kernel, segments=segments),
        grid=(b,),
        in_specs=[pl.BlockSpec((1, l, c), lambda i: (i, 0, 0)),
                  pl.BlockSpec((c, c), lambda i: (0, 0)),
                  pl.BlockSpec((1, c), lambda i: (0, 0))],
        out_specs=pl.BlockSpec((1, l, c), lambda i: (i, 0, 0)),
        out_shape=jax.ShapeDtypeStruct((b, l, c), F32),
        scratch_shapes=[pltpu.VMEM((l - n_ctx + 2 * SUBLANE, c), F32)],
        compiler_params=_cparams("parallel"), name="pool",
    )(a, wbd, scale)


def _gdn_kernel(q_ref, k_ref, v_ref, z_ref, ba_ref, cwq_ref, cwk_ref, cwv_ref, gp_ref, gain_ref, o_ref,
                pad_ref, yq_ref, yk_ref, yv_ref, bg_ref, of_ref, ob_ref, s_ref,
                qe_ref, mn_ref, oc_ref, nn_ref, egl_ref, *, n_ctx):
    l = q_ref.shape[1]
    n_x = l - n_ctx
    hd = HEAD_DIM
    lane = lax.broadcasted_iota(jnp.int32, (1, LANE), 1)
    lo = lane < hd
    zeros = jnp.zeros((SUBLANE, LANE), F32)
    x_off = 2 * SUBLANE + n_ctx

    def head_scale(ss):
        s_lo = jnp.sum(jnp.where(lo, ss, 0.0), axis=-1, keepdims=True)
        s_hi = jnp.sum(jnp.where(lo, 0.0, ss), axis=-1, keepdims=True)
        return jnp.where(lo, s_lo, s_hi)

    def conv(u_ref, cw_ref, y_ref, post):
        pad_ref[0:SUBLANE, :] = zeros
        pad_ref[SUBLANE:SUBLANE + n_ctx, :] = u_ref[0, 0:n_ctx, :]
        pad_ref[SUBLANE + n_ctx:x_off, :] = zeros
        pad_ref[x_off:x_off + n_x, :] = u_ref[0, n_ctx:l, :]
        pad_ref[x_off + n_x:x_off + n_x + SUBLANE, :] = zeros
        cw = cw_ref[...]
        for poff, yoff, n in ((SUBLANE, 0, n_ctx), (x_off, n_ctx, n_x)):
            def chunk(ci, carry, poff=poff, yoff=yoff):
                r0 = pl.multiple_of(ci * ROW_CHUNK, ROW_CHUNK)
                win = pad_ref[pl.ds(poff - SUBLANE + r0, ROW_CHUNK + 2 * SUBLANE), :]
                acc = win[SUBLANE - 2:SUBLANE - 2 + ROW_CHUNK, :] * cw[0:1, :]
                for j in range(1, CONV_K):
                    acc = acc + win[SUBLANE - 2 + j:SUBLANE - 2 + j + ROW_CHUNK, :] * cw[j:j + 1, :]
                y_ref[pl.ds(yoff + r0, ROW_CHUNK), :] = post(_silu(acc))
                return carry
            lax.fori_loop(0, n // ROW_CHUNK, chunk, 0)

    l2 = lambda y: y * lax.rsqrt(head_scale(y * y) + EPS)
    conv(q_ref, cwq_ref, yq_ref, lambda y: l2(y) * (hd ** -0.5))
    conv(k_ref, cwk_ref, yk_ref, l2)
    conv(v_ref, cwv_ref, yv_ref, lambda y: y)

    ba = ba_ref[0]
    g = -jnp.exp(gp_ref[0, 0:1, :]) * _softplus(ba + gp_ref[0, 1:2, :])
    bg_ref[...] = jnp.where(lane < 4, jax.nn.sigmoid(ba), g)

    ck = GDN_CHUNK
    rr = lax.broadcasted_iota(jnp.int32, (ck, ck), 0)
    cc = lax.broadcasted_iota(jnp.int32, (ck, ck), 1)
    eye = (rr == cc).astype(F32)
    tril = (rr >= cc).astype(F32)
    n_chunks = l // ck
    nc_ctx = n_ctx // ck

    def phase1(ci, carry):
        probs = []
        for gi in range(GDN_GROUP):
            c = ci * GDN_GROUP + gi
            r0 = pl.multiple_of(c * ck, ck)
            qc = yq_ref[pl.ds(r0, ck), :]
            kc = yk_ref[pl.ds(r0, ck), :]
            vc = yv_ref[pl.ds(r0, ck), :]
            bgc = bg_ref[pl.ds(r0, ck), :]
            gcf = _dot_hi(tril, bgc)
            gcb = gcf[ck - 1:ck, :] - gcf + bgc
            gct = (gcf.T, gcb.T)
            for j in range(2):
                qh = qc[:, j * hd:(j + 1) * hd]
                kh = kc[:, j * hd:(j + 1) * hd]
                vh = vc[:, j * hd:(j + 1) * hd]
                kk = _dot_nt(kh, kh)
                qk = _dot_nt(qh, kh)
                for d in range(2):
                    mask = (rr >= cc) if d == 0 else (rr <= cc)
                    smask = (rr > cc) if d == 0 else (rr < cc)
                    lg, lb = 4 + 2 * d + j, 2 * d + j
                    gcol = jnp.broadcast_to((gcf, gcb)[d][:, lg:lg + 1], (ck, hd))
                    grow = gct[d][lg:lg + 1, :]
                    bcol = jnp.broadcast_to(bgc[:, lb:lb + 1], (ck, hd))
                    decay = jnp.where(mask, jnp.exp(jnp.where(mask, gcol - grow, 0.0)), 0.0)
                    xp = jnp.where(smask, -(bcol * kk * decay), 0.0)
                    eg = jnp.exp(gcol)
                    last = ck - 1 if d == 0 else 0
                    glast = gcol[last:last + 1, :]
                    probs.append(dict(
                        idx=c * 4 + 2 * d + j, xp=xp, inv=eye + xp, attn=qk * decay, qd=qh * eg,
                        rhs=jnp.concatenate([vh * bcol, kh * bcol * eg], axis=1),
                        kd=kh * jnp.exp(glast - gcol), egl=jnp.exp(glast)))
        for _ in range(5):
            for p in probs:
                p["xp"] = _dot(p["xp"], p["xp"])
            for p in probs:
                p["inv"] = p["inv"] + _dot(p["inv"], p["xp"])
        for p in probs:
            p["sol"] = _dot(p["inv"], p["rhs"])
        for p in probs:
            p["as"] = _dot(p["attn"], p["sol"])
        for p in probs:
            p["ks"] = _dot_tn(p["kd"], p["sol"])
        for p in probs:
            i = p["idx"]
            qe_ref[i] = (p["qd"] - p["as"][:, hd:]).astype(BF16)
            oc_ref[i] = p["as"][:, :hd]
            mn_ref[i] = p["ks"][:, hd:].astype(BF16)
            nn_ref[i] = p["ks"][:, :hd]
            egl_ref[i] = jnp.broadcast_to(p["egl"], (ck, hd))
        return carry

    lax.fori_loop(0, n_chunks // GDN_GROUP, phase1, 0)

    s_ref[...] = jnp.zeros(s_ref.shape, F32)

    def phase2(i, carry):
        cb = jnp.where(i < nc_ctx, nc_ctx - 1 - i, n_chunks - 1 - (i - nc_ctx))
        work = [(d, j, cidx * 4 + 2 * d + j) for d, cidx in ((0, i), (1, cb)) for j in range(2)]
        states = [s_ref[2 * d + j] for d, j, _ in work]
        outs = [_dot(qe_ref[idx], s) + oc_ref[idx] for (_, _, idx), s in zip(work, states)]
        upd = [_dot(mn_ref[idx], s) for (_, _, idx), s in zip(work, states)]
        for (d, j, idx), s, m in zip(work, states, upd):
            s_ref[2 * d + j] = egl_ref[idx] * s - m + nn_ref[idx]
        of_ref[pl.ds(pl.multiple_of(i * ck, ck), ck), :] = jnp.concatenate(outs[0:2], axis=1)
        ob_ref[pl.ds(pl.multiple_of(cb * ck, ck), ck), :] = jnp.concatenate(outs[2:4], axis=1)
        return carry

    lax.fori_loop(0, n_chunks, phase2, 0)

    def finish(ci, carry):
        r0 = pl.multiple_of(ci * ROW_CHUNK, ROW_CHUNK)
        o = of_ref[pl.ds(r0, ROW_CHUNK), :] + ob_ref[pl.ds(r0, ROW_CHUNK), :]
        o = o * lax.rsqrt(head_scale(o * o) * (1.0 / hd) + EPS) * gain_ref[...]
        o_ref[0, pl.ds(r0, ROW_CHUNK), :] = o * _silu(z_ref[0, pl.ds(r0, ROW_CHUNK), :])
        return carry

    lax.fori_loop(0, l // ROW_CHUNK, finish, 0)


def _gdn(qkv, z, ba, conv_w, gparams, gain2, n_ctx):
    b, l, w3 = qkv.shape
    npair = w3 // (3 * LANE)
    seq = lambda off: pl.BlockSpec((1, l, LANE), lambda i, p: (i, 0, p + off))
    cw = lambda off: pl.BlockSpec((CONV_K, LANE), lambda i, p: (0, p + off))
    return pl.pallas_call(
        functools.partial(_gdn_kernel, n_ctx=n_ctx),
        grid=(b, npair),
        in_specs=[seq(0), seq(npair), seq(2 * npair), seq(0), seq(0),
                  cw(0), cw(npair), cw(2 * npair),
                  pl.BlockSpec((1, SUBLANE, LANE), lambda i, p: (p, 0, 0)),
                  pl.BlockSpec((1, LANE), lambda i, p: (0, 0))],
        out_specs=seq(0),
        out_shape=jax.ShapeDtypeStruct((b, l, npair * LANE), F32),
        scratch_shapes=[pltpu.VMEM((l + 3 * SUBLANE, LANE), F32)] + [pltpu.VMEM((l, LANE), F32)] * 6
        + [pltpu.VMEM((4, HEAD_DIM, HEAD_DIM), F32)]
        + [pltpu.VMEM((4 * l // GDN_CHUNK, GDN_CHUNK, HEAD_DIM), dt) for dt in (BF16, BF16, F32, F32, F32)],
        compiler_params=_cparams("parallel", "parallel"), name="gdn",
    )(qkv, qkv, qkv, z, ba, conv_w, conv_w, conv_w, gparams, gain2)


def _rope(x, cos, sin):
    w = x.shape[-1]
    reps = w // cos.shape[-1]
    if reps > 1:
        cos = jnp.concatenate([cos] * reps, axis=1)
        sin = jnp.concatenate([sin] * reps, axis=1)
    first = (lax.broadcasted_iota(jnp.int32, (1, w), 1) % (2 * ROPE_PAIRS)) < ROPE_PAIRS
    partner = jnp.where(first, pltpu.roll(x, w - ROPE_PAIRS, 1), pltpu.roll(x, ROPE_PAIRS, 1))
    return x * cos + partner * sin


def _attend(q, k, v, valid, sink_col):
    s = _dot_nt(q, k)
    if valid is not None:
        s = jnp.where(valid, s, NEG_INF)
    m = jnp.maximum(jnp.max(s, axis=-1, keepdims=True), sink_col)
    p = jnp.exp(s - m)
    den = jnp.sum(p, axis=-1, keepdims=True) + jnp.exp(sink_col - m)
    return _dot(p, v) / den


def _gqa(q, k_all, v_all, valid, sink_ref):
    nq = q.shape[0]
    hd = HEAD_DIM
    group = q.shape[1] // hd // KV_HEADS
    outs = []
    for kvh in range(KV_HEADS):
        heads = [kvh * group + g for g in range(group)]
        qg = jnp.concatenate([q[:, h * hd:(h + 1) * hd] for h in heads], axis=0)
        sink_col = jnp.concatenate(
            [jnp.broadcast_to(sink_ref[0:1, h:h + 1], (nq, 1)) for h in heads], axis=0)
        o = _attend(qg, k_all[:, kvh * hd:(kvh + 1) * hd], v_all[:, kvh * hd:(kvh + 1) * hd],
                    valid, sink_col)
        outs += [o[g * nq:(g + 1) * nq, :] for g in range(group)]
    return jnp.concatenate(outs, axis=1)


def _attn_kernel(q_ref, *refs, seq):
    nkb = ATTN_QBLOCKS + 2
    k_refs, v_refs = refs[0:nkb], refs[nkb:2 * nkb]
    kc_ref, vc_ref = refs[2 * nkb:2 * nkb + 2]
    cos_refs = refs[2 * nkb + 2:3 * nkb + 2]
    sin_refs = refs[3 * nkb + 2:4 * nkb + 2]
    sink_ref, o_ref = refs[4 * nkb + 2:]
    blk = ATTN_BLOCK
    n_ctx = kc_ref.shape[1]
    group = q_ref.shape[2] // HEAD_DIM // KV_HEADS
    k_rot = [_rope(k_refs[j][0], cos_refs[j][...], sin_refs[j][...]) for j in range(nkb)]
    nk = 3 * blk + n_ctx
    qo = lax.broadcasted_iota(jnp.int32, (group * blk, 1), 0) % blk
    ko = lax.broadcasted_iota(jnp.int32, (1, nk), 1)
    in_window = jnp.abs(ko - blk - qo) <= WINDOW
    for s in range(ATTN_QBLOCKS):
        i = pl.program_id(1) * ATTN_QBLOCKS + s
        rows = slice(s * blk, (s + 1) * blk)
        q = _rope(q_ref[0, rows, :], cos_refs[s + 1][...], sin_refs[s + 1][...]) * (HEAD_DIM ** -0.5)
        k_all = jnp.concatenate(k_rot[s:s + 3] + [kc_ref[0]], axis=0)
        v_all = jnp.concatenate([v_refs[j][0] for j in range(s, s + 3)] + [vc_ref[0]], axis=0)
        kpos = (i - 1) * blk + ko
        valid = (in_window & (kpos >= 0) & (kpos < seq)) | (ko >= 3 * blk)
        o_ref[0, rows, :] = _gqa(q, k_all, v_all, valid, sink_ref)


def _attn_x(aq, ak, av, cos_t, sin_t, sink_row, n_ctx):
    b, l, wq = aq.shape
    wk = ak.shape[-1]
    blk = ATTN_BLOCK
    seq = l - n_ctx
    nb = seq // blk
    cb = n_ctx // blk
    nq = ATTN_QBLOCKS
    assert nb % nq == 0 and cb % nq == 0
    blocks = [lambda t, j=j: jnp.clip(t * nq + j - 1, 0, nb - 1) for j in range(nq + 2)]
    kv = [pl.BlockSpec((1, blk, wk), lambda i, t, f=f: (i, f(t) + cb, 0)) for f in blocks]
    tab = [pl.BlockSpec((blk, LANE), lambda i, t, f=f: (f(t), 0)) for f in blocks]
    ctx = pl.BlockSpec((1, n_ctx, wk), lambda i, t: (i, 0, 0))
    qspec = pl.BlockSpec((1, nq * blk, wq), lambda i, t: (i, t + cb // nq, 0))
    return pl.pallas_call(
        functools.partial(_attn_kernel, seq=seq),
        grid=(b, nb // nq),
        in_specs=[qspec] + kv + kv + [ctx, ctx] + tab + tab + [pl.BlockSpec((1, LANE), lambda i, t: (0, 0))],
        out_specs=pl.BlockSpec((1, nq * blk, wq), lambda i, t: (i, t, 0)),
        out_shape=jax.ShapeDtypeStruct((b, seq, wq), F32),
        compiler_params=_cparams("parallel", "parallel"), name="attn_x",
    )(aq, *([ak] * (nq + 2)), *([av] * (nq + 2)), ak, av, *([cos_t] * (nq + 2)), *([sin_t] * (nq + 2)),
      sink_row)


def _attn_ctx_kernel(q_ref, k_ref, v_ref, sink_ref, o_ref):
    o_ref[0] = _gqa(q_ref[0] * (HEAD_DIM ** -0.5), k_ref[0], v_ref[0], None, sink_ref)


def _attn_ctx(aq, ak, av, sink_row, n_ctx):
    b, _, wq = aq.shape
    wk = ak.shape[-1]
    spec = lambda w: pl.BlockSpec((1, n_ctx, w), lambda i: (i, 0, 0))
    return pl.pallas_call(
        _attn_ctx_kernel, grid=(b,),
        in_specs=[spec(wq), spec(wk), spec(wk), pl.BlockSpec((1, LANE), lambda i: (0, 0))],
        out_specs=spec(wq), out_shape=jax.ShapeDtypeStruct((b, n_ctx, wq), F32),
        compiler_params=_cparams("parallel"), name="attn_ctx",
    )(aq, ak, av, sink_row)


def _outproj_kernel(x_ref, ya_ref, yb_ref, yc_ref, w_ref, mod_ref, gain_ref, xo_ref, hx_ref, *, row_off, n_ctx):
    wa, wb = ya_ref.shape[-1], yb_ref.shape[-1]
    tile = x_ref.shape[1]
    acc = (jnp.dot(ya_ref[0].astype(BF16), w_ref[0:wa, :], preferred_element_type=F32)
           + jnp.dot(yb_ref[0].astype(BF16), w_ref[wa:wa + wb, :], preferred_element_type=F32)
           + jnp.dot(yc_ref[0].astype(BF16), w_ref[wa + wb:, :], preferred_element_type=F32))
    mod = _row_mod(mod_ref, row_off + pl.program_id(1) * tile, tile, n_ctx)
    x = x_ref[0] + mod(2) * acc
    xo_ref[0] = x
    hx_ref[0] = _modnorm(x, gain_ref[...], mod(4), mod(3)).astype(BF16)


def _out_proj(h, ya, yb, yc, w, mod, gain, n_ctx, skip_ctx):
    b, l, d = h.shape
    tile = TOK_TILE if skip_ctx else PROJ_TILE
    off = n_ctx // tile if skip_ctx else 0
    nt = l // tile - off
    full = lambda wd: pl.BlockSpec((1, tile, wd), lambda i, t: (i, t + off, 0))
    own = lambda wd: pl.BlockSpec((1, tile, wd), lambda i, t: (i, t, 0))
    return pl.pallas_call(
        functools.partial(_outproj_kernel, row_off=off * tile, n_ctx=n_ctx), grid=(b, nt),
        in_specs=[full(d), full(ya.shape[-1]), full(yb.shape[-1]), own(yc.shape[-1]),
                  pl.BlockSpec(w.shape, lambda i, t: (0, 0)),
                  pl.BlockSpec((1, 2, 6, d), lambda i, t: (i, 0, 0, 0)),
                  pl.BlockSpec((1, d), lambda i, t: (0, 0))],
        out_specs=[own(d), own(d)],
        out_shape=[jax.ShapeDtypeStruct((b, nt * tile, d), F32),
                   jax.ShapeDtypeStruct((b, nt * tile, d), BF16)],
        compiler_params=_cparams("parallel", "parallel"), name="out_proj",
    )(h, ya, yb, yc, w, mod, gain)


def _sort_pairs(n):
    pairs, p = [], 1
    while p < n:
        k = p
        while k >= 1:
            for j in range(k % p, n - k, 2 * k):
                for i in range(min(k, n - j - k)):
                    if (i + j) // (2 * p) == (i + j + k) // (2 * p):
                        pairs.append((i + j, i + j + k))
            k //= 2
        p *= 2
    return pairs


def _topk_rows(v):
    n = len(v)
    v = list(v)

    def cmpx(i, j):
        v[i], v[j] = jnp.maximum(v[i], v[j]), jnp.minimum(v[i], v[j])

    for i, j in _sort_pairs(n):
        cmpx(i, j)
    shift = SUBLANE // 2
    while shift >= 1:
        partner = [pltpu.roll(x, shift, 0) for x in v]
        v = [jnp.maximum(v[j], partner[n - 1 - j]) for j in range(n)]
        stride = n // 2
        while stride >= 1:
            for i in range(n):
                if not i & stride:
                    cmpx(i, i + stride)
            stride //= 2
        shift //= 2
    return v


def _route_kernel(ht_ref, wq_ref, keys_ref, r2_ref, e2_ref, n1_ref, e1_ref, qt_ref, top_ref, cand_ref):
    tm = ht_ref.shape[1]
    k = PEER_TOPK
    nk = N_KEYS
    neg = jnp.float32(-jnp.inf)
    qt_ref[...] = _dot(wq_ref[...], ht_ref[...]).astype(BF16)
    cand_ref[_N_CAND:_CAND_PAD, :] = jnp.full((_CAND_PAD - _N_CAND, tm), neg, F32)
    for h in range(PEER_HEADS):
        st = []
        for p in range(2):
            hp = 2 * h + p
            s = jnp.dot(keys_ref[hp], qt_ref[hp * nk:(hp + 1) * nk, :], preferred_element_type=F32)
            st.append(s)
            top = _topk_rows([s[g * SUBLANE:(g + 1) * SUBLANE, :] for g in range(nk // SUBLANE)])
            for r in range(k):
                top_ref[p * k + r:p * k + r + 1, :] = top[r][0:1, :]
        v1 = top_ref[0:k, :]
        v2 = top_ref[k:2 * k, :]
        rank2 = jnp.full((nk, tm), float(k), F32)
        for r in range(k):
            rank2 = jnp.where(st[1] == v2[r:r + 1, :], float(r), rank2)
        row = 0
        for a, nb in enumerate(_CAND_ROWS):
            cand_ref[row:row + nb, :] = v1[a:a + 1, :] + v2[0:nb, :]
            row += nb
        cur = cand_ref[...]
        cmax = thr = zsum = None
        for r in range(k):
            m = jnp.max(cur, axis=0, keepdims=True)
            if r == 0:
                cmax, zsum = m, jnp.ones_like(m)
            else:
                zsum = zsum + jnp.exp(m - cmax)
            thr = m
            cur = jnp.where(cur == m, neg, cur)
        n1 = jnp.zeros((nk, tm), F32)
        row = 0
        for a, nb in enumerate(_CAND_ROWS):
            sel = cand_ref[row:row + nb, :] >= thr
            n_a = jnp.sum(sel.astype(F32), axis=0, keepdims=True)
            n1 = jnp.where(st[0] == v1[a:a + 1, :], n_a, n1)
            row += nb
        r2_ref[h] = rank2.T
        e2_ref[h] = jnp.exp(st[1] - v2[0:1, :]).T
        n1_ref[h] = n1.T
        e1_ref[h] = (jnp.exp(st[0] - v1[0:1, :]) / zsum).T


def _peer_route(hxt, wq_t, keys):
    d, t = hxt.shape
    nq = wq_t.shape[0]
    tm = PEER_ROUTE_TILE
    rows = PEER_HEADS * N_KEYS
    return pl.pallas_call(
        _route_kernel, grid=(t // tm,),
        in_specs=[pl.BlockSpec((d, tm), lambda i: (0, i)),
                  pl.BlockSpec(wq_t.shape, lambda i: (0, 0)),
                  pl.BlockSpec(keys.shape, lambda i: (0, 0, 0))],
        out_specs=[pl.BlockSpec((PEER_HEADS, tm, N_KEYS), lambda i: (0, i, 0))] * 4,
        out_shape=[jax.ShapeDtypeStruct((PEER_HEADS, t, N_KEYS), F32)] * 4,
        scratch_shapes=[pltpu.VMEM((nq, tm), BF16), pltpu.VMEM((2 * PEER_TOPK, tm), F32),
                        pltpu.VMEM((_CAND_PAD, tm), F32)],
        compiler_params=_cparams("parallel"), name="peer_route",
    )(hxt, wq_t, keys)


def _gates_kernel(r2t_ref, e2t_ref, n1t_ref, e1t_ref, w_ref):
    tb = w_ref.shape[1]
    nk = N_KEYS
    k = PEER_TOPK
    rank = lax.broadcasted_iota(jnp.int32, (k, nk), 0).astype(F32)

    def tok(i, carry):
        for j in range(PEER_GATE_UNROLL):
            t = i * PEER_GATE_UNROLL + j
            a, b = [], []
            for h in range(PEER_HEADS):
                r2, e2, n1, e1 = (ref[h, pl.ds(t, k, stride=0), :]
                                  for ref in (r2t_ref, e2t_ref, n1t_ref, e1t_ref))
                a.append(jnp.where(n1 > rank, e1, 0.0))
                b.append(jnp.where(r2 == rank, e2, 0.0))
            w = _dot_tn(jnp.concatenate(a, axis=0), jnp.concatenate(b, axis=0))
            for g in range(nk // SUBLANE):
                w_ref[g, t] = w[g * SUBLANE:(g + 1) * SUBLANE, :]
        return carry

    lax.fori_loop(0, tb // PEER_GATE_UNROLL, tok, 0)


def _peer_gates(route):
    t = route[0].shape[1]
    tb = PEER_GATE_TILE
    return pl.pallas_call(
        _gates_kernel, grid=(t // tb,),
        in_specs=[pl.BlockSpec((PEER_HEADS, tb, N_KEYS), lambda i: (0, i, 0))] * 4,
        out_specs=pl.BlockSpec((N_KEYS // SUBLANE, tb, SUBLANE, N_KEYS), lambda i: (0, i, 0, 0)),
        out_shape=jax.ShapeDtypeStruct((N_KEYS // SUBLANE, t, SUBLANE, N_KEYS), F32),
        compiler_params=_cparams("parallel"), name="peer_gates",
    )(*route)


def _peer_kernel(h_ref, w_ref, ut_ref, v_ref, o_ref, acc_ref, g0_ref, g1_ref):
    e = pl.program_id(1)
    n_blk = pl.num_programs(1) - 1

    @pl.when(e == 0)
    def _():
        acc_ref[...] = jnp.zeros(acc_ref.shape, F32)
        g1_ref[...] = jnp.zeros(g1_ref.shape, BF16)

    @pl.when(e % 2 == 0)
    def _():
        _peer_step(h_ref, w_ref, ut_ref, v_ref, acc_ref, g0_ref, g1_ref)

    @pl.when(e % 2 == 1)
    def _():
        _peer_step(h_ref, w_ref, ut_ref, v_ref, acc_ref, g1_ref, g0_ref)

    @pl.when(e == n_blk)
    def _():
        o_ref[...] = acc_ref[...]


def _peer_step(h_ref, w_ref, ut_ref, v_ref, acc_ref, g_w_ref, g_r_ref):
    sr = h_ref.shape[0] // PEER_SUB

    def sub(k, carry):
        rows = pl.ds(pl.multiple_of(k * sr, sr), sr)
        acc_ref[rows, :] += jnp.dot(g_r_ref[rows, :], v_ref[...].astype(BF16), preferred_element_type=F32)
        act = jnp.dot(h_ref[rows, :], ut_ref[0], preferred_element_type=F32)
        act = 0.5 * act * (1.0 + lax.erf(act * 0.7071067811865476))
        for j in range(SUBLANE):
            ln = slice(j * N_KEYS, (j + 1) * N_KEYS)
            wj = w_ref[pl.ds(k * (sr * SUBLANE) + j, sr, stride=SUBLANE), :]
            g_w_ref[rows, ln] = (wj * act[:, ln]).astype(BF16)
        return carry

    lax.fori_loop(0, PEER_SUB, sub, 0)


def _peer_dense(hx, w, ut, v_all, layer):
    t, d = hx.shape
    n_blk, _, eb = ut.shape
    tm = PEER_TOK_TILE
    last = n_blk - 1
    return pl.pallas_call(
        _peer_kernel, grid=(t // tm, n_blk + 1),
        in_specs=[pl.BlockSpec((tm, d), lambda i, e: (i, 0)),
                  pl.BlockSpec((tm * SUBLANE, N_KEYS), lambda i, e: (jnp.minimum(e, last) * (t // tm) + i, 0)),
                  pl.BlockSpec((1, d, eb), lambda i, e: (jnp.minimum(e, last), 0, 0)),
                  pl.BlockSpec((None, eb, d), lambda i, e: (layer, jnp.maximum(e - 1, 0), 0))],
        out_specs=pl.BlockSpec((tm, d), lambda i, e: (i, 0)),
        out_shape=jax.ShapeDtypeStruct((t, d), F32),
        scratch_shapes=[pltpu.VMEM((tm, d), F32), pltpu.VMEM((tm, eb), BF16), pltpu.VMEM((tm, eb), BF16)],
        compiler_params=_cparams("parallel", "arbitrary"), name="peer_dense",
    )(hx, w, ut, v_all)


def _final_kernel(x_ref, f_ref, mod_ref, gain_ref, o_ref):
    x = x_ref[0] + mod_ref[0, 0, 5:6, :] * f_ref[0]
    o_ref[0] = x * lax.rsqrt(jnp.mean(x * x, axis=-1, keepdims=True) + EPS) * gain_ref[...]


def _final(x, ffn, mod, gain):
    b, s, d = x.shape
    tok = pl.BlockSpec((1, TOK_TILE, d), lambda i, t: (i, t, 0))
    return pl.pallas_call(
        _final_kernel, grid=(b, s // TOK_TILE),
        in_specs=[tok, tok, pl.BlockSpec((1, 1, 6, d), lambda i, t: (i, 1, 0, 0)),
                  pl.BlockSpec((1, d), lambda i, t: (0, 0))],
        out_specs=tok, out_shape=jax.ShapeDtypeStruct((b, s, d), F32),
        compiler_params=_cparams("parallel", "parallel"), name="final_norm",
    )(x, ffn, mod, gain)


def _rope_tables(seq):
    pos = jnp.arange(seq)
    rc = jnp.stack([pos // GRID_W, pos % GRID_W], axis=-1).astype(F32)
    inv = jnp.power(ROPE_BASE, -jnp.arange(ROPE_PAIRS, dtype=F32) / ROPE_PAIRS)
    ang = rc[:, :, None] * inv
    cos = jnp.concatenate([jnp.cos(ang)] * 2, axis=-1).reshape(seq, HEAD_DIM)
    sin = jnp.concatenate([-jnp.sin(ang), jnp.sin(ang)], axis=-1).reshape(seq, HEAD_DIM)
    return jnp.concatenate([cos] * 2, axis=1), jnp.concatenate([sin] * 2, axis=1)


def _pair_lanes(cols, n_heads):
    rows = cols.shape[0]
    c = cols.reshape(rows, 4, n_heads // 2, 2).transpose(0, 2, 1, 3).reshape(rows, n_heads // 2, 8)
    return jnp.pad(c, ((0, 0), (0, 0), (0, LANE - 8))).reshape(rows, n_heads // 2 * LANE)


def kernel(x, c, ctx, c_ctx, w_ada, b_ada, norm1, norm2, w_in, conv_w, a_log, dt_bias, gdn_norm, w_pool,
           pool_scale, sink, w_out, peer_wq, peer_keys, peer_u, peer_v, norm_f):
    b, seq, d = x.shape
    n_ctx = ctx.shape[1]
    depth = w_ada.shape[0]
    pool_w = d // 4
    gdn_w = 3 * d // 8
    gdn_heads = gdn_w // HEAD_DIM
    attn_w = d - pool_w - gdn_w
    kv_w = KV_HEADS * HEAD_DIM
    assert n_ctx % TOK_TILE == 0 and seq % TOK_TILE == 0 and gdn_heads % 2 == 0
    assert (n_ctx + seq) % PROJ_TILE == 0 and PEER_EXP_TILE == SUBLANE * N_KEYS
    assert (b * (n_ctx + seq)) % PEER_TOK_TILE == 0 and (b * seq) % PEER_TOK_TILE == 0

    rows = -(-(b + 1) // SUBLANE) * SUBLANE
    c_all = jnp.zeros((rows, d), F32).at[:b].set(c).at[b].set(c_ctx)
    mod = _ada_mod(c_all, w_ada, b_ada)
    mod_x = mod[:, :b].reshape(depth, b, 1, 6, d)
    mod_c = jnp.broadcast_to(mod[:, b].reshape(depth, 1, 1, 6, d), (depth, b, 1, 6, d))
    mod = jnp.concatenate([mod_c, mod_x], axis=2)

    cos_t, sin_t = _rope_tables(seq)
    splits = (pool_w, 3 * gdn_w, gdn_w, gdn_heads // 2 * LANE, attn_w, kv_w, kv_w)
    o_ba = pool_w + 4 * gdn_w
    n_ba = 4 * gdn_heads

    h = jnp.concatenate([ctx, x], axis=1)
    ffn = None
    for l in range(depth):
        last = l == depth - 1
        w_l = w_in[l]
        w_big = jnp.concatenate([w_l[:, :o_ba], _pair_lanes(w_l[:, o_ba:o_ba + n_ba], gdn_heads),
                                 w_l[:, o_ba + n_ba:]], axis=1).astype(BF16)
        pmod = mod[l - 1] if l > 0 else None
        outs = _in_proj(h, ffn, pmod, mod[l], norm1[l][None], w_big, splits, n_ctx)
        if l > 0:
            h, *outs = outs
        pa, qkv, z, ba, aq, ak, av = outs

        wbd = jax.scipy.linalg.block_diag(*[w_pool[l, g] for g in range(len(POOL_WINDOWS))])
        ya = _pool(pa, wbd, pool_scale[l][None], n_ctx)

        gparams = jnp.stack([_pair_lanes(jnp.concatenate([jnp.zeros_like(a_log[l]), a_log[l]]).reshape(1, -1),
                                         gdn_heads),
                             _pair_lanes(jnp.concatenate([jnp.zeros_like(dt_bias[l]), dt_bias[l]]).reshape(1, -1),
                                         gdn_heads)], axis=1)
        gparams = gparams.reshape(2, gdn_heads // 2, LANE).transpose(1, 0, 2)
        gparams = jnp.pad(gparams, ((0, 0), (0, SUBLANE - 2), (0, 0)))
        yb = _gdn(qkv, z, ba, conv_w[l], gparams, jnp.concatenate([gdn_norm[l]] * 2)[None], n_ctx)

        sink_row = jnp.pad(sink[l], (0, LANE - sink.shape[1]))[None]
        yc = _attn_x(aq, ak, av, cos_t, sin_t, sink_row, n_ctx)
        if not last:
            yc = jnp.concatenate([_attn_ctx(aq, ak, av, sink_row, n_ctx), yc], axis=1)

        x_new, hx = _out_proj(h, ya, yb, yc, w_out[l].astype(BF16), mod[l], norm2[l][None], n_ctx,
                              skip_ctx=last)
        t = hx.shape[0] * hx.shape[1]
        hx = hx.reshape(t, d)
        keys = peer_keys[l].reshape(2 * PEER_HEADS, N_KEYS, -1).astype(BF16)
        route = _peer_route(hx.T, peer_wq[l].T.astype(BF16), keys)
        gates = _peer_gates(route).reshape(-1, N_KEYS)
        ut = peer_u[l].reshape(-1, PEER_EXP_TILE, d).transpose(0, 2, 1).astype(BF16)
        ffn = _peer_dense(hx, gates, ut, peer_v, l)
        ffn = ffn.reshape(x_new.shape)
        h = x_new
    return _final(h, ffn, mod[depth - 1], norm_f[None])
```

```python
import functools

import jax
import jax.numpy as jnp
from jax import lax
from jax.experimental import pallas as pl
from jax.experimental.pallas import tpu as pltpu

F32, BF16 = jnp.float32, jnp.bfloat16
HIGHEST = lax.Precision.HIGHEST

EPS = 1e-6
NEG_INF = -1e30
HEAD_DIM = 64
GRID_W = 64
POOL_WINDOWS = (2, 4, 8, 16)
GDN_CHUNK = 64
CONV_K = 5
KV_HEADS = 2
WINDOW = 128
ATTN_BLOCK = 128
ROPE_BASE = 10000.0
ROPE_PAIRS = HEAD_DIM // 4
PEER_HEADS = 8
N_KEYS = 128
PEER_TOPK = 16

LANE = 128
SUBLANE = 8
TOK_TILE = 256
PROJ_TILE = 768
ROW_CHUNK = 256
ATTN_QBLOCKS = 2
GDN_GROUP = 4
PEER_ROUTE_TILE = 512
PEER_GATE_TILE = 128
PEER_GATE_UNROLL = 32
PEER_TOK_TILE = 1024
PEER_EXP_TILE = 1024
PEER_SUB = 2
VMEM_LIMIT = 48 * 1024 * 1024

_CAND_ROWS = tuple(PEER_TOPK // (a + 1) for a in range(PEER_TOPK))
_N_CAND = sum(_CAND_ROWS)
_CAND_PAD = -(-_N_CAND // SUBLANE) * SUBLANE


def _cparams(*sem):
    return pltpu.CompilerParams(dimension_semantics=sem, vmem_limit_bytes=VMEM_LIMIT)


def _dot(a, b):
    return jnp.dot(a.astype(BF16), b.astype(BF16), preferred_element_type=F32)


def _dot_nt(a, b):
    return lax.dot_general(a.astype(BF16), b.astype(BF16), (((1,), (1,)), ((), ())),
                           preferred_element_type=F32)


def _dot_tn(a, b):
    return lax.dot_general(a.astype(BF16), b.astype(BF16), (((0,), (0,)), ((), ())),
                           preferred_element_type=F32)


def _dot_hi(a, b):
    return jnp.dot(a, b, precision=HIGHEST, preferred_element_type=F32)


def _silu(x):
    return x * jax.nn.sigmoid(x)


def _softplus(x):
    return jnp.maximum(x, 0.0) + jnp.log1p(jnp.exp(-jnp.abs(x)))


def _modnorm(x, gain, scale, shift):
    y = x * lax.rsqrt(jnp.mean(x * x, axis=-1, keepdims=True) + EPS) * gain
    return y * (1.0 + scale) + shift


def _ada_kernel(c_ref, w_ref, b_ref, o_ref):
    o_ref[0] = _dot_hi(_silu(c_ref[...]), w_ref[0]) + b_ref[0]


def _ada_mod(c_all, w_ada, b_ada):
    depth, d, n = w_ada.shape
    rows = c_all.shape[0]
    tn = n // 4
    return pl.pallas_call(
        _ada_kernel,
        grid=(depth, n // tn),
        in_specs=[pl.BlockSpec((rows, d), lambda l, j: (0, 0)),
                  pl.BlockSpec((1, d, tn), lambda l, j: (l, 0, j)),
                  pl.BlockSpec((1, 1, tn), lambda l, j: (l, 0, j))],
        out_specs=pl.BlockSpec((1, rows, tn), lambda l, j: (l, 0, j)),
        out_shape=jax.ShapeDtypeStruct((depth, rows, n), F32),
        compiler_params=_cparams("parallel", "parallel"),
        name="ada_mod",
    )(c_all, w_ada, b_ada.reshape(depth, 1, n))


def _row_mod(mod_ref, first_row, n_rows, n_ctx):
    is_x = first_row + lax.broadcasted_iota(jnp.int32, (n_rows, 1), 0) >= n_ctx
    return lambda k: jnp.where(is_x, mod_ref[0, 1, k:k + 1, :], mod_ref[0, 0, k:k + 1, :])


def _inproj_kernel(*refs, has_ffn, col_splits, n_ctx):
    tile = refs[0].shape[1]
    first_row = pl.program_id(1) * tile
    if has_ffn:
        h_ref, f_ref, pmod_ref, mod_ref, gain_ref, w_ref, res_ref, *outs = refs
        x = h_ref[0] + _row_mod(pmod_ref, first_row, tile, n_ctx)(5) * f_ref[0]
        res_ref[0] = x
    else:
        h_ref, mod_ref, gain_ref, w_ref, *outs = refs
        x = h_ref[0]
    mod = _row_mod(mod_ref, first_row, tile, n_ctx)
    hb = _modnorm(x, gain_ref[...], mod(1), mod(0)).astype(BF16)
    off = 0
    for o_ref, width in zip(outs, col_splits):
        o_ref[0] = jnp.dot(hb, w_ref[:, off:off + width], preferred_element_type=F32)
        off += width


def _in_proj(h, ffn, pmod, mod, gain, w, col_splits, n_ctx):
    b, l, d = h.shape
    tile = PROJ_TILE
    tok = pl.BlockSpec((1, tile, d), lambda i, t: (i, t, 0))
    modspec = pl.BlockSpec((1, 2, 6, d), lambda i, t: (i, 0, 0, 0))
    has_ffn = ffn is not None
    in_specs = [tok] + ([tok, modspec] if has_ffn else []) + [
        modspec, pl.BlockSpec((1, d), lambda i, t: (0, 0)), pl.BlockSpec(w.shape, lambda i, t: (0, 0))]
    out_shape = [jax.ShapeDtypeStruct((b, l, wd), F32) for wd in col_splits]
    out_specs = [pl.BlockSpec((1, tile, wd), lambda i, t: (i, t, 0)) for wd in col_splits]
    if has_ffn:
        out_shape = [jax.ShapeDtypeStruct((b, l, d), F32)] + out_shape
        out_specs = [tok] + out_specs
    args = (h, ffn, pmod, mod, gain, w) if has_ffn else (h, mod, gain, w)
    return pl.pallas_call(
        functools.partial(_inproj_kernel, has_ffn=has_ffn, col_splits=col_splits, n_ctx=n_ctx),
        grid=(b, l // tile), in_specs=in_specs, out_specs=out_specs, out_shape=out_shape,
        compiler_params=_cparams("parallel", "parallel"), name="in_proj",
    )(*args)


def _pool_kernel(a_ref, wbd_ref, scale_ref, o_ref, pad_ref, *, segments):
    c = a_ref.shape[-1]
    grp = lax.broadcasted_iota(jnp.int32, (1, c), 1) // (c // len(POOL_WINDOWS))
    zeros = jnp.zeros((SUBLANE, c), F32)
    for off, n in segments:
        pad_ref[0:SUBLANE, :] = zeros
        pad_ref[SUBLANE:SUBLANE + n, :] = a_ref[0, off:off + n, :]
        pad_ref[SUBLANE + n:2 * SUBLANE + n, :] = zeros

        def chunk(ci, carry, off=off, n=n):
            r0 = pl.multiple_of(ci * ROW_CHUNK, ROW_CHUNK)
            win = pad_ref[pl.ds(r0, ROW_CHUNK + 2 * SUBLANE), :]
            sh = lambda s: win[SUBLANE + s:SUBLANE + s + ROW_CHUNK, :]
            a0 = sh(0)
            s2 = sh(-1) + a0
            s4 = s2 + sh(-2) + sh(1)
            s8 = s4 + sh(-4) + sh(-3) + sh(2) + sh(3)
            s16 = s8 + (sh(-8) + sh(-7) + sh(-6) + sh(-5)) + (sh(4) + sh(5) + sh(6) + sh(7))
            t = r0 + lax.broadcasted_iota(jnp.int32, (ROW_CHUNK, 1), 0)

            def mean(s, w):
                cnt = jnp.minimum(t + (w - w // 2), n) - jnp.maximum(t - w // 2, 0)
                return s / cnt.astype(F32)

            pooled = jnp.where(grp == 0, mean(s2, 2), jnp.where(grp == 1, mean(s4, 4),
                               jnp.where(grp == 2, mean(s8, 8), mean(s16, 16)))) - a0
            o_ref[0, pl.ds(off + r0, ROW_CHUNK), :] = _dot_hi(pooled, wbd_ref[...]) * scale_ref[...]
            return carry

        lax.fori_loop(0, n // ROW_CHUNK, chunk, 0)


def _pool(a, wbd, scale, n_ctx):
    b, l, c = a.shape
    segments = ((0, n_ctx), (n_ctx, l - n_ctx))
    return pl.pallas_call(
        functools.partial(_pool_kernel, segments=segments),
        grid=(b,),
        in_specs=[pl.BlockSpec((1, l, c), lambda i: (i, 0, 0)),
                  pl.BlockSpec((c, c), lambda i: (0, 0)),
                  pl.BlockSpec((1, c), lambda i: (0, 0))],
        out_specs=pl.BlockSpec((1, l, c), lambda i: (i, 0, 0)),
        out_shape=jax.ShapeDtypeStruct((b, l, c), F32),
        scratch_shapes=[pltpu.VMEM((l - n_ctx + 2 * SUBLANE, c), F32)],
        compiler_params=_cparams("parallel"), name="pool",
    )(a, wbd, scale)


def _gdn_kernel(q_ref, k_ref, v_ref, z_ref, ba_ref, cwq_ref, cwk_ref, cwv_ref, gp_ref, gain_ref, o_ref,
                pad_ref, yq_ref, yk_ref, yv_ref, bg_ref, of_ref, ob_ref, s_ref,
                qe_ref, mn_ref, oc_ref, nn_ref, egl_ref, *, n_ctx):
    l = q_ref.shape[1]
    n_x = l - n_ctx
    hd = HEAD_DIM
    lane = lax.broadcasted_iota(jnp.int32, (1, LANE), 1)
    lo = lane < hd
    zeros = jnp.zeros((SUBLANE, LANE), F32)
    x_off = 2 * SUBLANE + n_ctx

    def head_scale(ss):
        s_lo = jnp.sum(jnp.where(lo, ss, 0.0), axis=-1, keepdims=True)
        s_hi = jnp.sum(jnp.where(lo, 0.0, ss), axis=-1, keepdims=True)
        return jnp.where(lo, s_lo, s_hi)

    def conv(u_ref, cw_ref, y_ref, post):
        pad_ref[0:SUBLANE, :] = zeros
        pad_ref[SUBLANE:SUBLANE + n_ctx, :] = u_ref[0, 0:n_ctx, :]
        pad_ref[SUBLANE + n_ctx:x_off, :] = zeros
        pad_ref[x_off:x_off + n_x, :] = u_ref[0, n_ctx:l, :]
        pad_ref[x_off + n_x:x_off + n_x + SUBLANE, :] = zeros
        cw = cw_ref[...]
        for poff, yoff, n in ((SUBLANE, 0, n_ctx), (x_off, n_ctx, n_x)):
            def chunk(ci, carry, poff=poff, yoff=yoff):
                r0 = pl.multiple_of(ci * ROW_CHUNK, ROW_CHUNK)
                win = pad_ref[pl.ds(poff - SUBLANE + r0, ROW_CHUNK + 2 * SUBLANE), :]
                acc = win[SUBLANE - 2:SUBLANE - 2 + ROW_CHUNK, :] * cw[0:1, :]
                for j in range(1, CONV_K):
                    acc = acc + win[SUBLANE - 2 + j:SUBLANE - 2 + j + ROW_CHUNK, :] * cw[j:j + 1, :]
                y_ref[pl.ds(yoff + r0, ROW_CHUNK), :] = post(_silu(acc))
                return carry
            lax.fori_loop(0, n // ROW_CHUNK, chunk, 0)

    l2 = lambda y: y * lax.rsqrt(head_scale(y * y) + EPS)
    conv(q_ref, cwq_ref, yq_ref, lambda y: l2(y) * (hd ** -0.5))
    conv(k_ref, cwk_ref, yk_ref, l2)
    conv(v_ref, cwv_ref, yv_ref, lambda y: y)

    ba = ba_ref[0]
    g = -jnp.exp(gp_ref[0, 0:1, :]) * _softplus(ba + gp_ref[0, 1:2, :])
    bg_ref[...] = jnp.where(lane < 4, jax.nn.sigmoid(ba), g)

    ck = GDN_CHUNK
    rr = lax.broadcasted_iota(jnp.int32, (ck, ck), 0)
    cc = lax.broadcasted_iota(jnp.int32, (ck, ck), 1)
    eye = (rr == cc).astype(F32)
    tril = (rr >= cc).astype(F32)
    n_chunks = l // ck
    nc_ctx = n_ctx // ck

    def phase1(ci, carry):
        probs = []
        for gi in range(GDN_GROUP):
            c = ci * GDN_GROUP + gi
            r0 = pl.multiple_of(c * ck, ck)
            qc = yq_ref[pl.ds(r0, ck), :]
            kc = yk_ref[pl.ds(r0, ck), :]
            vc = yv_ref[pl.ds(r0, ck), :]
            bgc = bg_ref[pl.ds(r0, ck), :]
            gcf = _dot_hi(tril, bgc)
            gcb = gcf[ck - 1:ck, :] - gcf + bgc
            gct = (gcf.T, gcb.T)
            for j in range(2):
                qh = qc[:, j * hd:(j + 1) * hd]
                kh = kc[:, j * hd:(j + 1) * hd]
                vh = vc[:, j * hd:(j + 1) * hd]
                kk = _dot_nt(kh, kh)
                qk = _dot_nt(qh, kh)
                for d in range(2):
                    mask = (rr >= cc) if d == 0 else (rr <= cc)
                    smask = (rr > cc) if d == 0 else (rr < cc)
                    lg, lb = 4 + 2 * d + j, 2 * d + j
                    gcol = jnp.broadcast_to((gcf, gcb)[d][:, lg:lg + 1], (ck, hd))
                    grow = gct[d][lg:lg + 1, :]
                    bcol = jnp.broadcast_to(bgc[:, lb:lb + 1], (ck, hd))
                    decay = jnp.where(mask, jnp.exp(jnp.where(mask, gcol - grow, 0.0)), 0.0)
                    xp = jnp.where(smask, -(bcol * kk * decay), 0.0)
                    eg = jnp.exp(gcol)
                    last = ck - 1 if d == 0 else 0
                    glast = gcol[last:last + 1, :]
                    probs.append(dict(
                        idx=c * 4 + 2 * d + j, xp=xp, inv=eye + xp, attn=qk * decay, qd=qh * eg,
                        rhs=jnp.concatenate([vh * bcol, kh * bcol * eg], axis=1),
                        kd=kh * jnp.exp(glast - gcol), egl=jnp.exp(glast)))
        for _ in range(5):
            for p in probs:
                p["xp"] = _dot(p["xp"], p["xp"])
            for p in probs:
                p["inv"] = p["inv"] + _dot(p["inv"], p["xp"])
        for p in probs:
            p["sol"] = _dot(p["inv"], p["rhs"])
        for p in probs:
            p["as"] = _dot(p["attn"], p["sol"])
        for p in probs:
            p["ks"] = _dot_tn(p["kd"], p["sol"])
        for p in probs:
            i = p["idx"]
            qe_ref[i] = (p["qd"] - p["as"][:, hd:]).astype(BF16)
            oc_ref[i] = p["as"][:, :hd]
            mn_ref[i] = p["ks"][:, hd:].astype(BF16)
            nn_ref[i] = p["ks"][:, :hd]
            egl_ref[i] = jnp.broadcast_to(p["egl"], (ck, hd))
        return carry

    lax.fori_loop(0, n_chunks // GDN_GROUP, phase1, 0)

    s_ref[...] = jnp.zeros(s_ref.shape, F32)

    def phase2(i, carry):
        cb = jnp.where(i < nc_ctx, nc_ctx - 1 - i, n_chunks - 1 - (i - nc_ctx))
        work = [(d, j, cidx * 4 + 2 * d + j) for d, cidx in ((0, i), (1, cb)) for j in range(2)]
        states = [s_ref[2 * d + j] for d, j, _ in work]
        outs = [_dot(qe_ref[idx], s) + oc_ref[idx] for (_, _, idx), s in zip(work, states)]
        upd = [_dot(mn_ref[idx], s) for (_, _, idx), s in zip(work, states)]
        for (d, j, idx), s, m in zip(work, states, upd):
            s_ref[2 * d + j] = egl_ref[idx] * s - m + nn_ref[idx]
        of_ref[pl.ds(pl.multiple_of(i * ck, ck), ck), :] = jnp.concatenate(outs[0:2], axis=1)
        ob_ref[pl.ds(pl.multiple_of(cb * ck, ck), ck), :] = jnp.concatenate(outs[2:4], axis=1)
        return carry

    lax.fori_loop(0, n_chunks, phase2, 0)

    def finish(ci, carry):
        r0 = pl.multiple_of(ci * ROW_CHUNK, ROW_CHUNK)
        o = of_ref[pl.ds(r0, ROW_CHUNK), :] + ob_ref[pl.ds(r0, ROW_CHUNK), :]
        o = o * lax.rsqrt(head_scale(o * o) * (1.0 / hd) + EPS) * gain_ref[...]
        o_ref[0, pl.ds(r0, ROW_CHUNK), :] = o * _silu(z_ref[0, pl.ds(r0, ROW_CHUNK), :])
        return carry

    lax.fori_loop(0, l // ROW_CHUNK, finish, 0)


def _gdn(qkv, z, ba, conv_w, gparams, gain2, n_ctx):
    b, l, w3 = qkv.shape
    npair = w3 // (3 * LANE)
    seq = lambda off: pl.BlockSpec((1, l, LANE), lambda i, p: (i, 0, p + off))
    cw = lambda off: pl.BlockSpec((CONV_K, LANE), lambda i, p: (0, p + off))
    return pl.pallas_call(
        functools.partial(_gdn_kernel, n_ctx=n_ctx),
        grid=(b, npair),
        in_specs=[seq(0), seq(npair), seq(2 * npair), seq(0), seq(0),
                  cw(0), cw(npair), cw(2 * npair),
                  pl.BlockSpec((1, SUBLANE, LANE), lambda i, p: (p, 0, 0)),
                  pl.BlockSpec((1, LANE), lambda i, p: (0, 0))],
        out_specs=seq(0),
        out_shape=jax.ShapeDtypeStruct((b, l, npair * LANE), F32),
        scratch_shapes=[pltpu.VMEM((l + 3 * SUBLANE, LANE), F32)] + [pltpu.VMEM((l, LANE), F32)] * 6
        + [pltpu.VMEM((4, HEAD_DIM, HEAD_DIM), F32)]
        + [pltpu.VMEM((4 * l // GDN_CHUNK, GDN_CHUNK, HEAD_DIM), dt) for dt in (BF16, BF16, F32, F32, F32)],
        compiler_params=_cparams("parallel", "parallel"), name="gdn",
    )(qkv, qkv, qkv, z, ba, conv_w, conv_w, conv_w, gparams, gain2)


def _rope(x, cos, sin):
    w = x.shape[-1]
    reps = w // cos.shape[-1]
    if reps > 1:
        cos = jnp.concatenate([cos] * reps, axis=1)
        sin = jnp.concatenate([sin] * reps, axis=1)
    first = (lax.broadcasted_iota(jnp.int32, (1, w), 1) % (2 * ROPE_PAIRS)) < ROPE_PAIRS
    partner = jnp.where(first, pltpu.roll(x, w - ROPE_PAIRS, 1), pltpu.roll(x, ROPE_PAIRS, 1))
    return x * cos + partner * sin


def _attend(q, k, v, valid, sink_col):
    s = _dot_nt(q, k)
    if valid is not None:
        s = jnp.where(valid, s, NEG_INF)
    m = jnp.maximum(jnp.max(s, axis=-1, keepdims=True), sink_col)
    p = jnp.exp(s - m)
    den = jnp.sum(p, axis=-1, keepdims=True) + jnp.exp(sink_col - m)
    return _dot(p, v) / den


def _gqa(q, k_all, v_all, valid, sink_ref):
    nq = q.shape[0]
    hd = HEAD_DIM
    group = q.shape[1] // hd // KV_HEADS
    outs = []
    for kvh in range(KV_HEADS):
        heads = [kvh * group + g for g in range(group)]
        qg = jnp.concatenate([q[:, h * hd:(h + 1) * hd] for h in heads], axis=0)
        sink_col = jnp.concatenate(
            [jnp.broadcast_to(sink_ref[0:1, h:h + 1], (nq, 1)) for h in heads], axis=0)
        o = _attend(qg, k_all[:, kvh * hd:(kvh + 1) * hd], v_all[:, kvh * hd:(kvh + 1) * hd],
                    valid, sink_col)
        outs += [o[g * nq:(g + 1) * nq, :] for g in range(group)]
    return jnp.concatenate(outs, axis=1)


def _attn_kernel(q_ref, *refs, seq):
    nkb = ATTN_QBLOCKS + 2
    k_refs, v_refs = refs[0:nkb], refs[nkb:2 * nkb]
    kc_ref, vc_ref = refs[2 * nkb:2 * nkb + 2]
    cos_refs = refs[2 * nkb + 2:3 * nkb + 2]
    sin_refs = refs[3 * nkb + 2:4 * nkb + 2]
    sink_ref, o_ref = refs[4 * nkb + 2:]
    blk = ATTN_BLOCK
    n_ctx = kc_ref.shape[1]
    group = q_ref.shape[2] // HEAD_DIM // KV_HEADS
    k_rot = [_rope(k_refs[j][0], cos_refs[j][...], sin_refs[j][...]) for j in range(nkb)]
    nk = 3 * blk + n_ctx
    qo = lax.broadcasted_iota(jnp.int32, (group * blk, 1), 0) % blk
    ko = lax.broadcasted_iota(jnp.int32, (1, nk), 1)
    in_window = jnp.abs(ko - blk - qo) <= WINDOW
    for s in range(ATTN_QBLOCKS):
        i = pl.program_id(1) * ATTN_QBLOCKS + s
        rows = slice(s * blk, (s + 1) * blk)
        q = _rope(q_ref[0, rows, :], cos_refs[s + 1][...], sin_refs[s + 1][...]) * (HEAD_DIM ** -0.5)
        k_all = jnp.concatenate(k_rot[s:s + 3] + [kc_ref[0]], axis=0)
        v_all = jnp.concatenate([v_refs[j][0] for j in range(s, s + 3)] + [vc_ref[0]], axis=0)
        kpos = (i - 1) * blk + ko
        valid = (in_window & (kpos >= 0) & (kpos < seq)) | (ko >= 3 * blk)
        o_ref[0, rows, :] = _gqa(q, k_all, v_all, valid, sink_ref)


def _attn_x(aq, ak, av, cos_t, sin_t, sink_row, n_ctx):
    b, l, wq = aq.shape
    wk = ak.shape[-1]
    blk = ATTN_BLOCK
    seq = l - n_ctx
    nb = seq // blk
    cb = n_ctx // blk
    nq = ATTN_QBLOCKS
    assert nb % nq == 0 and cb % nq == 0
    blocks = [lambda t, j=j: jnp.clip(t * nq + j - 1, 0, nb - 1) for j in range(nq + 2)]
    kv = [pl.BlockSpec((1, blk, wk), lambda i, t, f=f: (i, f(t) + cb, 0)) for f in blocks]
    tab = [pl.BlockSpec((blk, LANE), lambda i, t, f=f: (f(t), 0)) for f in blocks]
    ctx = pl.BlockSpec((1, n_ctx, wk), lambda i, t: (i, 0, 0))
    qspec = pl.BlockSpec((1, nq * blk, wq), lambda i, t: (i, t + cb // nq, 0))
    return pl.pallas_call(
        functools.partial(_attn_kernel, seq=seq),
        grid=(b, nb // nq),
        in_specs=[qspec] + kv + kv + [ctx, ctx] + tab + tab + [pl.BlockSpec((1, LANE), lambda i, t: (0, 0))],
        out_specs=pl.BlockSpec((1, nq * blk, wq), lambda i, t: (i, t, 0)),
        out_shape=jax.ShapeDtypeStruct((b, seq, wq), F32),
        compiler_params=_cparams("parallel", "parallel"), name="attn_x",
    )(aq, *([ak] * (nq + 2)), *([av] * (nq + 2)), ak, av, *([cos_t] * (nq + 2)), *([sin_t] * (nq + 2)),
      sink_row)


def _attn_ctx_kernel(q_ref, k_ref, v_ref, sink_ref, o_ref):
    o_ref[0] = _gqa(q_ref[0] * (HEAD_DIM ** -0.5), k_ref[0], v_ref[0], None, sink_ref)


def _attn_ctx(aq, ak, av, sink_row, n_ctx):
    b, _, wq = aq.shape
    wk = ak.shape[-1]
    spec = lambda w: pl.BlockSpec((1, n_ctx, w), lambda i: (i, 0, 0))
    return pl.pallas_call(
        _attn_ctx_kernel, grid=(b,),
        in_specs=[spec(wq), spec(wk), spec(wk), pl.BlockSpec((1, LANE), lambda i: (0, 0))],
        out_specs=spec(wq), out_shape=jax.ShapeDtypeStruct((b, n_ctx, wq), F32),
        compiler_params=_cparams("parallel"), name="attn_ctx",
    )(aq, ak, av, sink_row)


def _outproj_kernel(x_ref, ya_ref, yb_ref, yc_ref, w_ref, mod_ref, gain_ref, xo_ref, hx_ref, *, row_off, n_ctx):
    wa, wb = ya_ref.shape[-1], yb_ref.shape[-1]
    tile = x_ref.shape[1]
    acc = (jnp.dot(ya_ref[0].astype(BF16), w_ref[0:wa, :], preferred_element_type=F32)
           + jnp.dot(yb_ref[0].astype(BF16), w_ref[wa:wa + wb, :], preferred_element_type=F32)
           + jnp.dot(yc_ref[0].astype(BF16), w_ref[wa + wb:, :], preferred_element_type=F32))
    mod = _row_mod(mod_ref, row_off + pl.program_id(1) * tile, tile, n_ctx)
    x = x_ref[0] + mod(2) * acc
    xo_ref[0] = x
    hx_ref[0] = _modnorm(x, gain_ref[...], mod(4), mod(3)).astype(BF16)


def _out_proj(h, ya, yb, yc, w, mod, gain, n_ctx, skip_ctx):
    b, l, d = h.shape
    tile = TOK_TILE if skip_ctx else PROJ_TILE
    off = n_ctx // tile if skip_ctx else 0
    nt = l // tile - off
    full = lambda wd: pl.BlockSpec((1, tile, wd), lambda i, t: (i, t + off, 0))
    own = lambda wd: pl.BlockSpec((1, tile, wd), lambda i, t: (i, t, 0))
    return pl.pallas_call(
        functools.partial(_outproj_kernel, row_off=off * tile, n_ctx=n_ctx), grid=(b, nt),
        in_specs=[full(d), full(ya.shape[-1]), full(yb.shape[-1]), own(yc.shape[-1]),
                  pl.BlockSpec(w.shape, lambda i, t: (0, 0)),
                  pl.BlockSpec((1, 2, 6, d), lambda i, t: (i, 0, 0, 0)),
                  pl.BlockSpec((1, d), lambda i, t: (0, 0))],
        out_specs=[own(d), own(d)],
        out_shape=[jax.ShapeDtypeStruct((b, nt * tile, d), F32),
                   jax.ShapeDtypeStruct((b, nt * tile, d), BF16)],
        compiler_params=_cparams("parallel", "parallel"), name="out_proj",
    )(h, ya, yb, yc, w, mod, gain)


def _sort_pairs(n):
    pairs, p = [], 1
    while p < n:
        k = p
        while k >= 1:
            for j in range(k % p, n - k, 2 * k):
                for i in range(min(k, n - j - k)):
                    if (i + j) // (2 * p) == (i + j + k) // (2 * p):
                        pairs.append((i + j, i + j + k))
            k //= 2
        p *= 2
    return pairs


def _topk_rows(v):
    n = len(v)
    v = list(v)

    def cmpx(i, j):
        v[i], v[j] = jnp.maximum(v[i], v[j]), jnp.minimum(v[i], v[j])

    for i, j in _sort_pairs(n):
        cmpx(i, j)
    shift = SUBLANE // 2
    while shift >= 1:
        partner = [pltpu.roll(x, shift, 0) for x in v]
        v = [jnp.maximum(v[j], partner[n - 1 - j]) for j in range(n)]
        stride = n // 2
        while stride >= 1:
            for i in range(n):
                if not i & stride:
                    cmpx(i, i + stride)
            stride //= 2
        shift //= 2
    return v


def _route_kernel(ht_ref, wq_ref, keys_ref, r2_ref, e2_ref, n1_ref, e1_ref, qt_ref, top_ref, cand_ref):
    tm = ht_ref.shape[1]
    k = PEER_TOPK
    nk = N_KEYS
    neg = jnp.float32(-jnp.inf)
    qt_ref[...] = _dot(wq_ref[...], ht_ref[...]).astype(BF16)
    cand_ref[_N_CAND:_CAND_PAD, :] = jnp.full((_CAND_PAD - _N_CAND, tm), neg, F32)
    for h in range(PEER_HEADS):
        st = []
        for p in range(2):
            hp = 2 * h + p
            s = jnp.dot(keys_ref[hp], qt_ref[hp * nk:(hp + 1) * nk, :], preferred_element_type=F32)
            st.append(s)
            top = _topk_rows([s[g * SUBLANE:(g + 1) * SUBLANE, :] for g in range(nk // SUBLANE)])
            for r in range(k):
                top_ref[p * k + r:p * k + r + 1, :] = top[r][0:1, :]
        v1 = top_ref[0:k, :]
        v2 = top_ref[k:2 * k, :]
        rank2 = jnp.full((nk, tm), float(k), F32)
        for r in range(k):
            rank2 = jnp.where(st[1] == v2[r:r + 1, :], float(r), rank2)
        row = 0
        for a, nb in enumerate(_CAND_ROWS):
            cand_ref[row:row + nb, :] = v1[a:a + 1, :] + v2[0:nb, :]
            row += nb
        cur = cand_ref[...]
        cmax = thr = zsum = None
        for r in range(k):
            m = jnp.max(cur, axis=0, keepdims=True)
            if r == 0:
                cmax, zsum = m, jnp.ones_like(m)
            else:
                zsum = zsum + jnp.exp(m - cmax)
            thr = m
            cur = jnp.where(cur == m, neg, cur)
        n1 = jnp.zeros((nk, tm), F32)
        row = 0
        for a, nb in enumerate(_CAND_ROWS):
            sel = cand_ref[row:row + nb, :] >= thr
            n_a = jnp.sum(sel.astype(F32), axis=0, keepdims=True)
            n1 = jnp.where(st[0] == v1[a:a + 1, :], n_a, n1)
            row += nb
        r2_ref[h] = rank2.T
        e2_ref[h] = jnp.exp(st[1] - v2[0:1, :]).T
        n1_ref[h] = n1.T
        e1_ref[h] = (jnp.exp(st[0] - v1[0:1, :]) / zsum).T


def _peer_route(hxt, wq_t, keys):
    d, t = hxt.shape
    nq = wq_t.shape[0]
    tm = PEER_ROUTE_TILE
    rows = PEER_HEADS * N_KEYS
    return pl.pallas_call(
        _route_kernel, grid=(t // tm,),
        in_specs=[pl.BlockSpec((d, tm), lambda i: (0, i)),
                  pl.BlockSpec(wq_t.shape, lambda i: (0, 0)),
                  pl.BlockSpec(keys.shape, lambda i: (0, 0, 0))],
        out_specs=[pl.BlockSpec((PEER_HEADS, tm, N_KEYS), lambda i: (0, i, 0))] * 4,
        out_shape=[jax.ShapeDtypeStruct((PEER_HEADS, t, N_KEYS), F32)] * 4,
        scratch_shapes=[pltpu.VMEM((nq, tm), BF16), pltpu.VMEM((2 * PEER_TOPK, tm), F32),
                        pltpu.VMEM((_CAND_PAD, tm), F32)],
        compiler_params=_cparams("parallel"), name="peer_route",
    )(hxt, wq_t, keys)


def _gates_kernel(r2t_ref, e2t_ref, n1t_ref, e1t_ref, w_ref):
    tb = w_ref.shape[1]
    nk = N_KEYS
    k = PEER_TOPK
    rank = lax.broadcasted_iota(jnp.int32, (k, nk), 0).astype(F32)

    def tok(i, carry):
        for j in range(PEER_GATE_UNROLL):
            t = i * PEER_GATE_UNROLL + j
            a, b = [], []
            for h in range(PEER_HEADS):
                r2, e2, n1, e1 = (ref[h, pl.ds(t, k, stride=0), :]
                                  for ref in (r2t_ref, e2t_ref, n1t_ref, e1t_ref))
                a.append(jnp.where(n1 > rank, e1, 0.0))
                b.append(jnp.where(r2 == rank, e2, 0.0))
            w = _dot_tn(jnp.concatenate(a, axis=0), jnp.concatenate(b, axis=0))
            for g in range(nk // (2 * SUBLANE)):
                lo, hi = 2 * g * SUBLANE, (2 * g + 1) * SUBLANE
                w_ref[g, t] = pltpu.pack_elementwise([w[lo:lo + SUBLANE, :], w[hi:hi + SUBLANE, :]],
                                                     packed_dtype=jnp.bfloat16)
        return carry

    lax.fori_loop(0, tb // PEER_GATE_UNROLL, tok, 0)


def _peer_gates(route):
    t = route[0].shape[1]
    tb = PEER_GATE_TILE
    return pl.pallas_call(
        _gates_kernel, grid=(t // tb,),
        in_specs=[pl.BlockSpec((PEER_HEADS, tb, N_KEYS), lambda i: (0, i, 0))] * 4,
        out_specs=pl.BlockSpec((N_KEYS // (2 * SUBLANE), tb, SUBLANE, N_KEYS), lambda i: (0, i, 0, 0)),
        out_shape=jax.ShapeDtypeStruct((N_KEYS // (2 * SUBLANE), t, SUBLANE, N_KEYS), jnp.uint32),
        compiler_params=_cparams("parallel"), name="peer_gates",
    )(*route)


def _peer_kernel(h_ref, w_ref, ut_ref, v_ref, o_ref, acc_ref, g0_ref, g1_ref):
    e = pl.program_id(1)
    n_blk = pl.num_programs(1) - 1

    @pl.when(e == 0)
    def _():
        acc_ref[...] = jnp.zeros(acc_ref.shape, F32)
        g1_ref[...] = jnp.zeros(g1_ref.shape, BF16)

    for parity, g_w_ref, g_r_ref in ((0, g0_ref, g1_ref), (1, g1_ref, g0_ref)):
        for build in (True, False):
            @pl.when((e % 2 == parity) & ((e < n_blk) if build else (e == n_blk)))
            def _(g_w_ref=g_w_ref, g_r_ref=g_r_ref, build=build, parity=parity):
                _peer_step(h_ref, w_ref, ut_ref, v_ref, acc_ref, g_w_ref, g_r_ref, build, parity)

    @pl.when(e == n_blk)
    def _():
        o_ref[...] = acc_ref[...]


def _peer_step(h_ref, w_ref, ut_ref, v_ref, acc_ref, g_w_ref, g_r_ref, build, half):
    sr = h_ref.shape[0] // PEER_SUB

    def sub(k, carry):
        rows = pl.ds(pl.multiple_of(k * sr, sr), sr)
        acc_ref[rows, :] += jnp.dot(g_r_ref[rows, :], v_ref[...].astype(BF16), preferred_element_type=F32)
        if not build:
            return carry
        act = jnp.dot(h_ref[rows, :], ut_ref[0], preferred_element_type=F32)
        act = 0.5 * act * (1.0 + lax.erf(act * 0.7071067811865476))
        for j in range(SUBLANE):
            ln = slice(j * N_KEYS, (j + 1) * N_KEYS)
            wj = pltpu.unpack_elementwise(w_ref[pl.ds(k * (sr * SUBLANE) + j, sr, stride=SUBLANE), :],
                                          index=half, packed_dtype=jnp.bfloat16, unpacked_dtype=F32)
            g_w_ref[rows, ln] = (wj * act[:, ln]).astype(BF16)
        return carry

    lax.fori_loop(0, PEER_SUB, sub, 0)


def _peer_dense(hx, w, ut, v_all, layer):
    t, d = hx.shape
    n_blk, _, eb = ut.shape
    tm = PEER_TOK_TILE
    last = n_blk - 1
    return pl.pallas_call(
        _peer_kernel, grid=(t // tm, n_blk + 1),
        in_specs=[pl.BlockSpec((tm, d), lambda i, e: (i, 0)),
                  pl.BlockSpec((tm * SUBLANE, N_KEYS),
                               lambda i, e: (jnp.minimum(e, last) // 2 * (t // tm) + i, 0)),
                  pl.BlockSpec((1, d, eb), lambda i, e: (jnp.minimum(e, last), 0, 0)),
                  pl.BlockSpec((None, eb, d), lambda i, e: (layer, jnp.maximum(e - 1, 0), 0))],
        out_specs=pl.BlockSpec((tm, d), lambda i, e: (i, 0)),
        out_shape=jax.ShapeDtypeStruct((t, d), F32),
        scratch_shapes=[pltpu.VMEM((tm, d), F32), pltpu.VMEM((tm, eb), BF16), pltpu.VMEM((tm, eb), BF16)],
        compiler_params=_cparams("parallel", "arbitrary"), name="peer_dense",
    )(hx, w, ut, v_all)


def _final_kernel(x_ref, f_ref, mod_ref, gain_ref, o_ref):
    x = x_ref[0] + mod_ref[0, 0, 5:6, :] * f_ref[0]
    o_ref[0] = x * lax.rsqrt(jnp.mean(x * x, axis=-1, keepdims=True) + EPS) * gain_ref[...]


def _final(x, ffn, mod, gain):
    b, s, d = x.shape
    tok = pl.BlockSpec((1, TOK_TILE, d), lambda i, t: (i, t, 0))
    return pl.pallas_call(
        _final_kernel, grid=(b, s // TOK_TILE),
        in_specs=[tok, tok, pl.BlockSpec((1, 1, 6, d), lambda i, t: (i, 1, 0, 0)),
                  pl.BlockSpec((1, d), lambda i, t: (0, 0))],
        out_specs=tok, out_shape=jax.ShapeDtypeStruct((b, s, d), F32),
        compiler_params=_cparams("parallel", "parallel"), name="final_norm",
    )(x, ffn, mod, gain)


def _rope_tables(seq):
    pos = jnp.arange(seq)
    rc = jnp.stack([pos // GRID_W, pos % GRID_W], axis=-1).astype(F32)
    inv = jnp.power(ROPE_BASE, -jnp.arange(ROPE_PAIRS, dtype=F32) / ROPE_PAIRS)
    ang = rc[:, :, None] * inv
    cos = jnp.concatenate([jnp.cos(ang)] * 2, axis=-1).reshape(seq, HEAD_DIM)
    sin = jnp.concatenate([-jnp.sin(ang), jnp.sin(ang)], axis=-1).reshape(seq, HEAD_DIM)
    return jnp.concatenate([cos] * 2, axis=1), jnp.concatenate([sin] * 2, axis=1)


def _pair_lanes(cols, n_heads):
    rows = cols.shape[0]
    c = cols.reshape(rows, 4, n_heads // 2, 2).transpose(0, 2, 1, 3).reshape(rows, n_heads // 2, 8)
    return jnp.pad(c, ((0, 0), (0, 0), (0, LANE - 8))).reshape(rows, n_heads // 2 * LANE)


def kernel(x, c, ctx, c_ctx, w_ada, b_ada, norm1, norm2, w_in, conv_w, a_log, dt_bias, gdn_norm, w_pool,
           pool_scale, sink, w_out, peer_wq, peer_keys, peer_u, peer_v, norm_f):
    b, seq, d = x.shape
    n_ctx = ctx.shape[1]
    depth = w_ada.shape[0]
    pool_w = d // 4
    gdn_w = 3 * d // 8
    gdn_heads = gdn_w // HEAD_DIM
    attn_w = d - pool_w - gdn_w
    kv_w = KV_HEADS * HEAD_DIM
    assert n_ctx % TOK_TILE == 0 and seq % TOK_TILE == 0 and gdn_heads % 2 == 0
    assert (n_ctx + seq) % PROJ_TILE == 0 and PEER_EXP_TILE == SUBLANE * N_KEYS
    assert (b * (n_ctx + seq)) % PEER_TOK_TILE == 0 and (b * seq) % PEER_TOK_TILE == 0

    rows = -(-(b + 1) // SUBLANE) * SUBLANE
    c_all = jnp.zeros((rows, d), F32).at[:b].set(c).at[b].set(c_ctx)
    mod = _ada_mod(c_all, w_ada, b_ada)
    mod_x = mod[:, :b].reshape(depth, b, 1, 6, d)
    mod_c = jnp.broadcast_to(mod[:, b].reshape(depth, 1, 1, 6, d), (depth, b, 1, 6, d))
    mod = jnp.concatenate([mod_c, mod_x], axis=2)

    cos_t, sin_t = _rope_tables(seq)
    splits = (pool_w, 3 * gdn_w, gdn_w, gdn_heads // 2 * LANE, attn_w, kv_w, kv_w)
    o_ba = pool_w + 4 * gdn_w
    n_ba = 4 * gdn_heads

    h = jnp.concatenate([ctx, x], axis=1)
    ffn = None
    for l in range(depth):
        last = l == depth - 1
        w_l = w_in[l]
        w_big = jnp.concatenate([w_l[:, :o_ba], _pair_lanes(w_l[:, o_ba:o_ba + n_ba], gdn_heads),
                                 w_l[:, o_ba + n_ba:]], axis=1).astype(BF16)
        pmod = mod[l - 1] if l > 0 else None
        outs = _in_proj(h, ffn, pmod, mod[l], norm1[l][None], w_big, splits, n_ctx)
        if l > 0:
            h, *outs = outs
        pa, qkv, z, ba, aq, ak, av = outs

        wbd = jax.scipy.linalg.block_diag(*[w_pool[l, g] for g in range(len(POOL_WINDOWS))])
        ya = _pool(pa, wbd, pool_scale[l][None], n_ctx)

        gparams = jnp.stack([_pair_lanes(jnp.concatenate([jnp.zeros_like(a_log[l]), a_log[l]]).reshape(1, -1),
                                         gdn_heads),
                             _pair_lanes(jnp.concatenate([jnp.zeros_like(dt_bias[l]), dt_bias[l]]).reshape(1, -1),
                                         gdn_heads)], axis=1)
        gparams = gparams.reshape(2, gdn_heads // 2, LANE).transpose(1, 0, 2)
        gparams = jnp.pad(gparams, ((0, 0), (0, SUBLANE - 2), (0, 0)))
        yb = _gdn(qkv, z, ba, conv_w[l], gparams, jnp.concatenate([gdn_norm[l]] * 2)[None], n_ctx)

        sink_row = jnp.pad(sink[l], (0, LANE - sink.shape[1]))[None]
        yc = _attn_x(aq, ak, av, cos_t, sin_t, sink_row, n_ctx)
        if not last:
            yc = jnp.concatenate([_attn_ctx(aq, ak, av, sink_row, n_ctx), yc], axis=1)

        x_new, hx = _out_proj(h, ya, yb, yc, w_out[l].astype(BF16), mod[l], norm2[l][None], n_ctx,
                              skip_ctx=last)
        t = hx.shape[0] * hx.shape[1]
        hx = hx.reshape(t, d)
        keys = peer_keys[l].reshape(2 * PEER_HEADS, N_KEYS, -1).astype(BF16)
        route = _peer_route(hx.T, peer_wq[l].T.astype(BF16), keys)
        gates = _peer_gates(route).reshape(-1, N_KEYS)
        ut = peer_u[l].reshape(-1, PEER_EXP_TILE, d).transpose(0, 2, 1).astype(BF16)
        ffn = _peer_dense(hx, gates, ut, peer_v, l)
        ffn = ffn.reshape(x_new.shape)
        h = x_new
    return _final(h, ffn, mod[depth - 1], norm_f[None])
```

```python
import functools

import jax
import jax.numpy as jnp
from jax import lax
from jax.experimental import pallas as pl
from jax.experimental.pallas import tpu as pltpu

F32, BF16 = jnp.float32, jnp.bfloat16
HIGHEST = lax.Precision.HIGHEST

EPS = 1e-6
NEG_INF = -1e30
HEAD_DIM = 64
GRID_W = 64
POOL_WINDOWS = (2, 4, 8, 16)
GDN_CHUNK = 64
CONV_K = 5
KV_HEADS = 2
WINDOW = 128
ATTN_BLOCK = 128
ROPE_BASE = 10000.0
ROPE_PAIRS = HEAD_DIM // 4
PEER_HEADS = 8
N_KEYS = 128
PEER_TOPK = 16

LANE = 128
SUBLANE = 8
TOK_TILE = 256
PROJ_TILE = 768
ROW_CHUNK = 256
ATTN_QBLOCKS = 2
GDN_GROUP = 4
PEER_ROUTE_TILE = 512
PEER_GATE_TILE = 128
PEER_GATE_UNROLL = 32
PEER_TOK_TILE = 1024
PEER_EXP_TILE = 1024
PEER_SUB = 1
VMEM_LIMIT = 48 * 1024 * 1024

_CAND_ROWS = tuple(PEER_TOPK // (a + 1) for a in range(PEER_TOPK))
_N_CAND = sum(_CAND_ROWS)
_CAND_PAD = -(-_N_CAND // SUBLANE) * SUBLANE


def _cparams(*sem):
    return pltpu.CompilerParams(dimension_semantics=sem, vmem_limit_bytes=VMEM_LIMIT)


def _dot(a, b):
    return jnp.dot(a.astype(BF16), b.astype(BF16), preferred_element_type=F32)


def _dot_nt(a, b):
    return lax.dot_general(a.astype(BF16), b.astype(BF16), (((1,), (1,)), ((), ())),
                           preferred_element_type=F32)


def _dot_tn(a, b):
    return lax.dot_general(a.astype(BF16), b.astype(BF16), (((0,), (0,)), ((), ())),
                           preferred_element_type=F32)


def _dot_hi(a, b):
    return jnp.dot(a, b, precision=HIGHEST, preferred_element_type=F32)


def _silu(x):
    return x * jax.nn.sigmoid(x)


def _softplus(x):
    return jnp.maximum(x, 0.0) + jnp.log1p(jnp.exp(-jnp.abs(x)))


def _modnorm(x, gain, scale, shift):
    y = x * lax.rsqrt(jnp.mean(x * x, axis=-1, keepdims=True) + EPS) * gain
    return y * (1.0 + scale) + shift


def _ada_kernel(c_ref, w_ref, b_ref, o_ref):
    o_ref[0] = _dot_hi(_silu(c_ref[...]), w_ref[0]) + b_ref[0]


def _ada_mod(c_all, w_ada, b_ada):
    depth, d, n = w_ada.shape
    rows = c_all.shape[0]
    tn = n // 4
    return pl.pallas_call(
        _ada_kernel,
        grid=(depth, n // tn),
        in_specs=[pl.BlockSpec((rows, d), lambda l, j: (0, 0)),
                  pl.BlockSpec((1, d, tn), lambda l, j: (l, 0, j)),
                  pl.BlockSpec((1, 1, tn), lambda l, j: (l, 0, j))],
        out_specs=pl.BlockSpec((1, rows, tn), lambda l, j: (l, 0, j)),
        out_shape=jax.ShapeDtypeStruct((depth, rows, n), F32),
        compiler_params=_cparams("parallel", "parallel"),
        name="ada_mod",
    )(c_all, w_ada, b_ada.reshape(depth, 1, n))


def _row_mod(mod_ref, first_row, n_rows, n_ctx):
    is_x = first_row + lax.broadcasted_iota(jnp.int32, (n_rows, 1), 0) >= n_ctx
    return lambda k: jnp.where(is_x, mod_ref[0, 1, k:k + 1, :], mod_ref[0, 0, k:k + 1, :])


def _inproj_kernel(*refs, has_ffn, col_splits, n_ctx):
    tile = refs[0].shape[1]
    first_row = pl.program_id(1) * tile
    if has_ffn:
        h_ref, f_ref, pmod_ref, mod_ref, gain_ref, w_ref, res_ref, *outs = refs
        x = h_ref[0] + _row_mod(pmod_ref, first_row, tile, n_ctx)(5) * f_ref[0]
        res_ref[0] = x
    else:
        h_ref, mod_ref, gain_ref, w_ref, *outs = refs
        x = h_ref[0]
    mod = _row_mod(mod_ref, first_row, tile, n_ctx)
    hb = _modnorm(x, gain_ref[...], mod(1), mod(0)).astype(BF16)
    off = 0
    for o_ref, width in zip(outs, col_splits):
        o_ref[0] = jnp.dot(hb, w_ref[:, off:off + width], preferred_element_type=F32)
        off += width


def _in_proj(h, ffn, pmod, mod, gain, w, col_splits, n_ctx):
    b, l, d = h.shape
    tile = PROJ_TILE
    tok = pl.BlockSpec((1, tile, d), lambda i, t: (i, t, 0))
    modspec = pl.BlockSpec((1, 2, 6, d), lambda i, t: (i, 0, 0, 0))
    has_ffn = ffn is not None
    in_specs = [tok] + ([tok, modspec] if has_ffn else []) + [
        modspec, pl.BlockSpec((1, d), lambda i, t: (0, 0)), pl.BlockSpec(w.shape, lambda i, t: (0, 0))]
    out_shape = [jax.ShapeDtypeStruct((b, l, wd), F32) for wd in col_splits]
    out_specs = [pl.BlockSpec((1, tile, wd), lambda i, t: (i, t, 0)) for wd in col_splits]
    if has_ffn:
        out_shape = [jax.ShapeDtypeStruct((b, l, d), F32)] + out_shape
        out_specs = [tok] + out_specs
    args = (h, ffn, pmod, mod, gain, w) if has_ffn else (h, mod, gain, w)
    return pl.pallas_call(
        functools.partial(_inproj_kernel, has_ffn=has_ffn, col_splits=col_splits, n_ctx=n_ctx),
        grid=(b, l // tile), in_specs=in_specs, out_specs=out_specs, out_shape=out_shape,
        compiler_params=_cparams("parallel", "parallel"), name="in_proj",
    )(*args)


def _pool_kernel(a_ref, wbd_ref, scale_ref, o_ref, pad_ref, *, segments):
    c = a_ref.shape[-1]
    grp = lax.broadcasted_iota(jnp.int32, (1, c), 1) // (c // len(POOL_WINDOWS))
    zeros = jnp.zeros((SUBLANE, c), F32)
    for off, n in segments:
        pad_ref[0:SUBLANE, :] = zeros
        pad_ref[SUBLANE:SUBLANE + n, :] = a_ref[0, off:off + n, :]
        pad_ref[SUBLANE + n:2 * SUBLANE + n, :] = zeros

        def chunk(ci, carry, off=off, n=n):
            r0 = pl.multiple_of(ci * ROW_CHUNK, ROW_CHUNK)
            win = pad_ref[pl.ds(r0, ROW_CHUNK + 2 * SUBLANE), :]
            sh = lambda s: win[SUBLANE + s:SUBLANE + s + ROW_CHUNK, :]
            a0 = sh(0)
            s2 = sh(-1) + a0
            s4 = s2 + sh(-2) + sh(1)
            s8 = s4 + sh(-4) + sh(-3) + sh(2) + sh(3)
            s16 = s8 + (sh(-8) + sh(-7) + sh(-6) + sh(-5)) + (sh(4) + sh(5) + sh(6) + sh(7))
            t = r0 + lax.broadcasted_iota(jnp.int32, (ROW_CHUNK, 1), 0)

            def mean(s, w):
                cnt = jnp.minimum(t + (w - w // 2), n) - jnp.maximum(t - w // 2, 0)
                return s / cnt.astype(F32)

            pooled = jnp.where(grp == 0, mean(s2, 2), jnp.where(grp == 1, mean(s4, 4),
                               jnp.where(grp == 2, mean(s8, 8), mean(s16, 16)))) - a0
            o_ref[0, pl.ds(off + r0, ROW_CHUNK), :] = _dot_hi(pooled, wbd_ref[...]) * scale_ref[...]
            return carry

        lax.fori_loop(0, n // ROW_CHUNK, chunk, 0)


def _pool(a, wbd, scale, n_ctx):
    b, l, c = a.shape
    segments = ((0, n_ctx), (n_ctx, l - n_ctx))
    return pl.pallas_call(
        functools.partial(_pool_kernel, segments=segments),
        grid=(b,),
        in_specs=[pl.BlockSpec((1, l, c), lambda i: (i, 0, 0)),
                  pl.BlockSpec((c, c), lambda i: (0, 0)),
                  pl.BlockSpec((1, c), lambda i: (0, 0))],
        out_specs=pl.BlockSpec((1, l, c), lambda i: (i, 0, 0)),
        out_shape=jax.ShapeDtypeStruct((b, l, c), F32),
        scratch_shapes=[pltpu.VMEM((l - n_ctx + 2 * SUBLANE, c), F32)],
        compiler_params=_cparams("parallel"), name="pool",
    )(a, wbd, scale)


def _gdn_kernel(q_ref, k_ref, v_ref, z_ref, ba_ref, cwq_ref, cwk_ref, cwv_ref, gp_ref, gain_ref, o_ref,
                pad_ref, yq_ref, yk_ref, yv_ref, bg_ref, of_ref, ob_ref, s_ref,
                qe_ref, mn_ref, oc_ref, nn_ref, egl_ref, *, n_ctx):
    l = q_ref.shape[1]
    n_x = l - n_ctx
    hd = HEAD_DIM
    lane = lax.broadcasted_iota(jnp.int32, (1, LANE), 1)
    lo = lane < hd
    zeros = jnp.zeros((SUBLANE, LANE), F32)
    x_off = 2 * SUBLANE + n_ctx

    def head_scale(ss):
        s_lo = jnp.sum(jnp.where(lo, ss, 0.0), axis=-1, keepdims=True)
        s_hi = jnp.sum(jnp.where(lo, 0.0, ss), axis=-1, keepdims=True)
        return jnp.where(lo, s_lo, s_hi)

    def conv(u_ref, cw_ref, y_ref, post):
        pad_ref[0:SUBLANE, :] = zeros
        pad_ref[SUBLANE:SUBLANE + n_ctx, :] = u_ref[0, 0:n_ctx, :]
        pad_ref[SUBLANE + n_ctx:x_off, :] = zeros
        pad_ref[x_off:x_off + n_x, :] = u_ref[0, n_ctx:l, :]
        pad_ref[x_off + n_x:x_off + n_x + SUBLANE, :] = zeros
        cw = cw_ref[...]
        for poff, yoff, n in ((SUBLANE, 0, n_ctx), (x_off, n_ctx, n_x)):
            def chunk(ci, carry, poff=poff, yoff=yoff):
                r0 = pl.multiple_of(ci * ROW_CHUNK, ROW_CHUNK)
                win = pad_ref[pl.ds(poff - SUBLANE + r0, ROW_CHUNK + 2 * SUBLANE), :]
                acc = win[SUBLANE - 2:SUBLANE - 2 + ROW_CHUNK, :] * cw[0:1, :]
                for j in range(1, CONV_K):
                    acc = acc + win[SUBLANE - 2 + j:SUBLANE - 2 + j + ROW_CHUNK, :] * cw[j:j + 1, :]
                y_ref[pl.ds(yoff + r0, ROW_CHUNK), :] = post(_silu(acc))
                return carry
            lax.fori_loop(0, n // ROW_CHUNK, chunk, 0)

    l2 = lambda y: y * lax.rsqrt(head_scale(y * y) + EPS)
    conv(q_ref, cwq_ref, yq_ref, lambda y: l2(y) * (hd ** -0.5))
    conv(k_ref, cwk_ref, yk_ref, l2)
    conv(v_ref, cwv_ref, yv_ref, lambda y: y)

    ba = ba_ref[0]
    g = -jnp.exp(gp_ref[0, 0:1, :]) * _softplus(ba + gp_ref[0, 1:2, :])
    bg_ref[...] = jnp.where(lane < 4, jax.nn.sigmoid(ba), g)

    ck = GDN_CHUNK
    rr = lax.broadcasted_iota(jnp.int32, (ck, ck), 0)
    cc = lax.broadcasted_iota(jnp.int32, (ck, ck), 1)
    eye = (rr == cc).astype(F32)
    tril = (rr >= cc).astype(F32)
    n_chunks = l // ck
    nc_ctx = n_ctx // ck

    def phase1(ci, carry):
        probs = []
        for gi in range(GDN_GROUP):
            c = ci * GDN_GROUP + gi
            r0 = pl.multiple_of(c * ck, ck)
            qc = yq_ref[pl.ds(r0, ck), :]
            kc = yk_ref[pl.ds(r0, ck), :]
            vc = yv_ref[pl.ds(r0, ck), :]
            bgc = bg_ref[pl.ds(r0, ck), :]
            gcf = _dot_hi(tril, bgc)
            gcb = gcf[ck - 1:ck, :] - gcf + bgc
            gct = (gcf.T, gcb.T)
            for j in range(2):
                qh = qc[:, j * hd:(j + 1) * hd]
                kh = kc[:, j * hd:(j + 1) * hd]
                vh = vc[:, j * hd:(j + 1) * hd]
                kk = _dot_nt(kh, kh)
                qk = _dot_nt(qh, kh)
                for d in range(2):
                    mask = (rr >= cc) if d == 0 else (rr <= cc)
                    smask = (rr > cc) if d == 0 else (rr < cc)
                    lg, lb = 4 + 2 * d + j, 2 * d + j
                    gcol = jnp.broadcast_to((gcf, gcb)[d][:, lg:lg + 1], (ck, hd))
                    grow = gct[d][lg:lg + 1, :]
                    bcol = jnp.broadcast_to(bgc[:, lb:lb + 1], (ck, hd))
                    decay = jnp.where(mask, jnp.exp(jnp.where(mask, gcol - grow, 0.0)), 0.0)
                    xp = jnp.where(smask, -(bcol * kk * decay), 0.0)
                    eg = jnp.exp(gcol)
                    last = ck - 1 if d == 0 else 0
                    glast = gcol[last:last + 1, :]
                    probs.append(dict(
                        idx=c * 4 + 2 * d + j, xp=xp, inv=eye + xp, attn=qk * decay, qd=qh * eg,
                        rhs=jnp.concatenate([vh * bcol, kh * bcol * eg], axis=1),
                        kd=kh * jnp.exp(glast - gcol), egl=jnp.exp(glast)))
        for _ in range(5):
            for p in probs:
                p["xp"] = _dot(p["xp"], p["xp"])
            for p in probs:
                p["inv"] = p["inv"] + _dot(p["inv"], p["xp"])
        for p in probs:
            p["sol"] = _dot(p["inv"], p["rhs"])
        for p in probs:
            p["as"] = _dot(p["attn"], p["sol"])
        for p in probs:
            p["ks"] = _dot_tn(p["kd"], p["sol"])
        for p in probs:
            i = p["idx"]
            qe_ref[i] = (p["qd"] - p["as"][:, hd:]).astype(BF16)
            oc_ref[i] = p["as"][:, :hd]
            mn_ref[i] = p["ks"][:, hd:].astype(BF16)
            nn_ref[i] = p["ks"][:, :hd]
            egl_ref[i] = jnp.broadcast_to(p["egl"], (ck, hd))
        return carry

    lax.fori_loop(0, n_chunks // GDN_GROUP, phase1, 0)

    s_ref[...] = jnp.zeros(s_ref.shape, F32)

    def phase2(i, carry):
        cb = jnp.where(i < nc_ctx, nc_ctx - 1 - i, n_chunks - 1 - (i - nc_ctx))
        work = [(d, j, cidx * 4 + 2 * d + j) for d, cidx in ((0, i), (1, cb)) for j in range(2)]
        states = [s_ref[2 * d + j] for d, j, _ in work]
        outs = [_dot(qe_ref[idx], s) + oc_ref[idx] for (_, _, idx), s in zip(work, states)]
        upd = [_dot(mn_ref[idx], s) for (_, _, idx), s in zip(work, states)]
        for (d, j, idx), s, m in zip(work, states, upd):
            s_ref[2 * d + j] = egl_ref[idx] * s - m + nn_ref[idx]
        of_ref[pl.ds(pl.multiple_of(i * ck, ck), ck), :] = jnp.concatenate(outs[0:2], axis=1)
        ob_ref[pl.ds(pl.multiple_of(cb * ck, ck), ck), :] = jnp.concatenate(outs[2:4], axis=1)
        return carry

    lax.fori_loop(0, n_chunks, phase2, 0)

    def finish(ci, carry):
        r0 = pl.multiple_of(ci * ROW_CHUNK, ROW_CHUNK)
        o = of_ref[pl.ds(r0, ROW_CHUNK), :] + ob_ref[pl.ds(r0, ROW_CHUNK), :]
        o = o * lax.rsqrt(head_scale(o * o) * (1.0 / hd) + EPS) * gain_ref[...]
        o_ref[0, pl.ds(r0, ROW_CHUNK), :] = o * _silu(z_ref[0, pl.ds(r0, ROW_CHUNK), :])
        return carry

    lax.fori_loop(0, l // ROW_CHUNK, finish, 0)


def _gdn(qkv, z, ba, conv_w, gparams, gain2, n_ctx):
    b, l, w3 = qkv.shape
    npair = w3 // (3 * LANE)
    seq = lambda off: pl.BlockSpec((1, l, LANE), lambda i, p: (i, 0, p + off))
    cw = lambda off: pl.BlockSpec((CONV_K, LANE), lambda i, p: (0, p + off))
    return pl.pallas_call(
        functools.partial(_gdn_kernel, n_ctx=n_ctx),
        grid=(b, npair),
        in_specs=[seq(0), seq(npair), seq(2 * npair), seq(0), seq(0),
                  cw(0), cw(npair), cw(2 * npair),
                  pl.BlockSpec((1, SUBLANE, LANE), lambda i, p: (p, 0, 0)),
                  pl.BlockSpec((1, LANE), lambda i, p: (0, 0))],
        out_specs=seq(0),
        out_shape=jax.ShapeDtypeStruct((b, l, npair * LANE), F32),
        scratch_shapes=[pltpu.VMEM((l + 3 * SUBLANE, LANE), F32)] + [pltpu.VMEM((l, LANE), F32)] * 6
        + [pltpu.VMEM((4, HEAD_DIM, HEAD_DIM), F32)]
        + [pltpu.VMEM((4 * l // GDN_CHUNK, GDN_CHUNK, HEAD_DIM), dt) for dt in (BF16, BF16, F32, F32, F32)],
        compiler_params=_cparams("parallel", "parallel"), name="gdn",
    )(qkv, qkv, qkv, z, ba, conv_w, conv_w, conv_w, gparams, gain2)


def _rope(x, cos, sin):
    w = x.shape[-1]
    reps = w // cos.shape[-1]
    if reps > 1:
        cos = jnp.concatenate([cos] * reps, axis=1)
        sin = jnp.concatenate([sin] * reps, axis=1)
    first = (lax.broadcasted_iota(jnp.int32, (1, w), 1) % (2 * ROPE_PAIRS)) < ROPE_PAIRS
    partner = jnp.where(first, pltpu.roll(x, w - ROPE_PAIRS, 1), pltpu.roll(x, ROPE_PAIRS, 1))
    return x * cos + partner * sin


def _attend(q, k, v, valid, sink_col):
    s = _dot_nt(q, k)
    if valid is not None:
        s = jnp.where(valid, s, NEG_INF)
    m = jnp.maximum(jnp.max(s, axis=-1, keepdims=True), sink_col)
    p = jnp.exp(s - m)
    den = jnp.sum(p, axis=-1, keepdims=True) + jnp.exp(sink_col - m)
    return _dot(p, v) / den


def _gqa(q, k_all, v_all, valid, sink_ref):
    nq = q.shape[0]
    hd = HEAD_DIM
    group = q.shape[1] // hd // KV_HEADS
    outs = []
    for kvh in range(KV_HEADS):
        heads = [kvh * group + g for g in range(group)]
        qg = jnp.concatenate([q[:, h * hd:(h + 1) * hd] for h in heads], axis=0)
        sink_col = jnp.concatenate(
            [jnp.broadcast_to(sink_ref[0:1, h:h + 1], (nq, 1)) for h in heads], axis=0)
        o = _attend(qg, k_all[:, kvh * hd:(kvh + 1) * hd], v_all[:, kvh * hd:(kvh + 1) * hd],
                    valid, sink_col)
        outs += [o[g * nq:(g + 1) * nq, :] for g in range(group)]
    return jnp.concatenate(outs, axis=1)


def _attn_kernel(q_ref, *refs, seq):
    nkb = ATTN_QBLOCKS + 2
    k_refs, v_refs = refs[0:nkb], refs[nkb:2 * nkb]
    kc_ref, vc_ref = refs[2 * nkb:2 * nkb + 2]
    cos_refs = refs[2 * nkb + 2:3 * nkb + 2]
    sin_refs = refs[3 * nkb + 2:4 * nkb + 2]
    sink_ref, o_ref = refs[4 * nkb + 2:]
    blk = ATTN_BLOCK
    n_ctx = kc_ref.shape[1]
    group = q_ref.shape[2] // HEAD_DIM // KV_HEADS
    k_rot = [_rope(k_refs[j][0], cos_refs[j][...], sin_refs[j][...]) for j in range(nkb)]
    nk = 3 * blk + n_ctx
    qo = lax.broadcasted_iota(jnp.int32, (group * blk, 1), 0) % blk
    ko = lax.broadcasted_iota(jnp.int32, (1, nk), 1)
    in_window = jnp.abs(ko - blk - qo) <= WINDOW
    for s in range(ATTN_QBLOCKS):
        i = pl.program_id(1) * ATTN_QBLOCKS + s
        rows = slice(s * blk, (s + 1) * blk)
        q = _rope(q_ref[0, rows, :], cos_refs[s + 1][...], sin_refs[s + 1][...]) * (HEAD_DIM ** -0.5)
        k_all = jnp.concatenate(k_rot[s:s + 3] + [kc_ref[0]], axis=0)
        v_all = jnp.concatenate([v_refs[j][0] for j in range(s, s + 3)] + [vc_ref[0]], axis=0)
        kpos = (i - 1) * blk + ko
        valid = (in_window & (kpos >= 0) & (kpos < seq)) | (ko >= 3 * blk)
        o_ref[0, rows, :] = _gqa(q, k_all, v_all, valid, sink_ref)


def _attn_x(aq, ak, av, cos_t, sin_t, sink_row, n_ctx):
    b, l, wq = aq.shape
    wk = ak.shape[-1]
    blk = ATTN_BLOCK
    seq = l - n_ctx
    nb = seq // blk
    cb = n_ctx // blk
    nq = ATTN_QBLOCKS
    assert nb % nq == 0 and cb % nq == 0
    blocks = [lambda t, j=j: jnp.clip(t * nq + j - 1, 0, nb - 1) for j in range(nq + 2)]
    kv = [pl.BlockSpec((1, blk, wk), lambda i, t, f=f: (i, f(t) + cb, 0)) for f in blocks]
    tab = [pl.BlockSpec((blk, LANE), lambda i, t, f=f: (f(t), 0)) for f in blocks]
    ctx = pl.BlockSpec((1, n_ctx, wk), lambda i, t: (i, 0, 0))
    qspec = pl.BlockSpec((1, nq * blk, wq), lambda i, t: (i, t + cb // nq, 0))
    return pl.pallas_call(
        functools.partial(_attn_kernel, seq=seq),
        grid=(b, nb // nq),
        in_specs=[qspec] + kv + kv + [ctx, ctx] + tab + tab + [pl.BlockSpec((1, LANE), lambda i, t: (0, 0))],
        out_specs=pl.BlockSpec((1, nq * blk, wq), lambda i, t: (i, t, 0)),
        out_shape=jax.ShapeDtypeStruct((b, seq, wq), F32),
        compiler_params=_cparams("parallel", "parallel"), name="attn_x",
    )(aq, *([ak] * (nq + 2)), *([av] * (nq + 2)), ak, av, *([cos_t] * (nq + 2)), *([sin_t] * (nq + 2)),
      sink_row)


def _attn_ctx_kernel(q_ref, k_ref, v_ref, sink_ref, o_ref):
    o_ref[0] = _gqa(q_ref[0] * (HEAD_DIM ** -0.5), k_ref[0], v_ref[0], None, sink_ref)


def _attn_ctx(aq, ak, av, sink_row, n_ctx):
    b, _, wq = aq.shape
    wk = ak.shape[-1]
    spec = lambda w: pl.BlockSpec((1, n_ctx, w), lambda i: (i, 0, 0))
    return pl.pallas_call(
        _attn_ctx_kernel, grid=(b,),
        in_specs=[spec(wq), spec(wk), spec(wk), pl.BlockSpec((1, LANE), lambda i: (0, 0))],
        out_specs=spec(wq), out_shape=jax.ShapeDtypeStruct((b, n_ctx, wq), F32),
        compiler_params=_cparams("parallel"), name="attn_ctx",
    )(aq, ak, av, sink_row)


def _outproj_kernel(x_ref, ya_ref, yb_ref, yc_ref, w_ref, mod_ref, gain_ref, xo_ref, hx_ref, *, row_off, n_ctx):
    wa, wb = ya_ref.shape[-1], yb_ref.shape[-1]
    tile = x_ref.shape[1]
    acc = (jnp.dot(ya_ref[0].astype(BF16), w_ref[0:wa, :], preferred_element_type=F32)
           + jnp.dot(yb_ref[0].astype(BF16), w_ref[wa:wa + wb, :], preferred_element_type=F32)
           + jnp.dot(yc_ref[0].astype(BF16), w_ref[wa + wb:, :], preferred_element_type=F32))
    mod = _row_mod(mod_ref, row_off + pl.program_id(1) * tile, tile, n_ctx)
    x = x_ref[0] + mod(2) * acc
    xo_ref[0] = x
    hx_ref[0] = _modnorm(x, gain_ref[...], mod(4), mod(3)).astype(BF16)


def _out_proj(h, ya, yb, yc, w, mod, gain, n_ctx, skip_ctx):
    b, l, d = h.shape
    tile = TOK_TILE if skip_ctx else PROJ_TILE
    off = n_ctx // tile if skip_ctx else 0
    nt = l // tile - off
    full = lambda wd: pl.BlockSpec((1, tile, wd), lambda i, t: (i, t + off, 0))
    own = lambda wd: pl.BlockSpec((1, tile, wd), lambda i, t: (i, t, 0))
    return pl.pallas_call(
        functools.partial(_outproj_kernel, row_off=off * tile, n_ctx=n_ctx), grid=(b, nt),
        in_specs=[full(d), full(ya.shape[-1]), full(yb.shape[-1]), own(yc.shape[-1]),
                  pl.BlockSpec(w.shape, lambda i, t: (0, 0)),
                  pl.BlockSpec((1, 2, 6, d), lambda i, t: (i, 0, 0, 0)),
                  pl.BlockSpec((1, d), lambda i, t: (0, 0))],
        out_specs=[own(d), own(d)],
        out_shape=[jax.ShapeDtypeStruct((b, nt * tile, d), F32),
                   jax.ShapeDtypeStruct((b, nt * tile, d), BF16)],
        compiler_params=_cparams("parallel", "parallel"), name="out_proj",
    )(h, ya, yb, yc, w, mod, gain)


def _sort_pairs(n):
    pairs, p = [], 1
    while p < n:
        k = p
        while k >= 1:
            for j in range(k % p, n - k, 2 * k):
                for i in range(min(k, n - j - k)):
                    if (i + j) // (2 * p) == (i + j + k) // (2 * p):
                        pairs.append((i + j, i + j + k))
            k //= 2
        p *= 2
    return pairs


def _topk_rows(v):
    n = len(v)
    v = list(v)

    def cmpx(i, j):
        v[i], v[j] = jnp.maximum(v[i], v[j]), jnp.minimum(v[i], v[j])

    for i, j in _sort_pairs(n):
        cmpx(i, j)
    shift = SUBLANE // 2
    while shift >= 1:
        partner = [pltpu.roll(x, shift, 0) for x in v]
        v = [jnp.maximum(v[j], partner[n - 1 - j]) for j in range(n)]
        stride = n // 2
        while stride >= 1:
            for i in range(n):
                if not i & stride:
                    cmpx(i, i + stride)
            stride //= 2
        shift //= 2
    return v


def _route_kernel(ht_ref, wq_ref, keys_ref, r2_ref, e2_ref, n1_ref, e1_ref, qt_ref, top_ref, cand_ref):
    tm = ht_ref.shape[1]
    k = PEER_TOPK
    nk = N_KEYS
    neg = jnp.float32(-jnp.inf)
    qt_ref[...] = _dot(wq_ref[...], ht_ref[...]).astype(BF16)
    cand_ref[_N_CAND:_CAND_PAD, :] = jnp.full((_CAND_PAD - _N_CAND, tm), neg, F32)
    for h in range(PEER_HEADS):
        st = []
        for p in range(2):
            hp = 2 * h + p
            s = jnp.dot(keys_ref[hp], qt_ref[hp * nk:(hp + 1) * nk, :], preferred_element_type=F32)
            st.append(s)
            top = _topk_rows([s[g * SUBLANE:(g + 1) * SUBLANE, :] for g in range(nk // SUBLANE)])
            for r in range(k):
                top_ref[p * k + r:p * k + r + 1, :] = top[r][0:1, :]
        v1 = top_ref[0:k, :]
        v2 = top_ref[k:2 * k, :]
        rank2 = jnp.full((nk, tm), float(k), F32)
        for r in range(k):
            rank2 = jnp.where(st[1] == v2[r:r + 1, :], float(r), rank2)
        row = 0
        for a, nb in enumerate(_CAND_ROWS):
            cand_ref[row:row + nb, :] = v1[a:a + 1, :] + v2[0:nb, :]
            row += nb
        cur = cand_ref[...]
        cmax = thr = zsum = None
        for r in range(k):
            m = jnp.max(cur, axis=0, keepdims=True)
            if r == 0:
                cmax, zsum = m, jnp.ones_like(m)
            else:
                zsum = zsum + jnp.exp(m - cmax)
            thr = m
            cur = jnp.where(cur == m, neg, cur)
        n1 = jnp.zeros((nk, tm), F32)
        row = 0
        for a, nb in enumerate(_CAND_ROWS):
            sel = cand_ref[row:row + nb, :] >= thr
            n_a = jnp.sum(sel.astype(F32), axis=0, keepdims=True)
            n1 = jnp.where(st[0] == v1[a:a + 1, :], n_a, n1)
            row += nb
        r2_ref[h] = rank2.T
        e2_ref[h] = jnp.exp(st[1] - v2[0:1, :]).T
        n1_ref[h] = n1.T
        e1_ref[h] = (jnp.exp(st[0] - v1[0:1, :]) / zsum).T


def _peer_route(hxt, wq_t, keys):
    d, t = hxt.shape
    nq = wq_t.shape[0]
    tm = PEER_ROUTE_TILE
    rows = PEER_HEADS * N_KEYS
    return pl.pallas_call(
        _route_kernel, grid=(t // tm,),
        in_specs=[pl.BlockSpec((d, tm), lambda i: (0, i)),
                  pl.BlockSpec(wq_t.shape, lambda i: (0, 0)),
                  pl.BlockSpec(keys.shape, lambda i: (0, 0, 0))],
        out_specs=[pl.BlockSpec((PEER_HEADS, tm, N_KEYS), lambda i: (0, i, 0))] * 4,
        out_shape=[jax.ShapeDtypeStruct((PEER_HEADS, t, N_KEYS), F32)] * 4,
        scratch_shapes=[pltpu.VMEM((nq, tm), BF16), pltpu.VMEM((2 * PEER_TOPK, tm), F32),
                        pltpu.VMEM((_CAND_PAD, tm), F32)],
        compiler_params=_cparams("parallel"), name="peer_route",
    )(hxt, wq_t, keys)


def _gates_kernel(r2t_ref, e2t_ref, n1t_ref, e1t_ref, w_ref):
    tb = w_ref.shape[1]
    nk = N_KEYS
    k = PEER_TOPK
    rank = lax.broadcasted_iota(jnp.int32, (k, nk), 0).astype(F32)

    def tok(i, carry):
        for j in range(PEER_GATE_UNROLL):
            t = i * PEER_GATE_UNROLL + j
            a, b = [], []
            for h in range(PEER_HEADS):
                r2, e2, n1, e1 = (ref[h, pl.ds(t, k, stride=0), :]
                                  for ref in (r2t_ref, e2t_ref, n1t_ref, e1t_ref))
                a.append(jnp.where(n1 > rank, e1, 0.0))
                b.append(jnp.where(r2 == rank, e2, 0.0))
            w = _dot_tn(jnp.concatenate(a, axis=0), jnp.concatenate(b, axis=0))
            for g in range(nk // (2 * SUBLANE)):
                lo, hi = 2 * g * SUBLANE, (2 * g + 1) * SUBLANE
                w_ref[g, t] = pltpu.pack_elementwise([w[lo:lo + SUBLANE, :], w[hi:hi + SUBLANE, :]],
                                                     packed_dtype=jnp.bfloat16)
        return carry

    lax.fori_loop(0, tb // PEER_GATE_UNROLL, tok, 0)


def _peer_gates(route):
    t = route[0].shape[1]
    tb = PEER_GATE_TILE
    return pl.pallas_call(
        _gates_kernel, grid=(t // tb,),
        in_specs=[pl.BlockSpec((PEER_HEADS, tb, N_KEYS), lambda i: (0, i, 0))] * 4,
        out_specs=pl.BlockSpec((N_KEYS // (2 * SUBLANE), tb, SUBLANE, N_KEYS), lambda i: (0, i, 0, 0)),
        out_shape=jax.ShapeDtypeStruct((N_KEYS // (2 * SUBLANE), t, SUBLANE, N_KEYS), jnp.uint32),
        compiler_params=_cparams("parallel"), name="peer_gates",
    )(*route)


def _peer_kernel(h_ref, w_ref, ut_ref, v_ref, o_ref, acc_ref, g0_ref, g1_ref):
    e = pl.program_id(1)
    n_blk = pl.num_programs(1) - 1

    @pl.when(e == 0)
    def _():
        acc_ref[...] = jnp.zeros(acc_ref.shape, F32)
        g1_ref[...] = jnp.zeros(g1_ref.shape, BF16)

    for parity, g_w_ref, g_r_ref in ((0, g0_ref, g1_ref), (1, g1_ref, g0_ref)):
        for build in (True, False):
            @pl.when((e % 2 == parity) & ((e < n_blk) if build else (e == n_blk)))
            def _(g_w_ref=g_w_ref, g_r_ref=g_r_ref, build=build, parity=parity):
                _peer_step(h_ref, w_ref, ut_ref, v_ref, acc_ref, g_w_ref, g_r_ref, build, parity)

    @pl.when(e == n_blk)
    def _():
        o_ref[...] = acc_ref[...]


def _peer_step(h_ref, w_ref, ut_ref, v_ref, acc_ref, g_w_ref, g_r_ref, build, half):
    sr = h_ref.shape[0] // PEER_SUB

    def sub(k, carry):
        rows = pl.ds(pl.multiple_of(k * sr, sr), sr)
        acc_ref[rows, :] += jnp.dot(g_r_ref[rows, :], v_ref[...].astype(BF16), preferred_element_type=F32)
        if not build:
            return carry
        act = jnp.dot(h_ref[rows, :], ut_ref[0], preferred_element_type=F32)
        act = 0.5 * act * (1.0 + lax.erf(act * 0.7071067811865476))
        for j in range(SUBLANE):
            ln = slice(j * N_KEYS, (j + 1) * N_KEYS)
            wj = pltpu.unpack_elementwise(w_ref[pl.ds(k * (sr * SUBLANE) + j, sr, stride=SUBLANE), :],
                                          index=half, packed_dtype=jnp.bfloat16, unpacked_dtype=F32)
            g_w_ref[rows, ln] = (wj * act[:, ln]).astype(BF16)
        return carry

    lax.fori_loop(0, PEER_SUB, sub, 0)


def _peer_dense(hx, w, ut, v_all, layer):
    t, d = hx.shape
    n_blk, _, eb = ut.shape
    tm = PEER_TOK_TILE
    last = n_blk - 1
    return pl.pallas_call(
        _peer_kernel, grid=(t // tm, n_blk + 1),
        in_specs=[pl.BlockSpec((tm, d), lambda i, e: (i, 0)),
                  pl.BlockSpec((tm * SUBLANE, N_KEYS),
                               lambda i, e: (jnp.minimum(e, last) // 2 * (t // tm) + i, 0)),
                  pl.BlockSpec((1, d, eb), lambda i, e: (jnp.minimum(e, last), 0, 0)),
                  pl.BlockSpec((None, eb, d), lambda i, e: (layer, jnp.maximum(e - 1, 0), 0))],
        out_specs=pl.BlockSpec((tm, d), lambda i, e: (i, 0)),
        out_shape=jax.ShapeDtypeStruct((t, d), F32),
        scratch_shapes=[pltpu.VMEM((tm, d), F32), pltpu.VMEM((tm, eb), BF16), pltpu.VMEM((tm, eb), BF16)],
        compiler_params=_cparams("parallel", "arbitrary"), name="peer_dense",
    )(hx, w, ut, v_all)


def _final_kernel(x_ref, f_ref, mod_ref, gain_ref, o_ref):
    x = x_ref[0] + mod_ref[0, 0, 5:6, :] * f_ref[0]
    o_ref[0] = x * lax.rsqrt(jnp.mean(x * x, axis=-1, keepdims=True) + EPS) * gain_ref[...]


def _final(x, ffn, mod, gain):
    b, s, d = x.shape
    tok = pl.BlockSpec((1, TOK_TILE, d), lambda i, t: (i, t, 0))
    return pl.pallas_call(
        _final_kernel, grid=(b, s // TOK_TILE),
        in_specs=[tok, tok, pl.BlockSpec((1, 1, 6, d), lambda i, t: (i, 1, 0, 0)),
                  pl.BlockSpec((1, d), lambda i, t: (0, 0))],
        out_specs=tok, out_shape=jax.ShapeDtypeStruct((b, s, d), F32),
        compiler_params=_cparams("parallel", "parallel"), name="final_norm",
    )(x, ffn, mod, gain)


def _rope_tables(seq):
    pos = jnp.arange(seq)
    rc = jnp.stack([pos // GRID_W, pos % GRID_W], axis=-1).astype(F32)
    inv = jnp.power(ROPE_BASE, -jnp.arange(ROPE_PAIRS, dtype=F32) / ROPE_PAIRS)
    ang = rc[:, :, None] * inv
    cos = jnp.concatenate([jnp.cos(ang)] * 2, axis=-1).reshape(seq, HEAD_DIM)
    sin = jnp.concatenate([-jnp.sin(ang), jnp.sin(ang)], axis=-1).reshape(seq, HEAD_DIM)
    return jnp.concatenate([cos] * 2, axis=1), jnp.concatenate([sin] * 2, axis=1)


def _pair_lanes(cols, n_heads):
    rows = cols.shape[0]
    c = cols.reshape(rows, 4, n_heads // 2, 2).transpose(0, 2, 1, 3).reshape(rows, n_heads // 2, 8)
    return jnp.pad(c, ((0, 0), (0, 0), (0, LANE - 8))).reshape(rows, n_heads // 2 * LANE)


def kernel(x, c, ctx, c_ctx, w_ada, b_ada, norm1, norm2, w_in, conv_w, a_log, dt_bias, gdn_norm, w_pool,
           pool_scale, sink, w_out, peer_wq, peer_keys, peer_u, peer_v, norm_f):
    b, seq, d = x.shape
    n_ctx = ctx.shape[1]
    depth = w_ada.shape[0]
    pool_w = d // 4
    gdn_w = 3 * d // 8
    gdn_heads = gdn_w // HEAD_DIM
    attn_w = d - pool_w - gdn_w
    kv_w = KV_HEADS * HEAD_DIM
    assert n_ctx % TOK_TILE == 0 and seq % TOK_TILE == 0 and gdn_heads % 2 == 0
    assert (n_ctx + seq) % PROJ_TILE == 0 and PEER_EXP_TILE == SUBLANE * N_KEYS
    assert (b * (n_ctx + seq)) % PEER_TOK_TILE == 0 and (b * seq) % PEER_TOK_TILE == 0

    rows = -(-(b + 1) // SUBLANE) * SUBLANE
    c_all = jnp.zeros((rows, d), F32).at[:b].set(c).at[b].set(c_ctx)
    mod = _ada_mod(c_all, w_ada, b_ada)
    mod_x = mod[:, :b].reshape(depth, b, 1, 6, d)
    mod_c = jnp.broadcast_to(mod[:, b].reshape(depth, 1, 1, 6, d), (depth, b, 1, 6, d))
    mod = jnp.concatenate([mod_c, mod_x], axis=2)

    cos_t, sin_t = _rope_tables(seq)
    splits = (pool_w, 3 * gdn_w, gdn_w, gdn_heads // 2 * LANE, attn_w, kv_w, kv_w)
    o_ba = pool_w + 4 * gdn_w
    n_ba = 4 * gdn_heads

    h = jnp.concatenate([ctx, x], axis=1)
    ffn = None
    for l in range(depth):
        last = l == depth - 1
        w_l = w_in[l]
        w_big = jnp.concatenate([w_l[:, :o_ba], _pair_lanes(w_l[:, o_ba:o_ba + n_ba], gdn_heads),
                                 w_l[:, o_ba + n_ba:]], axis=1).astype(BF16)
        pmod = mod[l - 1] if l > 0 else None
        outs = _in_proj(h, ffn, pmod, mod[l], norm1[l][None], w_big, splits, n_ctx)
        if l > 0:
            h, *outs = outs
        pa, qkv, z, ba, aq, ak, av = outs

        wbd = jax.scipy.linalg.block_diag(*[w_pool[l, g] for g in range(len(POOL_WINDOWS))])
        ya = _pool(pa, wbd, pool_scale[l][None], n_ctx)

        gparams = jnp.stack([_pair_lanes(jnp.concatenate([jnp.zeros_like(a_log[l]), a_log[l]]).reshape(1, -1),
                                         gdn_heads),
                             _pair_lanes(jnp.concatenate([jnp.zeros_like(dt_bias[l]), dt_bias[l]]).reshape(1, -1),
                                         gdn_heads)], axis=1)
        gparams = gparams.reshape(2, gdn_heads // 2, LANE).transpose(1, 0, 2)
        gparams = jnp.pad(gparams, ((0, 0), (0, SUBLANE - 2), (0, 0)))
        yb = _gdn(qkv, z, ba, conv_w[l], gparams, jnp.concatenate([gdn_norm[l]] * 2)[None], n_ctx)

        sink_row = jnp.pad(sink[l], (0, LANE - sink.shape[1]))[None]
        yc = _attn_x(aq, ak, av, cos_t, sin_t, sink_row, n_ctx)
        if not last:
            yc = jnp.concatenate([_attn_ctx(aq, ak, av, sink_row, n_ctx), yc], axis=1)

        x_new, hx = _out_proj(h, ya, yb, yc, w_out[l].astype(BF16), mod[l], norm2[l][None], n_ctx,
                              skip_ctx=last)
        t = hx.shape[0] * hx.shape[1]
        hx = hx.reshape(t, d)
        keys = peer_keys[l].reshape(2 * PEER_HEADS, N_KEYS, -1).astype(BF16)
        route = _peer_route(hx.T, peer_wq[l].T.astype(BF16), keys)
        gates = _peer_gates(route).reshape(-1, N_KEYS)
        ut = peer_u[l].reshape(-1, PEER_EXP_TILE, d).transpose(0, 2, 1).astype(BF16)
        ffn = _peer_dense(hx, gates, ut, peer_v, l)
        ffn = ffn.reshape(x_new.shape)
        h = x_new
    return _final(h, ffn, mod[depth - 1], norm_f[None])
```

```python
import functools

import jax
import jax.numpy as jnp
from jax import lax
from jax.experimental import pallas as pl
from jax.experimental.pallas import tpu as pltpu

F32, BF16 = jnp.float32, jnp.bfloat16
HIGHEST = lax.Precision.HIGHEST

EPS = 1e-6
NEG_INF = -1e30
HEAD_DIM = 64
GRID_W = 64
POOL_WINDOWS = (2, 4, 8, 16)
GDN_CHUNK = 64
CONV_K = 5
KV_HEADS = 2
WINDOW = 128
ATTN_BLOCK = 128
ROPE_BASE = 10000.0
ROPE_PAIRS = HEAD_DIM // 4
PEER_HEADS = 8
N_KEYS = 128
PEER_TOPK = 16

LANE = 128
SUBLANE = 8
TOK_TILE = 256
PROJ_TILE = 768
ROW_CHUNK = 256
ATTN_QBLOCKS = 2
GDN_GROUP = 4
PEER_ROUTE_TILE = 512
PEER_GATE_TILE = 128
PEER_GATE_UNROLL = 64
PEER_TOK_TILE = 1024
PEER_EXP_TILE = 1024
PEER_SUB = 1
VMEM_LIMIT = 48 * 1024 * 1024

_CAND_ROWS = tuple(PEER_TOPK // (a + 1) for a in range(PEER_TOPK))
_N_CAND = sum(_CAND_ROWS)
_CAND_PAD = -(-_N_CAND // SUBLANE) * SUBLANE


def _cparams(*sem):
    return pltpu.CompilerParams(dimension_semantics=sem, vmem_limit_bytes=VMEM_LIMIT)


def _dot(a, b):
    return jnp.dot(a.astype(BF16), b.astype(BF16), preferred_element_type=F32)


def _dot_nt(a, b):
    return lax.dot_general(a.astype(BF16), b.astype(BF16), (((1,), (1,)), ((), ())),
                           preferred_element_type=F32)


def _dot_tn(a, b):
    return lax.dot_general(a.astype(BF16), b.astype(BF16), (((0,), (0,)), ((), ())),
                           preferred_element_type=F32)


def _dot_hi(a, b):
    return jnp.dot(a, b, precision=HIGHEST, preferred_element_type=F32)


def _silu(x):
    return x * jax.nn.sigmoid(x)


def _softplus(x):
    return jnp.maximum(x, 0.0) + jnp.log1p(jnp.exp(-jnp.abs(x)))


def _modnorm(x, gain, scale, shift):
    y = x * lax.rsqrt(jnp.mean(x * x, axis=-1, keepdims=True) + EPS) * gain
    return y * (1.0 + scale) + shift


def _ada_kernel(c_ref, w_ref, b_ref, o_ref):
    o_ref[0] = _dot_hi(_silu(c_ref[...]), w_ref[0]) + b_ref[0]


def _ada_mod(c_all, w_ada, b_ada):
    depth, d, n = w_ada.shape
    rows = c_all.shape[0]
    tn = n // 4
    return pl.pallas_call(
        _ada_kernel,
        grid=(depth, n // tn),
        in_specs=[pl.BlockSpec((rows, d), lambda l, j: (0, 0)),
                  pl.BlockSpec((1, d, tn), lambda l, j: (l, 0, j)),
                  pl.BlockSpec((1, 1, tn), lambda l, j: (l, 0, j))],
        out_specs=pl.BlockSpec((1, rows, tn), lambda l, j: (l, 0, j)),
        out_shape=jax.ShapeDtypeStruct((depth, rows, n), F32),
        compiler_params=_cparams("parallel", "parallel"),
        name="ada_mod",
    )(c_all, w_ada, b_ada.reshape(depth, 1, n))


def _row_mod(mod_ref, first_row, n_rows, n_ctx):
    is_x = first_row + lax.broadcasted_iota(jnp.int32, (n_rows, 1), 0) >= n_ctx
    return lambda k: jnp.where(is_x, mod_ref[0, 1, k:k + 1, :], mod_ref[0, 0, k:k + 1, :])


def _inproj_kernel(*refs, has_ffn, col_splits, n_ctx):
    tile = refs[0].shape[1]
    first_row = pl.program_id(1) * tile
    if has_ffn:
        h_ref, f_ref, pmod_ref, mod_ref, gain_ref, w_ref, res_ref, *outs = refs
        x = h_ref[0] + _row_mod(pmod_ref, first_row, tile, n_ctx)(5) * f_ref[0]
        res_ref[0] = x
    else:
        h_ref, mod_ref, gain_ref, w_ref, *outs = refs
        x = h_ref[0]
    mod = _row_mod(mod_ref, first_row, tile, n_ctx)
    hb = _modnorm(x, gain_ref[...], mod(1), mod(0)).astype(BF16)
    off = 0
    for o_ref, width in zip(outs, col_splits):
        o_ref[0] = jnp.dot(hb, w_ref[:, off:off + width], preferred_element_type=F32)
        off += width


def _in_proj(h, ffn, pmod, mod, gain, w, col_splits, n_ctx):
    b, l, d = h.shape
    tile = PROJ_TILE
    tok = pl.BlockSpec((1, tile, d), lambda i, t: (i, t, 0))
    modspec = pl.BlockSpec((1, 2, 6, d), lambda i, t: (i, 0, 0, 0))
    has_ffn = ffn is not None
    in_specs = [tok] + ([tok, modspec] if has_ffn else []) + [
        modspec, pl.BlockSpec((1, d), lambda i, t: (0, 0)), pl.BlockSpec(w.shape, lambda i, t: (0, 0))]
    out_shape = [jax.ShapeDtypeStruct((b, l, wd), F32) for wd in col_splits]
    out_specs = [pl.BlockSpec((1, tile, wd), lambda i, t: (i, t, 0)) for wd in col_splits]
    if has_ffn:
        out_shape = [jax.ShapeDtypeStruct((b, l, d), F32)] + out_shape
        out_specs = [tok] + out_specs
    args = (h, ffn, pmod, mod, gain, w) if has_ffn else (h, mod, gain, w)
    return pl.pallas_call(
        functools.partial(_inproj_kernel, has_ffn=has_ffn, col_splits=col_splits, n_ctx=n_ctx),
        grid=(b, l // tile), in_specs=in_specs, out_specs=out_specs, out_shape=out_shape,
        compiler_params=_cparams("parallel", "parallel"), name="in_proj",
    )(*args)


def _pool_kernel(a_ref, wbd_ref, scale_ref, o_ref, pad_ref, *, segments):
    c = a_ref.shape[-1]
    grp = lax.broadcasted_iota(jnp.int32, (1, c), 1) // (c // len(POOL_WINDOWS))
    zeros = jnp.zeros((SUBLANE, c), F32)
    for off, n in segments:
        pad_ref[0:SUBLANE, :] = zeros
        pad_ref[SUBLANE:SUBLANE + n, :] = a_ref[0, off:off + n, :]
        pad_ref[SUBLANE + n:2 * SUBLANE + n, :] = zeros

        def chunk(ci, carry, off=off, n=n):
            r0 = pl.multiple_of(ci * ROW_CHUNK, ROW_CHUNK)
            win = pad_ref[pl.ds(r0, ROW_CHUNK + 2 * SUBLANE), :]
            sh = lambda s: win[SUBLANE + s:SUBLANE + s + ROW_CHUNK, :]
            a0 = sh(0)
            s2 = sh(-1) + a0
            s4 = s2 + sh(-2) + sh(1)
            s8 = s4 + sh(-4) + sh(-3) + sh(2) + sh(3)
            s16 = s8 + (sh(-8) + sh(-7) + sh(-6) + sh(-5)) + (sh(4) + sh(5) + sh(6) + sh(7))
            t = r0 + lax.broadcasted_iota(jnp.int32, (ROW_CHUNK, 1), 0)

            def mean(s, w):
                cnt = jnp.minimum(t + (w - w // 2), n) - jnp.maximum(t - w // 2, 0)
                return s / cnt.astype(F32)

            pooled = jnp.where(grp == 0, mean(s2, 2), jnp.where(grp == 1, mean(s4, 4),
                               jnp.where(grp == 2, mean(s8, 8), mean(s16, 16)))) - a0
            o_ref[0, pl.ds(off + r0, ROW_CHUNK), :] = _dot_hi(pooled, wbd_ref[...]) * scale_ref[...]
            return carry

        lax.fori_loop(0, n // ROW_CHUNK, chunk, 0)


def _pool(a, wbd, scale, n_ctx):
    b, l, c = a.shape
    segments = ((0, n_ctx), (n_ctx, l - n_ctx))
    return pl.pallas_call(
        functools.partial(_pool_kernel, segments=segments),
        grid=(b,),
        in_specs=[pl.BlockSpec((1, l, c), lambda i: (i, 0, 0)),
                  pl.BlockSpec((c, c), lambda i: (0, 0)),
                  pl.BlockSpec((1, c), lambda i: (0, 0))],
        out_specs=pl.BlockSpec((1, l, c), lambda i: (i, 0, 0)),
        out_shape=jax.ShapeDtypeStruct((b, l, c), F32),
        scratch_shapes=[pltpu.VMEM((l - n_ctx + 2 * SUBLANE, c), F32)],
        compiler_params=_cparams("parallel"), name="pool",
    )(a, wbd, scale)


def _gdn_kernel(q_ref, k_ref, v_ref, z_ref, ba_ref, cwq_ref, cwk_ref, cwv_ref, gp_ref, gain_ref, o_ref,
                pad_ref, yq_ref, yk_ref, yv_ref, bg_ref, of_ref, ob_ref, s_ref,
                qe_ref, mn_ref, oc_ref, nn_ref, egl_ref, *, n_ctx):
    l = q_ref.shape[1]
    n_x = l - n_ctx
    hd = HEAD_DIM
    lane = lax.broadcasted_iota(jnp.int32, (1, LANE), 1)
    lo = lane < hd
    zeros = jnp.zeros((SUBLANE, LANE), F32)
    x_off = 2 * SUBLANE + n_ctx

    def head_scale(ss):
        s_lo = jnp.sum(jnp.where(lo, ss, 0.0), axis=-1, keepdims=True)
        s_hi = jnp.sum(jnp.where(lo, 0.0, ss), axis=-1, keepdims=True)
        return jnp.where(lo, s_lo, s_hi)

    def conv(u_ref, cw_ref, y_ref, post):
        pad_ref[0:SUBLANE, :] = zeros
        pad_ref[SUBLANE:SUBLANE + n_ctx, :] = u_ref[0, 0:n_ctx, :]
        pad_ref[SUBLANE + n_ctx:x_off, :] = zeros
        pad_ref[x_off:x_off + n_x, :] = u_ref[0, n_ctx:l, :]
        pad_ref[x_off + n_x:x_off + n_x + SUBLANE, :] = zeros
        cw = cw_ref[...]
        for poff, yoff, n in ((SUBLANE, 0, n_ctx), (x_off, n_ctx, n_x)):
            def chunk(ci, carry, poff=poff, yoff=yoff):
                r0 = pl.multiple_of(ci * ROW_CHUNK, ROW_CHUNK)
                win = pad_ref[pl.ds(poff - SUBLANE + r0, ROW_CHUNK + 2 * SUBLANE), :]
                acc = win[SUBLANE - 2:SUBLANE - 2 + ROW_CHUNK, :] * cw[0:1, :]
                for j in range(1, CONV_K):
                    acc = acc + win[SUBLANE - 2 + j:SUBLANE - 2 + j + ROW_CHUNK, :] * cw[j:j + 1, :]
                y_ref[pl.ds(yoff + r0, ROW_CHUNK), :] = post(_silu(acc))
                return carry
            lax.fori_loop(0, n // ROW_CHUNK, chunk, 0)

    l2 = lambda y: y * lax.rsqrt(head_scale(y * y) + EPS)
    conv(q_ref, cwq_ref, yq_ref, lambda y: l2(y) * (hd ** -0.5))
    conv(k_ref, cwk_ref, yk_ref, l2)
    conv(v_ref, cwv_ref, yv_ref, lambda y: y)

    ba = ba_ref[0]
    g = -jnp.exp(gp_ref[0, 0:1, :]) * _softplus(ba + gp_ref[0, 1:2, :])
    bg_ref[...] = jnp.where(lane < 4, jax.nn.sigmoid(ba), g)

    ck = GDN_CHUNK
    rr = lax.broadcasted_iota(jnp.int32, (ck, ck), 0)
    cc = lax.broadcasted_iota(jnp.int32, (ck, ck), 1)
    eye = (rr == cc).astype(F32)
    tril = (rr >= cc).astype(F32)
    n_chunks = l // ck
    nc_ctx = n_ctx // ck

    def phase1(ci, carry):
        probs = []
        for gi in range(GDN_GROUP):
            c = ci * GDN_GROUP + gi
            r0 = pl.multiple_of(c * ck, ck)
            qc = yq_ref[pl.ds(r0, ck), :]
            kc = yk_ref[pl.ds(r0, ck), :]
            vc = yv_ref[pl.ds(r0, ck), :]
            bgc = bg_ref[pl.ds(r0, ck), :]
            gcf = _dot_hi(tril, bgc)
            gcb = gcf[ck - 1:ck, :] - gcf + bgc
            gct = (gcf.T, gcb.T)
            for j in range(2):
                qh = qc[:, j * hd:(j + 1) * hd]
                kh = kc[:, j * hd:(j + 1) * hd]
                vh = vc[:, j * hd:(j + 1) * hd]
                kk = _dot_nt(kh, kh)
                qk = _dot_nt(qh, kh)
                for d in range(2):
                    mask = (rr >= cc) if d == 0 else (rr <= cc)
                    smask = (rr > cc) if d == 0 else (rr < cc)
                    lg, lb = 4 + 2 * d + j, 2 * d + j
                    gcol = jnp.broadcast_to((gcf, gcb)[d][:, lg:lg + 1], (ck, hd))
                    grow = gct[d][lg:lg + 1, :]
                    bcol = jnp.broadcast_to(bgc[:, lb:lb + 1], (ck, hd))
                    decay = jnp.where(mask, jnp.exp(jnp.where(mask, gcol - grow, 0.0)), 0.0)
                    xp = jnp.where(smask, -(bcol * kk * decay), 0.0)
                    eg = jnp.exp(gcol)
                    last = ck - 1 if d == 0 else 0
                    glast = gcol[last:last + 1, :]
                    probs.append(dict(
                        idx=c * 4 + 2 * d + j, xp=xp, inv=eye + xp, attn=qk * decay, qd=qh * eg,
                        rhs=jnp.concatenate([vh * bcol, kh * bcol * eg], axis=1),
                        kd=kh * jnp.exp(glast - gcol), egl=jnp.exp(glast)))
        for _ in range(5):
            for p in probs:
                p["xp"] = _dot(p["xp"], p["xp"])
            for p in probs:
                p["inv"] = p["inv"] + _dot(p["inv"], p["xp"])
        for p in probs:
            p["sol"] = _dot(p["inv"], p["rhs"])
        for p in probs:
            p["as"] = _dot(p["attn"], p["sol"])
        for p in probs:
            p["ks"] = _dot_tn(p["kd"], p["sol"])
        for p in probs:
            i = p["idx"]
            qe_ref[i] = (p["qd"] - p["as"][:, hd:]).astype(BF16)
            oc_ref[i] = p["as"][:, :hd]
            mn_ref[i] = p["ks"][:, hd:].astype(BF16)
            nn_ref[i] = p["ks"][:, :hd]
            egl_ref[i] = jnp.broadcast_to(p["egl"], (ck, hd))
        return carry

    lax.fori_loop(0, n_chunks // GDN_GROUP, phase1, 0)

    s_ref[...] = jnp.zeros(s_ref.shape, F32)

    def phase2(i, carry):
        cb = jnp.where(i < nc_ctx, nc_ctx - 1 - i, n_chunks - 1 - (i - nc_ctx))
        work = [(d, j, cidx * 4 + 2 * d + j) for d, cidx in ((0, i), (1, cb)) for j in range(2)]
        states = [s_ref[2 * d + j] for d, j, _ in work]
        outs = [_dot(qe_ref[idx], s) + oc_ref[idx] for (_, _, idx), s in zip(work, states)]
        upd = [_dot(mn_ref[idx], s) for (_, _, idx), s in zip(work, states)]
        for (d, j, idx), s, m in zip(work, states, upd):
            s_ref[2 * d + j] = egl_ref[idx] * s - m + nn_ref[idx]
        of_ref[pl.ds(pl.multiple_of(i * ck, ck), ck), :] = jnp.concatenate(outs[0:2], axis=1)
        ob_ref[pl.ds(pl.multiple_of(cb * ck, ck), ck), :] = jnp.concatenate(outs[2:4], axis=1)
        return carry

    lax.fori_loop(0, n_chunks, phase2, 0)

    def finish(ci, carry):
        r0 = pl.multiple_of(ci * ROW_CHUNK, ROW_CHUNK)
        o = of_ref[pl.ds(r0, ROW_CHUNK), :] + ob_ref[pl.ds(r0, ROW_CHUNK), :]
        o = o * lax.rsqrt(head_scale(o * o) * (1.0 / hd) + EPS) * gain_ref[...]
        o_ref[0, pl.ds(r0, ROW_CHUNK), :] = o * _silu(z_ref[0, pl.ds(r0, ROW_CHUNK), :])
        return carry

    lax.fori_loop(0, l // ROW_CHUNK, finish, 0)


def _gdn(qkv, z, ba, conv_w, gparams, gain2, n_ctx):
    b, l, w3 = qkv.shape
    npair = w3 // (3 * LANE)
    seq = lambda off: pl.BlockSpec((1, l, LANE), lambda i, p: (i, 0, p + off))
    cw = lambda off: pl.BlockSpec((CONV_K, LANE), lambda i, p: (0, p + off))
    return pl.pallas_call(
        functools.partial(_gdn_kernel, n_ctx=n_ctx),
        grid=(b, npair),
        in_specs=[seq(0), seq(npair), seq(2 * npair), seq(0), seq(0),
                  cw(0), cw(npair), cw(2 * npair),
                  pl.BlockSpec((1, SUBLANE, LANE), lambda i, p: (p, 0, 0)),
                  pl.BlockSpec((1, LANE), lambda i, p: (0, 0))],
        out_specs=seq(0),
        out_shape=jax.ShapeDtypeStruct((b, l, npair * LANE), F32),
        scratch_shapes=[pltpu.VMEM((l + 3 * SUBLANE, LANE), F32)] + [pltpu.VMEM((l, LANE), F32)] * 6
        + [pltpu.VMEM((4, HEAD_DIM, HEAD_DIM), F32)]
        + [pltpu.VMEM((4 * l // GDN_CHUNK, GDN_CHUNK, HEAD_DIM), dt) for dt in (BF16, BF16, F32, F32, F32)],
        compiler_params=_cparams("parallel", "parallel"), name="gdn",
    )(qkv, qkv, qkv, z, ba, conv_w, conv_w, conv_w, gparams, gain2)


def _rope(x, cos, sin):
    w = x.shape[-1]
    reps = w // cos.shape[-1]
    if reps > 1:
        cos = jnp.concatenate([cos] * reps, axis=1)
        sin = jnp.concatenate([sin] * reps, axis=1)
    first = (lax.broadcasted_iota(jnp.int32, (1, w), 1) % (2 * ROPE_PAIRS)) < ROPE_PAIRS
    partner = jnp.where(first, pltpu.roll(x, w - ROPE_PAIRS, 1), pltpu.roll(x, ROPE_PAIRS, 1))
    return x * cos + partner * sin


def _attend(q, k, v, valid, sink_col):
    s = _dot_nt(q, k)
    if valid is not None:
        s = jnp.where(valid, s, NEG_INF)
    m = jnp.maximum(jnp.max(s, axis=-1, keepdims=True), sink_col)
    p = jnp.exp(s - m)
    den = jnp.sum(p, axis=-1, keepdims=True) + jnp.exp(sink_col - m)
    return _dot(p, v) / den


def _gqa(q, k_all, v_all, valid, sink_ref):
    nq = q.shape[0]
    hd = HEAD_DIM
    group = q.shape[1] // hd // KV_HEADS
    outs = []
    for kvh in range(KV_HEADS):
        heads = [kvh * group + g for g in range(group)]
        qg = jnp.concatenate([q[:, h * hd:(h + 1) * hd] for h in heads], axis=0)
        sink_col = jnp.concatenate(
            [jnp.broadcast_to(sink_ref[0:1, h:h + 1], (nq, 1)) for h in heads], axis=0)
        o = _attend(qg, k_all[:, kvh * hd:(kvh + 1) * hd], v_all[:, kvh * hd:(kvh + 1) * hd],
                    valid, sink_col)
        outs += [o[g * nq:(g + 1) * nq, :] for g in range(group)]
    return jnp.concatenate(outs, axis=1)


def _attn_kernel(q_ref, *refs, seq):
    nkb = ATTN_QBLOCKS + 2
    k_refs, v_refs = refs[0:nkb], refs[nkb:2 * nkb]
    kc_ref, vc_ref = refs[2 * nkb:2 * nkb + 2]
    cos_refs = refs[2 * nkb + 2:3 * nkb + 2]
    sin_refs = refs[3 * nkb + 2:4 * nkb + 2]
    sink_ref, o_ref = refs[4 * nkb + 2:]
    blk = ATTN_BLOCK
    n_ctx = kc_ref.shape[1]
    group = q_ref.shape[2] // HEAD_DIM // KV_HEADS
    k_rot = [_rope(k_refs[j][0], cos_refs[j][...], sin_refs[j][...]) for j in range(nkb)]
    nk = 3 * blk + n_ctx
    qo = lax.broadcasted_iota(jnp.int32, (group * blk, 1), 0) % blk
    ko = lax.broadcasted_iota(jnp.int32, (1, nk), 1)
    in_window = jnp.abs(ko - blk - qo) <= WINDOW
    for s in range(ATTN_QBLOCKS):
        i = pl.program_id(1) * ATTN_QBLOCKS + s
        rows = slice(s * blk, (s + 1) * blk)
        q = _rope(q_ref[0, rows, :], cos_refs[s + 1][...], sin_refs[s + 1][...]) * (HEAD_DIM ** -0.5)
        k_all = jnp.concatenate(k_rot[s:s + 3] + [kc_ref[0]], axis=0)
        v_all = jnp.concatenate([v_refs[j][0] for j in range(s, s + 3)] + [vc_ref[0]], axis=0)
        kpos = (i - 1) * blk + ko
        valid = (in_window & (kpos >= 0) & (kpos < seq)) | (ko >= 3 * blk)
        o_ref[0, rows, :] = _gqa(q, k_all, v_all, valid, sink_ref)


def _attn_x(aq, ak, av, cos_t, sin_t, sink_row, n_ctx):
    b, l, wq = aq.shape
    wk = ak.shape[-1]
    blk = ATTN_BLOCK
    seq = l - n_ctx
    nb = seq // blk
    cb = n_ctx // blk
    nq = ATTN_QBLOCKS
    assert nb % nq == 0 and cb % nq == 0
    blocks = [lambda t, j=j: jnp.clip(t * nq + j - 1, 0, nb - 1) for j in range(nq + 2)]
    kv = [pl.BlockSpec((1, blk, wk), lambda i, t, f=f: (i, f(t) + cb, 0)) for f in blocks]
    tab = [pl.BlockSpec((blk, LANE), lambda i, t, f=f: (f(t), 0)) for f in blocks]
    ctx = pl.BlockSpec((1, n_ctx, wk), lambda i, t: (i, 0, 0))
    qspec = pl.BlockSpec((1, nq * blk, wq), lambda i, t: (i, t + cb // nq, 0))
    return pl.pallas_call(
        functools.partial(_attn_kernel, seq=seq),
        grid=(b, nb // nq),
        in_specs=[qspec] + kv + kv + [ctx, ctx] + tab + tab + [pl.BlockSpec((1, LANE), lambda i, t: (0, 0))],
        out_specs=pl.BlockSpec((1, nq * blk, wq), lambda i, t: (i, t, 0)),
        out_shape=jax.ShapeDtypeStruct((b, seq, wq), F32),
        compiler_params=_cparams("parallel", "parallel"), name="attn_x",
    )(aq, *([ak] * (nq + 2)), *([av] * (nq + 2)), ak, av, *([cos_t] * (nq + 2)), *([sin_t] * (nq + 2)),
      sink_row)


def _attn_ctx_kernel(q_ref, k_ref, v_ref, sink_ref, o_ref):
    o_ref[0] = _gqa(q_ref[0] * (HEAD_DIM ** -0.5), k_ref[0], v_ref[0], None, sink_ref)


def _attn_ctx(aq, ak, av, sink_row, n_ctx):
    b, _, wq = aq.shape
    wk = ak.shape[-1]
    spec = lambda w: pl.BlockSpec((1, n_ctx, w), lambda i: (i, 0, 0))
    return pl.pallas_call(
        _attn_ctx_kernel, grid=(b,),
        in_specs=[spec(wq), spec(wk), spec(wk), pl.BlockSpec((1, LANE), lambda i: (0, 0))],
        out_specs=spec(wq), out_shape=jax.ShapeDtypeStruct((b, n_ctx, wq), F32),
        compiler_params=_cparams("parallel"), name="attn_ctx",
    )(aq, ak, av, sink_row)


def _outproj_kernel(x_ref, ya_ref, yb_ref, yc_ref, w_ref, mod_ref, gain_ref, xo_ref, hx_ref, *, row_off, n_ctx):
    wa, wb = ya_ref.shape[-1], yb_ref.shape[-1]
    tile = x_ref.shape[1]
    acc = (jnp.dot(ya_ref[0].astype(BF16), w_ref[0:wa, :], preferred_element_type=F32)
           + jnp.dot(yb_ref[0].astype(BF16), w_ref[wa:wa + wb, :], preferred_element_type=F32)
           + jnp.dot(yc_ref[0].astype(BF16), w_ref[wa + wb:, :], preferred_element_type=F32))
    mod = _row_mod(mod_ref, row_off + pl.program_id(1) * tile, tile, n_ctx)
    x = x_ref[0] + mod(2) * acc
    xo_ref[0] = x
    hx_ref[0] = _modnorm(x, gain_ref[...], mod(4), mod(3)).astype(BF16)


def _out_proj(h, ya, yb, yc, w, mod, gain, n_ctx, skip_ctx):
    b, l, d = h.shape
    tile = TOK_TILE if skip_ctx else PROJ_TILE
    off = n_ctx // tile if skip_ctx else 0
    nt = l // tile - off
    full = lambda wd: pl.BlockSpec((1, tile, wd), lambda i, t: (i, t + off, 0))
    own = lambda wd: pl.BlockSpec((1, tile, wd), lambda i, t: (i, t, 0))
    return pl.pallas_call(
        functools.partial(_outproj_kernel, row_off=off * tile, n_ctx=n_ctx), grid=(b, nt),
        in_specs=[full(d), full(ya.shape[-1]), full(yb.shape[-1]), own(yc.shape[-1]),
                  pl.BlockSpec(w.shape, lambda i, t: (0, 0)),
                  pl.BlockSpec((1, 2, 6, d), lambda i, t: (i, 0, 0, 0)),
                  pl.BlockSpec((1, d), lambda i, t: (0, 0))],
        out_specs=[own(d), own(d)],
        out_shape=[jax.ShapeDtypeStruct((b, nt * tile, d), F32),
                   jax.ShapeDtypeStruct((b, nt * tile, d), BF16)],
        compiler_params=_cparams("parallel", "parallel"), name="out_proj",
    )(h, ya, yb, yc, w, mod, gain)


def _sort_pairs(n):
    pairs, p = [], 1
    while p < n:
        k = p
        while k >= 1:
            for j in range(k % p, n - k, 2 * k):
                for i in range(min(k, n - j - k)):
                    if (i + j) // (2 * p) == (i + j + k) // (2 * p):
                        pairs.append((i + j, i + j + k))
            k //= 2
        p *= 2
    return pairs


def _topk_rows(v):
    n = len(v)
    v = list(v)

    def cmpx(i, j):
        v[i], v[j] = jnp.maximum(v[i], v[j]), jnp.minimum(v[i], v[j])

    for i, j in _sort_pairs(n):
        cmpx(i, j)
    shift = SUBLANE // 2
    while shift >= 1:
        partner = [pltpu.roll(x, shift, 0) for x in v]
        v = [jnp.maximum(v[j], partner[n - 1 - j]) for j in range(n)]
        stride = n // 2
        while stride >= 1:
            for i in range(n):
                if not i & stride:
                    cmpx(i, i + stride)
            stride //= 2
        shift //= 2
    return v


def _route_kernel(ht_ref, wq_ref, keys_ref, r2_ref, e2_ref, n1_ref, e1_ref, qt_ref, top_ref, cand_ref):
    tm = ht_ref.shape[1]
    k = PEER_TOPK
    nk = N_KEYS
    neg = jnp.float32(-jnp.inf)
    qt_ref[...] = _dot(wq_ref[...], ht_ref[...]).astype(BF16)
    cand_ref[_N_CAND:_CAND_PAD, :] = jnp.full((_CAND_PAD - _N_CAND, tm), neg, F32)
    for h in range(PEER_HEADS):
        st = []
        for p in range(2):
            hp = 2 * h + p
            s = jnp.dot(keys_ref[hp], qt_ref[hp * nk:(hp + 1) * nk, :], preferred_element_type=F32)
            st.append(s)
            top = _topk_rows([s[g * SUBLANE:(g + 1) * SUBLANE, :] for g in range(nk // SUBLANE)])
            for r in range(k):
                top_ref[p * k + r:p * k + r + 1, :] = top[r][0:1, :]
        v1 = top_ref[0:k, :]
        v2 = top_ref[k:2 * k, :]
        rank2 = jnp.full((nk, tm), float(k), F32)
        for r in range(k):
            rank2 = jnp.where(st[1] == v2[r:r + 1, :], float(r), rank2)
        row = 0
        for a, nb in enumerate(_CAND_ROWS):
            cand_ref[row:row + nb, :] = v1[a:a + 1, :] + v2[0:nb, :]
            row += nb
        cur = cand_ref[...]
        cmax = thr = zsum = None
        for r in range(k):
            m = jnp.max(cur, axis=0, keepdims=True)
            if r == 0:
                cmax, zsum = m, jnp.ones_like(m)
            else:
                zsum = zsum + jnp.exp(m - cmax)
            thr = m
            cur = jnp.where(cur == m, neg, cur)
        n1 = jnp.zeros((nk, tm), F32)
        row = 0
        for a, nb in enumerate(_CAND_ROWS):
            sel = cand_ref[row:row + nb, :] >= thr
            n_a = jnp.sum(sel.astype(F32), axis=0, keepdims=True)
            n1 = jnp.where(st[0] == v1[a:a + 1, :], n_a, n1)
            row += nb
        r2_ref[h] = rank2.T
        e2_ref[h] = jnp.exp(st[1] - v2[0:1, :]).T
        n1_ref[h] = n1.T
        e1_ref[h] = (jnp.exp(st[0] - v1[0:1, :]) / zsum).T


def _peer_route(hxt, wq_t, keys):
    d, t = hxt.shape
    nq = wq_t.shape[0]
    tm = PEER_ROUTE_TILE
    rows = PEER_HEADS * N_KEYS
    return pl.pallas_call(
        _route_kernel, grid=(t // tm,),
        in_specs=[pl.BlockSpec((d, tm), lambda i: (0, i)),
                  pl.BlockSpec(wq_t.shape, lambda i: (0, 0)),
                  pl.BlockSpec(keys.shape, lambda i: (0, 0, 0))],
        out_specs=[pl.BlockSpec((PEER_HEADS, tm, N_KEYS), lambda i: (0, i, 0))] * 4,
        out_shape=[jax.ShapeDtypeStruct((PEER_HEADS, t, N_KEYS), F32)] * 4,
        scratch_shapes=[pltpu.VMEM((nq, tm), BF16), pltpu.VMEM((2 * PEER_TOPK, tm), F32),
                        pltpu.VMEM((_CAND_PAD, tm), F32)],
        compiler_params=_cparams("parallel"), name="peer_route",
    )(hxt, wq_t, keys)


def _gates_kernel(r2t_ref, e2t_ref, n1t_ref, e1t_ref, w_ref):
    tb = w_ref.shape[1]
    nk = N_KEYS
    k = PEER_TOPK
    rank = lax.broadcasted_iota(jnp.int32, (k, nk), 0).astype(F32)

    def tok(i, carry):
        for j in range(PEER_GATE_UNROLL):
            t = i * PEER_GATE_UNROLL + j
            a, b = [], []
            for h in range(PEER_HEADS):
                r2, e2, n1, e1 = (ref[h, pl.ds(t, k, stride=0), :]
                                  for ref in (r2t_ref, e2t_ref, n1t_ref, e1t_ref))
                a.append(jnp.where(n1 > rank, e1, 0.0))
                b.append(jnp.where(r2 == rank, e2, 0.0))
            w = _dot_tn(jnp.concatenate(a, axis=0), jnp.concatenate(b, axis=0))
            for g in range(nk // (2 * SUBLANE)):
                lo, hi = 2 * g * SUBLANE, (2 * g + 1) * SUBLANE
                w_ref[g, t] = pltpu.pack_elementwise([w[lo:lo + SUBLANE, :], w[hi:hi + SUBLANE, :]],
                                                     packed_dtype=jnp.bfloat16)
        return carry

    lax.fori_loop(0, tb // PEER_GATE_UNROLL, tok, 0)


def _peer_gates(route):
    t = route[0].shape[1]
    tb = PEER_GATE_TILE
    return pl.pallas_call(
        _gates_kernel, grid=(t // tb,),
        in_specs=[pl.BlockSpec((PEER_HEADS, tb, N_KEYS), lambda i: (0, i, 0))] * 4,
        out_specs=pl.BlockSpec((N_KEYS // (2 * SUBLANE), tb, SUBLANE, N_KEYS), lambda i: (0, i, 0, 0)),
        out_shape=jax.ShapeDtypeStruct((N_KEYS // (2 * SUBLANE), t, SUBLANE, N_KEYS), jnp.uint32),
        compiler_params=_cparams("parallel"), name="peer_gates",
    )(*route)


def _peer_kernel(h_ref, w_ref, ut_ref, v_ref, o_ref, g0_ref, g1_ref):
    e = pl.program_id(1)
    n_blk = pl.num_programs(1) - 1

    @pl.when(e == 0)
    def _():
        o_ref[...] = jnp.zeros(o_ref.shape, F32)
        g1_ref[...] = jnp.zeros(g1_ref.shape, BF16)

    for parity, g_w_ref, g_r_ref in ((0, g0_ref, g1_ref), (1, g1_ref, g0_ref)):
        for build in (True, False):
            @pl.when((e % 2 == parity) & ((e < n_blk) if build else (e == n_blk)))
            def _(g_w_ref=g_w_ref, g_r_ref=g_r_ref, build=build, parity=parity):
                _peer_step(h_ref, w_ref, ut_ref, v_ref, o_ref, g_w_ref, g_r_ref, build, parity)


def _peer_step(h_ref, w_ref, ut_ref, v_ref, acc_ref, g_w_ref, g_r_ref, build, half):
    sr = h_ref.shape[0] // PEER_SUB

    def sub(k, carry):
        rows = pl.ds(pl.multiple_of(k * sr, sr), sr)
        acc_ref[rows, :] += jnp.dot(g_r_ref[rows, :], v_ref[...].astype(BF16), preferred_element_type=F32)
        if not build:
            return carry
        act = jnp.dot(h_ref[rows, :], ut_ref[0], preferred_element_type=F32)
        act = 0.5 * act * (1.0 + lax.erf(act * 0.7071067811865476))
        for j in range(SUBLANE):
            ln = slice(j * N_KEYS, (j + 1) * N_KEYS)
            wj = pltpu.unpack_elementwise(w_ref[pl.ds(k * (sr * SUBLANE) + j, sr, stride=SUBLANE), :],
                                          index=half, packed_dtype=jnp.bfloat16, unpacked_dtype=F32)
            g_w_ref[rows, ln] = (wj * act[:, ln]).astype(BF16)
        return carry

    lax.fori_loop(0, PEER_SUB, sub, 0)


def _peer_dense(hx, w, ut, v_all, layer):
    t, d = hx.shape
    n_blk, _, eb = ut.shape
    tm = PEER_TOK_TILE
    last = n_blk - 1
    return pl.pallas_call(
        _peer_kernel, grid=(t // tm, n_blk + 1),
        in_specs=[pl.BlockSpec((tm, d), lambda i, e: (i, 0)),
                  pl.BlockSpec((tm * SUBLANE, N_KEYS),
                               lambda i, e: (jnp.minimum(e, last) // 2 * (t // tm) + i, 0)),
                  pl.BlockSpec((1, d, eb), lambda i, e: (jnp.minimum(e, last), 0, 0)),
                  pl.BlockSpec((None, eb, d), lambda i, e: (layer, jnp.maximum(e - 1, 0), 0))],
        out_specs=pl.BlockSpec((tm, d), lambda i, e: (i, 0)),
        out_shape=jax.ShapeDtypeStruct((t, d), F32),
        scratch_shapes=[pltpu.VMEM((tm, eb), BF16), pltpu.VMEM((tm, eb), BF16)],
        compiler_params=_cparams("parallel", "arbitrary"), name="peer_dense",
    )(hx, w, ut, v_all)


def _final_kernel(x_ref, f_ref, mod_ref, gain_ref, o_ref):
    x = x_ref[0] + mod_ref[0, 0, 5:6, :] * f_ref[0]
    o_ref[0] = x * lax.rsqrt(jnp.mean(x * x, axis=-1, keepdims=True) + EPS) * gain_ref[...]


def _final(x, ffn, mod, gain):
    b, s, d = x.shape
    tok = pl.BlockSpec((1, TOK_TILE, d), lambda i, t: (i, t, 0))
    return pl.pallas_call(
        _final_kernel, grid=(b, s // TOK_TILE),
        in_specs=[tok, tok, pl.BlockSpec((1, 1, 6, d), lambda i, t: (i, 1, 0, 0)),
                  pl.BlockSpec((1, d), lambda i, t: (0, 0))],
        out_specs=tok, out_shape=jax.ShapeDtypeStruct((b, s, d), F32),
        compiler_params=_cparams("parallel", "parallel"), name="final_norm",
    )(x, ffn, mod, gain)


def _rope_tables(seq):
    pos = jnp.arange(seq)
    rc = jnp.stack([pos // GRID_W, pos % GRID_W], axis=-1).astype(F32)
    inv = jnp.power(ROPE_BASE, -jnp.arange(ROPE_PAIRS, dtype=F32) / ROPE_PAIRS)
    ang = rc[:, :, None] * inv
    cos = jnp.concatenate([jnp.cos(ang)] * 2, axis=-1).reshape(seq, HEAD_DIM)
    sin = jnp.concatenate([-jnp.sin(ang), jnp.sin(ang)], axis=-1).reshape(seq, HEAD_DIM)
    return jnp.concatenate([cos] * 2, axis=1), jnp.concatenate([sin] * 2, axis=1)


def _pair_lanes(cols, n_heads):
    rows = cols.shape[0]
    c = cols.reshape(rows, 4, n_heads // 2, 2).transpose(0, 2, 1, 3).reshape(rows, n_heads // 2, 8)
    return jnp.pad(c, ((0, 0), (0, 0), (0, LANE - 8))).reshape(rows, n_heads // 2 * LANE)


def kernel(x, c, ctx, c_ctx, w_ada, b_ada, norm1, norm2, w_in, conv_w, a_log, dt_bias, gdn_norm, w_pool,
           pool_scale, sink, w_out, peer_wq, peer_keys, peer_u, peer_v, norm_f):
    b, seq, d = x.shape
    n_ctx = ctx.shape[1]
    depth = w_ada.shape[0]
    pool_w = d // 4
    gdn_w = 3 * d // 8
    gdn_heads = gdn_w // HEAD_DIM
    attn_w = d - pool_w - gdn_w
    kv_w = KV_HEADS * HEAD_DIM
    assert n_ctx % TOK_TILE == 0 and seq % TOK_TILE == 0 and gdn_heads % 2 == 0
    assert (n_ctx + seq) % PROJ_TILE == 0 and PEER_EXP_TILE == SUBLANE * N_KEYS
    assert (b * (n_ctx + seq)) % PEER_TOK_TILE == 0 and (b * seq) % PEER_TOK_TILE == 0

    rows = -(-(b + 1) // SUBLANE) * SUBLANE
    c_all = jnp.zeros((rows, d), F32).at[:b].set(c).at[b].set(c_ctx)
    mod = _ada_mod(c_all, w_ada, b_ada)
    mod_x = mod[:, :b].reshape(depth, b, 1, 6, d)
    mod_c = jnp.broadcast_to(mod[:, b].reshape(depth, 1, 1, 6, d), (depth, b, 1, 6, d))
    mod = jnp.concatenate([mod_c, mod_x], axis=2)

    cos_t, sin_t = _rope_tables(seq)
    splits = (pool_w, 3 * gdn_w, gdn_w, gdn_heads // 2 * LANE, attn_w, kv_w, kv_w)
    o_ba = pool_w + 4 * gdn_w
    n_ba = 4 * gdn_heads

    h = jnp.concatenate([ctx, x], axis=1)
    ffn = None
    for l in range(depth):
        last = l == depth - 1
        w_l = w_in[l]
        w_big = jnp.concatenate([w_l[:, :o_ba], _pair_lanes(w_l[:, o_ba:o_ba + n_ba], gdn_heads),
                                 w_l[:, o_ba + n_ba:]], axis=1).astype(BF16)
        pmod = mod[l - 1] if l > 0 else None
        outs = _in_proj(h, ffn, pmod, mod[l], norm1[l][None], w_big, splits, n_ctx)
        if l > 0:
            h, *outs = outs
        pa, qkv, z, ba, aq, ak, av = outs

        wbd = jax.scipy.linalg.block_diag(*[w_pool[l, g] for g in range(len(POOL_WINDOWS))])
        ya = _pool(pa, wbd, pool_scale[l][None], n_ctx)

        gparams = jnp.stack([_pair_lanes(jnp.concatenate([jnp.zeros_like(a_log[l]), a_log[l]]).reshape(1, -1),
                                         gdn_heads),
                             _pair_lanes(jnp.concatenate([jnp.zeros_like(dt_bias[l]), dt_bias[l]]).reshape(1, -1),
                                         gdn_heads)], axis=1)
        gparams = gparams.reshape(2, gdn_heads // 2, LANE).transpose(1, 0, 2)
        gparams = jnp.pad(gparams, ((0, 0), (0, SUBLANE - 2), (0, 0)))
        yb = _gdn(qkv, z, ba, conv_w[l], gparams, jnp.concatenate([gdn_norm[l]] * 2)[None], n_ctx)

        sink_row = jnp.pad(sink[l], (0, LANE - sink.shape[1]))[None]
        yc = _attn_x(aq, ak, av, cos_t, sin_t, sink_row, n_ctx)
        if not last:
            yc = jnp.concatenate([_attn_ctx(aq, ak, av, sink_row, n_ctx), yc], axis=1)

        x_new, hx = _out_proj(h, ya, yb, yc, w_out[l].astype(BF16), mod[l], norm2[l][None], n_ctx,
                              skip_ctx=last)
        t = hx.shape[0] * hx.shape[1]
        hx = hx.reshape(t, d)
        keys = peer_keys[l].reshape(2 * PEER_HEADS, N_KEYS, -1).astype(BF16)
        route = _peer_route(hx.T, peer_wq[l].T.astype(BF16), keys)
        gates = _peer_gates(route).reshape(-1, N_KEYS)
        ut = peer_u[l].reshape(-1, PEER_EXP_TILE, d).transpose(0, 2, 1).astype(BF16)
        ffn = _peer_dense(hx, gates, ut, peer_v, l)
        ffn = ffn.reshape(x_new.shape)
        h = x_new
    return _final(h, ffn, mod[depth - 1], norm_f[None])
```

```python
import functools

import jax
import jax.numpy as jnp
from jax import lax
from jax.experimental import pallas as pl
from jax.experimental.pallas import tpu as pltpu

F32, BF16 = jnp.float32, jnp.bfloat16
HIGHEST = lax.Precision.HIGHEST

EPS = 1e-6
NEG_INF = -1e30
HEAD_DIM = 64
GRID_W = 64
POOL_WINDOWS = (2, 4, 8, 16)
GDN_CHUNK = 64
CONV_K = 5
KV_HEADS = 2
WINDOW = 128
ATTN_BLOCK = 128
ROPE_BASE = 10000.0
ROPE_PAIRS = HEAD_DIM // 4
PEER_HEADS = 8
N_KEYS = 128
PEER_TOPK = 16

LANE = 128
SUBLANE = 8
TOK_TILE = 256
PROJ_TILE = 768
ROW_CHUNK = 256
ATTN_QBLOCKS = 2
GDN_GROUP = 4
PEER_ROUTE_TILE = 512
PEER_GATE_TILE = 128
PEER_GATE_UNROLL = 64
PEER_TOK_TILE = 1024
PEER_EXP_TILE = 1024
PEER_SUB = 1
VMEM_LIMIT = 48 * 1024 * 1024

_CAND_ROWS = tuple(PEER_TOPK // (a + 1) for a in range(PEER_TOPK))
_N_CAND = sum(_CAND_ROWS)
_CAND_PAD = -(-_N_CAND // SUBLANE) * SUBLANE


def _cparams(*sem):
    return pltpu.CompilerParams(dimension_semantics=sem, vmem_limit_bytes=VMEM_LIMIT)


def _dot(a, b):
    return jnp.dot(a.astype(BF16), b.astype(BF16), preferred_element_type=F32)


def _dot_nt(a, b):
    return lax.dot_general(a.astype(BF16), b.astype(BF16), (((1,), (1,)), ((), ())),
                           preferred_element_type=F32)


def _dot_tn(a, b):
    return lax.dot_general(a.astype(BF16), b.astype(BF16), (((0,), (0,)), ((), ())),
                           preferred_element_type=F32)


def _dot_hi(a, b):
    return jnp.dot(a, b, precision=HIGHEST, preferred_element_type=F32)


def _silu(x):
    return x * jax.nn.sigmoid(x)


def _softplus(x):
    return jnp.maximum(x, 0.0) + jnp.log1p(jnp.exp(-jnp.abs(x)))


def _modnorm(x, gain, scale, shift):
    y = x * lax.rsqrt(jnp.mean(x * x, axis=-1, keepdims=True) + EPS) * gain
    return y * (1.0 + scale) + shift


def _ada_kernel(c_ref, w_ref, b_ref, o_ref):
    o_ref[0] = _dot_hi(_silu(c_ref[...]), w_ref[0]) + b_ref[0]


def _ada_mod(c_all, w_ada, b_ada):
    depth, d, n = w_ada.shape
    rows = c_all.shape[0]
    tn = n // 4
    return pl.pallas_call(
        _ada_kernel,
        grid=(depth, n // tn),
        in_specs=[pl.BlockSpec((rows, d), lambda l, j: (0, 0)),
                  pl.BlockSpec((1, d, tn), lambda l, j: (l, 0, j)),
                  pl.BlockSpec((1, 1, tn), lambda l, j: (l, 0, j))],
        out_specs=pl.BlockSpec((1, rows, tn), lambda l, j: (l, 0, j)),
        out_shape=jax.ShapeDtypeStruct((depth, rows, n), F32),
        compiler_params=_cparams("parallel", "parallel"),
        name="ada_mod",
    )(c_all, w_ada, b_ada.reshape(depth, 1, n))


def _row_mod(mod_ref, first_row, n_rows, n_ctx):
    is_x = first_row + lax.broadcasted_iota(jnp.int32, (n_rows, 1), 0) >= n_ctx
    return lambda k: jnp.where(is_x, mod_ref[0, 1, k:k + 1, :], mod_ref[0, 0, k:k + 1, :])


def _inproj_kernel(*refs, has_ffn, col_splits, n_ctx):
    tile = refs[0].shape[1]
    first_row = pl.program_id(1) * tile
    if has_ffn:
        h_ref, f_ref, pmod_ref, mod_ref, gain_ref, w_ref, res_ref, *outs = refs
        x = h_ref[0] + _row_mod(pmod_ref, first_row, tile, n_ctx)(5) * f_ref[0]
        res_ref[0] = x
    else:
        h_ref, mod_ref, gain_ref, w_ref, *outs = refs
        x = h_ref[0]
    mod = _row_mod(mod_ref, first_row, tile, n_ctx)
    hb = _modnorm(x, gain_ref[...], mod(1), mod(0)).astype(BF16)
    off = 0
    for o_ref, width in zip(outs, col_splits):
        o_ref[0] = jnp.dot(hb, w_ref[:, off:off + width], preferred_element_type=F32)
        off += width


def _in_proj(h, ffn, pmod, mod, gain, w, col_splits, n_ctx):
    b, l, d = h.shape
    tile = PROJ_TILE
    tok = pl.BlockSpec((1, tile, d), lambda i, t: (i, t, 0))
    modspec = pl.BlockSpec((1, 2, 6, d), lambda i, t: (i, 0, 0, 0))
    has_ffn = ffn is not None
    in_specs = [tok] + ([tok, modspec] if has_ffn else []) + [
        modspec, pl.BlockSpec((1, d), lambda i, t: (0, 0)), pl.BlockSpec(w.shape, lambda i, t: (0, 0))]
    out_shape = [jax.ShapeDtypeStruct((b, l, wd), F32) for wd in col_splits]
    out_specs = [pl.BlockSpec((1, tile, wd), lambda i, t: (i, t, 0)) for wd in col_splits]
    if has_ffn:
        out_shape = [jax.ShapeDtypeStruct((b, l, d), F32)] + out_shape
        out_specs = [tok] + out_specs
    args = (h, ffn, pmod, mod, gain, w) if has_ffn else (h, mod, gain, w)
    return pl.pallas_call(
        functools.partial(_inproj_kernel, has_ffn=has_ffn, col_splits=col_splits, n_ctx=n_ctx),
        grid=(b, l // tile), in_specs=in_specs, out_specs=out_specs, out_shape=out_shape,
        compiler_params=_cparams("parallel", "parallel"), name="in_proj",
    )(*args)


def _pool_kernel(a_ref, wbd_ref, scale_ref, o_ref, pad_ref, *, segments):
    c = a_ref.shape[-1]
    grp = lax.broadcasted_iota(jnp.int32, (1, c), 1) // (c // len(POOL_WINDOWS))
    zeros = jnp.zeros((SUBLANE, c), F32)
    for off, n in segments:
        pad_ref[0:SUBLANE, :] = zeros
        pad_ref[SUBLANE:SUBLANE + n, :] = a_ref[0, off:off + n, :]
        pad_ref[SUBLANE + n:2 * SUBLANE + n, :] = zeros

        def chunk(ci, carry, off=off, n=n):
            r0 = pl.multiple_of(ci * ROW_CHUNK, ROW_CHUNK)
            win = pad_ref[pl.ds(r0, ROW_CHUNK + 2 * SUBLANE), :]
            sh = lambda s: win[SUBLANE + s:SUBLANE + s + ROW_CHUNK, :]
            a0 = sh(0)
            s2 = sh(-1) + a0
            s4 = s2 + sh(-2) + sh(1)
            s8 = s4 + sh(-4) + sh(-3) + sh(2) + sh(3)
            s16 = s8 + (sh(-8) + sh(-7) + sh(-6) + sh(-5)) + (sh(4) + sh(5) + sh(6) + sh(7))
            t = r0 + lax.broadcasted_iota(jnp.int32, (ROW_CHUNK, 1), 0)

            def mean(s, w):
                cnt = jnp.minimum(t + (w - w // 2), n) - jnp.maximum(t - w // 2, 0)
                return s / cnt.astype(F32)

            pooled = jnp.where(grp == 0, mean(s2, 2), jnp.where(grp == 1, mean(s4, 4),
                               jnp.where(grp == 2, mean(s8, 8), mean(s16, 16)))) - a0
            o_ref[0, pl.ds(off + r0, ROW_CHUNK), :] = _dot_hi(pooled, wbd_ref[...]) * scale_ref[...]
            return carry

        lax.fori_loop(0, n // ROW_CHUNK, chunk, 0)


def _pool(a, wbd, scale, n_ctx):
    b, l, c = a.shape
    segments = ((0, n_ctx), (n_ctx, l - n_ctx))
    return pl.pallas_call(
        functools.partial(_pool_kernel, segments=segments),
        grid=(b,),
        in_specs=[pl.BlockSpec((1, l, c), lambda i: (i, 0, 0)),
                  pl.BlockSpec((c, c), lambda i: (0, 0)),
                  pl.BlockSpec((1, c), lambda i: (0, 0))],
        out_specs=pl.BlockSpec((1, l, c), lambda i: (i, 0, 0)),
        out_shape=jax.ShapeDtypeStruct((b, l, c), F32),
        scratch_shapes=[pltpu.VMEM((l - n_ctx + 2 * SUBLANE, c), F32)],
        compiler_params=_cparams("parallel"), name="pool",
    )(a, wbd, scale)


def _gdn_kernel(q_ref, k_ref, v_ref, z_ref, ba_ref, cwq_ref, cwk_ref, cwv_ref, gp_ref, gain_ref, o_ref,
                pad_ref, yq_ref, yk_ref, yv_ref, bg_ref, of_ref, ob_ref, s_ref,
                qe_ref, mn_ref, oc_ref, nn_ref, egl_ref, *, n_ctx):
    l = q_ref.shape[1]
    n_x = l - n_ctx
    hd = HEAD_DIM
    lane = lax.broadcasted_iota(jnp.int32, (1, LANE), 1)
    lo = lane < hd
    zeros = jnp.zeros((SUBLANE, LANE), F32)
    x_off = 2 * SUBLANE + n_ctx

    def head_scale(ss):
        s_lo = jnp.sum(jnp.where(lo, ss, 0.0), axis=-1, keepdims=True)
        s_hi = jnp.sum(jnp.where(lo, 0.0, ss), axis=-1, keepdims=True)
        return jnp.where(lo, s_lo, s_hi)

    def conv(u_ref, cw_ref, y_ref, post):
        pad_ref[0:SUBLANE, :] = zeros
        pad_ref[SUBLANE:SUBLANE + n_ctx, :] = u_ref[0, 0:n_ctx, :]
        pad_ref[SUBLANE + n_ctx:x_off, :] = zeros
        pad_ref[x_off:x_off + n_x, :] = u_ref[0, n_ctx:l, :]
        pad_ref[x_off + n_x:x_off + n_x + SUBLANE, :] = zeros
        cw = cw_ref[...]
        for poff, yoff, n in ((SUBLANE, 0, n_ctx), (x_off, n_ctx, n_x)):
            def chunk(ci, carry, poff=poff, yoff=yoff):
                r0 = pl.multiple_of(ci * ROW_CHUNK, ROW_CHUNK)
                win = pad_ref[pl.ds(poff - SUBLANE + r0, ROW_CHUNK + 2 * SUBLANE), :]
                acc = win[SUBLANE - 2:SUBLANE - 2 + ROW_CHUNK, :] * cw[0:1, :]
                for j in range(1, CONV_K):
                    acc = acc + win[SUBLANE - 2 + j:SUBLANE - 2 + j + ROW_CHUNK, :] * cw[j:j + 1, :]
                y_ref[pl.ds(yoff + r0, ROW_CHUNK), :] = post(_silu(acc))
                return carry
            lax.fori_loop(0, n // ROW_CHUNK, chunk, 0)

    l2 = lambda y: y * lax.rsqrt(head_scale(y * y) + EPS)
    conv(q_ref, cwq_ref, yq_ref, lambda y: l2(y) * (hd ** -0.5))
    conv(k_ref, cwk_ref, yk_ref, l2)
    conv(v_ref, cwv_ref, yv_ref, lambda y: y)

    ba = ba_ref[0]
    g = -jnp.exp(gp_ref[0, 0:1, :]) * _softplus(ba + gp_ref[0, 1:2, :])
    bg_ref[...] = jnp.where(lane < 4, jax.nn.sigmoid(ba), g)

    ck = GDN_CHUNK
    rr = lax.broadcasted_iota(jnp.int32, (ck, ck), 0)
    cc = lax.broadcasted_iota(jnp.int32, (ck, ck), 1)
    eye = (rr == cc).astype(F32)
    tril = (rr >= cc).astype(F32)
    n_chunks = l // ck
    nc_ctx = n_ctx // ck

    def phase1(ci, carry):
        probs = []
        for gi in range(GDN_GROUP):
            c = ci * GDN_GROUP + gi
            r0 = pl.multiple_of(c * ck, ck)
            qc = yq_ref[pl.ds(r0, ck), :]
            kc = yk_ref[pl.ds(r0, ck), :]
            vc = yv_ref[pl.ds(r0, ck), :]
            bgc = bg_ref[pl.ds(r0, ck), :]
            gcf = _dot_hi(tril, bgc)
            gcb = gcf[ck - 1:ck, :] - gcf + bgc
            gct = (gcf.T, gcb.T)
            for j in range(2):
                qh = qc[:, j * hd:(j + 1) * hd]
                kh = kc[:, j * hd:(j + 1) * hd]
                vh = vc[:, j * hd:(j + 1) * hd]
                kk = _dot_nt(kh, kh)
                qk = _dot_nt(qh, kh)
                for d in range(2):
                    mask = (rr >= cc) if d == 0 else (rr <= cc)
                    smask = (rr > cc) if d == 0 else (rr < cc)
                    lg, lb = 4 + 2 * d + j, 2 * d + j
                    gcol = jnp.broadcast_to((gcf, gcb)[d][:, lg:lg + 1], (ck, hd))
                    grow = gct[d][lg:lg + 1, :]
                    bcol = jnp.broadcast_to(bgc[:, lb:lb + 1], (ck, hd))
                    decay = jnp.where(mask, jnp.exp(jnp.where(mask, gcol - grow, 0.0)), 0.0)
                    xp = jnp.where(smask, -(bcol * kk * decay), 0.0)
                    eg = jnp.exp(gcol)
                    last = ck - 1 if d == 0 else 0
                    glast = gcol[last:last + 1, :]
                    probs.append(dict(
                        idx=c * 4 + 2 * d + j, xp=xp, inv=eye + xp, attn=qk * decay, qd=qh * eg,
                        rhs=jnp.concatenate([vh * bcol, kh * bcol * eg], axis=1),
                        kd=kh * jnp.exp(glast - gcol), egl=jnp.exp(glast)))
        for _ in range(5):
            for p in probs:
                p["xp"] = _dot(p["xp"], p["xp"])
            for p in probs:
                p["inv"] = p["inv"] + _dot(p["inv"], p["xp"])
        for p in probs:
            p["sol"] = _dot(p["inv"], p["rhs"])
        for p in probs:
            p["as"] = _dot(p["attn"], p["sol"])
        for p in probs:
            p["ks"] = _dot_tn(p["kd"], p["sol"])
        for p in probs:
            i = p["idx"]
            qe_ref[i] = (p["qd"] - p["as"][:, hd:]).astype(BF16)
            oc_ref[i] = p["as"][:, :hd]
            mn_ref[i] = p["ks"][:, hd:].astype(BF16)
            nn_ref[i] = p["ks"][:, :hd]
            egl_ref[i] = jnp.broadcast_to(p["egl"], (ck, hd))
        return carry

    lax.fori_loop(0, n_chunks // GDN_GROUP, phase1, 0)

    s_ref[...] = jnp.zeros(s_ref.shape, F32)

    def phase2(i, carry):
        cb = jnp.where(i < nc_ctx, nc_ctx - 1 - i, n_chunks - 1 - (i - nc_ctx))
        work = [(d, j, cidx * 4 + 2 * d + j) for d, cidx in ((0, i), (1, cb)) for j in range(2)]
        states = [s_ref[2 * d + j] for d, j, _ in work]
        outs = [_dot(qe_ref[idx], s) + oc_ref[idx] for (_, _, idx), s in zip(work, states)]
        upd = [_dot(mn_ref[idx], s) for (_, _, idx), s in zip(work, states)]
        for (d, j, idx), s, m in zip(work, states, upd):
            s_ref[2 * d + j] = egl_ref[idx] * s - m + nn_ref[idx]
        of_ref[pl.ds(pl.multiple_of(i * ck, ck), ck), :] = jnp.concatenate(outs[0:2], axis=1)
        ob_ref[pl.ds(pl.multiple_of(cb * ck, ck), ck), :] = jnp.concatenate(outs[2:4], axis=1)
        return carry

    lax.fori_loop(0, n_chunks, phase2, 0, unroll=4)

    def finish(ci, carry):
        r0 = pl.multiple_of(ci * ROW_CHUNK, ROW_CHUNK)
        o = of_ref[pl.ds(r0, ROW_CHUNK), :] + ob_ref[pl.ds(r0, ROW_CHUNK), :]
        o = o * lax.rsqrt(head_scale(o * o) * (1.0 / hd) + EPS) * gain_ref[...]
        o_ref[0, pl.ds(r0, ROW_CHUNK), :] = o * _silu(z_ref[0, pl.ds(r0, ROW_CHUNK), :])
        return carry

    lax.fori_loop(0, l // ROW_CHUNK, finish, 0)


def _gdn(qkv, z, ba, conv_w, gparams, gain2, n_ctx):
    b, l, w3 = qkv.shape
    npair = w3 // (3 * LANE)
    seq = lambda off: pl.BlockSpec((1, l, LANE), lambda i, p: (i, 0, p + off))
    cw = lambda off: pl.BlockSpec((CONV_K, LANE), lambda i, p: (0, p + off))
    return pl.pallas_call(
        functools.partial(_gdn_kernel, n_ctx=n_ctx),
        grid=(b, npair),
        in_specs=[seq(0), seq(npair), seq(2 * npair), seq(0), seq(0),
                  cw(0), cw(npair), cw(2 * npair),
                  pl.BlockSpec((1, SUBLANE, LANE), lambda i, p: (p, 0, 0)),
                  pl.BlockSpec((1, LANE), lambda i, p: (0, 0))],
        out_specs=seq(0),
        out_shape=jax.ShapeDtypeStruct((b, l, npair * LANE), F32),
        scratch_shapes=[pltpu.VMEM((l + 3 * SUBLANE, LANE), F32)] + [pltpu.VMEM((l, LANE), F32)] * 6
        + [pltpu.VMEM((4, HEAD_DIM, HEAD_DIM), F32)]
        + [pltpu.VMEM((4 * l // GDN_CHUNK, GDN_CHUNK, HEAD_DIM), dt) for dt in (BF16, BF16, F32, F32, F32)],
        compiler_params=_cparams("parallel", "parallel"), name="gdn",
    )(qkv, qkv, qkv, z, ba, conv_w, conv_w, conv_w, gparams, gain2)


def _rope(x, cos, sin):
    w = x.shape[-1]
    reps = w // cos.shape[-1]
    if reps > 1:
        cos = jnp.concatenate([cos] * reps, axis=1)
        sin = jnp.concatenate([sin] * reps, axis=1)
    first = (lax.broadcasted_iota(jnp.int32, (1, w), 1) % (2 * ROPE_PAIRS)) < ROPE_PAIRS
    partner = jnp.where(first, pltpu.roll(x, w - ROPE_PAIRS, 1), pltpu.roll(x, ROPE_PAIRS, 1))
    return x * cos + partner * sin


def _attend(q, k, v, valid, sink_col):
    s = _dot_nt(q, k)
    if valid is not None:
        s = jnp.where(valid, s, NEG_INF)
    m = jnp.maximum(jnp.max(s, axis=-1, keepdims=True), sink_col)
    p = jnp.exp(s - m)
    den = jnp.sum(p, axis=-1, keepdims=True) + jnp.exp(sink_col - m)
    return _dot(p, v) / den


def _gqa(q, k_all, v_all, valid, sink_ref):
    nq = q.shape[0]
    hd = HEAD_DIM
    group = q.shape[1] // hd // KV_HEADS
    outs = []
    for kvh in range(KV_HEADS):
        heads = [kvh * group + g for g in range(group)]
        qg = jnp.concatenate([q[:, h * hd:(h + 1) * hd] for h in heads], axis=0)
        sink_col = jnp.concatenate(
            [jnp.broadcast_to(sink_ref[0:1, h:h + 1], (nq, 1)) for h in heads], axis=0)
        o = _attend(qg, k_all[:, kvh * hd:(kvh + 1) * hd], v_all[:, kvh * hd:(kvh + 1) * hd],
                    valid, sink_col)
        outs += [o[g * nq:(g + 1) * nq, :] for g in range(group)]
    return jnp.concatenate(outs, axis=1)


def _attn_kernel(q_ref, *refs, seq):
    nkb = ATTN_QBLOCKS + 2
    k_refs, v_refs = refs[0:nkb], refs[nkb:2 * nkb]
    kc_ref, vc_ref = refs[2 * nkb:2 * nkb + 2]
    cos_refs = refs[2 * nkb + 2:3 * nkb + 2]
    sin_refs = refs[3 * nkb + 2:4 * nkb + 2]
    sink_ref, o_ref = refs[4 * nkb + 2:]
    blk = ATTN_BLOCK
    n_ctx = kc_ref.shape[1]
    group = q_ref.shape[2] // HEAD_DIM // KV_HEADS
    k_rot = [_rope(k_refs[j][0], cos_refs[j][...], sin_refs[j][...]) for j in range(nkb)]
    nk = 3 * blk + n_ctx
    qo = lax.broadcasted_iota(jnp.int32, (group * blk, 1), 0) % blk
    ko = lax.broadcasted_iota(jnp.int32, (1, nk), 1)
    in_window = jnp.abs(ko - blk - qo) <= WINDOW
    for s in range(ATTN_QBLOCKS):
        i = pl.program_id(1) * ATTN_QBLOCKS + s
        rows = slice(s * blk, (s + 1) * blk)
        q = _rope(q_ref[0, rows, :], cos_refs[s + 1][...], sin_refs[s + 1][...]) * (HEAD_DIM ** -0.5)
        k_all = jnp.concatenate(k_rot[s:s + 3] + [kc_ref[0]], axis=0)
        v_all = jnp.concatenate([v_refs[j][0] for j in range(s, s + 3)] + [vc_ref[0]], axis=0)
        kpos = (i - 1) * blk + ko
        valid = (in_window & (kpos >= 0) & (kpos < seq)) | (ko >= 3 * blk)
        o_ref[0, rows, :] = _gqa(q, k_all, v_all, valid, sink_ref)


def _attn_x(aq, ak, av, cos_t, sin_t, sink_row, n_ctx):
    b, l, wq = aq.shape
    wk = ak.shape[-1]
    blk = ATTN_BLOCK
    seq = l - n_ctx
    nb = seq // blk
    cb = n_ctx // blk
    nq = ATTN_QBLOCKS
    assert nb % nq == 0 and cb % nq == 0
    blocks = [lambda t, j=j: jnp.clip(t * nq + j - 1, 0, nb - 1) for j in range(nq + 2)]
    kv = [pl.BlockSpec((1, blk, wk), lambda i, t, f=f: (i, f(t) + cb, 0)) for f in blocks]
    tab = [pl.BlockSpec((blk, LANE), lambda i, t, f=f: (f(t), 0)) for f in blocks]
    ctx = pl.BlockSpec((1, n_ctx, wk), lambda i, t: (i, 0, 0))
    qspec = pl.BlockSpec((1, nq * blk, wq), lambda i, t: (i, t + cb // nq, 0))
    return pl.pallas_call(
        functools.partial(_attn_kernel, seq=seq),
        grid=(b, nb // nq),
        in_specs=[qspec] + kv + kv + [ctx, ctx] + tab + tab + [pl.BlockSpec((1, LANE), lambda i, t: (0, 0))],
        out_specs=pl.BlockSpec((1, nq * blk, wq), lambda i, t: (i, t, 0)),
        out_shape=jax.ShapeDtypeStruct((b, seq, wq), F32),
        compiler_params=_cparams("parallel", "parallel"), name="attn_x",
    )(aq, *([ak] * (nq + 2)), *([av] * (nq + 2)), ak, av, *([cos_t] * (nq + 2)), *([sin_t] * (nq + 2)),
      sink_row)


def _attn_ctx_kernel(q_ref, k_ref, v_ref, sink_ref, o_ref):
    o_ref[0] = _gqa(q_ref[0] * (HEAD_DIM ** -0.5), k_ref[0], v_ref[0], None, sink_ref)


def _attn_ctx(aq, ak, av, sink_row, n_ctx):
    b, _, wq = aq.shape
    wk = ak.shape[-1]
    spec = lambda w: pl.BlockSpec((1, n_ctx, w), lambda i: (i, 0, 0))
    return pl.pallas_call(
        _attn_ctx_kernel, grid=(b,),
        in_specs=[spec(wq), spec(wk), spec(wk), pl.BlockSpec((1, LANE), lambda i: (0, 0))],
        out_specs=spec(wq), out_shape=jax.ShapeDtypeStruct((b, n_ctx, wq), F32),
        compiler_params=_cparams("parallel"), name="attn_ctx",
    )(aq, ak, av, sink_row)


def _outproj_kernel(x_ref, ya_ref, yb_ref, yc_ref, w_ref, mod_ref, gain_ref, xo_ref, hx_ref, *, row_off, n_ctx):
    wa, wb = ya_ref.shape[-1], yb_ref.shape[-1]
    tile = x_ref.shape[1]
    acc = (jnp.dot(ya_ref[0].astype(BF16), w_ref[0:wa, :], preferred_element_type=F32)
           + jnp.dot(yb_ref[0].astype(BF16), w_ref[wa:wa + wb, :], preferred_element_type=F32)
           + jnp.dot(yc_ref[0].astype(BF16), w_ref[wa + wb:, :], preferred_element_type=F32))
    mod = _row_mod(mod_ref, row_off + pl.program_id(1) * tile, tile, n_ctx)
    x = x_ref[0] + mod(2) * acc
    xo_ref[0] = x
    hx_ref[0] = _modnorm(x, gain_ref[...], mod(4), mod(3)).astype(BF16)


def _out_proj(h, ya, yb, yc, w, mod, gain, n_ctx, skip_ctx):
    b, l, d = h.shape
    tile = TOK_TILE if skip_ctx else PROJ_TILE
    off = n_ctx // tile if skip_ctx else 0
    nt = l // tile - off
    full = lambda wd: pl.BlockSpec((1, tile, wd), lambda i, t: (i, t + off, 0))
    own = lambda wd: pl.BlockSpec((1, tile, wd), lambda i, t: (i, t, 0))
    return pl.pallas_call(
        functools.partial(_outproj_kernel, row_off=off * tile, n_ctx=n_ctx), grid=(b, nt),
        in_specs=[full(d), full(ya.shape[-1]), full(yb.shape[-1]), own(yc.shape[-1]),
                  pl.BlockSpec(w.shape, lambda i, t: (0, 0)),
                  pl.BlockSpec((1, 2, 6, d), lambda i, t: (i, 0, 0, 0)),
                  pl.BlockSpec((1, d), lambda i, t: (0, 0))],
        out_specs=[own(d), own(d)],
        out_shape=[jax.ShapeDtypeStruct((b, nt * tile, d), F32),
                   jax.ShapeDtypeStruct((b, nt * tile, d), BF16)],
        compiler_params=_cparams("parallel", "parallel"), name="out_proj",
    )(h, ya, yb, yc, w, mod, gain)


def _sort_pairs(n):
    pairs, p = [], 1
    while p < n:
        k = p
        while k >= 1:
            for j in range(k % p, n - k, 2 * k):
                for i in range(min(k, n - j - k)):
                    if (i + j) // (2 * p) == (i + j + k) // (2 * p):
                        pairs.append((i + j, i + j + k))
            k //= 2
        p *= 2
    return pairs


def _topk_rows(v):
    n = len(v)
    v = list(v)

    def cmpx(i, j):
        v[i], v[j] = jnp.maximum(v[i], v[j]), jnp.minimum(v[i], v[j])

    for i, j in _sort_pairs(n):
        cmpx(i, j)
    shift = SUBLANE // 2
    while shift >= 1:
        partner = [pltpu.roll(x, shift, 0) for x in v]
        v = [jnp.maximum(v[j], partner[n - 1 - j]) for j in range(n)]
        stride = n // 2
        while stride >= 1:
            for i in range(n):
                if not i & stride:
                    cmpx(i, i + stride)
            stride //= 2
        shift //= 2
    return v


def _route_kernel(ht_ref, wq_ref, keys_ref, r2_ref, e2_ref, n1_ref, e1_ref, qt_ref, top_ref, cand_ref):
    tm = ht_ref.shape[1]
    k = PEER_TOPK
    nk = N_KEYS
    neg = jnp.float32(-jnp.inf)
    qt_ref[...] = _dot(wq_ref[...], ht_ref[...]).astype(BF16)
    cand_ref[_N_CAND:_CAND_PAD, :] = jnp.full((_CAND_PAD - _N_CAND, tm), neg, F32)
    for h in range(PEER_HEADS):
        st = []
        for p in range(2):
            hp = 2 * h + p
            s = jnp.dot(keys_ref[hp], qt_ref[hp * nk:(hp + 1) * nk, :], preferred_element_type=F32)
            st.append(s)
            top = _topk_rows([s[g * SUBLANE:(g + 1) * SUBLANE, :] for g in range(nk // SUBLANE)])
            for r in range(k):
                top_ref[p * k + r:p * k + r + 1, :] = top[r][0:1, :]
        v1 = top_ref[0:k, :]
        v2 = top_ref[k:2 * k, :]
        rank2 = jnp.full((nk, tm), float(k), F32)
        for r in range(k):
            rank2 = jnp.where(st[1] == v2[r:r + 1, :], float(r), rank2)
        row = 0
        for a, nb in enumerate(_CAND_ROWS):
            cand_ref[row:row + nb, :] = v1[a:a + 1, :] + v2[0:nb, :]
            row += nb
        cur = cand_ref[...]
        cmax = thr = zsum = None
        for r in range(k):
            m = jnp.max(cur, axis=0, keepdims=True)
            if r == 0:
                cmax, zsum = m, jnp.ones_like(m)
            else:
                zsum = zsum + jnp.exp(m - cmax)
            thr = m
            cur = jnp.where(cur == m, neg, cur)
        n1 = jnp.zeros((nk, tm), F32)
        row = 0
        for a, nb in enumerate(_CAND_ROWS):
            sel = cand_ref[row:row + nb, :] >= thr
            n_a = jnp.sum(sel.astype(F32), axis=0, keepdims=True)
            n1 = jnp.where(st[0] == v1[a:a + 1, :], n_a, n1)
            row += nb
        r2_ref[h] = rank2.T
        e2_ref[h] = jnp.exp(st[1] - v2[0:1, :]).T
        n1_ref[h] = n1.T
        e1_ref[h] = (jnp.exp(st[0] - v1[0:1, :]) / zsum).T


def _peer_route(hxt, wq_t, keys):
    d, t = hxt.shape
    nq = wq_t.shape[0]
    tm = PEER_ROUTE_TILE
    rows = PEER_HEADS * N_KEYS
    return pl.pallas_call(
        _route_kernel, grid=(t // tm,),
        in_specs=[pl.BlockSpec((d, tm), lambda i: (0, i)),
                  pl.BlockSpec(wq_t.shape, lambda i: (0, 0)),
                  pl.BlockSpec(keys.shape, lambda i: (0, 0, 0))],
        out_specs=[pl.BlockSpec((PEER_HEADS, tm, N_KEYS), lambda i: (0, i, 0))] * 4,
        out_shape=[jax.ShapeDtypeStruct((PEER_HEADS, t, N_KEYS), F32)] * 4,
        scratch_shapes=[pltpu.VMEM((nq, tm), BF16), pltpu.VMEM((2 * PEER_TOPK, tm), F32),
                        pltpu.VMEM((_CAND_PAD, tm), F32)],
        compiler_params=_cparams("parallel"), name="peer_route",
    )(hxt, wq_t, keys)


def _gates_kernel(r2t_ref, e2t_ref, n1t_ref, e1t_ref, w_ref):
    tb = w_ref.shape[1]
    nk = N_KEYS
    k = PEER_TOPK
    rank = lax.broadcasted_iota(jnp.int32, (k, nk), 0).astype(F32)

    def tok(i, carry):
        for j in range(PEER_GATE_UNROLL):
            t = i * PEER_GATE_UNROLL + j
            a, b = [], []
            for h in range(PEER_HEADS):
                r2, e2, n1, e1 = (ref[h, pl.ds(t, k, stride=0), :]
                                  for ref in (r2t_ref, e2t_ref, n1t_ref, e1t_ref))
                a.append(jnp.where(n1 > rank, e1, 0.0))
                b.append(jnp.where(r2 == rank, e2, 0.0))
            w = _dot_tn(jnp.concatenate(a, axis=0), jnp.concatenate(b, axis=0))
            for g in range(nk // (2 * SUBLANE)):
                lo, hi = 2 * g * SUBLANE, (2 * g + 1) * SUBLANE
                w_ref[g, t] = pltpu.pack_elementwise([w[lo:lo + SUBLANE, :], w[hi:hi + SUBLANE, :]],
                                                     packed_dtype=jnp.bfloat16)
        return carry

    lax.fori_loop(0, tb // PEER_GATE_UNROLL, tok, 0)


def _peer_gates(route):
    t = route[0].shape[1]
    tb = PEER_GATE_TILE
    return pl.pallas_call(
        _gates_kernel, grid=(t // tb,),
        in_specs=[pl.BlockSpec((PEER_HEADS, tb, N_KEYS), lambda i: (0, i, 0))] * 4,
        out_specs=pl.BlockSpec((N_KEYS // (2 * SUBLANE), tb, SUBLANE, N_KEYS), lambda i: (0, i, 0, 0)),
        out_shape=jax.ShapeDtypeStruct((N_KEYS // (2 * SUBLANE), t, SUBLANE, N_KEYS), jnp.uint32),
        compiler_params=_cparams("parallel"), name="peer_gates",
    )(*route)


def _peer_kernel(h_ref, w_ref, ut_ref, v_ref, o_ref, g0_ref, g1_ref):
    e = pl.program_id(1)
    n_blk = pl.num_programs(1) - 1

    @pl.when(e == 0)
    def _():
        o_ref[...] = jnp.zeros(o_ref.shape, F32)
        g1_ref[...] = jnp.zeros(g1_ref.shape, BF16)

    for parity, g_w_ref, g_r_ref in ((0, g0_ref, g1_ref), (1, g1_ref, g0_ref)):
        for build in (True, False):
            @pl.when((e % 2 == parity) & ((e < n_blk) if build else (e == n_blk)))
            def _(g_w_ref=g_w_ref, g_r_ref=g_r_ref, build=build, parity=parity):
                _peer_step(h_ref, w_ref, ut_ref, v_ref, o_ref, g_w_ref, g_r_ref, build, parity)


def _peer_step(h_ref, w_ref, ut_ref, v_ref, acc_ref, g_w_ref, g_r_ref, build, half):
    sr = h_ref.shape[0] // PEER_SUB

    def sub(k, carry):
        rows = pl.ds(pl.multiple_of(k * sr, sr), sr)
        acc_ref[rows, :] += jnp.dot(g_r_ref[rows, :], v_ref[...].astype(BF16), preferred_element_type=F32)
        if not build:
            return carry
        act = jnp.dot(h_ref[rows, :], ut_ref[0], preferred_element_type=F32)
        act = 0.5 * act * (1.0 + lax.erf(act * 0.7071067811865476))
        for j in range(SUBLANE):
            ln = slice(j * N_KEYS, (j + 1) * N_KEYS)
            wj = pltpu.unpack_elementwise(w_ref[pl.ds(k * (sr * SUBLANE) + j, sr, stride=SUBLANE), :],
                                          index=half, packed_dtype=jnp.bfloat16, unpacked_dtype=F32)
            g_w_ref[rows, ln] = (wj * act[:, ln]).astype(BF16)
        return carry

    lax.fori_loop(0, PEER_SUB, sub, 0)


def _peer_dense(hx, w, ut, v_all, layer):
    t, d = hx.shape
    n_blk, _, eb = ut.shape
    tm = PEER_TOK_TILE
    last = n_blk - 1
    return pl.pallas_call(
        _peer_kernel, grid=(t // tm, n_blk + 1),
        in_specs=[pl.BlockSpec((tm, d), lambda i, e: (i, 0)),
                  pl.BlockSpec((tm * SUBLANE, N_KEYS),
                               lambda i, e: (jnp.minimum(e, last) // 2 * (t // tm) + i, 0)),
                  pl.BlockSpec((1, d, eb), lambda i, e: (jnp.minimum(e, last), 0, 0)),
                  pl.BlockSpec((None, eb, d), lambda i, e: (layer, jnp.maximum(e - 1, 0), 0))],
        out_specs=pl.BlockSpec((tm, d), lambda i, e: (i, 0)),
        out_shape=jax.ShapeDtypeStruct((t, d), F32),
        scratch_shapes=[pltpu.VMEM((tm, eb), BF16), pltpu.VMEM((tm, eb), BF16)],
        compiler_params=_cparams("parallel", "arbitrary"), name="peer_dense",
    )(hx, w, ut, v_all)


def _final_kernel(x_ref, f_ref, mod_ref, gain_ref, o_ref):
    x = x_ref[0] + mod_ref[0, 0, 5:6, :] * f_ref[0]
    o_ref[0] = x * lax.rsqrt(jnp.mean(x * x, axis=-1, keepdims=True) + EPS) * gain_ref[...]


def _final(x, ffn, mod, gain):
    b, s, d = x.shape
    tok = pl.BlockSpec((1, TOK_TILE, d), lambda i, t: (i, t, 0))
    return pl.pallas_call(
        _final_kernel, grid=(b, s // TOK_TILE),
        in_specs=[tok, tok, pl.BlockSpec((1, 1, 6, d), lambda i, t: (i, 1, 0, 0)),
                  pl.BlockSpec((1, d), lambda i, t: (0, 0))],
        out_specs=tok, out_shape=jax.ShapeDtypeStruct((b, s, d), F32),
        compiler_params=_cparams("parallel", "parallel"), name="final_norm",
    )(x, ffn, mod, gain)


def _rope_tables(seq):
    pos = jnp.arange(seq)
    rc = jnp.stack([pos // GRID_W, pos % GRID_W], axis=-1).astype(F32)
    inv = jnp.power(ROPE_BASE, -jnp.arange(ROPE_PAIRS, dtype=F32) / ROPE_PAIRS)
    ang = rc[:, :, None] * inv
    cos = jnp.concatenate([jnp.cos(ang)] * 2, axis=-1).reshape(seq, HEAD_DIM)
    sin = jnp.concatenate([-jnp.sin(ang), jnp.sin(ang)], axis=-1).reshape(seq, HEAD_DIM)
    return jnp.concatenate([cos] * 2, axis=1), jnp.concatenate([sin] * 2, axis=1)


def _pair_lanes(cols, n_heads):
    rows = cols.shape[0]
    c = cols.reshape(rows, 4, n_heads // 2, 2).transpose(0, 2, 1, 3).reshape(rows, n_heads // 2, 8)
    return jnp.pad(c, ((0, 0), (0, 0), (0, LANE - 8))).reshape(rows, n_heads // 2 * LANE)


def kernel(x, c, ctx, c_ctx, w_ada, b_ada, norm1, norm2, w_in, conv_w, a_log, dt_bias, gdn_norm, w_pool,
           pool_scale, sink, w_out, peer_wq, peer_keys, peer_u, peer_v, norm_f):
    b, seq, d = x.shape
    n_ctx = ctx.shape[1]
    depth = w_ada.shape[0]
    pool_w = d // 4
    gdn_w = 3 * d // 8
    gdn_heads = gdn_w // HEAD_DIM
    attn_w = d - pool_w - gdn_w
    kv_w = KV_HEADS * HEAD_DIM
    assert n_ctx % TOK_TILE == 0 and seq % TOK_TILE == 0 and gdn_heads % 2 == 0
    assert (n_ctx + seq) % PROJ_TILE == 0 and PEER_EXP_TILE == SUBLANE * N_KEYS
    assert (b * (n_ctx + seq)) % PEER_TOK_TILE == 0 and (b * seq) % PEER_TOK_TILE == 0

    rows = -(-(b + 1) // SUBLANE) * SUBLANE
    c_all = jnp.zeros((rows, d), F32).at[:b].set(c).at[b].set(c_ctx)
    mod = _ada_mod(c_all, w_ada, b_ada)
    mod_x = mod[:, :b].reshape(depth, b, 1, 6, d)
    mod_c = jnp.broadcast_to(mod[:, b].reshape(depth, 1, 1, 6, d), (depth, b, 1, 6, d))
    mod = jnp.concatenate([mod_c, mod_x], axis=2)

    cos_t, sin_t = _rope_tables(seq)
    splits = (pool_w, 3 * gdn_w, gdn_w, gdn_heads // 2 * LANE, attn_w, kv_w, kv_w)
    o_ba = pool_w + 4 * gdn_w
    n_ba = 4 * gdn_heads

    h = jnp.concatenate([ctx, x], axis=1)
    ffn = None
    for l in range(depth):
        last = l == depth - 1
        w_l = w_in[l]
        w_big = jnp.concatenate([w_l[:, :o_ba], _pair_lanes(w_l[:, o_ba:o_ba + n_ba], gdn_heads),
                                 w_l[:, o_ba + n_ba:]], axis=1).astype(BF16)
        pmod = mod[l - 1] if l > 0 else None
        outs = _in_proj(h, ffn, pmod, mod[l], norm1[l][None], w_big, splits, n_ctx)
        if l > 0:
            h, *outs = outs
        pa, qkv, z, ba, aq, ak, av = outs

        wbd = jax.scipy.linalg.block_diag(*[w_pool[l, g] for g in range(len(POOL_WINDOWS))])
        ya = _pool(pa, wbd, pool_scale[l][None], n_ctx)

        gparams = jnp.stack([_pair_lanes(jnp.concatenate([jnp.zeros_like(a_log[l]), a_log[l]]).reshape(1, -1),
                                         gdn_heads),
                             _pair_lanes(jnp.concatenate([jnp.zeros_like(dt_bias[l]), dt_bias[l]]).reshape(1, -1),
                                         gdn_heads)], axis=1)
        gparams = gparams.reshape(2, gdn_heads // 2, LANE).transpose(1, 0, 2)
        gparams = jnp.pad(gparams, ((0, 0), (0, SUBLANE - 2), (0, 0)))
        yb = _gdn(qkv, z, ba, conv_w[l], gparams, jnp.concatenate([gdn_norm[l]] * 2)[None], n_ctx)

        sink_row = jnp.pad(sink[l], (0, LANE - sink.shape[1]))[None]
        yc = _attn_x(aq, ak, av, cos_t, sin_t, sink_row, n_ctx)
        if not last:
            yc = jnp.concatenate([_attn_ctx(aq, ak, av, sink_row, n_ctx), yc], axis=1)

        x_new, hx = _out_proj(h, ya, yb, yc, w_out[l].astype(BF16), mod[l], norm2[l][None], n_ctx,
                              skip_ctx=last)
        t = hx.shape[0] * hx.shape[1]
        hx = hx.reshape(t, d)
        keys = peer_keys[l].reshape(2 * PEER_HEADS, N_KEYS, -1).astype(BF16)
        route = _peer_route(hx.T, peer_wq[l].T.astype(BF16), keys)
        gates = _peer_gates(route).reshape(-1, N_KEYS)
        ut = peer_u[l].reshape(-1, PEER_EXP_TILE, d).transpose(0, 2, 1).astype(BF16)
        ffn = _peer_dense(hx, gates, ut, peer_v, l)
        ffn = ffn.reshape(x_new.shape)
        h = x_new
    return _final(h, ffn, mod[depth - 1], norm_f[None])
```

```python
import functools

import jax
import jax.numpy as jnp
from jax import lax
from jax.experimental import pallas as pl
from jax.experimental.pallas import tpu as pltpu

F32, BF16 = jnp.float32, jnp.bfloat16
HIGHEST = lax.Precision.HIGHEST

EPS = 1e-6
NEG_INF = -1e30
HEAD_DIM = 64
GRID_W = 64
POOL_WINDOWS = (2, 4, 8, 16)
GDN_CHUNK = 64
CONV_K = 5
KV_HEADS = 2
WINDOW = 128
ATTN_BLOCK = 128
ROPE_BASE = 10000.0
ROPE_PAIRS = HEAD_DIM // 4
PEER_HEADS = 8
N_KEYS = 128
PEER_TOPK = 16

LANE = 128
SUBLANE = 8
TOK_TILE = 256
PROJ_TILE = 768
ROW_CHUNK = 256
ATTN_QBLOCKS = 2
GDN_GROUP = 4
PEER_ROUTE_TILE = 512
PEER_GATE_TILE = 128
PEER_GATE_UNROLL = 64
PEER_TOK_TILE = 1024
PEER_EXP_TILE = 1024
PEER_SUB = 1
VMEM_LIMIT = 48 * 1024 * 1024

_CAND_ROWS = tuple(PEER_TOPK // (a + 1) for a in range(PEER_TOPK))
_N_CAND = sum(_CAND_ROWS)
_CAND_PAD = -(-_N_CAND // SUBLANE) * SUBLANE


def _cparams(*sem):
    return pltpu.CompilerParams(dimension_semantics=sem, vmem_limit_bytes=VMEM_LIMIT)


def _dot(a, b):
    return jnp.dot(a.astype(BF16), b.astype(BF16), preferred_element_type=F32)


def _dot_nt(a, b):
    return lax.dot_general(a.astype(BF16), b.astype(BF16), (((1,), (1,)), ((), ())),
                           preferred_element_type=F32)


def _dot_tn(a, b):
    return lax.dot_general(a.astype(BF16), b.astype(BF16), (((0,), (0,)), ((), ())),
                           preferred_element_type=F32)


def _dot_hi(a, b):
    return jnp.dot(a, b, precision=HIGHEST, preferred_element_type=F32)


def _silu(x):
    return x * jax.nn.sigmoid(x)


def _softplus(x):
    return jnp.maximum(x, 0.0) + jnp.log1p(jnp.exp(-jnp.abs(x)))


def _modnorm(x, gain, scale, shift):
    y = x * lax.rsqrt(jnp.mean(x * x, axis=-1, keepdims=True) + EPS) * gain
    return y * (1.0 + scale) + shift


def _ada_kernel(c_ref, w_ref, b_ref, o_ref):
    o_ref[0] = _dot_hi(_silu(c_ref[...]), w_ref[0]) + b_ref[0]


def _ada_mod(c_all, w_ada, b_ada):
    depth, d, n = w_ada.shape
    rows = c_all.shape[0]
    tn = n // 4
    return pl.pallas_call(
        _ada_kernel,
        grid=(depth, n // tn),
        in_specs=[pl.BlockSpec((rows, d), lambda l, j: (0, 0)),
                  pl.BlockSpec((1, d, tn), lambda l, j: (l, 0, j)),
                  pl.BlockSpec((1, 1, tn), lambda l, j: (l, 0, j))],
        out_specs=pl.BlockSpec((1, rows, tn), lambda l, j: (l, 0, j)),
        out_shape=jax.ShapeDtypeStruct((depth, rows, n), F32),
        compiler_params=_cparams("parallel", "parallel"),
        name="ada_mod",
    )(c_all, w_ada, b_ada.reshape(depth, 1, n))


def _row_mod(mod_ref, first_row, n_rows, n_ctx):
    is_x = first_row + lax.broadcasted_iota(jnp.int32, (n_rows, 1), 0) >= n_ctx
    return lambda k: jnp.where(is_x, mod_ref[0, 1, k:k + 1, :], mod_ref[0, 0, k:k + 1, :])


def _inproj_kernel(*refs, has_ffn, col_splits, n_ctx):
    tile = refs[0].shape[1]
    first_row = pl.program_id(1) * tile
    if has_ffn:
        h_ref, f_ref, pmod_ref, mod_ref, gain_ref, w_ref, res_ref, *outs = refs
        x = h_ref[0] + _row_mod(pmod_ref, first_row, tile, n_ctx)(5) * f_ref[0]
        res_ref[0] = x
    else:
        h_ref, mod_ref, gain_ref, w_ref, *outs = refs
        x = h_ref[0]
    mod = _row_mod(mod_ref, first_row, tile, n_ctx)
    hb = _modnorm(x, gain_ref[...], mod(1), mod(0)).astype(BF16)
    off = 0
    for o_ref, width in zip(outs, col_splits):
        o_ref[0] = jnp.dot(hb, w_ref[:, off:off + width], preferred_element_type=F32)
        off += width


def _in_proj(h, ffn, pmod, mod, gain, w, col_splits, n_ctx):
    b, l, d = h.shape
    tile = PROJ_TILE
    tok = pl.BlockSpec((1, tile, d), lambda i, t: (i, t, 0))
    modspec = pl.BlockSpec((1, 2, 6, d), lambda i, t: (i, 0, 0, 0))
    has_ffn = ffn is not None
    in_specs = [tok] + ([tok, modspec] if has_ffn else []) + [
        modspec, pl.BlockSpec((1, d), lambda i, t: (0, 0)), pl.BlockSpec(w.shape, lambda i, t: (0, 0))]
    out_shape = [jax.ShapeDtypeStruct((b, l, wd), F32) for wd in col_splits]
    out_specs = [pl.BlockSpec((1, tile, wd), lambda i, t: (i, t, 0)) for wd in col_splits]
    if has_ffn:
        out_shape = [jax.ShapeDtypeStruct((b, l, d), F32)] + out_shape
        out_specs = [tok] + out_specs
    args = (h, ffn, pmod, mod, gain, w) if has_ffn else (h, mod, gain, w)
    return pl.pallas_call(
        functools.partial(_inproj_kernel, has_ffn=has_ffn, col_splits=col_splits, n_ctx=n_ctx),
        grid=(b, l // tile), in_specs=in_specs, out_specs=out_specs, out_shape=out_shape,
        compiler_params=_cparams("parallel", "parallel"), name="in_proj",
    )(*args)


def _pool_kernel(a_ref, wbd_ref, scale_ref, o_ref, pad_ref, *, segments):
    c = a_ref.shape[-1]
    grp = lax.broadcasted_iota(jnp.int32, (1, c), 1) // (c // len(POOL_WINDOWS))
    zeros = jnp.zeros((SUBLANE, c), F32)
    for off, n in segments:
        pad_ref[0:SUBLANE, :] = zeros
        pad_ref[SUBLANE:SUBLANE + n, :] = a_ref[0, off:off + n, :]
        pad_ref[SUBLANE + n:2 * SUBLANE + n, :] = zeros

        def chunk(ci, carry, off=off, n=n):
            r0 = pl.multiple_of(ci * ROW_CHUNK, ROW_CHUNK)
            win = pad_ref[pl.ds(r0, ROW_CHUNK + 2 * SUBLANE), :]
            sh = lambda s: win[SUBLANE + s:SUBLANE + s + ROW_CHUNK, :]
            a0 = sh(0)
            s2 = sh(-1) + a0
            s4 = s2 + sh(-2) + sh(1)
            s8 = s4 + sh(-4) + sh(-3) + sh(2) + sh(3)
            s16 = s8 + (sh(-8) + sh(-7) + sh(-6) + sh(-5)) + (sh(4) + sh(5) + sh(6) + sh(7))
            t = r0 + lax.broadcasted_iota(jnp.int32, (ROW_CHUNK, 1), 0)

            def mean(s, w):
                cnt = jnp.minimum(t + (w - w // 2), n) - jnp.maximum(t - w // 2, 0)
                return s / cnt.astype(F32)

            pooled = jnp.where(grp == 0, mean(s2, 2), jnp.where(grp == 1, mean(s4, 4),
                               jnp.where(grp == 2, mean(s8, 8), mean(s16, 16)))) - a0
            o_ref[0, pl.ds(off + r0, ROW_CHUNK), :] = _dot_hi(pooled, wbd_ref[...]) * scale_ref[...]
            return carry

        lax.fori_loop(0, n // ROW_CHUNK, chunk, 0)


def _pool(a, wbd, scale, n_ctx):
    b, l, c = a.shape
    segments = ((0, n_ctx), (n_ctx, l - n_ctx))
    return pl.pallas_call(
        functools.partial(_pool_kernel, segments=segments),
        grid=(b,),
        in_specs=[pl.BlockSpec((1, l, c), lambda i: (i, 0, 0)),
                  pl.BlockSpec((c, c), lambda i: (0, 0)),
                  pl.BlockSpec((1, c), lambda i: (0, 0))],
        out_specs=pl.BlockSpec((1, l, c), lambda i: (i, 0, 0)),
        out_shape=jax.ShapeDtypeStruct((b, l, c), F32),
        scratch_shapes=[pltpu.VMEM((l - n_ctx + 2 * SUBLANE, c), F32)],
        compiler_params=_cparams("parallel"), name="pool",
    )(a, wbd, scale)


def _gdn_kernel(q_ref, k_ref, v_ref, z_ref, ba_ref, cwq_ref, cwk_ref, cwv_ref, gp_ref, gain_ref, o_ref,
                pad_ref, yq_ref, yk_ref, yv_ref, bg_ref, of_ref, ob_ref, s_ref,
                qe_ref, mn_ref, oc_ref, nn_ref, egl_ref, *, n_ctx):
    l = q_ref.shape[1]
    n_x = l - n_ctx
    hd = HEAD_DIM
    lane = lax.broadcasted_iota(jnp.int32, (1, LANE), 1)
    lo = lane < hd
    zeros = jnp.zeros((SUBLANE, LANE), F32)
    x_off = 2 * SUBLANE + n_ctx

    def head_scale(ss):
        s_lo = jnp.sum(jnp.where(lo, ss, 0.0), axis=-1, keepdims=True)
        s_hi = jnp.sum(jnp.where(lo, 0.0, ss), axis=-1, keepdims=True)
        return jnp.where(lo, s_lo, s_hi)

    def conv(u_ref, cw_ref, y_ref, post):
        pad_ref[0:SUBLANE, :] = zeros
        pad_ref[SUBLANE:SUBLANE + n_ctx, :] = u_ref[0, 0:n_ctx, :]
        pad_ref[SUBLANE + n_ctx:x_off, :] = zeros
        pad_ref[x_off:x_off + n_x, :] = u_ref[0, n_ctx:l, :]
        pad_ref[x_off + n_x:x_off + n_x + SUBLANE, :] = zeros
        cw = cw_ref[...]
        for poff, yoff, n in ((SUBLANE, 0, n_ctx), (x_off, n_ctx, n_x)):
            def chunk(ci, carry, poff=poff, yoff=yoff):
                r0 = pl.multiple_of(ci * ROW_CHUNK, ROW_CHUNK)
                win = pad_ref[pl.ds(poff - SUBLANE + r0, ROW_CHUNK + 2 * SUBLANE), :]
                acc = win[SUBLANE - 2:SUBLANE - 2 + ROW_CHUNK, :] * cw[0:1, :]
                for j in range(1, CONV_K):
                    acc = acc + win[SUBLANE - 2 + j:SUBLANE - 2 + j + ROW_CHUNK, :] * cw[j:j + 1, :]
                y_ref[pl.ds(yoff + r0, ROW_CHUNK), :] = post(_silu(acc))
                return carry
            lax.fori_loop(0, n // ROW_CHUNK, chunk, 0, unroll=2)

    l2 = lambda y: y * lax.rsqrt(head_scale(y * y) + EPS)
    conv(q_ref, cwq_ref, yq_ref, lambda y: l2(y) * (hd ** -0.5))
    conv(k_ref, cwk_ref, yk_ref, l2)
    conv(v_ref, cwv_ref, yv_ref, lambda y: y)

    ba = ba_ref[0]
    g = -jnp.exp(gp_ref[0, 0:1, :]) * _softplus(ba + gp_ref[0, 1:2, :])
    bg_ref[...] = jnp.where(lane < 4, jax.nn.sigmoid(ba), g)

    ck = GDN_CHUNK
    rr = lax.broadcasted_iota(jnp.int32, (ck, ck), 0)
    cc = lax.broadcasted_iota(jnp.int32, (ck, ck), 1)
    eye = (rr == cc).astype(F32)
    tril = (rr >= cc).astype(F32)
    n_chunks = l // ck
    nc_ctx = n_ctx // ck

    def phase1(ci, carry):
        probs = []
        for gi in range(GDN_GROUP):
            c = ci * GDN_GROUP + gi
            r0 = pl.multiple_of(c * ck, ck)
            qc = yq_ref[pl.ds(r0, ck), :]
            kc = yk_ref[pl.ds(r0, ck), :]
            vc = yv_ref[pl.ds(r0, ck), :]
            bgc = bg_ref[pl.ds(r0, ck), :]
            gcf = _dot_hi(tril, bgc)
            gcb = gcf[ck - 1:ck, :] - gcf + bgc
            gct = (gcf.T, gcb.T)
            for j in range(2):
                qh = qc[:, j * hd:(j + 1) * hd]
                kh = kc[:, j * hd:(j + 1) * hd]
                vh = vc[:, j * hd:(j + 1) * hd]
                kk = _dot_nt(kh, kh)
                qk = _dot_nt(qh, kh)
                for d in range(2):
                    mask = (rr >= cc) if d == 0 else (rr <= cc)
                    smask = (rr > cc) if d == 0 else (rr < cc)
                    lg, lb = 4 + 2 * d + j, 2 * d + j
                    gcol = jnp.broadcast_to((gcf, gcb)[d][:, lg:lg + 1], (ck, hd))
                    grow = gct[d][lg:lg + 1, :]
                    bcol = jnp.broadcast_to(bgc[:, lb:lb + 1], (ck, hd))
                    decay = jnp.where(mask, jnp.exp(jnp.where(mask, gcol - grow, 0.0)), 0.0)
                    xp = jnp.where(smask, -(bcol * kk * decay), 0.0)
                    eg = jnp.exp(gcol)
                    last = ck - 1 if d == 0 else 0
                    glast = gcol[last:last + 1, :]
                    probs.append(dict(
                        idx=c * 4 + 2 * d + j, xp=xp, inv=eye + xp, attn=qk * decay, qd=qh * eg,
                        rhs=jnp.concatenate([vh * bcol, kh * bcol * eg], axis=1),
                        kd=kh * jnp.exp(glast - gcol), egl=jnp.exp(glast)))
        for _ in range(5):
            for p in probs:
                p["xp"] = _dot(p["xp"], p["xp"])
            for p in probs:
                p["inv"] = p["inv"] + _dot(p["inv"], p["xp"])
        for p in probs:
            p["sol"] = _dot(p["inv"], p["rhs"])
        for p in probs:
            p["as"] = _dot(p["attn"], p["sol"])
        for p in probs:
            p["ks"] = _dot_tn(p["kd"], p["sol"])
        for p in probs:
            i = p["idx"]
            qe_ref[i] = (p["qd"] - p["as"][:, hd:]).astype(BF16)
            oc_ref[i] = p["as"][:, :hd]
            mn_ref[i] = p["ks"][:, hd:].astype(BF16)
            nn_ref[i] = p["ks"][:, :hd]
            egl_ref[i] = jnp.broadcast_to(p["egl"], (ck, hd))
        return carry

    lax.fori_loop(0, n_chunks // GDN_GROUP, phase1, 0)

    s_ref[...] = jnp.zeros(s_ref.shape, F32)

    def phase2(i, carry):
        cb = jnp.where(i < nc_ctx, nc_ctx - 1 - i, n_chunks - 1 - (i - nc_ctx))
        work = [(d, j, cidx * 4 + 2 * d + j) for d, cidx in ((0, i), (1, cb)) for j in range(2)]
        states = [s_ref[2 * d + j] for d, j, _ in work]
        outs = [_dot(qe_ref[idx], s) + oc_ref[idx] for (_, _, idx), s in zip(work, states)]
        upd = [_dot(mn_ref[idx], s) for (_, _, idx), s in zip(work, states)]
        for (d, j, idx), s, m in zip(work, states, upd):
            s_ref[2 * d + j] = egl_ref[idx] * s - m + nn_ref[idx]
        of_ref[pl.ds(pl.multiple_of(i * ck, ck), ck), :] = jnp.concatenate(outs[0:2], axis=1)
        ob_ref[pl.ds(pl.multiple_of(cb * ck, ck), ck), :] = jnp.concatenate(outs[2:4], axis=1)
        return carry

    lax.fori_loop(0, n_chunks, phase2, 0, unroll=12)

    def finish(ci, carry):
        r0 = pl.multiple_of(ci * ROW_CHUNK, ROW_CHUNK)
        o = of_ref[pl.ds(r0, ROW_CHUNK), :] + ob_ref[pl.ds(r0, ROW_CHUNK), :]
        o = o * lax.rsqrt(head_scale(o * o) * (1.0 / hd) + EPS) * gain_ref[...]
        o_ref[0, pl.ds(r0, ROW_CHUNK), :] = o * _silu(z_ref[0, pl.ds(r0, ROW_CHUNK), :])
        return carry

    lax.fori_loop(0, l // ROW_CHUNK, finish, 0, unroll=3)


def _gdn(qkv, z, ba, conv_w, gparams, gain2, n_ctx):
    b, l, w3 = qkv.shape
    npair = w3 // (3 * LANE)
    seq = lambda off: pl.BlockSpec((1, l, LANE), lambda i, p: (i, 0, p + off))
    cw = lambda off: pl.BlockSpec((CONV_K, LANE), lambda i, p: (0, p + off))
    return pl.pallas_call(
        functools.partial(_gdn_kernel, n_ctx=n_ctx),
        grid=(b, npair),
        in_specs=[seq(0), seq(npair), seq(2 * npair), seq(0), seq(0),
                  cw(0), cw(npair), cw(2 * npair),
                  pl.BlockSpec((1, SUBLANE, LANE), lambda i, p: (p, 0, 0)),
                  pl.BlockSpec((1, LANE), lambda i, p: (0, 0))],
        out_specs=seq(0),
        out_shape=jax.ShapeDtypeStruct((b, l, npair * LANE), F32),
        scratch_shapes=[pltpu.VMEM((l + 3 * SUBLANE, LANE), F32)] + [pltpu.VMEM((l, LANE), F32)] * 6
        + [pltpu.VMEM((4, HEAD_DIM, HEAD_DIM), F32)]
        + [pltpu.VMEM((4 * l // GDN_CHUNK, GDN_CHUNK, HEAD_DIM), dt) for dt in (BF16, BF16, F32, F32, F32)],
        compiler_params=_cparams("parallel", "parallel"), name="gdn",
    )(qkv, qkv, qkv, z, ba, conv_w, conv_w, conv_w, gparams, gain2)


def _rope(x, cos, sin):
    w = x.shape[-1]
    reps = w // cos.shape[-1]
    if reps > 1:
        cos = jnp.concatenate([cos] * reps, axis=1)
        sin = jnp.concatenate([sin] * reps, axis=1)
    first = (lax.broadcasted_iota(jnp.int32, (1, w), 1) % (2 * ROPE_PAIRS)) < ROPE_PAIRS
    partner = jnp.where(first, pltpu.roll(x, w - ROPE_PAIRS, 1), pltpu.roll(x, ROPE_PAIRS, 1))
    return x * cos + partner * sin


def _attend(q, k, v, valid, sink_col):
    s = _dot_nt(q, k)
    if valid is not None:
        s = jnp.where(valid, s, NEG_INF)
    m = jnp.maximum(jnp.max(s, axis=-1, keepdims=True), sink_col)
    p = jnp.exp(s - m)
    den = jnp.sum(p, axis=-1, keepdims=True) + jnp.exp(sink_col - m)
    return _dot(p, v) / den


def _gqa(q, k_all, v_all, valid, sink_ref):
    nq = q.shape[0]
    hd = HEAD_DIM
    group = q.shape[1] // hd // KV_HEADS
    outs = []
    for kvh in range(KV_HEADS):
        heads = [kvh * group + g for g in range(group)]
        qg = jnp.concatenate([q[:, h * hd:(h + 1) * hd] for h in heads], axis=0)
        sink_col = jnp.concatenate(
            [jnp.broadcast_to(sink_ref[0:1, h:h + 1], (nq, 1)) for h in heads], axis=0)
        o = _attend(qg, k_all[:, kvh * hd:(kvh + 1) * hd], v_all[:, kvh * hd:(kvh + 1) * hd],
                    valid, sink_col)
        outs += [o[g * nq:(g + 1) * nq, :] for g in range(group)]
    return jnp.concatenate(outs, axis=1)


def _attn_kernel(q_ref, *refs, seq):
    nkb = ATTN_QBLOCKS + 2
    k_refs, v_refs = refs[0:nkb], refs[nkb:2 * nkb]
    kc_ref, vc_ref = refs[2 * nkb:2 * nkb + 2]
    cos_refs = refs[2 * nkb + 2:3 * nkb + 2]
    sin_refs = refs[3 * nkb + 2:4 * nkb + 2]
    sink_ref, o_ref = refs[4 * nkb + 2:]
    blk = ATTN_BLOCK
    n_ctx = kc_ref.shape[1]
    group = q_ref.shape[2] // HEAD_DIM // KV_HEADS
    k_rot = [_rope(k_refs[j][0], cos_refs[j][...], sin_refs[j][...]) for j in range(nkb)]
    nk = 3 * blk + n_ctx
    qo = lax.broadcasted_iota(jnp.int32, (group * blk, 1), 0) % blk
    ko = lax.broadcasted_iota(jnp.int32, (1, nk), 1)
    in_window = jnp.abs(ko - blk - qo) <= WINDOW
    for s in range(ATTN_QBLOCKS):
        i = pl.program_id(1) * ATTN_QBLOCKS + s
        rows = slice(s * blk, (s + 1) * blk)
        q = _rope(q_ref[0, rows, :], cos_refs[s + 1][...], sin_refs[s + 1][...]) * (HEAD_DIM ** -0.5)
        k_all = jnp.concatenate(k_rot[s:s + 3] + [kc_ref[0]], axis=0)
        v_all = jnp.concatenate([v_refs[j][0] for j in range(s, s + 3)] + [vc_ref[0]], axis=0)
        kpos = (i - 1) * blk + ko
        valid = (in_window & (kpos >= 0) & (kpos < seq)) | (ko >= 3 * blk)
        o_ref[0, rows, :] = _gqa(q, k_all, v_all, valid, sink_ref)


def _attn_x(aq, ak, av, cos_t, sin_t, sink_row, n_ctx):
    b, l, wq = aq.shape
    wk = ak.shape[-1]
    blk = ATTN_BLOCK
    seq = l - n_ctx
    nb = seq // blk
    cb = n_ctx // blk
    nq = ATTN_QBLOCKS
    assert nb % nq == 0 and cb % nq == 0
    blocks = [lambda t, j=j: jnp.clip(t * nq + j - 1, 0, nb - 1) for j in range(nq + 2)]
    kv = [pl.BlockSpec((1, blk, wk), lambda i, t, f=f: (i, f(t) + cb, 0)) for f in blocks]
    tab = [pl.BlockSpec((blk, LANE), lambda i, t, f=f: (f(t), 0)) for f in blocks]
    ctx = pl.BlockSpec((1, n_ctx, wk), lambda i, t: (i, 0, 0))
    qspec = pl.BlockSpec((1, nq * blk, wq), lambda i, t: (i, t + cb // nq, 0))
    return pl.pallas_call(
        functools.partial(_attn_kernel, seq=seq),
        grid=(b, nb // nq),
        in_specs=[qspec] + kv + kv + [ctx, ctx] + tab + tab + [pl.BlockSpec((1, LANE), lambda i, t: (0, 0))],
        out_specs=pl.BlockSpec((1, nq * blk, wq), lambda i, t: (i, t, 0)),
        out_shape=jax.ShapeDtypeStruct((b, seq, wq), F32),
        compiler_params=_cparams("parallel", "parallel"), name="attn_x",
    )(aq, *([ak] * (nq + 2)), *([av] * (nq + 2)), ak, av, *([cos_t] * (nq + 2)), *([sin_t] * (nq + 2)),
      sink_row)


def _attn_ctx_kernel(q_ref, k_ref, v_ref, sink_ref, o_ref):
    o_ref[0] = _gqa(q_ref[0] * (HEAD_DIM ** -0.5), k_ref[0], v_ref[0], None, sink_ref)


def _attn_ctx(aq, ak, av, sink_row, n_ctx):
    b, _, wq = aq.shape
    wk = ak.shape[-1]
    spec = lambda w: pl.BlockSpec((1, n_ctx, w), lambda i: (i, 0, 0))
    return pl.pallas_call(
        _attn_ctx_kernel, grid=(b,),
        in_specs=[spec(wq), spec(wk), spec(wk), pl.BlockSpec((1, LANE), lambda i: (0, 0))],
        out_specs=spec(wq), out_shape=jax.ShapeDtypeStruct((b, n_ctx, wq), F32),
        compiler_params=_cparams("parallel"), name="attn_ctx",
    )(aq, ak, av, sink_row)


def _outproj_kernel(x_ref, ya_ref, yb_ref, yc_ref, w_ref, mod_ref, gain_ref, xo_ref, hx_ref, *, row_off, n_ctx):
    wa, wb = ya_ref.shape[-1], yb_ref.shape[-1]
    tile = x_ref.shape[1]
    acc = (jnp.dot(ya_ref[0].astype(BF16), w_ref[0:wa, :], preferred_element_type=F32)
           + jnp.dot(yb_ref[0].astype(BF16), w_ref[wa:wa + wb, :], preferred_element_type=F32)
           + jnp.dot(yc_ref[0].astype(BF16), w_ref[wa + wb:, :], preferred_element_type=F32))
    mod = _row_mod(mod_ref, row_off + pl.program_id(1) * tile, tile, n_ctx)
    x = x_ref[0] + mod(2) * acc
    xo_ref[0] = x
    hx_ref[0] = _modnorm(x, gain_ref[...], mod(4), mod(3)).astype(BF16)


def _out_proj(h, ya, yb, yc, w, mod, gain, n_ctx, skip_ctx):
    b, l, d = h.shape
    tile = TOK_TILE if skip_ctx else PROJ_TILE
    off = n_ctx // tile if skip_ctx else 0
    nt = l // tile - off
    full = lambda wd: pl.BlockSpec((1, tile, wd), lambda i, t: (i, t + off, 0))
    own = lambda wd: pl.BlockSpec((1, tile, wd), lambda i, t: (i, t, 0))
    return pl.pallas_call(
        functools.partial(_outproj_kernel, row_off=off * tile, n_ctx=n_ctx), grid=(b, nt),
        in_specs=[full(d), full(ya.shape[-1]), full(yb.shape[-1]), own(yc.shape[-1]),
                  pl.BlockSpec(w.shape, lambda i, t: (0, 0)),
                  pl.BlockSpec((1, 2, 6, d), lambda i, t: (i, 0, 0, 0)),
                  pl.BlockSpec((1, d), lambda i, t: (0, 0))],
        out_specs=[own(d), own(d)],
        out_shape=[jax.ShapeDtypeStruct((b, nt * tile, d), F32),
                   jax.ShapeDtypeStruct((b, nt * tile, d), BF16)],
        compiler_params=_cparams("parallel", "parallel"), name="out_proj",
    )(h, ya, yb, yc, w, mod, gain)


def _sort_pairs(n):
    pairs, p = [], 1
    while p < n:
        k = p
        while k >= 1:
            for j in range(k % p, n - k, 2 * k):
                for i in range(min(k, n - j - k)):
                    if (i + j) // (2 * p) == (i + j + k) // (2 * p):
                        pairs.append((i + j, i + j + k))
            k //= 2
        p *= 2
    return pairs


def _topk_rows(v):
    n = len(v)
    v = list(v)

    def cmpx(i, j):
        v[i], v[j] = jnp.maximum(v[i], v[j]), jnp.minimum(v[i], v[j])

    for i, j in _sort_pairs(n):
        cmpx(i, j)
    shift = SUBLANE // 2
    while shift >= 1:
        partner = [pltpu.roll(x, shift, 0) for x in v]
        v = [jnp.maximum(v[j], partner[n - 1 - j]) for j in range(n)]
        stride = n // 2
        while stride >= 1:
            for i in range(n):
                if not i & stride:
                    cmpx(i, i + stride)
            stride //= 2
        shift //= 2
    return v


def _route_kernel(ht_ref, wq_ref, keys_ref, r2_ref, e2_ref, n1_ref, e1_ref, qt_ref, top_ref, cand_ref):
    tm = ht_ref.shape[1]
    k = PEER_TOPK
    nk = N_KEYS
    neg = jnp.float32(-jnp.inf)
    qt_ref[...] = _dot(wq_ref[...], ht_ref[...]).astype(BF16)
    cand_ref[_N_CAND:_CAND_PAD, :] = jnp.full((_CAND_PAD - _N_CAND, tm), neg, F32)
    for h in range(PEER_HEADS):
        st = []
        for p in range(2):
            hp = 2 * h + p
            s = jnp.dot(keys_ref[hp], qt_ref[hp * nk:(hp + 1) * nk, :], preferred_element_type=F32)
            st.append(s)
            top = _topk_rows([s[g * SUBLANE:(g + 1) * SUBLANE, :] for g in range(nk // SUBLANE)])
            for r in range(k):
                top_ref[p * k + r:p * k + r + 1, :] = top[r][0:1, :]
        v1 = top_ref[0:k, :]
        v2 = top_ref[k:2 * k, :]
        rank2 = jnp.full((nk, tm), float(k), F32)
        for r in range(k):
            rank2 = jnp.where(st[1] == v2[r:r + 1, :], float(r), rank2)
        row = 0
        for a, nb in enumerate(_CAND_ROWS):
            cand_ref[row:row + nb, :] = v1[a:a + 1, :] + v2[0:nb, :]
            row += nb
        cur = cand_ref[...]
        cmax = thr = zsum = None
        for r in range(k):
            m = jnp.max(cur, axis=0, keepdims=True)
            if r == 0:
                cmax, zsum = m, jnp.ones_like(m)
            else:
                zsum = zsum + jnp.exp(m - cmax)
            thr = m
            cur = jnp.where(cur == m, neg, cur)
        n1 = jnp.zeros((nk, tm), F32)
        row = 0
        for a, nb in enumerate(_CAND_ROWS):
            sel = cand_ref[row:row + nb, :] >= thr
            n_a = jnp.sum(sel.astype(F32), axis=0, keepdims=True)
            n1 = jnp.where(st[0] == v1[a:a + 1, :], n_a, n1)
            row += nb
        r2_ref[h] = rank2.T
        e2_ref[h] = jnp.exp(st[1] - v2[0:1, :]).T
        n1_ref[h] = n1.T
        e1_ref[h] = (jnp.exp(st[0] - v1[0:1, :]) / zsum).T


def _peer_route(hxt, wq_t, keys):
    d, t = hxt.shape
    nq = wq_t.shape[0]
    tm = PEER_ROUTE_TILE
    rows = PEER_HEADS * N_KEYS
    return pl.pallas_call(
        _route_kernel, grid=(t // tm,),
        in_specs=[pl.BlockSpec((d, tm), lambda i: (0, i)),
                  pl.BlockSpec(wq_t.shape, lambda i: (0, 0)),
                  pl.BlockSpec(keys.shape, lambda i: (0, 0, 0))],
        out_specs=[pl.BlockSpec((PEER_HEADS, tm, N_KEYS), lambda i: (0, i, 0))] * 4,
        out_shape=[jax.ShapeDtypeStruct((PEER_HEADS, t, N_KEYS), F32)] * 4,
        scratch_shapes=[pltpu.VMEM((nq, tm), BF16), pltpu.VMEM((2 * PEER_TOPK, tm), F32),
                        pltpu.VMEM((_CAND_PAD, tm), F32)],
        compiler_params=_cparams("parallel"), name="peer_route",
    )(hxt, wq_t, keys)


def _gates_kernel(r2t_ref, e2t_ref, n1t_ref, e1t_ref, w_ref):
    tb = w_ref.shape[1]
    nk = N_KEYS
    k = PEER_TOPK
    rank = lax.broadcasted_iota(jnp.int32, (k, nk), 0).astype(F32)

    def tok(i, carry):
        for j in range(PEER_GATE_UNROLL):
            t = i * PEER_GATE_UNROLL + j
            a, b = [], []
            for h in range(PEER_HEADS):
                r2, e2, n1, e1 = (ref[h, pl.ds(t, k, stride=0), :]
                                  for ref in (r2t_ref, e2t_ref, n1t_ref, e1t_ref))
                a.append(jnp.where(n1 > rank, e1, 0.0))
                b.append(jnp.where(r2 == rank, e2, 0.0))
            w = _dot_tn(jnp.concatenate(a, axis=0), jnp.concatenate(b, axis=0))
            for g in range(nk // (2 * SUBLANE)):
                lo, hi = 2 * g * SUBLANE, (2 * g + 1) * SUBLANE
                w_ref[g, t] = pltpu.pack_elementwise([w[lo:lo + SUBLANE, :], w[hi:hi + SUBLANE, :]],
                                                     packed_dtype=jnp.bfloat16)
        return carry

    lax.fori_loop(0, tb // PEER_GATE_UNROLL, tok, 0)


def _peer_gates(route):
    t = route[0].shape[1]
    tb = PEER_GATE_TILE
    return pl.pallas_call(
        _gates_kernel, grid=(t // tb,),
        in_specs=[pl.BlockSpec((PEER_HEADS, tb, N_KEYS), lambda i: (0, i, 0))] * 4,
        out_specs=pl.BlockSpec((N_KEYS // (2 * SUBLANE), tb, SUBLANE, N_KEYS), lambda i: (0, i, 0, 0)),
        out_shape=jax.ShapeDtypeStruct((N_KEYS // (2 * SUBLANE), t, SUBLANE, N_KEYS), jnp.uint32),
        compiler_params=_cparams("parallel"), name="peer_gates",
    )(*route)


def _peer_kernel(h_ref, w_ref, ut_ref, v_ref, o_ref, g0_ref, g1_ref):
    e = pl.program_id(1)
    n_blk = pl.num_programs(1) - 1

    @pl.when(e == 0)
    def _():
        o_ref[...] = jnp.zeros(o_ref.shape, F32)
        g1_ref[...] = jnp.zeros(g1_ref.shape, BF16)

    for parity, g_w_ref, g_r_ref in ((0, g0_ref, g1_ref), (1, g1_ref, g0_ref)):
        for build in (True, False):
            @pl.when((e % 2 == parity) & ((e < n_blk) if build else (e == n_blk)))
            def _(g_w_ref=g_w_ref, g_r_ref=g_r_ref, build=build, parity=parity):
                _peer_step(h_ref, w_ref, ut_ref, v_ref, o_ref, g_w_ref, g_r_ref, build, parity)


def _peer_step(h_ref, w_ref, ut_ref, v_ref, acc_ref, g_w_ref, g_r_ref, build, half):
    sr = h_ref.shape[0] // PEER_SUB

    def sub(k, carry):
        rows = pl.ds(pl.multiple_of(k * sr, sr), sr)
        acc_ref[rows, :] += jnp.dot(g_r_ref[rows, :], v_ref[...].astype(BF16), preferred_element_type=F32)
        if not build:
            return carry
        act = jnp.dot(h_ref[rows, :], ut_ref[0], preferred_element_type=F32)
        act = 0.5 * act * (1.0 + lax.erf(act * 0.7071067811865476))
        for j in range(SUBLANE):
            ln = slice(j * N_KEYS, (j + 1) * N_KEYS)
            wj = pltpu.unpack_elementwise(w_ref[pl.ds(k * (sr * SUBLANE) + j, sr, stride=SUBLANE), :],
                                          index=half, packed_dtype=jnp.bfloat16, unpacked_dtype=F32)
            g_w_ref[rows, ln] = (wj * act[:, ln]).astype(BF16)
        return carry

    lax.fori_loop(0, PEER_SUB, sub, 0)


def _peer_dense(hx, w, ut, v_all, layer):
    t, d = hx.shape
    n_blk, _, eb = ut.shape
    tm = PEER_TOK_TILE
    last = n_blk - 1
    return pl.pallas_call(
        _peer_kernel, grid=(t // tm, n_blk + 1),
        in_specs=[pl.BlockSpec((tm, d), lambda i, e: (i, 0)),
                  pl.BlockSpec((tm * SUBLANE, N_KEYS),
                               lambda i, e: (jnp.minimum(e, last) // 2 * (t // tm) + i, 0)),
                  pl.BlockSpec((1, d, eb), lambda i, e: (jnp.minimum(e, last), 0, 0)),
                  pl.BlockSpec((None, eb, d), lambda i, e: (layer, jnp.maximum(e - 1, 0), 0))],
        out_specs=pl.BlockSpec((tm, d), lambda i, e: (i, 0)),
        out_shape=jax.ShapeDtypeStruct((t, d), F32),
        scratch_shapes=[pltpu.VMEM((tm, eb), BF16), pltpu.VMEM((tm, eb), BF16)],
        compiler_params=_cparams("parallel", "arbitrary"), name="peer_dense",
    )(hx, w, ut, v_all)


def _final_kernel(x_ref, f_ref, mod_ref, gain_ref, o_ref):
    x = x_ref[0] + mod_ref[0, 0, 5:6, :] * f_ref[0]
    o_ref[0] = x * lax.rsqrt(jnp.mean(x * x, axis=-1, keepdims=True) + EPS) * gain_ref[...]


def _final(x, ffn, mod, gain):
    b, s, d = x.shape
    tok = pl.BlockSpec((1, TOK_TILE, d), lambda i, t: (i, t, 0))
    return pl.pallas_call(
        _final_kernel, grid=(b, s // TOK_TILE),
        in_specs=[tok, tok, pl.BlockSpec((1, 1, 6, d), lambda i, t: (i, 1, 0, 0)),
                  pl.BlockSpec((1, d), lambda i, t: (0, 0))],
        out_specs=tok, out_shape=jax.ShapeDtypeStruct((b, s, d), F32),
        compiler_params=_cparams("parallel", "parallel"), name="final_norm",
    )(x, ffn, mod, gain)


def _rope_tables(seq):
    pos = jnp.arange(seq)
    rc = jnp.stack([pos // GRID_W, pos % GRID_W], axis=-1).astype(F32)
    inv = jnp.power(ROPE_BASE, -jnp.arange(ROPE_PAIRS, dtype=F32) / ROPE_PAIRS)
    ang = rc[:, :, None] * inv
    cos = jnp.concatenate([jnp.cos(ang)] * 2, axis=-1).reshape(seq, HEAD_DIM)
    sin = jnp.concatenate([-jnp.sin(ang), jnp.sin(ang)], axis=-1).reshape(seq, HEAD_DIM)
    return jnp.concatenate([cos] * 2, axis=1), jnp.concatenate([sin] * 2, axis=1)


def _pair_lanes(cols, n_heads):
    rows = cols.shape[0]
    c = cols.reshape(rows, 4, n_heads // 2, 2).transpose(0, 2, 1, 3).reshape(rows, n_heads // 2, 8)
    return jnp.pad(c, ((0, 0), (0, 0), (0, LANE - 8))).reshape(rows, n_heads // 2 * LANE)


def kernel(x, c, ctx, c_ctx, w_ada, b_ada, norm1, norm2, w_in, conv_w, a_log, dt_bias, gdn_norm, w_pool,
           pool_scale, sink, w_out, peer_wq, peer_keys, peer_u, peer_v, norm_f):
    b, seq, d = x.shape
    n_ctx = ctx.shape[1]
    depth = w_ada.shape[0]
    pool_w = d // 4
    gdn_w = 3 * d // 8
    gdn_heads = gdn_w // HEAD_DIM
    attn_w = d - pool_w - gdn_w
    kv_w = KV_HEADS * HEAD_DIM
    assert n_ctx % TOK_TILE == 0 and seq % TOK_TILE == 0 and gdn_heads % 2 == 0
    assert (n_ctx + seq) % PROJ_TILE == 0 and PEER_EXP_TILE == SUBLANE * N_KEYS
    assert (b * (n_ctx + seq)) % PEER_TOK_TILE == 0 and (b * seq) % PEER_TOK_TILE == 0

    rows = -(-(b + 1) // SUBLANE) * SUBLANE
    c_all = jnp.zeros((rows, d), F32).at[:b].set(c).at[b].set(c_ctx)
    mod = _ada_mod(c_all, w_ada, b_ada)
    mod_x = mod[:, :b].reshape(depth, b, 1, 6, d)
    mod_c = jnp.broadcast_to(mod[:, b].reshape(depth, 1, 1, 6, d), (depth, b, 1, 6, d))
    mod = jnp.concatenate([mod_c, mod_x], axis=2)

    cos_t, sin_t = _rope_tables(seq)
    splits = (pool_w, 3 * gdn_w, gdn_w, gdn_heads // 2 * LANE, attn_w, kv_w, kv_w)
    o_ba = pool_w + 4 * gdn_w
    n_ba = 4 * gdn_heads

    h = jnp.concatenate([ctx, x], axis=1)
    ffn = None
    for l in range(depth):
        last = l == depth - 1
        w_l = w_in[l]
        w_big = jnp.concatenate([w_l[:, :o_ba], _pair_lanes(w_l[:, o_ba:o_ba + n_ba], gdn_heads),
                                 w_l[:, o_ba + n_ba:]], axis=1).astype(BF16)
        pmod = mod[l - 1] if l > 0 else None
        outs = _in_proj(h, ffn, pmod, mod[l], norm1[l][None], w_big, splits, n_ctx)
        if l > 0:
            h, *outs = outs
        pa, qkv, z, ba, aq, ak, av = outs

        wbd = jax.scipy.linalg.block_diag(*[w_pool[l, g] for g in range(len(POOL_WINDOWS))])
        ya = _pool(pa, wbd, pool_scale[l][None], n_ctx)

        gparams = jnp.stack([_pair_lanes(jnp.concatenate([jnp.zeros_like(a_log[l]), a_log[l]]).reshape(1, -1),
                                         gdn_heads),
                             _pair_lanes(jnp.concatenate([jnp.zeros_like(dt_bias[l]), dt_bias[l]]).reshape(1, -1),
                                         gdn_heads)], axis=1)
        gparams = gparams.reshape(2, gdn_heads // 2, LANE).transpose(1, 0, 2)
        gparams = jnp.pad(gparams, ((0, 0), (0, SUBLANE - 2), (0, 0)))
        yb = _gdn(qkv, z, ba, conv_w[l], gparams, jnp.concatenate([gdn_norm[l]] * 2)[None], n_ctx)

        sink_row = jnp.pad(sink[l], (0, LANE - sink.shape[1]))[None]
        yc = _attn_x(aq, ak, av, cos_t, sin_t, sink_row, n_ctx)
        if not last:
            yc = jnp.concatenate([_attn_ctx(aq, ak, av, sink_row, n_ctx), yc], axis=1)

        x_new, hx = _out_proj(h, ya, yb, yc, w_out[l].astype(BF16), mod[l], norm2[l][None], n_ctx,
                              skip_ctx=last)
        t = hx.shape[0] * hx.shape[1]
        hx = hx.reshape(t, d)
        keys = peer_keys[l].reshape(2 * PEER_HEADS, N_KEYS, -1).astype(BF16)
        route = _peer_route(hx.T, peer_wq[l].T.astype(BF16), keys)
        gates = _peer_gates(route).reshape(-1, N_KEYS)
        ut = peer_u[l].reshape(-1, PEER_EXP_TILE, d).transpose(0, 2, 1).astype(BF16)
        ffn = _peer_dense(hx, gates, ut, peer_v, l)
        ffn = ffn.reshape(x_new.shape)
        h = x_new
    return _final(h, ffn, mod[depth - 1], norm_f[None])
```

```python
import functools

import jax
import jax.numpy as jnp
from jax import lax
from jax.experimental import pallas as pl
from jax.experimental.pallas import tpu as pltpu

F32, BF16 = jnp.float32, jnp.bfloat16
HIGHEST = lax.Precision.HIGHEST

EPS = 1e-6
NEG_INF = -1e30
HEAD_DIM = 64
GRID_W = 64
POOL_WINDOWS = (2, 4, 8, 16)
GDN_CHUNK = 64
CONV_K = 5
KV_HEADS = 2
WINDOW = 128
ATTN_BLOCK = 128
ROPE_BASE = 10000.0
ROPE_PAIRS = HEAD_DIM // 4
PEER_HEADS = 8
N_KEYS = 128
PEER_TOPK = 16

LANE = 128
SUBLANE = 8
TOK_TILE = 256
PROJ_TILE = 768
ROW_CHUNK = 256
ATTN_QBLOCKS = 2
GDN_GROUP = 4
PEER_ROUTE_TILE = 512
PEER_GATE_TILE = 256
PEER_GATE_UNROLL = 64
PEER_TOK_TILE = 1024
PEER_EXP_TILE = 1024
PEER_SUB = 1
VMEM_LIMIT = 48 * 1024 * 1024

_CAND_ROWS = tuple(PEER_TOPK // (a + 1) for a in range(PEER_TOPK))
_N_CAND = sum(_CAND_ROWS)
_CAND_PAD = -(-_N_CAND // SUBLANE) * SUBLANE


def _cparams(*sem):
    return pltpu.CompilerParams(dimension_semantics=sem, vmem_limit_bytes=VMEM_LIMIT)


def _dot(a, b):
    return jnp.dot(a.astype(BF16), b.astype(BF16), preferred_element_type=F32)


def _dot_nt(a, b):
    return lax.dot_general(a.astype(BF16), b.astype(BF16), (((1,), (1,)), ((), ())),
                           preferred_element_type=F32)


def _dot_tn(a, b):
    return lax.dot_general(a.astype(BF16), b.astype(BF16), (((0,), (0,)), ((), ())),
                           preferred_element_type=F32)


def _dot_hi(a, b):
    return jnp.dot(a, b, precision=HIGHEST, preferred_element_type=F32)


def _silu(x):
    return x * jax.nn.sigmoid(x)


def _softplus(x):
    return jnp.maximum(x, 0.0) + jnp.log1p(jnp.exp(-jnp.abs(x)))


def _modnorm(x, gain, scale, shift):
    y = x * lax.rsqrt(jnp.mean(x * x, axis=-1, keepdims=True) + EPS) * gain
    return y * (1.0 + scale) + shift


def _ada_kernel(c_ref, w_ref, b_ref, o_ref):
    o_ref[0] = _dot_hi(_silu(c_ref[...]), w_ref[0]) + b_ref[0]


def _ada_mod(c_all, w_ada, b_ada):
    depth, d, n = w_ada.shape
    rows = c_all.shape[0]
    tn = n // 4
    return pl.pallas_call(
        _ada_kernel,
        grid=(depth, n // tn),
        in_specs=[pl.BlockSpec((rows, d), lambda l, j: (0, 0)),
                  pl.BlockSpec((1, d, tn), lambda l, j: (l, 0, j)),
                  pl.BlockSpec((1, 1, tn), lambda l, j: (l, 0, j))],
        out_specs=pl.BlockSpec((1, rows, tn), lambda l, j: (l, 0, j)),
        out_shape=jax.ShapeDtypeStruct((depth, rows, n), F32),
        compiler_params=_cparams("parallel", "parallel"),
        name="ada_mod",
    )(c_all, w_ada, b_ada.reshape(depth, 1, n))


def _row_mod(mod_ref, first_row, n_rows, n_ctx):
    is_x = first_row + lax.broadcasted_iota(jnp.int32, (n_rows, 1), 0) >= n_ctx
    return lambda k: jnp.where(is_x, mod_ref[0, 1, k:k + 1, :], mod_ref[0, 0, k:k + 1, :])


def _inproj_kernel(*refs, has_ffn, col_splits, n_ctx):
    tile = refs[0].shape[1]
    first_row = pl.program_id(1) * tile
    if has_ffn:
        h_ref, f_ref, pmod_ref, mod_ref, gain_ref, w_ref, res_ref, *outs = refs
        x = h_ref[0] + _row_mod(pmod_ref, first_row, tile, n_ctx)(5) * f_ref[0]
        res_ref[0] = x
    else:
        h_ref, mod_ref, gain_ref, w_ref, *outs = refs
        x = h_ref[0]
    mod = _row_mod(mod_ref, first_row, tile, n_ctx)
    hb = _modnorm(x, gain_ref[...], mod(1), mod(0)).astype(BF16)
    off = 0
    for o_ref, width in zip(outs, col_splits):
        o_ref[0] = jnp.dot(hb, w_ref[:, off:off + width], preferred_element_type=F32)
        off += width


def _in_proj(h, ffn, pmod, mod, gain, w, col_splits, n_ctx):
    b, l, d = h.shape
    tile = PROJ_TILE
    tok = pl.BlockSpec((1, tile, d), lambda i, t: (i, t, 0))
    modspec = pl.BlockSpec((1, 2, 6, d), lambda i, t: (i, 0, 0, 0))
    has_ffn = ffn is not None
    in_specs = [tok] + ([tok, modspec] if has_ffn else []) + [
        modspec, pl.BlockSpec((1, d), lambda i, t: (0, 0)), pl.BlockSpec(w.shape, lambda i, t: (0, 0))]
    out_shape = [jax.ShapeDtypeStruct((b, l, wd), F32) for wd in col_splits]
    out_specs = [pl.BlockSpec((1, tile, wd), lambda i, t: (i, t, 0)) for wd in col_splits]
    if has_ffn:
        out_shape = [jax.ShapeDtypeStruct((b, l, d), F32)] + out_shape
        out_specs = [tok] + out_specs
    args = (h, ffn, pmod, mod, gain, w) if has_ffn else (h, mod, gain, w)
    return pl.pallas_call(
        functools.partial(_inproj_kernel, has_ffn=has_ffn, col_splits=col_splits, n_ctx=n_ctx),
        grid=(b, l // tile), in_specs=in_specs, out_specs=out_specs, out_shape=out_shape,
        compiler_params=_cparams("parallel", "parallel"), name="in_proj",
    )(*args)


def _pool_kernel(a_ref, wbd_ref, scale_ref, o_ref, pad_ref, *, segments):
    c = a_ref.shape[-1]
    grp = lax.broadcasted_iota(jnp.int32, (1, c), 1) // (c // len(POOL_WINDOWS))
    zeros = jnp.zeros((SUBLANE, c), F32)
    for off, n in segments:
        pad_ref[0:SUBLANE, :] = zeros
        pad_ref[SUBLANE:SUBLANE + n, :] = a_ref[0, off:off + n, :]
        pad_ref[SUBLANE + n:2 * SUBLANE + n, :] = zeros

        def chunk(ci, carry, off=off, n=n):
            r0 = pl.multiple_of(ci * ROW_CHUNK, ROW_CHUNK)
            win = pad_ref[pl.ds(r0, ROW_CHUNK + 2 * SUBLANE), :]
            sh = lambda s: win[SUBLANE + s:SUBLANE + s + ROW_CHUNK, :]
            a0 = sh(0)
            s2 = sh(-1) + a0
            s4 = s2 + sh(-2) + sh(1)
            s8 = s4 + sh(-4) + sh(-3) + sh(2) + sh(3)
            s16 = s8 + (sh(-8) + sh(-7) + sh(-6) + sh(-5)) + (sh(4) + sh(5) + sh(6) + sh(7))
            t = r0 + lax.broadcasted_iota(jnp.int32, (ROW_CHUNK, 1), 0)

            def mean(s, w):
                cnt = jnp.minimum(t + (w - w // 2), n) - jnp.maximum(t - w // 2, 0)
                return s / cnt.astype(F32)

            pooled = jnp.where(grp == 0, mean(s2, 2), jnp.where(grp == 1, mean(s4, 4),
                               jnp.where(grp == 2, mean(s8, 8), mean(s16, 16)))) - a0
            o_ref[0, pl.ds(off + r0, ROW_CHUNK), :] = _dot_hi(pooled, wbd_ref[...]) * scale_ref[...]
            return carry

        lax.fori_loop(0, n // ROW_CHUNK, chunk, 0, unroll=2)


def _pool(a, wbd, scale, n_ctx):
    b, l, c = a.shape
    segments = ((0, n_ctx), (n_ctx, l - n_ctx))
    return pl.pallas_call(
        functools.partial(_pool_kernel, segments=segments),
        grid=(b,),
        in_specs=[pl.BlockSpec((1, l, c), lambda i: (i, 0, 0)),
                  pl.BlockSpec((c, c), lambda i: (0, 0)),
                  pl.BlockSpec((1, c), lambda i: (0, 0))],
        out_specs=pl.BlockSpec((1, l, c), lambda i: (i, 0, 0)),
        out_shape=jax.ShapeDtypeStruct((b, l, c), F32),
        scratch_shapes=[pltpu.VMEM((l - n_ctx + 2 * SUBLANE, c), F32)],
        compiler_params=_cparams("parallel"), name="pool",
    )(a, wbd, scale)


def _gdn_kernel(q_ref, k_ref, v_ref, z_ref, ba_ref, cwq_ref, cwk_ref, cwv_ref, gp_ref, gain_ref, o_ref,
                pad_ref, yq_ref, yk_ref, yv_ref, bg_ref, of_ref, ob_ref, s_ref,
                qe_ref, mn_ref, oc_ref, nn_ref, egl_ref, *, n_ctx):
    l = q_ref.shape[1]
    n_x = l - n_ctx
    hd = HEAD_DIM
    lane = lax.broadcasted_iota(jnp.int32, (1, LANE), 1)
    lo = lane < hd
    zeros = jnp.zeros((SUBLANE, LANE), F32)
    x_off = 2 * SUBLANE + n_ctx

    def head_scale(ss):
        s_lo = jnp.sum(jnp.where(lo, ss, 0.0), axis=-1, keepdims=True)
        s_hi = jnp.sum(jnp.where(lo, 0.0, ss), axis=-1, keepdims=True)
        return jnp.where(lo, s_lo, s_hi)

    def conv(u_ref, cw_ref, y_ref, post):
        pad_ref[0:SUBLANE, :] = zeros
        pad_ref[SUBLANE:SUBLANE + n_ctx, :] = u_ref[0, 0:n_ctx, :]
        pad_ref[SUBLANE + n_ctx:x_off, :] = zeros
        pad_ref[x_off:x_off + n_x, :] = u_ref[0, n_ctx:l, :]
        pad_ref[x_off + n_x:x_off + n_x + SUBLANE, :] = zeros
        cw = cw_ref[...]
        for poff, yoff, n in ((SUBLANE, 0, n_ctx), (x_off, n_ctx, n_x)):
            def chunk(ci, carry, poff=poff, yoff=yoff):
                r0 = pl.multiple_of(ci * ROW_CHUNK, ROW_CHUNK)
                win = pad_ref[pl.ds(poff - SUBLANE + r0, ROW_CHUNK + 2 * SUBLANE), :]
                acc = win[SUBLANE - 2:SUBLANE - 2 + ROW_CHUNK, :] * cw[0:1, :]
                for j in range(1, CONV_K):
                    acc = acc + win[SUBLANE - 2 + j:SUBLANE - 2 + j + ROW_CHUNK, :] * cw[j:j + 1, :]
                y_ref[pl.ds(yoff + r0, ROW_CHUNK), :] = post(_silu(acc))
                return carry
            lax.fori_loop(0, n // ROW_CHUNK, chunk, 0, unroll=4)

    l2 = lambda y: y * lax.rsqrt(head_scale(y * y) + EPS)
    conv(q_ref, cwq_ref, yq_ref, lambda y: l2(y) * (hd ** -0.5))
    conv(k_ref, cwk_ref, yk_ref, l2)
    conv(v_ref, cwv_ref, yv_ref, lambda y: y)

    ba = ba_ref[0]
    g = -jnp.exp(gp_ref[0, 0:1, :]) * _softplus(ba + gp_ref[0, 1:2, :])
    bg_ref[...] = jnp.where(lane < 4, jax.nn.sigmoid(ba), g)

    ck = GDN_CHUNK
    rr = lax.broadcasted_iota(jnp.int32, (ck, ck), 0)
    cc = lax.broadcasted_iota(jnp.int32, (ck, ck), 1)
    eye = (rr == cc).astype(F32)
    tril = (rr >= cc).astype(F32)
    n_chunks = l // ck
    nc_ctx = n_ctx // ck

    def phase1(ci, carry):
        probs = []
        for gi in range(GDN_GROUP):
            c = ci * GDN_GROUP + gi
            r0 = pl.multiple_of(c * ck, ck)
            qc = yq_ref[pl.ds(r0, ck), :]
            kc = yk_ref[pl.ds(r0, ck), :]
            vc = yv_ref[pl.ds(r0, ck), :]
            bgc = bg_ref[pl.ds(r0, ck), :]
            gcf = _dot_hi(tril, bgc)
            gcb = gcf[ck - 1:ck, :] - gcf + bgc
            gct = (gcf.T, gcb.T)
            for j in range(2):
                qh = qc[:, j * hd:(j + 1) * hd]
                kh = kc[:, j * hd:(j + 1) * hd]
                vh = vc[:, j * hd:(j + 1) * hd]
                kk = _dot_nt(kh, kh)
                qk = _dot_nt(qh, kh)
                for d in range(2):
                    mask = (rr >= cc) if d == 0 else (rr <= cc)
                    smask = (rr > cc) if d == 0 else (rr < cc)
                    lg, lb = 4 + 2 * d + j, 2 * d + j
                    gcol = jnp.broadcast_to((gcf, gcb)[d][:, lg:lg + 1], (ck, hd))
                    grow = gct[d][lg:lg + 1, :]
                    bcol = jnp.broadcast_to(bgc[:, lb:lb + 1], (ck, hd))
                    decay = jnp.where(mask, jnp.exp(jnp.where(mask, gcol - grow, 0.0)), 0.0)
                    xp = jnp.where(smask, -(bcol * kk * decay), 0.0)
                    eg = jnp.exp(gcol)
                    last = ck - 1 if d == 0 else 0
                    glast = gcol[last:last + 1, :]
                    probs.append(dict(
                        idx=c * 4 + 2 * d + j, xp=xp, inv=eye + xp, attn=qk * decay, qd=qh * eg,
                        rhs=jnp.concatenate([vh * bcol, kh * bcol * eg], axis=1),
                        kd=kh * jnp.exp(glast - gcol), egl=jnp.exp(glast)))
        for _ in range(5):
            for p in probs:
                p["xp"] = _dot(p["xp"], p["xp"])
            for p in probs:
                p["inv"] = p["inv"] + _dot(p["inv"], p["xp"])
        for p in probs:
            p["sol"] = _dot(p["inv"], p["rhs"])
        for p in probs:
            p["as"] = _dot(p["attn"], p["sol"])
        for p in probs:
            p["ks"] = _dot_tn(p["kd"], p["sol"])
        for p in probs:
            i = p["idx"]
            qe_ref[i] = (p["qd"] - p["as"][:, hd:]).astype(BF16)
            oc_ref[i] = p["as"][:, :hd]
            mn_ref[i] = p["ks"][:, hd:].astype(BF16)
            nn_ref[i] = p["ks"][:, :hd]
            egl_ref[i] = jnp.broadcast_to(p["egl"], (ck, hd))
        return carry

    lax.fori_loop(0, n_chunks // GDN_GROUP, phase1, 0)

    s_ref[...] = jnp.zeros(s_ref.shape, F32)

    def phase2(i, carry):
        cb = jnp.where(i < nc_ctx, nc_ctx - 1 - i, n_chunks - 1 - (i - nc_ctx))
        work = [(d, j, cidx * 4 + 2 * d + j) for d, cidx in ((0, i), (1, cb)) for j in range(2)]
        states = [s_ref[2 * d + j] for d, j, _ in work]
        outs = [_dot(qe_ref[idx], s) + oc_ref[idx] for (_, _, idx), s in zip(work, states)]
        upd = [_dot(mn_ref[idx], s) for (_, _, idx), s in zip(work, states)]
        for (d, j, idx), s, m in zip(work, states, upd):
            s_ref[2 * d + j] = egl_ref[idx] * s - m + nn_ref[idx]
        of_ref[pl.ds(pl.multiple_of(i * ck, ck), ck), :] = jnp.concatenate(outs[0:2], axis=1)
        ob_ref[pl.ds(pl.multiple_of(cb * ck, ck), ck), :] = jnp.concatenate(outs[2:4], axis=1)
        return carry

    lax.fori_loop(0, n_chunks, phase2, 0, unroll=12)

    def finish(ci, carry):
        r0 = pl.multiple_of(ci * ROW_CHUNK, ROW_CHUNK)
        o = of_ref[pl.ds(r0, ROW_CHUNK), :] + ob_ref[pl.ds(r0, ROW_CHUNK), :]
        o = o * lax.rsqrt(head_scale(o * o) * (1.0 / hd) + EPS) * gain_ref[...]
        o_ref[0, pl.ds(r0, ROW_CHUNK), :] = o * _silu(z_ref[0, pl.ds(r0, ROW_CHUNK), :])
        return carry

    lax.fori_loop(0, l // ROW_CHUNK, finish, 0, unroll=3)


def _gdn(qkv, z, ba, conv_w, gparams, gain2, n_ctx):
    b, l, w3 = qkv.shape
    npair = w3 // (3 * LANE)
    seq = lambda off: pl.BlockSpec((1, l, LANE), lambda i, p: (i, 0, p + off))
    cw = lambda off: pl.BlockSpec((CONV_K, LANE), lambda i, p: (0, p + off))
    return pl.pallas_call(
        functools.partial(_gdn_kernel, n_ctx=n_ctx),
        grid=(b, npair),
        in_specs=[seq(0), seq(npair), seq(2 * npair), seq(0), seq(0),
                  cw(0), cw(npair), cw(2 * npair),
                  pl.BlockSpec((1, SUBLANE, LANE), lambda i, p: (p, 0, 0)),
                  pl.BlockSpec((1, LANE), lambda i, p: (0, 0))],
        out_specs=seq(0),
        out_shape=jax.ShapeDtypeStruct((b, l, npair * LANE), F32),
        scratch_shapes=[pltpu.VMEM((l + 3 * SUBLANE, LANE), F32)] + [pltpu.VMEM((l, LANE), F32)] * 6
        + [pltpu.VMEM((4, HEAD_DIM, HEAD_DIM), F32)]
        + [pltpu.VMEM((4 * l // GDN_CHUNK, GDN_CHUNK, HEAD_DIM), dt) for dt in (BF16, BF16, F32, F32, F32)],
        compiler_params=_cparams("parallel", "parallel"), name="gdn",
    )(qkv, qkv, qkv, z, ba, conv_w, conv_w, conv_w, gparams, gain2)


def _rope(x, cos, sin):
    w = x.shape[-1]
    reps = w // cos.shape[-1]
    if reps > 1:
        cos = jnp.concatenate([cos] * reps, axis=1)
        sin = jnp.concatenate([sin] * reps, axis=1)
    first = (lax.broadcasted_iota(jnp.int32, (1, w), 1) % (2 * ROPE_PAIRS)) < ROPE_PAIRS
    partner = jnp.where(first, pltpu.roll(x, w - ROPE_PAIRS, 1), pltpu.roll(x, ROPE_PAIRS, 1))
    return x * cos + partner * sin


def _attend(q, k, v, valid, sink_col):
    s = _dot_nt(q, k)
    if valid is not None:
        s = jnp.where(valid, s, NEG_INF)
    m = jnp.maximum(jnp.max(s, axis=-1, keepdims=True), sink_col)
    p = jnp.exp(s - m)
    den = jnp.sum(p, axis=-1, keepdims=True) + jnp.exp(sink_col - m)
    return _dot(p, v) / den


def _gqa(q, k_all, v_all, valid, sink_ref):
    nq = q.shape[0]
    hd = HEAD_DIM
    group = q.shape[1] // hd // KV_HEADS
    outs = []
    for kvh in range(KV_HEADS):
        heads = [kvh * group + g for g in range(group)]
        qg = jnp.concatenate([q[:, h * hd:(h + 1) * hd] for h in heads], axis=0)
        sink_col = jnp.concatenate(
            [jnp.broadcast_to(sink_ref[0:1, h:h + 1], (nq, 1)) for h in heads], axis=0)
        o = _attend(qg, k_all[:, kvh * hd:(kvh + 1) * hd], v_all[:, kvh * hd:(kvh + 1) * hd],
                    valid, sink_col)
        outs += [o[g * nq:(g + 1) * nq, :] for g in range(group)]
    return jnp.concatenate(outs, axis=1)


def _attn_kernel(q_ref, *refs, seq):
    nkb = ATTN_QBLOCKS + 2
    k_refs, v_refs = refs[0:nkb], refs[nkb:2 * nkb]
    kc_ref, vc_ref = refs[2 * nkb:2 * nkb + 2]
    cos_refs = refs[2 * nkb + 2:3 * nkb + 2]
    sin_refs = refs[3 * nkb + 2:4 * nkb + 2]
    sink_ref, o_ref = refs[4 * nkb + 2:]
    blk = ATTN_BLOCK
    n_ctx = kc_ref.shape[1]
    group = q_ref.shape[2] // HEAD_DIM // KV_HEADS
    k_rot = [_rope(k_refs[j][0], cos_refs[j][...], sin_refs[j][...]) for j in range(nkb)]
    nk = 3 * blk + n_ctx
    qo = lax.broadcasted_iota(jnp.int32, (group * blk, 1), 0) % blk
    ko = lax.broadcasted_iota(jnp.int32, (1, nk), 1)
    in_window = jnp.abs(ko - blk - qo) <= WINDOW
    for s in range(ATTN_QBLOCKS):
        i = pl.program_id(1) * ATTN_QBLOCKS + s
        rows = slice(s * blk, (s + 1) * blk)
        q = _rope(q_ref[0, rows, :], cos_refs[s + 1][...], sin_refs[s + 1][...]) * (HEAD_DIM ** -0.5)
        k_all = jnp.concatenate(k_rot[s:s + 3] + [kc_ref[0]], axis=0)
        v_all = jnp.concatenate([v_refs[j][0] for j in range(s, s + 3)] + [vc_ref[0]], axis=0)
        kpos = (i - 1) * blk + ko
        valid = (in_window & (kpos >= 0) & (kpos < seq)) | (ko >= 3 * blk)
        o_ref[0, rows, :] = _gqa(q, k_all, v_all, valid, sink_ref)


def _attn_x(aq, ak, av, cos_t, sin_t, sink_row, n_ctx):
    b, l, wq = aq.shape
    wk = ak.shape[-1]
    blk = ATTN_BLOCK
    seq = l - n_ctx
    nb = seq // blk
    cb = n_ctx // blk
    nq = ATTN_QBLOCKS
    assert nb % nq == 0 and cb % nq == 0
    blocks = [lambda t, j=j: jnp.clip(t * nq + j - 1, 0, nb - 1) for j in range(nq + 2)]
    kv = [pl.BlockSpec((1, blk, wk), lambda i, t, f=f: (i, f(t) + cb, 0)) for f in blocks]
    tab = [pl.BlockSpec((blk, LANE), lambda i, t, f=f: (f(t), 0)) for f in blocks]
    ctx = pl.BlockSpec((1, n_ctx, wk), lambda i, t: (i, 0, 0))
    qspec = pl.BlockSpec((1, nq * blk, wq), lambda i, t: (i, t + cb // nq, 0))
    return pl.pallas_call(
        functools.partial(_attn_kernel, seq=seq),
        grid=(b, nb // nq),
        in_specs=[qspec] + kv + kv + [ctx, ctx] + tab + tab + [pl.BlockSpec((1, LANE), lambda i, t: (0, 0))],
        out_specs=pl.BlockSpec((1, nq * blk, wq), lambda i, t: (i, t, 0)),
        out_shape=jax.ShapeDtypeStruct((b, seq, wq), F32),
        compiler_params=_cparams("parallel", "parallel"), name="attn_x",
    )(aq, *([ak] * (nq + 2)), *([av] * (nq + 2)), ak, av, *([cos_t] * (nq + 2)), *([sin_t] * (nq + 2)),
      sink_row)


def _attn_ctx_kernel(q_ref, k_ref, v_ref, sink_ref, o_ref):
    o_ref[0] = _gqa(q_ref[0] * (HEAD_DIM ** -0.5), k_ref[0], v_ref[0], None, sink_ref)


def _attn_ctx(aq, ak, av, sink_row, n_ctx):
    b, _, wq = aq.shape
    wk = ak.shape[-1]
    spec = lambda w: pl.BlockSpec((1, n_ctx, w), lambda i: (i, 0, 0))
    return pl.pallas_call(
        _attn_ctx_kernel, grid=(b,),
        in_specs=[spec(wq), spec(wk), spec(wk), pl.BlockSpec((1, LANE), lambda i: (0, 0))],
        out_specs=spec(wq), out_shape=jax.ShapeDtypeStruct((b, n_ctx, wq), F32),
        compiler_params=_cparams("parallel"), name="attn_ctx",
    )(aq, ak, av, sink_row)


def _outproj_kernel(x_ref, ya_ref, yb_ref, yc_ref, w_ref, mod_ref, gain_ref, xo_ref, hx_ref, *, row_off, n_ctx):
    wa, wb = ya_ref.shape[-1], yb_ref.shape[-1]
    tile = x_ref.shape[1]
    acc = (jnp.dot(ya_ref[0].astype(BF16), w_ref[0:wa, :], preferred_element_type=F32)
           + jnp.dot(yb_ref[0].astype(BF16), w_ref[wa:wa + wb, :], preferred_element_type=F32)
           + jnp.dot(yc_ref[0].astype(BF16), w_ref[wa + wb:, :], preferred_element_type=F32))
    mod = _row_mod(mod_ref, row_off + pl.program_id(1) * tile, tile, n_ctx)
    x = x_ref[0] + mod(2) * acc
    xo_ref[0] = x
    hx_ref[0] = _modnorm(x, gain_ref[...], mod(4), mod(3)).astype(BF16)


def _out_proj(h, ya, yb, yc, w, mod, gain, n_ctx, skip_ctx):
    b, l, d = h.shape
    tile = TOK_TILE if skip_ctx else PROJ_TILE
    off = n_ctx // tile if skip_ctx else 0
    nt = l // tile - off
    full = lambda wd: pl.BlockSpec((1, tile, wd), lambda i, t: (i, t + off, 0))
    own = lambda wd: pl.BlockSpec((1, tile, wd), lambda i, t: (i, t, 0))
    return pl.pallas_call(
        functools.partial(_outproj_kernel, row_off=off * tile, n_ctx=n_ctx), grid=(b, nt),
        in_specs=[full(d), full(ya.shape[-1]), full(yb.shape[-1]), own(yc.shape[-1]),
                  pl.BlockSpec(w.shape, lambda i, t: (0, 0)),
                  pl.BlockSpec((1, 2, 6, d), lambda i, t: (i, 0, 0, 0)),
                  pl.BlockSpec((1, d), lambda i, t: (0, 0))],
        out_specs=[own(d), own(d)],
        out_shape=[jax.ShapeDtypeStruct((b, nt * tile, d), F32),
                   jax.ShapeDtypeStruct((b, nt * tile, d), BF16)],
        compiler_params=_cparams("parallel", "parallel"), name="out_proj",
    )(h, ya, yb, yc, w, mod, gain)


def _sort_pairs(n):
    pairs, p = [], 1
    while p < n:
        k = p
        while k >= 1:
            for j in range(k % p, n - k, 2 * k):
                for i in range(min(k, n - j - k)):
                    if (i + j) // (2 * p) == (i + j + k) // (2 * p):
                        pairs.append((i + j, i + j + k))
            k //= 2
        p *= 2
    return pairs


def _topk_rows(v):
    n = len(v)
    v = list(v)

    def cmpx(i, j):
        v[i], v[j] = jnp.maximum(v[i], v[j]), jnp.minimum(v[i], v[j])

    for i, j in _sort_pairs(n):
        cmpx(i, j)
    shift = SUBLANE // 2
    while shift >= 1:
        partner = [pltpu.roll(x, shift, 0) for x in v]
        v = [jnp.maximum(v[j], partner[n - 1 - j]) for j in range(n)]
        stride = n // 2
        while stride >= 1:
            for i in range(n):
                if not i & stride:
                    cmpx(i, i + stride)
            stride //= 2
        shift //= 2
    return v


def _route_kernel(ht_ref, wq_ref, keys_ref, r2_ref, e2_ref, n1_ref, e1_ref, qt_ref, top_ref, cand_ref):
    tm = ht_ref.shape[1]
    k = PEER_TOPK
    nk = N_KEYS
    neg = jnp.float32(-jnp.inf)
    qt_ref[...] = _dot(wq_ref[...], ht_ref[...]).astype(BF16)
    cand_ref[_N_CAND:_CAND_PAD, :] = jnp.full((_CAND_PAD - _N_CAND, tm), neg, F32)
    for h in range(PEER_HEADS):
        st = []
        for p in range(2):
            hp = 2 * h + p
            s = jnp.dot(keys_ref[hp], qt_ref[hp * nk:(hp + 1) * nk, :], preferred_element_type=F32)
            st.append(s)
            top = _topk_rows([s[g * SUBLANE:(g + 1) * SUBLANE, :] for g in range(nk // SUBLANE)])
            for r in range(k):
                top_ref[p * k + r:p * k + r + 1, :] = top[r][0:1, :]
        v1 = top_ref[0:k, :]
        v2 = top_ref[k:2 * k, :]
        rank2 = jnp.full((nk, tm), float(k), F32)
        for r in range(k):
            rank2 = jnp.where(st[1] == v2[r:r + 1, :], float(r), rank2)
        row = 0
        for a, nb in enumerate(_CAND_ROWS):
            cand_ref[row:row + nb, :] = v1[a:a + 1, :] + v2[0:nb, :]
            row += nb
        cur = cand_ref[...]
        cmax = thr = zsum = None
        for r in range(k):
            m = jnp.max(cur, axis=0, keepdims=True)
            if r == 0:
                cmax, zsum = m, jnp.ones_like(m)
            else:
                zsum = zsum + jnp.exp(m - cmax)
            thr = m
            cur = jnp.where(cur == m, neg, cur)
        n1 = jnp.zeros((nk, tm), F32)
        row = 0
        for a, nb in enumerate(_CAND_ROWS):
            sel = cand_ref[row:row + nb, :] >= thr
            n_a = jnp.sum(sel.astype(F32), axis=0, keepdims=True)
            n1 = jnp.where(st[0] == v1[a:a + 1, :], n_a, n1)
            row += nb
        r2_ref[h] = rank2.T
        e2_ref[h] = jnp.exp(st[1] - v2[0:1, :]).T
        n1_ref[h] = n1.T
        e1_ref[h] = (jnp.exp(st[0] - v1[0:1, :]) / zsum).T


def _peer_route(hxt, wq_t, keys):
    d, t = hxt.shape
    nq = wq_t.shape[0]
    tm = PEER_ROUTE_TILE
    rows = PEER_HEADS * N_KEYS
    return pl.pallas_call(
        _route_kernel, grid=(t // tm,),
        in_specs=[pl.BlockSpec((d, tm), lambda i: (0, i)),
                  pl.BlockSpec(wq_t.shape, lambda i: (0, 0)),
                  pl.BlockSpec(keys.shape, lambda i: (0, 0, 0))],
        out_specs=[pl.BlockSpec((PEER_HEADS, tm, N_KEYS), lambda i: (0, i, 0))] * 4,
        out_shape=[jax.ShapeDtypeStruct((PEER_HEADS, t, N_KEYS), F32)] * 4,
        scratch_shapes=[pltpu.VMEM((nq, tm), BF16), pltpu.VMEM((2 * PEER_TOPK, tm), F32),
                        pltpu.VMEM((_CAND_PAD, tm), F32)],
        compiler_params=_cparams("parallel"), name="peer_route",
    )(hxt, wq_t, keys)


def _gates_kernel(r2t_ref, e2t_ref, n1t_ref, e1t_ref, w_ref):
    tb = w_ref.shape[1]
    nk = N_KEYS
    k = PEER_TOPK
    rank = lax.broadcasted_iota(jnp.int32, (k, nk), 0).astype(F32)

    def tok(i, carry):
        for j in range(PEER_GATE_UNROLL):
            t = i * PEER_GATE_UNROLL + j
            a, b = [], []
            for h in range(PEER_HEADS):
                r2, e2, n1, e1 = (ref[h, pl.ds(t, k, stride=0), :]
                                  for ref in (r2t_ref, e2t_ref, n1t_ref, e1t_ref))
                a.append(jnp.where(n1 > rank, e1, 0.0))
                b.append(jnp.where(r2 == rank, e2, 0.0))
            w = _dot_tn(jnp.concatenate(a, axis=0), jnp.concatenate(b, axis=0))
            for g in range(nk // (2 * SUBLANE)):
                lo, hi = 2 * g * SUBLANE, (2 * g + 1) * SUBLANE
                w_ref[g, t] = pltpu.pack_elementwise([w[lo:lo + SUBLANE, :], w[hi:hi + SUBLANE, :]],
                                                     packed_dtype=jnp.bfloat16)
        return carry

    lax.fori_loop(0, tb // PEER_GATE_UNROLL, tok, 0)


def _peer_gates(route):
    t = route[0].shape[1]
    tb = PEER_GATE_TILE
    return pl.pallas_call(
        _gates_kernel, grid=(t // tb,),
        in_specs=[pl.BlockSpec((PEER_HEADS, tb, N_KEYS), lambda i: (0, i, 0))] * 4,
        out_specs=pl.BlockSpec((N_KEYS // (2 * SUBLANE), tb, SUBLANE, N_KEYS), lambda i: (0, i, 0, 0)),
        out_shape=jax.ShapeDtypeStruct((N_KEYS // (2 * SUBLANE), t, SUBLANE, N_KEYS), jnp.uint32),
        compiler_params=_cparams("parallel"), name="peer_gates",
    )(*route)


def _peer_kernel(h_ref, w_ref, ut_ref, v_ref, o_ref, g0_ref, g1_ref):
    e = pl.program_id(1)
    n_blk = pl.num_programs(1) - 1

    @pl.when(e == 0)
    def _():
        o_ref[...] = jnp.zeros(o_ref.shape, F32)
        g1_ref[...] = jnp.zeros(g1_ref.shape, BF16)

    for parity, g_w_ref, g_r_ref in ((0, g0_ref, g1_ref), (1, g1_ref, g0_ref)):
        for build in (True, False):
            @pl.when((e % 2 == parity) & ((e < n_blk) if build else (e == n_blk)))
            def _(g_w_ref=g_w_ref, g_r_ref=g_r_ref, build=build, parity=parity):
                _peer_step(h_ref, w_ref, ut_ref, v_ref, o_ref, g_w_ref, g_r_ref, build, parity)


def _peer_step(h_ref, w_ref, ut_ref, v_ref, acc_ref, g_w_ref, g_r_ref, build, half):
    sr = h_ref.shape[0] // PEER_SUB

    def sub(k, carry):
        rows = pl.ds(pl.multiple_of(k * sr, sr), sr)
        acc_ref[rows, :] += jnp.dot(g_r_ref[rows, :], v_ref[...].astype(BF16), preferred_element_type=F32)
        if not build:
            return carry
        act = jnp.dot(h_ref[rows, :], ut_ref[0], preferred_element_type=F32)
        act = 0.5 * act * (1.0 + lax.erf(act * 0.7071067811865476))
        for j in range(SUBLANE):
            ln = slice(j * N_KEYS, (j + 1) * N_KEYS)
            wj = pltpu.unpack_elementwise(w_ref[pl.ds(k * (sr * SUBLANE) + j, sr, stride=SUBLANE), :],
                                          index=half, packed_dtype=jnp.bfloat16, unpacked_dtype=F32)
            g_w_ref[rows, ln] = (wj * act[:, ln]).astype(BF16)
        return carry

    lax.fori_loop(0, PEER_SUB, sub, 0)


def _peer_dense(hx, w, ut, v_all, layer):
    t, d = hx.shape
    n_blk, _, eb = ut.shape
    tm = PEER_TOK_TILE
    last = n_blk - 1
    return pl.pallas_call(
        _peer_kernel, grid=(t // tm, n_blk + 1),
        in_specs=[pl.BlockSpec((tm, d), lambda i, e: (i, 0)),
                  pl.BlockSpec((tm * SUBLANE, N_KEYS),
                               lambda i, e: (jnp.minimum(e, last) // 2 * (t // tm) + i, 0)),
                  pl.BlockSpec((1, d, eb), lambda i, e: (jnp.minimum(e, last), 0, 0)),
                  pl.BlockSpec((None, eb, d), lambda i, e: (layer, jnp.maximum(e - 1, 0), 0))],
        out_specs=pl.BlockSpec((tm, d), lambda i, e: (i, 0)),
        out_shape=jax.ShapeDtypeStruct((t, d), F32),
        scratch_shapes=[pltpu.VMEM((tm, eb), BF16), pltpu.VMEM((tm, eb), BF16)],
        compiler_params=_cparams("parallel", "arbitrary"), name="peer_dense",
    )(hx, w, ut, v_all)


def _final_kernel(x_ref, f_ref, mod_ref, gain_ref, o_ref):
    x = x_ref[0] + mod_ref[0, 0, 5:6, :] * f_ref[0]
    o_ref[0] = x * lax.rsqrt(jnp.mean(x * x, axis=-1, keepdims=True) + EPS) * gain_ref[...]


def _final(x, ffn, mod, gain):
    b, s, d = x.shape
    tok = pl.BlockSpec((1, TOK_TILE, d), lambda i, t: (i, t, 0))
    return pl.pallas_call(
        _final_kernel, grid=(b, s // TOK_TILE),
        in_specs=[tok, tok, pl.BlockSpec((1, 1, 6, d), lambda i, t: (i, 1, 0, 0)),
                  pl.BlockSpec((1, d), lambda i, t: (0, 0))],
        out_specs=tok, out_shape=jax.ShapeDtypeStruct((b, s, d), F32),
        compiler_params=_cparams("parallel", "parallel"), name="final_norm",
    )(x, ffn, mod, gain)


def _rope_tables(seq):
    pos = jnp.arange(seq)
    rc = jnp.stack([pos // GRID_W, pos % GRID_W], axis=-1).astype(F32)
    inv = jnp.power(ROPE_BASE, -jnp.arange(ROPE_PAIRS, dtype=F32) / ROPE_PAIRS)
    ang = rc[:, :, None] * inv
    cos = jnp.concatenate([jnp.cos(ang)] * 2, axis=-1).reshape(seq, HEAD_DIM)
    sin = jnp.concatenate([-jnp.sin(ang), jnp.sin(ang)], axis=-1).reshape(seq, HEAD_DIM)
    return jnp.concatenate([cos] * 2, axis=1), jnp.concatenate([sin] * 2, axis=1)


def _pair_lanes(cols, n_heads):
    rows = cols.shape[0]
    c = cols.reshape(rows, 4, n_heads // 2, 2).transpose(0, 2, 1, 3).reshape(rows, n_heads // 2, 8)
    return jnp.pad(c, ((0, 0), (0, 0), (0, LANE - 8))).reshape(rows, n_heads // 2 * LANE)


def kernel(x, c, ctx, c_ctx, w_ada, b_ada, norm1, norm2, w_in, conv_w, a_log, dt_bias, gdn_norm, w_pool,
           pool_scale, sink, w_out, peer_wq, peer_keys, peer_u, peer_v, norm_f):
    b, seq, d = x.shape
    n_ctx = ctx.shape[1]
    depth = w_ada.shape[0]
    pool_w = d // 4
    gdn_w = 3 * d // 8
    gdn_heads = gdn_w // HEAD_DIM
    attn_w = d - pool_w - gdn_w
    kv_w = KV_HEADS * HEAD_DIM
    assert n_ctx % TOK_TILE == 0 and seq % TOK_TILE == 0 and gdn_heads % 2 == 0
    assert (n_ctx + seq) % PROJ_TILE == 0 and PEER_EXP_TILE == SUBLANE * N_KEYS
    assert (b * (n_ctx + seq)) % PEER_TOK_TILE == 0 and (b * seq) % PEER_TOK_TILE == 0

    rows = -(-(b + 1) // SUBLANE) * SUBLANE
    c_all = jnp.zeros((rows, d), F32).at[:b].set(c).at[b].set(c_ctx)
    mod = _ada_mod(c_all, w_ada, b_ada)
    mod_x = mod[:, :b].reshape(depth, b, 1, 6, d)
    mod_c = jnp.broadcast_to(mod[:, b].reshape(depth, 1, 1, 6, d), (depth, b, 1, 6, d))
    mod = jnp.concatenate([mod_c, mod_x], axis=2)

    cos_t, sin_t = _rope_tables(seq)
    splits = (pool_w, 3 * gdn_w, gdn_w, gdn_heads // 2 * LANE, attn_w, kv_w, kv_w)
    o_ba = pool_w + 4 * gdn_w
    n_ba = 4 * gdn_heads

    h = jnp.concatenate([ctx, x], axis=1)
    ffn = None
    for l in range(depth):
        last = l == depth - 1
        w_l = w_in[l]
        w_big = jnp.concatenate([w_l[:, :o_ba], _pair_lanes(w_l[:, o_ba:o_ba + n_ba], gdn_heads),
                                 w_l[:, o_ba + n_ba:]], axis=1).astype(BF16)
        pmod = mod[l - 1] if l > 0 else None
        outs = _in_proj(h, ffn, pmod, mod[l], norm1[l][None], w_big, splits, n_ctx)
        if l > 0:
            h, *outs = outs
        pa, qkv, z, ba, aq, ak, av = outs

        wbd = jax.scipy.linalg.block_diag(*[w_pool[l, g] for g in range(len(POOL_WINDOWS))])
        ya = _pool(pa, wbd, pool_scale[l][None], n_ctx)

        gparams = jnp.stack([_pair_lanes(jnp.concatenate([jnp.zeros_like(a_log[l]), a_log[l]]).reshape(1, -1),
                                         gdn_heads),
                             _pair_lanes(jnp.concatenate([jnp.zeros_like(dt_bias[l]), dt_bias[l]]).reshape(1, -1),
                                         gdn_heads)], axis=1)
        gparams = gparams.reshape(2, gdn_heads // 2, LANE).transpose(1, 0, 2)
        gparams = jnp.pad(gparams, ((0, 0), (0, SUBLANE - 2), (0, 0)))
        yb = _gdn(qkv, z, ba, conv_w[l], gparams, jnp.concatenate([gdn_norm[l]] * 2)[None], n_ctx)

        sink_row = jnp.pad(sink[l], (0, LANE - sink.shape[1]))[None]
        yc = _attn_x(aq, ak, av, cos_t, sin_t, sink_row, n_ctx)
        if not last:
            yc = jnp.concatenate([_attn_ctx(aq, ak, av, sink_row, n_ctx), yc], axis=1)

        x_new, hx = _out_proj(h, ya, yb, yc, w_out[l].astype(BF16), mod[l], norm2[l][None], n_ctx,
                              skip_ctx=last)
        t = hx.shape[0] * hx.shape[1]
        hx = hx.reshape(t, d)
        keys = peer_keys[l].reshape(2 * PEER_HEADS, N_KEYS, -1).astype(BF16)
        route = _peer_route(hx.T, peer_wq[l].T.astype(BF16), keys)
        gates = _peer_gates(route).reshape(-1, N_KEYS)
        ut = peer_u[l].reshape(-1, PEER_EXP_TILE, d).transpose(0, 2, 1).astype(BF16)
        ffn = _peer_dense(hx, gates, ut, peer_v, l)
        ffn = ffn.reshape(x_new.shape)
        h = x_new
    return _final(h, ffn, mod[depth - 1], norm_f[None])
```

```python
import functools

import jax
import jax.numpy as jnp
from jax import lax
from jax.experimental import pallas as pl
from jax.experimental.pallas import tpu as pltpu

F32, BF16 = jnp.float32, jnp.bfloat16
HIGHEST = lax.Precision.HIGHEST

EPS = 1e-6
NEG_INF = -1e30
HEAD_DIM = 64
GRID_W = 64
POOL_WINDOWS = (2, 4, 8, 16)
GDN_CHUNK = 64
CONV_K = 5
KV_HEADS = 2
WINDOW = 128
ATTN_BLOCK = 128
ROPE_BASE = 10000.0
ROPE_PAIRS = HEAD_DIM // 4
PEER_HEADS = 8
N_KEYS = 128
PEER_TOPK = 16

LANE = 128
SUBLANE = 8
TOK_TILE = 256
PROJ_TILE = 768
ROW_CHUNK = 256
ATTN_QBLOCKS = 2
GDN_GROUP = 4
PEER_ROUTE_TILE = 512
PEER_GATE_TILE = 256
PEER_GATE_UNROLL = 64
PEER_TOK_TILE = 1024
PEER_EXP_TILE = 1024
PEER_SUB = 1
VMEM_LIMIT = 48 * 1024 * 1024

_CAND_ROWS = tuple(PEER_TOPK // (a + 1) for a in range(PEER_TOPK))
_N_CAND = sum(_CAND_ROWS)
_CAND_PAD = -(-_N_CAND // SUBLANE) * SUBLANE


def _cparams(*sem):
    return pltpu.CompilerParams(dimension_semantics=sem, vmem_limit_bytes=VMEM_LIMIT)


def _dot(a, b):
    return jnp.dot(a.astype(BF16), b.astype(BF16), preferred_element_type=F32)


def _dot_nt(a, b):
    return lax.dot_general(a.astype(BF16), b.astype(BF16), (((1,), (1,)), ((), ())),
                           preferred_element_type=F32)


def _dot_tn(a, b):
    return lax.dot_general(a.astype(BF16), b.astype(BF16), (((0,), (0,)), ((), ())),
                           preferred_element_type=F32)


def _dot_hi(a, b):
    return jnp.dot(a, b, precision=HIGHEST, preferred_element_type=F32)


def _silu(x):
    return x * jax.nn.sigmoid(x)


def _softplus(x):
    return jnp.maximum(x, 0.0) + jnp.log1p(jnp.exp(-jnp.abs(x)))


def _modnorm(x, gain, scale, shift):
    y = x * lax.rsqrt(jnp.mean(x * x, axis=-1, keepdims=True) + EPS) * gain
    return y * (1.0 + scale) + shift


def _ada_kernel(c_ref, w_ref, b_ref, o_ref):
    o_ref[0] = _dot_hi(_silu(c_ref[...]), w_ref[0]) + b_ref[0]


def _ada_mod(c_all, w_ada, b_ada):
    depth, d, n = w_ada.shape
    rows = c_all.shape[0]
    tn = n // 4
    return pl.pallas_call(
        _ada_kernel,
        grid=(depth, n // tn),
        in_specs=[pl.BlockSpec((rows, d), lambda l, j: (0, 0)),
                  pl.BlockSpec((1, d, tn), lambda l, j: (l, 0, j)),
                  pl.BlockSpec((1, 1, tn), lambda l, j: (l, 0, j))],
        out_specs=pl.BlockSpec((1, rows, tn), lambda l, j: (l, 0, j)),
        out_shape=jax.ShapeDtypeStruct((depth, rows, n), F32),
        compiler_params=_cparams("parallel", "parallel"),
        name="ada_mod",
    )(c_all, w_ada, b_ada.reshape(depth, 1, n))


def _row_mod(mod_ref, first_row, n_rows, n_ctx):
    is_x = first_row + lax.broadcasted_iota(jnp.int32, (n_rows, 1), 0) >= n_ctx
    return lambda k: jnp.where(is_x, mod_ref[0, 1, k:k + 1, :], mod_ref[0, 0, k:k + 1, :])


def _inproj_kernel(*refs, has_ffn, col_splits, n_ctx):
    tile = refs[0].shape[1]
    first_row = pl.program_id(1) * tile
    if has_ffn:
        h_ref, f_ref, pmod_ref, mod_ref, gain_ref, w_ref, res_ref, *outs = refs
        x = h_ref[0] + _row_mod(pmod_ref, first_row, tile, n_ctx)(5) * f_ref[0]
        res_ref[0] = x
    else:
        h_ref, mod_ref, gain_ref, w_ref, *outs = refs
        x = h_ref[0]
    mod = _row_mod(mod_ref, first_row, tile, n_ctx)
    hb = _modnorm(x, gain_ref[...], mod(1), mod(0)).astype(BF16)
    off = 0
    for o_ref, width in zip(outs, col_splits):
        o_ref[0] = jnp.dot(hb, w_ref[:, off:off + width], preferred_element_type=F32)
        off += width


def _in_proj(h, ffn, pmod, mod, gain, w, col_splits, n_ctx):
    b, l, d = h.shape
    tile = PROJ_TILE
    tok = pl.BlockSpec((1, tile, d), lambda i, t: (i, t, 0))
    modspec = pl.BlockSpec((1, 2, 6, d), lambda i, t: (i, 0, 0, 0))
    has_ffn = ffn is not None
    in_specs = [tok] + ([tok, modspec] if has_ffn else []) + [
        modspec, pl.BlockSpec((1, d), lambda i, t: (0, 0)), pl.BlockSpec(w.shape, lambda i, t: (0, 0))]
    out_shape = [jax.ShapeDtypeStruct((b, l, wd), F32) for wd in col_splits]
    out_specs = [pl.BlockSpec((1, tile, wd), lambda i, t: (i, t, 0)) for wd in col_splits]
    if has_ffn:
        out_shape = [jax.ShapeDtypeStruct((b, l, d), F32)] + out_shape
        out_specs = [tok] + out_specs
    args = (h, ffn, pmod, mod, gain, w) if has_ffn else (h, mod, gain, w)
    return pl.pallas_call(
        functools.partial(_inproj_kernel, has_ffn=has_ffn, col_splits=col_splits, n_ctx=n_ctx),
        grid=(b, l // tile), in_specs=in_specs, out_specs=out_specs, out_shape=out_shape,
        compiler_params=_cparams("parallel", "parallel"), name="in_proj",
    )(*args)


def _pool_kernel(a_ref, wbd_ref, scale_ref, o_ref, pad_ref, *, segments):
    c = a_ref.shape[-1]
    grp = lax.broadcasted_iota(jnp.int32, (1, c), 1) // (c // len(POOL_WINDOWS))
    zeros = jnp.zeros((SUBLANE, c), F32)
    for off, n in segments:
        pad_ref[0:SUBLANE, :] = zeros
        pad_ref[SUBLANE:SUBLANE + n, :] = a_ref[0, off:off + n, :]
        pad_ref[SUBLANE + n:2 * SUBLANE + n, :] = zeros

        def chunk(ci, carry, off=off, n=n):
            r0 = pl.multiple_of(ci * ROW_CHUNK, ROW_CHUNK)
            win = pad_ref[pl.ds(r0, ROW_CHUNK + 2 * SUBLANE), :]
            sh = lambda s: win[SUBLANE + s:SUBLANE + s + ROW_CHUNK, :]
            a0 = sh(0)
            s2 = sh(-1) + a0
            s4 = s2 + sh(-2) + sh(1)
            s8 = s4 + sh(-4) + sh(-3) + sh(2) + sh(3)
            s16 = s8 + (sh(-8) + sh(-7) + sh(-6) + sh(-5)) + (sh(4) + sh(5) + sh(6) + sh(7))
            t = r0 + lax.broadcasted_iota(jnp.int32, (ROW_CHUNK, 1), 0)

            def mean(s, w):
                cnt = jnp.minimum(t + (w - w // 2), n) - jnp.maximum(t - w // 2, 0)
                return s / cnt.astype(F32)

            pooled = jnp.where(grp == 0, mean(s2, 2), jnp.where(grp == 1, mean(s4, 4),
                               jnp.where(grp == 2, mean(s8, 8), mean(s16, 16)))) - a0
            o_ref[0, pl.ds(off + r0, ROW_CHUNK), :] = _dot_hi(pooled, wbd_ref[...]) * scale_ref[...]
            return carry

        lax.fori_loop(0, n // ROW_CHUNK, chunk, 0, unroll=2)


def _pool(a, wbd, scale, n_ctx):
    b, l, c = a.shape
    segments = ((0, n_ctx), (n_ctx, l - n_ctx))
    return pl.pallas_call(
        functools.partial(_pool_kernel, segments=segments),
        grid=(b,),
        in_specs=[pl.BlockSpec((1, l, c), lambda i: (i, 0, 0)),
                  pl.BlockSpec((c, c), lambda i: (0, 0)),
                  pl.BlockSpec((1, c), lambda i: (0, 0))],
        out_specs=pl.BlockSpec((1, l, c), lambda i: (i, 0, 0)),
        out_shape=jax.ShapeDtypeStruct((b, l, c), F32),
        scratch_shapes=[pltpu.VMEM((l - n_ctx + 2 * SUBLANE, c), F32)],
        compiler_params=_cparams("parallel"), name="pool",
    )(a, wbd, scale)


def _gdn_kernel(q_ref, k_ref, v_ref, z_ref, ba_ref, cwq_ref, cwk_ref, cwv_ref, gp_ref, gain_ref, o_ref,
                pad_ref, yq_ref, yk_ref, yv_ref, bg_ref, of_ref, ob_ref, s_ref,
                qe_ref, mn_ref, oc_ref, nn_ref, egl_ref, *, n_ctx):
    l = q_ref.shape[1]
    n_x = l - n_ctx
    hd = HEAD_DIM
    lane = lax.broadcasted_iota(jnp.int32, (1, LANE), 1)
    lo = lane < hd
    zeros = jnp.zeros((SUBLANE, LANE), F32)
    x_off = 2 * SUBLANE + n_ctx

    def head_scale(ss):
        s_lo = jnp.sum(jnp.where(lo, ss, 0.0), axis=-1, keepdims=True)
        s_hi = jnp.sum(jnp.where(lo, 0.0, ss), axis=-1, keepdims=True)
        return jnp.where(lo, s_lo, s_hi)

    def conv(u_ref, cw_ref, y_ref, post):
        pad_ref[0:SUBLANE, :] = zeros
        pad_ref[SUBLANE:SUBLANE + n_ctx, :] = u_ref[0, 0:n_ctx, :]
        pad_ref[SUBLANE + n_ctx:x_off, :] = zeros
        pad_ref[x_off:x_off + n_x, :] = u_ref[0, n_ctx:l, :]
        pad_ref[x_off + n_x:x_off + n_x + SUBLANE, :] = zeros
        cw = cw_ref[...]
        for poff, yoff, n in ((SUBLANE, 0, n_ctx), (x_off, n_ctx, n_x)):
            def chunk(ci, carry, poff=poff, yoff=yoff):
                r0 = pl.multiple_of(ci * ROW_CHUNK, ROW_CHUNK)
                win = pad_ref[pl.ds(poff - SUBLANE + r0, ROW_CHUNK + 2 * SUBLANE), :]
                acc = win[SUBLANE - 2:SUBLANE - 2 + ROW_CHUNK, :] * cw[0:1, :]
                for j in range(1, CONV_K):
                    acc = acc + win[SUBLANE - 2 + j:SUBLANE - 2 + j + ROW_CHUNK, :] * cw[j:j + 1, :]
                y_ref[pl.ds(yoff + r0, ROW_CHUNK), :] = post(_silu(acc))
                return carry
            lax.fori_loop(0, n // ROW_CHUNK, chunk, 0, unroll=4)

    l2 = lambda y: y * lax.rsqrt(head_scale(y * y) + EPS)
    conv(q_ref, cwq_ref, yq_ref, lambda y: l2(y) * (hd ** -0.5))
    conv(k_ref, cwk_ref, yk_ref, l2)
    conv(v_ref, cwv_ref, yv_ref, lambda y: y)

    ba = ba_ref[0]
    g = -jnp.exp(gp_ref[0, 0:1, :]) * _softplus(ba + gp_ref[0, 1:2, :])
    bg_ref[...] = jnp.where(lane < 4, jax.nn.sigmoid(ba), g)

    ck = GDN_CHUNK
    rr = lax.broadcasted_iota(jnp.int32, (ck, ck), 0)
    cc = lax.broadcasted_iota(jnp.int32, (ck, ck), 1)
    eye = (rr == cc).astype(F32)
    tril = (rr >= cc).astype(F32)
    n_chunks = l // ck
    nc_ctx = n_ctx // ck

    def phase1(ci, carry):
        probs = []
        for gi in range(GDN_GROUP):
            c = ci * GDN_GROUP + gi
            r0 = pl.multiple_of(c * ck, ck)
            qc = yq_ref[pl.ds(r0, ck), :]
            kc = yk_ref[pl.ds(r0, ck), :]
            vc = yv_ref[pl.ds(r0, ck), :]
            bgc = bg_ref[pl.ds(r0, ck), :]
            gcf = _dot_hi(tril, bgc)
            gcb = gcf[ck - 1:ck, :] - gcf + bgc
            gct = (gcf.T, gcb.T)
            for j in range(2):
                qh = qc[:, j * hd:(j + 1) * hd]
                kh = kc[:, j * hd:(j + 1) * hd]
                vh = vc[:, j * hd:(j + 1) * hd]
                kk = _dot_nt(kh, kh)
                qk = _dot_nt(qh, kh)
                for d in range(2):
                    mask = (rr >= cc) if d == 0 else (rr <= cc)
                    smask = (rr > cc) if d == 0 else (rr < cc)
                    lg, lb = 4 + 2 * d + j, 2 * d + j
                    gcol = jnp.broadcast_to((gcf, gcb)[d][:, lg:lg + 1], (ck, hd))
                    grow = gct[d][lg:lg + 1, :]
                    bcol = jnp.broadcast_to(bgc[:, lb:lb + 1], (ck, hd))
                    decay = jnp.where(mask, jnp.exp(jnp.where(mask, gcol - grow, 0.0)), 0.0)
                    xp = jnp.where(smask, -(bcol * kk * decay), 0.0)
                    eg = jnp.exp(gcol)
                    last = ck - 1 if d == 0 else 0
                    glast = gcol[last:last + 1, :]
                    probs.append(dict(
                        idx=c * 4 + 2 * d + j, xp=xp, inv=eye + xp, attn=qk * decay, qd=qh * eg,
                        rhs=jnp.concatenate([vh * bcol, kh * bcol * eg], axis=1),
                        kd=kh * jnp.exp(glast - gcol), egl=jnp.exp(glast)))
        for _ in range(5):
            for p in probs:
                p["xp"] = _dot(p["xp"], p["xp"])
            for p in probs:
                p["inv"] = p["inv"] + _dot(p["inv"], p["xp"])
        for p in probs:
            p["sol"] = _dot(p["inv"], p["rhs"])
        for p in probs:
            p["as"] = _dot(p["attn"], p["sol"])
        for p in probs:
            p["ks"] = _dot_tn(p["kd"], p["sol"])
        for p in probs:
            i = p["idx"]
            qe_ref[i] = (p["qd"] - p["as"][:, hd:]).astype(BF16)
            oc_ref[i] = p["as"][:, :hd]
            mn_ref[i] = p["ks"][:, hd:].astype(BF16)
            nn_ref[i] = p["ks"][:, :hd]
            egl_ref[i] = jnp.broadcast_to(p["egl"], (ck, hd))
        return carry

    lax.fori_loop(0, n_chunks // GDN_GROUP, phase1, 0)

    s_ref[...] = jnp.zeros(s_ref.shape, F32)

    def phase2(i, carry):
        cb = jnp.where(i < nc_ctx, nc_ctx - 1 - i, n_chunks - 1 - (i - nc_ctx))
        work = [(d, j, cidx * 4 + 2 * d + j) for d, cidx in ((0, i), (1, cb)) for j in range(2)]
        states = [s_ref[2 * d + j] for d, j, _ in work]
        outs = [_dot(qe_ref[idx], s) + oc_ref[idx] for (_, _, idx), s in zip(work, states)]
        upd = [_dot(mn_ref[idx], s) for (_, _, idx), s in zip(work, states)]
        for (d, j, idx), s, m in zip(work, states, upd):
            s_ref[2 * d + j] = egl_ref[idx] * s - m + nn_ref[idx]
        of_ref[pl.ds(pl.multiple_of(i * ck, ck), ck), :] = jnp.concatenate(outs[0:2], axis=1)
        ob_ref[pl.ds(pl.multiple_of(cb * ck, ck), ck), :] = jnp.concatenate(outs[2:4], axis=1)
        return carry

    lax.fori_loop(0, n_chunks, phase2, 0, unroll=12)

    def finish(ci, carry):
        r0 = pl.multiple_of(ci * ROW_CHUNK, ROW_CHUNK)
        o = of_ref[pl.ds(r0, ROW_CHUNK), :] + ob_ref[pl.ds(r0, ROW_CHUNK), :]
        o = o * lax.rsqrt(head_scale(o * o) * (1.0 / hd) + EPS) * gain_ref[...]
        o_ref[0, pl.ds(r0, ROW_CHUNK), :] = o * _silu(z_ref[0, pl.ds(r0, ROW_CHUNK), :])
        return carry

    lax.fori_loop(0, l // ROW_CHUNK, finish, 0, unroll=3)


def _gdn(qkv, z, ba, conv_w, gparams, gain2, n_ctx):
    b, l, w3 = qkv.shape
    npair = w3 // (3 * LANE)
    seq = lambda off: pl.BlockSpec((1, l, LANE), lambda i, p: (i, 0, p + off))
    cw = lambda off: pl.BlockSpec((CONV_K, LANE), lambda i, p: (0, p + off))
    return pl.pallas_call(
        functools.partial(_gdn_kernel, n_ctx=n_ctx),
        grid=(b, npair),
        in_specs=[seq(0), seq(npair), seq(2 * npair), seq(0), seq(0),
                  cw(0), cw(npair), cw(2 * npair),
                  pl.BlockSpec((1, SUBLANE, LANE), lambda i, p: (p, 0, 0)),
                  pl.BlockSpec((1, LANE), lambda i, p: (0, 0))],
        out_specs=seq(0),
        out_shape=jax.ShapeDtypeStruct((b, l, npair * LANE), F32),
        scratch_shapes=[pltpu.VMEM((l + 3 * SUBLANE, LANE), F32)] + [pltpu.VMEM((l, LANE), F32)] * 6
        + [pltpu.VMEM((4, HEAD_DIM, HEAD_DIM), F32)]
        + [pltpu.VMEM((4 * l // GDN_CHUNK, GDN_CHUNK, HEAD_DIM), dt) for dt in (BF16, BF16, F32, F32, F32)],
        compiler_params=_cparams("parallel", "parallel"), name="gdn",
    )(qkv, qkv, qkv, z, ba, conv_w, conv_w, conv_w, gparams, gain2)


def _rope(x, cos, sin):
    w = x.shape[-1]
    reps = w // cos.shape[-1]
    if reps > 1:
        cos = jnp.concatenate([cos] * reps, axis=1)
        sin = jnp.concatenate([sin] * reps, axis=1)
    first = (lax.broadcasted_iota(jnp.int32, (1, w), 1) % (2 * ROPE_PAIRS)) < ROPE_PAIRS
    partner = jnp.where(first, pltpu.roll(x, w - ROPE_PAIRS, 1), pltpu.roll(x, ROPE_PAIRS, 1))
    return x * cos + partner * sin


def _attend(q, k, v, valid, sink_col):
    s = _dot_nt(q, k)
    if valid is not None:
        s = jnp.where(valid, s, NEG_INF)
    m = jnp.maximum(jnp.max(s, axis=-1, keepdims=True), sink_col)
    p = jnp.exp(s - m)
    den = jnp.sum(p, axis=-1, keepdims=True) + jnp.exp(sink_col - m)
    return _dot(p, v) / den


def _gqa(q, k_all, v_all, valid, sink_ref):
    nq = q.shape[0]
    hd = HEAD_DIM
    group = q.shape[1] // hd // KV_HEADS
    outs = []
    for kvh in range(KV_HEADS):
        heads = [kvh * group + g for g in range(group)]
        qg = jnp.concatenate([q[:, h * hd:(h + 1) * hd] for h in heads], axis=0)
        sink_col = jnp.concatenate(
            [jnp.broadcast_to(sink_ref[0:1, h:h + 1], (nq, 1)) for h in heads], axis=0)
        o = _attend(qg, k_all[:, kvh * hd:(kvh + 1) * hd], v_all[:, kvh * hd:(kvh + 1) * hd],
                    valid, sink_col)
        outs += [o[g * nq:(g + 1) * nq, :] for g in range(group)]
    return jnp.concatenate(outs, axis=1)


def _attn_kernel(q_ref, *refs, seq):
    nkb = ATTN_QBLOCKS + 2
    k_refs, v_refs = refs[0:nkb], refs[nkb:2 * nkb]
    kc_ref, vc_ref = refs[2 * nkb:2 * nkb + 2]
    cos_refs = refs[2 * nkb + 2:3 * nkb + 2]
    sin_refs = refs[3 * nkb + 2:4 * nkb + 2]
    sink_ref, o_ref = refs[4 * nkb + 2:]
    blk = ATTN_BLOCK
    n_ctx = kc_ref.shape[1]
    group = q_ref.shape[2] // HEAD_DIM // KV_HEADS
    k_rot = [_rope(k_refs[j][0], cos_refs[j][...], sin_refs[j][...]) for j in range(nkb)]
    nk = 3 * blk + n_ctx
    qo = lax.broadcasted_iota(jnp.int32, (group * blk, 1), 0) % blk
    ko = lax.broadcasted_iota(jnp.int32, (1, nk), 1)
    in_window = jnp.abs(ko - blk - qo) <= WINDOW
    for s in range(ATTN_QBLOCKS):
        i = pl.program_id(1) * ATTN_QBLOCKS + s
        rows = slice(s * blk, (s + 1) * blk)
        q = _rope(q_ref[0, rows, :], cos_refs[s + 1][...], sin_refs[s + 1][...]) * (HEAD_DIM ** -0.5)
        k_all = jnp.concatenate(k_rot[s:s + 3] + [kc_ref[0]], axis=0)
        v_all = jnp.concatenate([v_refs[j][0] for j in range(s, s + 3)] + [vc_ref[0]], axis=0)
        kpos = (i - 1) * blk + ko
        valid = (in_window & (kpos >= 0) & (kpos < seq)) | (ko >= 3 * blk)
        o_ref[0, rows, :] = _gqa(q, k_all, v_all, valid, sink_ref)


def _attn_x(aq, ak, av, cos_t, sin_t, sink_row, n_ctx):
    b, l, wq = aq.shape
    wk = ak.shape[-1]
    blk = ATTN_BLOCK
    seq = l - n_ctx
    nb = seq // blk
    cb = n_ctx // blk
    nq = ATTN_QBLOCKS
    assert nb % nq == 0 and cb % nq == 0
    blocks = [lambda t, j=j: jnp.clip(t * nq + j - 1, 0, nb - 1) for j in range(nq + 2)]
    kv = [pl.BlockSpec((1, blk, wk), lambda i, t, f=f: (i, f(t) + cb, 0)) for f in blocks]
    tab = [pl.BlockSpec((blk, LANE), lambda i, t, f=f: (f(t), 0)) for f in blocks]
    ctx = pl.BlockSpec((1, n_ctx, wk), lambda i, t: (i, 0, 0))
    qspec = pl.BlockSpec((1, nq * blk, wq), lambda i, t: (i, t + cb // nq, 0))
    return pl.pallas_call(
        functools.partial(_attn_kernel, seq=seq),
        grid=(b, nb // nq),
        in_specs=[qspec] + kv + kv + [ctx, ctx] + tab + tab + [pl.BlockSpec((1, LANE), lambda i, t: (0, 0))],
        out_specs=pl.BlockSpec((1, nq * blk, wq), lambda i, t: (i, t, 0)),
        out_shape=jax.ShapeDtypeStruct((b, seq, wq), F32),
        compiler_params=_cparams("parallel", "parallel"), name="attn_x",
    )(aq, *([ak] * (nq + 2)), *([av] * (nq + 2)), ak, av, *([cos_t] * (nq + 2)), *([sin_t] * (nq + 2)),
      sink_row)


def _attn_ctx_kernel(q_ref, k_ref, v_ref, sink_ref, o_ref):
    o_ref[0] = _gqa(q_ref[0] * (HEAD_DIM ** -0.5), k_ref[0], v_ref[0], None, sink_ref)


def _attn_ctx(aq, ak, av, sink_row, n_ctx):
    b, _, wq = aq.shape
    wk = ak.shape[-1]
    spec = lambda w: pl.BlockSpec((1, n_ctx, w), lambda i: (i, 0, 0))
    return pl.pallas_call(
        _attn_ctx_kernel, grid=(b,),
        in_specs=[spec(wq), spec(wk), spec(wk), pl.BlockSpec((1, LANE), lambda i: (0, 0))],
        out_specs=spec(wq), out_shape=jax.ShapeDtypeStruct((b, n_ctx, wq), F32),
        compiler_params=_cparams("parallel"), name="attn_ctx",
    )(aq, ak, av, sink_row)


def _outproj_kernel(x_ref, ya_ref, yb_ref, yc_ref, w_ref, mod_ref, gain_ref, xo_ref, hx_ref, *, row_off, n_ctx):
    wa, wb = ya_ref.shape[-1], yb_ref.shape[-1]
    tile = x_ref.shape[1]
    acc = (jnp.dot(ya_ref[0].astype(BF16), w_ref[0:wa, :], preferred_element_type=F32)
           + jnp.dot(yb_ref[0].astype(BF16), w_ref[wa:wa + wb, :], preferred_element_type=F32)
           + jnp.dot(yc_ref[0].astype(BF16), w_ref[wa + wb:, :], preferred_element_type=F32))
    mod = _row_mod(mod_ref, row_off + pl.program_id(1) * tile, tile, n_ctx)
    x = x_ref[0] + mod(2) * acc
    xo_ref[0] = x
    hx_ref[0] = _modnorm(x, gain_ref[...], mod(4), mod(3)).astype(BF16)


def _out_proj(h, ya, yb, yc, w, mod, gain, n_ctx, skip_ctx):
    b, l, d = h.shape
    tile = TOK_TILE if skip_ctx else PROJ_TILE
    off = n_ctx // tile if skip_ctx else 0
    nt = l // tile - off
    full = lambda wd: pl.BlockSpec((1, tile, wd), lambda i, t: (i, t + off, 0))
    own = lambda wd: pl.BlockSpec((1, tile, wd), lambda i, t: (i, t, 0))
    return pl.pallas_call(
        functools.partial(_outproj_kernel, row_off=off * tile, n_ctx=n_ctx), grid=(b, nt),
        in_specs=[full(d), full(ya.shape[-1]), full(yb.shape[-1]), own(yc.shape[-1]),
                  pl.BlockSpec(w.shape, lambda i, t: (0, 0)),
                  pl.BlockSpec((1, 2, 6, d), lambda i, t: (i, 0, 0, 0)),
                  pl.BlockSpec((1, d), lambda i, t: (0, 0))],
        out_specs=[own(d), own(d)],
        out_shape=[jax.ShapeDtypeStruct((b, nt * tile, d), F32),
                   jax.ShapeDtypeStruct((b, nt * tile, d), BF16)],
        compiler_params=_cparams("parallel", "parallel"), name="out_proj",
    )(h, ya, yb, yc, w, mod, gain)


def _sort_pairs(n):
    pairs, p = [], 1
    while p < n:
        k = p
        while k >= 1:
            for j in range(k % p, n - k, 2 * k):
                for i in range(min(k, n - j - k)):
                    if (i + j) // (2 * p) == (i + j + k) // (2 * p):
                        pairs.append((i + j, i + j + k))
            k //= 2
        p *= 2
    return pairs


def _topk_rows(v):
    n = len(v)
    v = list(v)

    def cmpx(i, j):
        v[i], v[j] = jnp.maximum(v[i], v[j]), jnp.minimum(v[i], v[j])

    for i, j in _sort_pairs(n):
        cmpx(i, j)
    shift = SUBLANE // 2
    while shift >= 1:
        partner = [pltpu.roll(x, shift, 0) for x in v]
        v = [jnp.maximum(v[j], partner[n - 1 - j]) for j in range(n)]
        stride = n // 2
        while stride >= 1:
            for i in range(n):
                if not i & stride:
                    cmpx(i, i + stride)
            stride //= 2
        shift //= 2
    return v


def _route_kernel(ht_ref, wq_ref, keys_ref, r2_ref, e2_ref, n1_ref, e1_ref, qt_ref, top_ref, cand_ref):
    tm = ht_ref.shape[1]
    k = PEER_TOPK
    nk = N_KEYS
    neg = jnp.float32(-jnp.inf)
    qt_ref[...] = _dot(wq_ref[...], ht_ref[...]).astype(BF16)
    cand_ref[_N_CAND:_CAND_PAD, :] = jnp.full((_CAND_PAD - _N_CAND, tm), neg, F32)
    for h in range(PEER_HEADS):
        st = []
        for p in range(2):
            hp = 2 * h + p
            s = jnp.dot(keys_ref[hp], qt_ref[hp * nk:(hp + 1) * nk, :], preferred_element_type=F32)
            st.append(s)
            top = _topk_rows([s[g * SUBLANE:(g + 1) * SUBLANE, :] for g in range(nk // SUBLANE)])
            for r in range(k):
                top_ref[p * k + r:p * k + r + 1, :] = top[r][0:1, :]
        v1 = top_ref[0:k, :]
        v2 = top_ref[k:2 * k, :]
        rank2 = jnp.full((nk, tm), float(k), F32)
        for r in range(k):
            rank2 = jnp.where(st[1] == v2[r:r + 1, :], float(r), rank2)
        row = 0
        for a, nb in enumerate(_CAND_ROWS):
            cand_ref[row:row + nb, :] = v1[a:a + 1, :] + v2[0:nb, :]
            row += nb
        cur = cand_ref[...]
        cmax = thr = zsum = None
        for r in range(k):
            m = jnp.max(cur, axis=0, keepdims=True)
            if r == 0:
                cmax, zsum = m, jnp.ones_like(m)
            else:
                zsum = zsum + jnp.exp(m - cmax)
            thr = m
            cur = jnp.where(cur == m, neg, cur)
        n1 = jnp.zeros((nk, tm), F32)
        row = 0
        for a, nb in enumerate(_CAND_ROWS):
            sel = cand_ref[row:row + nb, :] >= thr
            n_a = jnp.sum(sel.astype(F32), axis=0, keepdims=True)
            n1 = jnp.where(st[0] == v1[a:a + 1, :], n_a, n1)
            row += nb
        r2_ref[h] = rank2.T
        e2_ref[h] = jnp.exp(st[1] - v2[0:1, :]).T
        n1_ref[h] = n1.T
        e1_ref[h] = (jnp.exp(st[0] - v1[0:1, :]) / zsum).T


def _peer_route(hxt, wq_t, keys):
    d, t = hxt.shape
    nq = wq_t.shape[0]
    tm = PEER_ROUTE_TILE
    rows = PEER_HEADS * N_KEYS
    return pl.pallas_call(
        _route_kernel, grid=(t // tm,),
        in_specs=[pl.BlockSpec((d, tm), lambda i: (0, i)),
                  pl.BlockSpec(wq_t.shape, lambda i: (0, 0)),
                  pl.BlockSpec(keys.shape, lambda i: (0, 0, 0))],
        out_specs=[pl.BlockSpec((PEER_HEADS, tm, N_KEYS), lambda i: (0, i, 0))] * 4,
        out_shape=[jax.ShapeDtypeStruct((PEER_HEADS, t, N_KEYS), F32)] * 4,
        scratch_shapes=[pltpu.VMEM((nq, tm), BF16), pltpu.VMEM((2 * PEER_TOPK, tm), F32),
                        pltpu.VMEM((_CAND_PAD, tm), F32)],
        compiler_params=pltpu.CompilerParams(dimension_semantics=("parallel",), vmem_limit_bytes=VMEM_LIMIT,
                                             allow_input_fusion=[True, True, True]),
        name="peer_route",
    )(hxt, wq_t, keys)


def _gates_kernel(r2t_ref, e2t_ref, n1t_ref, e1t_ref, w_ref):
    tb = w_ref.shape[1]
    nk = N_KEYS
    k = PEER_TOPK
    rank = lax.broadcasted_iota(jnp.int32, (k, nk), 0).astype(F32)

    def tok(i, carry):
        for j in range(PEER_GATE_UNROLL):
            t = i * PEER_GATE_UNROLL + j
            a, b = [], []
            for h in range(PEER_HEADS):
                r2, e2, n1, e1 = (ref[h, pl.ds(t, k, stride=0), :]
                                  for ref in (r2t_ref, e2t_ref, n1t_ref, e1t_ref))
                a.append(jnp.where(n1 > rank, e1, 0.0))
                b.append(jnp.where(r2 == rank, e2, 0.0))
            w = _dot_tn(jnp.concatenate(a, axis=0), jnp.concatenate(b, axis=0))
            for g in range(nk // (2 * SUBLANE)):
                lo, hi = 2 * g * SUBLANE, (2 * g + 1) * SUBLANE
                w_ref[g, t] = pltpu.pack_elementwise([w[lo:lo + SUBLANE, :], w[hi:hi + SUBLANE, :]],
                                                     packed_dtype=jnp.bfloat16)
        return carry

    lax.fori_loop(0, tb // PEER_GATE_UNROLL, tok, 0)


def _peer_gates(route):
    t = route[0].shape[1]
    tb = PEER_GATE_TILE
    return pl.pallas_call(
        _gates_kernel, grid=(t // tb,),
        in_specs=[pl.BlockSpec((PEER_HEADS, tb, N_KEYS), lambda i: (0, i, 0))] * 4,
        out_specs=pl.BlockSpec((N_KEYS // (2 * SUBLANE), tb, SUBLANE, N_KEYS), lambda i: (0, i, 0, 0)),
        out_shape=jax.ShapeDtypeStruct((N_KEYS // (2 * SUBLANE), t, SUBLANE, N_KEYS), jnp.uint32),
        compiler_params=_cparams("parallel"), name="peer_gates",
    )(*route)


def _peer_kernel(h_ref, w_ref, ut_ref, v_ref, o_ref, g0_ref, g1_ref):
    e = pl.program_id(1)
    n_blk = pl.num_programs(1) - 1

    @pl.when(e == 0)
    def _():
        o_ref[...] = jnp.zeros(o_ref.shape, F32)
        g1_ref[...] = jnp.zeros(g1_ref.shape, BF16)

    for parity, g_w_ref, g_r_ref in ((0, g0_ref, g1_ref), (1, g1_ref, g0_ref)):
        for build in (True, False):
            @pl.when((e % 2 == parity) & ((e < n_blk) if build else (e == n_blk)))
            def _(g_w_ref=g_w_ref, g_r_ref=g_r_ref, build=build, parity=parity):
                _peer_step(h_ref, w_ref, ut_ref, v_ref, o_ref, g_w_ref, g_r_ref, build, parity)


def _peer_step(h_ref, w_ref, ut_ref, v_ref, acc_ref, g_w_ref, g_r_ref, build, half):
    sr = h_ref.shape[0] // PEER_SUB

    def sub(k, carry):
        rows = pl.ds(pl.multiple_of(k * sr, sr), sr)
        acc_ref[rows, :] += jnp.dot(g_r_ref[rows, :], v_ref[...].astype(BF16), preferred_element_type=F32)
        if not build:
            return carry
        act = jnp.dot(h_ref[rows, :], ut_ref[0], preferred_element_type=F32)
        act = 0.5 * act * (1.0 + lax.erf(act * 0.7071067811865476))
        for j in range(SUBLANE):
            ln = slice(j * N_KEYS, (j + 1) * N_KEYS)
            wj = pltpu.unpack_elementwise(w_ref[pl.ds(k * (sr * SUBLANE) + j, sr, stride=SUBLANE), :],
                                          index=half, packed_dtype=jnp.bfloat16, unpacked_dtype=F32)
            g_w_ref[rows, ln] = (wj * act[:, ln]).astype(BF16)
        return carry

    lax.fori_loop(0, PEER_SUB, sub, 0)


def _peer_dense(hx, w, ut, v_all, layer):
    t, d = hx.shape
    n_blk, _, eb = ut.shape
    tm = PEER_TOK_TILE
    last = n_blk - 1
    return pl.pallas_call(
        _peer_kernel, grid=(t // tm, n_blk + 1),
        in_specs=[pl.BlockSpec((tm, d), lambda i, e: (i, 0)),
                  pl.BlockSpec((tm * SUBLANE, N_KEYS),
                               lambda i, e: (jnp.minimum(e, last) // 2 * (t // tm) + i, 0)),
                  pl.BlockSpec((1, d, eb), lambda i, e: (jnp.minimum(e, last), 0, 0)),
                  pl.BlockSpec((None, eb, d), lambda i, e: (layer, jnp.maximum(e - 1, 0), 0))],
        out_specs=pl.BlockSpec((tm, d), lambda i, e: (i, 0)),
        out_shape=jax.ShapeDtypeStruct((t, d), F32),
        scratch_shapes=[pltpu.VMEM((tm, eb), BF16), pltpu.VMEM((tm, eb), BF16)],
        compiler_params=pltpu.CompilerParams(dimension_semantics=("parallel", "arbitrary"),
                                             vmem_limit_bytes=VMEM_LIMIT,
                                             allow_input_fusion=[False, False, True, False]),
        name="peer_dense",
    )(hx, w, ut, v_all)


def _final_kernel(x_ref, f_ref, mod_ref, gain_ref, o_ref):
    x = x_ref[0] + mod_ref[0, 0, 5:6, :] * f_ref[0]
    o_ref[0] = x * lax.rsqrt(jnp.mean(x * x, axis=-1, keepdims=True) + EPS) * gain_ref[...]


def _final(x, ffn, mod, gain):
    b, s, d = x.shape
    tok = pl.BlockSpec((1, TOK_TILE, d), lambda i, t: (i, t, 0))
    return pl.pallas_call(
        _final_kernel, grid=(b, s // TOK_TILE),
        in_specs=[tok, tok, pl.BlockSpec((1, 1, 6, d), lambda i, t: (i, 1, 0, 0)),
                  pl.BlockSpec((1, d), lambda i, t: (0, 0))],
        out_specs=tok, out_shape=jax.ShapeDtypeStruct((b, s, d), F32),
        compiler_params=_cparams("parallel", "parallel"), name="final_norm",
    )(x, ffn, mod, gain)


def _rope_tables(seq):
    pos = jnp.arange(seq)
    rc = jnp.stack([pos // GRID_W, pos % GRID_W], axis=-1).astype(F32)
    inv = jnp.power(ROPE_BASE, -jnp.arange(ROPE_PAIRS, dtype=F32) / ROPE_PAIRS)
    ang = rc[:, :, None] * inv
    cos = jnp.concatenate([jnp.cos(ang)] * 2, axis=-1).reshape(seq, HEAD_DIM)
    sin = jnp.concatenate([-jnp.sin(ang), jnp.sin(ang)], axis=-1).reshape(seq, HEAD_DIM)
    return jnp.concatenate([cos] * 2, axis=1), jnp.concatenate([sin] * 2, axis=1)


def _pair_lanes(cols, n_heads):
    rows = cols.shape[0]
    c = cols.reshape(rows, 4, n_heads // 2, 2).transpose(0, 2, 1, 3).reshape(rows, n_heads // 2, 8)
    return jnp.pad(c, ((0, 0), (0, 0), (0, LANE - 8))).reshape(rows, n_heads // 2 * LANE)


def kernel(x, c, ctx, c_ctx, w_ada, b_ada, norm1, norm2, w_in, conv_w, a_log, dt_bias, gdn_norm, w_pool,
           pool_scale, sink, w_out, peer_wq, peer_keys, peer_u, peer_v, norm_f):
    b, seq, d = x.shape
    n_ctx = ctx.shape[1]
    depth = w_ada.shape[0]
    pool_w = d // 4
    gdn_w = 3 * d // 8
    gdn_heads = gdn_w // HEAD_DIM
    attn_w = d - pool_w - gdn_w
    kv_w = KV_HEADS * HEAD_DIM
    assert n_ctx % TOK_TILE == 0 and seq % TOK_TILE == 0 and gdn_heads % 2 == 0
    assert (n_ctx + seq) % PROJ_TILE == 0 and PEER_EXP_TILE == SUBLANE * N_KEYS
    assert (b * (n_ctx + seq)) % PEER_TOK_TILE == 0 and (b * seq) % PEER_TOK_TILE == 0

    rows = -(-(b + 1) // SUBLANE) * SUBLANE
    c_all = jnp.zeros((rows, d), F32).at[:b].set(c).at[b].set(c_ctx)
    mod = _ada_mod(c_all, w_ada, b_ada)
    mod_x = mod[:, :b].reshape(depth, b, 1, 6, d)
    mod_c = jnp.broadcast_to(mod[:, b].reshape(depth, 1, 1, 6, d), (depth, b, 1, 6, d))
    mod = jnp.concatenate([mod_c, mod_x], axis=2)

    cos_t, sin_t = _rope_tables(seq)
    splits = (pool_w, 3 * gdn_w, gdn_w, gdn_heads // 2 * LANE, attn_w, kv_w, kv_w)
    o_ba = pool_w + 4 * gdn_w
    n_ba = 4 * gdn_heads

    h = jnp.concatenate([ctx, x], axis=1)
    ffn = None
    for l in range(depth):
        last = l == depth - 1
        w_l = w_in[l]
        w_big = jnp.concatenate([w_l[:, :o_ba], _pair_lanes(w_l[:, o_ba:o_ba + n_ba], gdn_heads),
                                 w_l[:, o_ba + n_ba:]], axis=1).astype(BF16)
        pmod = mod[l - 1] if l > 0 else None
        outs = _in_proj(h, ffn, pmod, mod[l], norm1[l][None], w_big, splits, n_ctx)
        if l > 0:
            h, *outs = outs
        pa, qkv, z, ba, aq, ak, av = outs

        wbd = jax.scipy.linalg.block_diag(*[w_pool[l, g] for g in range(len(POOL_WINDOWS))])
        ya = _pool(pa, wbd, pool_scale[l][None], n_ctx)

        gparams = jnp.stack([_pair_lanes(jnp.concatenate([jnp.zeros_like(a_log[l]), a_log[l]]).reshape(1, -1),
                                         gdn_heads),
                             _pair_lanes(jnp.concatenate([jnp.zeros_like(dt_bias[l]), dt_bias[l]]).reshape(1, -1),
                                         gdn_heads)], axis=1)
        gparams = gparams.reshape(2, gdn_heads // 2, LANE).transpose(1, 0, 2)
        gparams = jnp.pad(gparams, ((0, 0), (0, SUBLANE - 2), (0, 0)))
        yb = _gdn(qkv, z, ba, conv_w[l], gparams, jnp.concatenate([gdn_norm[l]] * 2)[None], n_ctx)

        sink_row = jnp.pad(sink[l], (0, LANE - sink.shape[1]))[None]
        yc = _attn_x(aq, ak, av, cos_t, sin_t, sink_row, n_ctx)
        if not last:
            yc = jnp.concatenate([_attn_ctx(aq, ak, av, sink_row, n_ctx), yc], axis=1)

        x_new, hx = _out_proj(h, ya, yb, yc, w_out[l].astype(BF16), mod[l], norm2[l][None], n_ctx,
                              skip_ctx=last)
        t = hx.shape[0] * hx.shape[1]
        hx = hx.reshape(t, d)
        keys = peer_keys[l].reshape(2 * PEER_HEADS, N_KEYS, -1).astype(BF16)
        route = _peer_route(hx.T, peer_wq[l].T.astype(BF16), keys)
        gates = _peer_gates(route).reshape(-1, N_KEYS)
        ut = peer_u[l].reshape(-1, PEER_EXP_TILE, d).transpose(0, 2, 1).astype(BF16)
        ffn = _peer_dense(hx, gates, ut, peer_v, l)
        ffn = ffn.reshape(x_new.shape)
        h = x_new
    return _final(h, ffn, mod[depth - 1], norm_f[None])
```
